```python
import math
import jax
import jax.numpy as jnp
from jax import lax
import numpy as np

D_MODEL = 1024
BATCH = 16
SEQ = 256
DEPTH = 2
DEC_BATCH = 8
DEC_SEQ = 1024
PAST_LEN = 512

GRID_W = 64
HEAD_DIM = 64
N_EVEN = (DEPTH + 1) // 2
N_ODD = DEPTH // 2
HY_CH = D_MODEL // 2
HY_BANDS = 8
HY_EMB = 1 + 2 * HY_BANDS
HY_FILTER_W = 64
HY_SHORT = 3
B_HEADS = (D_MODEL // 2) // HEAD_DIM
B_KV = B_HEADS // 4
WINDOW = 128
C_HEADS = D_MODEL // HEAD_DIM
C_KV = C_HEADS // 4
Q_BLOCK = 128
ROPE_BASE = 10000.0
N_EXPERTS = 16
EC_CAPACITY = 2
EXPERT_FF = 2 * D_MODEL
AB_IN = 3 * HY_CH + (B_HEADS + 2 * B_KV) * HEAD_DIM
C_IN = (C_HEADS + 2 * C_KV) * HEAD_DIM
EPS = 1e-6
NEG_INF = -1e30

kernel_name = 'hybrid_hyena_swa_axialgqa_ec_diffusion_step'


def rms_norm(x, g):
    xf = x.astype(jnp.float32)
    y = xf * lax.rsqrt(jnp.mean(xf * xf, axis=-1, keepdims=True) + EPS)
    return (y * g.astype(jnp.float32)).astype(x.dtype)


def adaln(cond, w, b):
    m = (jax.nn.silu(cond) @ w + b)[..., None, :]
    return jnp.split(m, 6, axis=-1)


def axial_rope(x):
    b, L, H, hd = x.shape
    rows = L // GRID_W
    r, col = jnp.meshgrid(jnp.arange(rows), jnp.arange(GRID_W), indexing='ij')
    r = r.reshape(-1).astype(jnp.float32)
    col = col.reshape(-1).astype(jnp.float32)
    half = hd // 2
    inv = ROPE_BASE ** (-jnp.arange(0, half, 2, dtype=jnp.float32) / half)

    def rot(xh, pos):
        ang = pos[:, None] * inv[None, :]
        cos = jnp.cos(ang)[None, :, None, :]
        sin = jnp.sin(ang)[None, :, None, :]
        x1, x2 = jnp.split(xh, 2, axis=-1)
        return jnp.concatenate([x1 * cos - x2 * sin, x1 * sin + x2 * cos], axis=-1)

    xf = x.astype(jnp.float32)
    out = jnp.concatenate([rot(xf[..., :half], r), rot(xf[..., half:], col)], axis=-1)
    return out.astype(x.dtype)


def short_conv3(u, w, b):
    up = jnp.pad(u, ((0, 0), (1, 1), (0, 0)))
    return up[:, :-2] * w[0] + up[:, 1:-1] * w[1] + up[:, 2:] * w[2] + b


def hyena_filters(L, pa):
    f32 = jnp.float32
    t = jnp.linspace(0.0, 1.0, L, dtype=f32)[:, None]
    w = (2.0 * math.pi / L) * jnp.arange(L, dtype=f32)[:, None]
    bands = jnp.linspace(1e-4, HY_BANDS - 1, HY_BANDS, dtype=f32)[None, :]
    z = jnp.concatenate([t, jnp.cos(bands * w), -jnp.sin(bands * w)], axis=-1)
    freq = pa['freq'].astype(f32)
    h = jnp.sin(freq[0] * (z @ pa['w1'].astype(f32) + pa['b1'].astype(f32)))
    h = jnp.sin(freq[1] * (h @ pa['w2'].astype(f32) + pa['b2'].astype(f32)))
    h = (h @ pa['w3'].astype(f32)) * jnp.exp(-t * jnp.exp(pa['log_decay'].astype(f32)))
    return h[:, :HY_CH], h[:, HY_CH:]


def bidir_fft_conv(u, hf, hb):
    b, L, C = u.shape
    k = jnp.concatenate([hf, jnp.zeros((1, C), jnp.float32), hb[1:][::-1]], axis=0)
    k = k * lax.rsqrt(jnp.sum(k * k, axis=0, keepdims=True) + EPS)
    n = 2 * L
    y = jnp.fft.irfft(jnp.fft.rfft(u, n=n, axis=1) * jnp.fft.rfft(k, n=n, axis=0)[None], n=n, axis=1)
    return y[:, :L]


def hyena_mix(p, pa):
    u = short_conv3(p, pa['short_w'], pa['short_b'])
    x0, x1, v = jnp.split(u, 3, axis=-1)
    hf, hb = hyena_filters(p.shape[1], pa)
    z = (v * x1).astype(jnp.float32)
    y = bidir_fft_conv(z, hf, hb) + z * pa['bias'].astype(jnp.float32)
    return x0 * y.astype(p.dtype)


def dense_attention(q, k, v, sink=None):
    b, L, H, hd = q.shape
    kvh = k.shape[2]
    g = H // kvh
    nb = L // Q_BLOCK
    scale = hd ** -0.5
    qb = jnp.moveaxis(q.reshape(b, nb, Q_BLOCK, kvh, g, hd), 1, 0)

    def block(qi):
        s = jnp.einsum('bqkgd,bskd->bkgqs', qi, k, preferred_element_type=jnp.float32) * scale
        if sink is not None:
            sk = jnp.broadcast_to(sink.astype(jnp.float32).reshape(kvh, g)[None, :, :, None, None], s.shape[:-1] + (1,))
            p = jax.nn.softmax(jnp.concatenate([sk, s], axis=-1), axis=-1)[..., 1:]
        else:
            p = jax.nn.softmax(s, axis=-1)
        return jnp.einsum('bkgqs,bskd->bqkgd', p.astype(v.dtype), v)

    o = lax.map(block, qb)
    return jnp.moveaxis(o, 0, 1).reshape(b, L, H * hd)


def window_ctx_attention(q, k, v, kc, vc, sink):
    b, L, H, hd = q.shape
    kvh = k.shape[2]
    g = H // kvh
    nb = L // WINDOW
    lc = kc.shape[1]
    scale = hd ** -0.5
    f32 = jnp.float32
    qb = q.reshape(b, nb, WINDOW, kvh, g, hd)

    def bands(a):
        ap = jnp.pad(a, ((0, 0), (WINDOW, WINDOW), (0, 0), (0, 0))).reshape(b, nb + 2, WINDOW, kvh, hd)
        return jnp.concatenate([ap[:, :-2], ap[:, 1:-1], ap[:, 2:]], axis=2)

    kw, vw = bands(k), bands(v)
    qpos = jnp.arange(L).reshape(nb, WINDOW)
    kpos = jnp.arange(nb)[:, None] * WINDOW - WINDOW + jnp.arange(3 * WINDOW)[None, :]
    valid = ((kpos[:, None, :] >= 0) & (kpos[:, None, :] < L)
             & (jnp.abs(qpos[:, :, None] - kpos[:, None, :]) <= WINDOW))
    s_w = jnp.einsum('bnqkgd,bnskd->bnkgqs', qb, kw, preferred_element_type=f32) * scale
    s_w = jnp.where(valid[None, :, None, None], s_w, NEG_INF)
    s_c = jnp.einsum('bnqkgd,bskd->bnkgqs', qb, kc, preferred_element_type=f32) * scale
    sk = jnp.broadcast_to(sink.astype(f32).reshape(kvh, g)[None, None, :, :, None, None], s_c.shape[:-1] + (1,))
    p = jax.nn.softmax(jnp.concatenate([sk, s_c, s_w], axis=-1), axis=-1)
    p_c = p[..., 1:1 + lc].astype(v.dtype)
    p_w = p[..., 1 + lc:].astype(v.dtype)
    o = jnp.einsum('bnkgqs,bskd->bnqkgd', p_c, vc) + jnp.einsum('bnkgqs,bnskd->bnqkgd', p_w, vw)
    return o.reshape(b, L, H * hd)


def mix_ab(h, pa, ctx_kv=None):
    b, L, _ = h.shape
    p = h @ pa['w_in']
    ph = p[..., :3 * HY_CH]
    q, k, v = jnp.split(p[..., 3 * HY_CH:], [B_HEADS * HEAD_DIM, (B_HEADS + B_KV) * HEAD_DIM], axis=-1)
    q = q.reshape(b, L, B_HEADS, HEAD_DIM)
    k = k.reshape(b, L, B_KV, HEAD_DIM)
    v = v.reshape(b, L, B_KV, HEAD_DIM)
    ya = hyena_mix(ph, pa)
    if ctx_kv is None:
        yb = dense_attention(q, k, v, pa['sink'])
    else:
        yb = window_ctx_attention(axial_rope(q), axial_rope(k), v, ctx_kv[0], ctx_kv[1], pa['sink'])
    y = jnp.concatenate([ya, yb], axis=-1) @ pa['w_out']
    return y, k, v


def mix_c(h, pc, ctx_kv=None):
    b, L, _ = h.shape
    p = h @ pc['w_qkv']
    q, k, v = jnp.split(p, [C_HEADS * HEAD_DIM, (C_HEADS + C_KV) * HEAD_DIM], axis=-1)
    q = rms_norm(q.reshape(b, L, C_HEADS, HEAD_DIM), pc['q_norm'])
    k = rms_norm(k.reshape(b, L, C_KV, HEAD_DIM), pc['k_norm'])
    v = v.reshape(b, L, C_KV, HEAD_DIM)
    if ctx_kv is None:
        y = dense_attention(q, k, v)
    else:
        k_all = jnp.concatenate([ctx_kv[0], axial_rope(k)], axis=1)
        v_all = jnp.concatenate([ctx_kv[1], v], axis=1)
        y = dense_attention(axial_rope(q), k_all, v_all)
    return y @ pc['w_out'], k, v


def expert_choice_ffn(h, w_router, w_gate, w_up, w_down):
    b, L, d = h.shape
    n = b * L
    xt = h.reshape(n, d)
    cap = EC_CAPACITY * n // N_EXPERTS
    aff = jax.nn.softmax((xt @ w_router).astype(jnp.float32), axis=-1)
    gate, idx = lax.top_k(aff.T, cap)
    xe = xt[idx]
    a = jnp.einsum('ecd,edf->ecf', xe, w_gate)
    u = jnp.einsum('ecd,edf->ecf', xe, w_up)
    ye = jnp.einsum('ecf,efd->ecd', jax.nn.silu(a) * u, w_down) * gate[..., None].astype(xe.dtype)
    out = jnp.zeros((n, d), ye.dtype).at[idx.reshape(-1)].add(ye.reshape(-1, d))
    return out.reshape(b, L, d)


def setup_inputs(seed: int = 0) -> dict:
    key = jax.random.key(seed)
    ks = iter(jax.random.split(key, 48))
    f32 = jnp.float32

    def nrm(shape, scale=1.0):
        return scale * jax.random.normal(next(ks), shape, f32)

    D, HD = D_MODEL, HEAD_DIM
    inp = {}
    inp['x_prompt'] = nrm((BATCH, SEQ, D))
    inp['x_sample'] = nrm((DEC_BATCH, DEC_SEQ, D))
    inp['cache_b_k'] = nrm((DEC_BATCH, N_EVEN, PAST_LEN, B_KV, HD))
    inp['cache_b_v'] = nrm((DEC_BATCH, N_EVEN, PAST_LEN, B_KV, HD))
    inp['cache_c_k'] = nrm((DEC_BATCH, N_ODD, PAST_LEN, C_KV, HD))
    inp['cache_c_v'] = nrm((DEC_BATCH, N_ODD, PAST_LEN, C_KV, HD))
    inp['c'] = nrm((DEC_BATCH, D))
    inp['c_ctx'] = nrm((D,))
    inp['ada_w'] = nrm((DEPTH, D, 6 * D), 0.5 * D ** -0.5)
    inp['ada_b'] = nrm((DEPTH, 6 * D), 0.02)
    inp['norm_g'] = 1.0 + nrm((DEPTH, 4, D), 0.02)
    inp['ab_w_in'] = nrm((N_EVEN, D, AB_IN), D ** -0.5)
    inp['hy_short_w'] = nrm((N_EVEN, HY_SHORT, 3 * HY_CH), HY_SHORT ** -0.5)
    inp['hy_short_b'] = nrm((N_EVEN, 3 * HY_CH), 0.02)
    inp['hy_w1'] = nrm((N_EVEN, HY_EMB, HY_FILTER_W), HY_EMB ** -0.5)
    inp['hy_b1'] = nrm((N_EVEN, HY_FILTER_W), 0.1)
    inp['hy_w2'] = nrm((N_EVEN, HY_FILTER_W, HY_FILTER_W), HY_FILTER_W ** -0.5)
    inp['hy_b2'] = nrm((N_EVEN, HY_FILTER_W), 0.1)
    inp['hy_freq'] = 1.0 + nrm((N_EVEN, 2, HY_FILTER_W), 0.05)
    inp['hy_w3'] = nrm((N_EVEN, HY_FILTER_W, 2 * HY_CH), HY_FILTER_W ** -0.5)
    inp['hy_log_decay'] = jax.random.uniform(next(ks), (N_EVEN, 2 * HY_CH), f32, math.log(3.0), math.log(15.0))
    inp['hy_bias'] = nrm((N_EVEN, HY_CH), 0.5)
    inp['b_sink'] = nrm((N_EVEN, B_HEADS), 0.5)
    inp['ab_w_out'] = nrm((N_EVEN, D, D), D ** -0.5)
    inp['c_w_qkv'] = nrm((N_ODD, D, C_IN), D ** -0.5)
    inp['c_q_norm'] = 1.0 + nrm((N_ODD, HD), 0.02)
    inp['c_k_norm'] = 1.0 + nrm((N_ODD, HD), 0.02)
    inp['c_w_out'] = nrm((N_ODD, D, D), D ** -0.5)
    inp['ec_router'] = nrm((DEPTH, D, N_EXPERTS), D ** -0.5)
    inp['ec_w_gate'] = nrm((DEPTH, N_EXPERTS, D, EXPERT_FF), D ** -0.5)
    inp['ec_w_up'] = nrm((DEPTH, N_EXPERTS, D, EXPERT_FF), D ** -0.5)
    inp['ec_w_down'] = nrm((DEPTH, N_EXPERTS, EXPERT_FF, D), EXPERT_FF ** -0.5)
    return inp


def reference(x_prompt, x_sample, cache_b_k, cache_b_v, cache_c_k, cache_c_v, c, c_ctx,
              ada_w, ada_b, norm_g,
              ab_w_in, hy_short_w, hy_short_b, hy_w1, hy_b1, hy_w2, hy_b2, hy_freq, hy_w3,
              hy_log_decay, hy_bias, b_sink, ab_w_out,
              c_w_qkv, c_q_norm, c_k_norm, c_w_out,
              ec_router, ec_w_gate, ec_w_up, ec_w_down):
    xp = x_prompt
    xs = x_sample
    new_bk, new_bv, new_ck, new_cv = [], [], [], []
    for layer in range(DEPTH):
        sh1p, sc1p, g1p, sh2p, sc2p, g2p = adaln(c_ctx, ada_w[layer], ada_b[layer])
        sh1s, sc1s, g1s, sh2s, sc2s, g2s = adaln(c, ada_w[layer], ada_b[layer])
        hp = rms_norm(xp, norm_g[layer, 0]) * (1.0 + sc1p) + sh1p
        hs = rms_norm(xs, norm_g[layer, 0]) * (1.0 + sc1s) + sh1s
        if layer % 2 == 0:
            e = layer // 2
            pa = {'w_in': ab_w_in[e], 'short_w': hy_short_w[e], 'short_b': hy_short_b[e],
                  'w1': hy_w1[e], 'b1': hy_b1[e], 'w2': hy_w2[e], 'b2': hy_b2[e],
                  'freq': hy_freq[e], 'w3': hy_w3[e], 'log_decay': hy_log_decay[e],
                  'bias': hy_bias[e], 'sink': b_sink[e], 'w_out': ab_w_out[e]}
            yp, kp, vp = mix_ab(hp, pa)
            ys, _, _ = mix_ab(hs, pa, (cache_b_k[:, e], cache_b_v[:, e]))
            new_bk.append(kp)
            new_bv.append(vp)
        else:
            o = layer // 2
            pc = {'w_qkv': c_w_qkv[o], 'q_norm': c_q_norm[o], 'k_norm': c_k_norm[o], 'w_out': c_w_out[o]}
            yp, kp, vp = mix_c(hp, pc)
            ys, _, _ = mix_c(hs, pc, (cache_c_k[:, o], cache_c_v[:, o]))
            new_ck.append(kp)
            new_cv.append(vp)
        xp = xp + g1p * rms_norm(yp, norm_g[layer, 1])
        xs = xs + g1s * rms_norm(ys, norm_g[layer, 1])
        hp = rms_norm(xp, norm_g[layer, 2]) * (1.0 + sc2p) + sh2p
        hs = rms_norm(xs, norm_g[layer, 2]) * (1.0 + sc2s) + sh2s
        fp = expert_choice_ffn(hp, ec_router[layer], ec_w_gate[layer], ec_w_up[layer], ec_w_down[layer])
        fs = expert_choice_ffn(hs, ec_router[layer], ec_w_gate[layer], ec_w_up[layer], ec_w_down[layer])
        xp = xp + g2p * rms_norm(fp, norm_g[layer, 3])
        xs = xs + g2s * rms_norm(fs, norm_g[layer, 3])
    state_b_k = jnp.stack(new_bk, axis=1)
    state_b_v = jnp.stack(new_bv, axis=1)
    state_c_k = jnp.stack(new_ck, axis=1)
    state_c_v = jnp.stack(new_cv, axis=1)
    return (xp, xs, state_b_k, state_b_v, state_c_k, state_c_v)
```

```python
import functools
import math

import jax
import jax.numpy as jnp
from jax import lax
from jax.experimental import pallas as pl
from jax.experimental.pallas import tpu as pltpu

F32 = jnp.float32
BF16 = jnp.bfloat16

D_MODEL = 1024
BATCH = 16
SEQ = 256
DEPTH = 2
DEC_BATCH = 8
DEC_SEQ = 1024
PAST_LEN = 512
GRID_W = 64
HEAD_DIM = 64
HY_CH = D_MODEL // 2
HY_BANDS = 8
B_HEADS = 8
B_KV = 2
WINDOW = 128
C_HEADS = 16
C_KV = 4
ROPE_BASE = 10000.0
N_EXPERTS = 16
EC_CAPACITY = 2
EXPERT_FF = 2 * D_MODEL
EPS = 1e-6
NEG_INF = -1e30

NS = DEC_BATCH * DEC_SEQ
NP = BATCH * SEQ
NT = NS + NP
MOD_ROWS = 16
LANES = 128
VMEM_LIMIT = 56 * 1024 * 1024


def _cparams(sem):
    return pltpu.CompilerParams(dimension_semantics=sem, vmem_limit_bytes=VMEM_LIMIT)


def _split(x):
    hi = x.astype(BF16)
    lo = (x - hi.astype(F32)).astype(BF16)
    return hi, lo


_NN = (((1,), (0,)), ((), ()))
_NT = (((1,), (1,)), ((), ()))


def _dot(a, b, dims=_NN):
    return lax.dot_general(a, b, dims, preferred_element_type=F32)


def _dot3(a, b, dims=_NN):
    ah, al = _split(a)
    bh, bl = _split(b)
    return _dot(ah, bh, dims) + _dot(al, bh, dims) + _dot(ah, bl, dims)


def _rms(x, g):
    return x * lax.rsqrt(jnp.mean(x * x, axis=-1, keepdims=True) + EPS) * g


def _mod_row(tile_rows):
    def f(i):
        return jnp.minimum(i * tile_rows // DEC_SEQ, DEC_BATCH)
    return f


def _adaln_kernel(c_ref, w_ref, b_ref, o_ref):
    s = jax.nn.silu(c_ref[...])
    o_ref[0] = _dot3(s, w_ref[0]) + b_ref[0]


def _adaln(cond, ada_w, ada_b):
    tn = 1536
    return pl.pallas_call(
        _adaln_kernel,
        out_shape=jax.ShapeDtypeStruct((DEPTH, MOD_ROWS, 6 * D_MODEL), F32),
        grid=(DEPTH, 6 * D_MODEL // tn),
        in_specs=[pl.BlockSpec((MOD_ROWS, D_MODEL), lambda l, j: (0, 0)),
                  pl.BlockSpec((1, D_MODEL, tn), lambda l, j: (l, 0, j)),
                  pl.BlockSpec((1, 1, tn), lambda l, j: (l, 0, j))],
        out_specs=pl.BlockSpec((1, MOD_ROWS, tn), lambda l, j: (l, 0, j)),
        compiler_params=_cparams(("arbitrary", "arbitrary")),
    )(cond, ada_w, ada_b.reshape(DEPTH, 1, 6 * D_MODEL))


def _mod_spec(layer, which, tile_rows):
    row = _mod_row(tile_rows)
    return pl.BlockSpec((1, 1, D_MODEL), lambda i, *_: ((layer * MOD_ROWS + row(i)) * 6 + which, 0, 0))


def _nmm_kernel(x_ref, g_ref, sh_ref, sc_ref, w_ref, *o_refs):
    h = _rms(x_ref[...], g_ref[...]) * (1.0 + sc_ref[0]) + sh_ref[0]
    p = _dot(h.astype(BF16), w_ref[...])
    off = 0
    for o in o_refs:
        n = o.shape[1]
        o[...] = p[:, off:off + n]
        off += n


def _norm_mod_matmul(x, g, mod, layer, w_bf, splits):
    tm = 512
    n_out = w_bf.shape[1]
    return pl.pallas_call(
        _nmm_kernel,
        out_shape=[jax.ShapeDtypeStruct((NT, n), F32) for n in splits],
        grid=(NT // tm,),
        in_specs=[pl.BlockSpec((tm, D_MODEL), lambda i: (i, 0)),
                  pl.BlockSpec((1, D_MODEL), lambda i: (0, 0)),
                  _mod_spec(layer, 0, tm), _mod_spec(layer, 1, tm),
                  pl.BlockSpec((D_MODEL, n_out), lambda i: (0, 0))],
        out_specs=[pl.BlockSpec((tm, n), lambda i: (i, 0)) for n in splits],
        compiler_params=_cparams(("arbitrary",)),
    )(x, g.reshape(1, D_MODEL), mod, mod, w_bf)


def _hyena_filters(L, w1, b1, w2, b2, freq, w3, log_decay):
    hp = lax.Precision.HIGHEST
    t = jnp.linspace(0.0, 1.0, L, dtype=F32)[:, None]
    w = (2.0 * math.pi / L) * jnp.arange(L, dtype=F32)[:, None]
    bands = jnp.linspace(1e-4, HY_BANDS - 1, HY_BANDS, dtype=F32)[None, :]
    z = jnp.concatenate([t, jnp.cos(bands * w), -jnp.sin(bands * w)], axis=-1)
    h = jnp.sin(freq[0] * (jnp.dot(z, w1, precision=hp) + b1))
    h = jnp.sin(freq[1] * (jnp.dot(h, w2, precision=hp) + b2))
    return jnp.dot(h, w3, precision=hp) * jnp.exp(-t * jnp.exp(log_decay))


def _dft_table(L):
    n = 2 * L
    f = jnp.arange(L, dtype=jnp.int32)[:, None]
    t = jnp.arange(L, dtype=jnp.int32)[None, :]
    ang = ((f * t) % n).astype(F32) * (2.0 * math.pi / n)
    nyq = jnp.where(t % 2 == 0, 1.0, -1.0).astype(F32)
    fs = jnp.where(f == 0, nyq, -jnp.sin(ang))
    return jnp.concatenate([jnp.cos(ang), fs], axis=0)


def _filt_kernel(f_ref, hf_ref, hb_ref, p_ref, q_ref, s_ref, *, L):
    hf = hf_ref[...]
    hb = hb_ref[...]
    ct = hf.shape[1]
    nrm = lax.rsqrt(jnp.sum(hf * hf, axis=0, keepdims=True) + jnp.sum(hb * hb, axis=0, keepdims=True) + EPS)
    fh = _dot3(f_ref[...], jnp.concatenate([hf, hb], axis=1))
    row = lax.broadcasted_iota(jnp.int32, (L, ct), 0)
    w = jnp.where(row == 0, 1.0 / (2 * L), 2.0 / (2 * L)) * nrm
    re = fh[:L, :ct] + fh[:L, ct:]
    p_ref[...] = re * w
    q_ref[...] = jnp.where(row == 0, 0.0, fh[L:, :ct] - fh[L:, ct:]) * w
    s_ref[...] = jnp.where(row == 0, fh[L:, :ct] + fh[L:, ct:], re) * w


def _filter_spectrum(L, table, h):
    ct = 128
    nc = HY_CH // ct
    shp = jax.ShapeDtypeStruct((L, HY_CH), F32)
    return pl.pallas_call(
        functools.partial(_filt_kernel, L=L),
        out_shape=[shp, shp, shp],
        grid=(nc,),
        in_specs=[pl.BlockSpec((2 * L, L), lambda c: (0, 0)),
                  pl.BlockSpec((L, ct), lambda c: (0, c)),
                  pl.BlockSpec((L, ct), lambda c: (0, nc + c))],
        out_specs=[pl.BlockSpec((L, ct), lambda c: (0, c))] * 3,
        compiler_params=_cparams(("arbitrary",)),
    )(table, h, h)


def _hyena_kernel(x0_ref, x1_ref, v_ref, w0_ref, w1_ref, wv_ref, b0_ref, b1_ref, bv_ref, bias_ref,
                  f_ref, g_ref, p_ref, q_ref, s_ref, o_ref, *, L):
    ct = x0_ref.shape[1]
    row = lax.broadcasted_iota(jnp.int32, (L, ct), 0)

    def sconv(x_ref, w_ref, b_ref):
        x = x_ref[...]
        xm = jnp.where(row == 0, 0.0, pltpu.roll(x, 1, 0))
        xp = jnp.where(row == L - 1, 0.0, pltpu.roll(x, L - 1, 0))
        return xm * w_ref[0:1, :] + x * w_ref[1:2, :] + xp * w_ref[2:3, :] + b_ref[...]

    z = sconv(v_ref, wv_ref, bv_ref) * sconv(x1_ref, w1_ref, b1_ref)
    zf = _dot(f_ref[...], z.astype(BF16))
    a = zf[:L]
    b = zf[L:]
    p = p_ref[...]
    q = q_ref[...]
    yre = a * p - b * q
    yim = a * q + b * s_ref[...]
    ycat = jnp.concatenate([yre, yim], axis=0).astype(BF16)
    y = _dot(g_ref[...], ycat) + z * bias_ref[...]
    o_ref[...] = sconv(x0_ref, w0_ref, b0_ref) * y


def _hyena(p_hy, short_w, short_b, bias, table_bf, table_t_bf, spec, L, nb, row_blk0):
    ct = 256
    nc = HY_CH // ct
    P, Q, S = spec

    def xs(k):
        return pl.BlockSpec((L, ct), lambda b, c: (row_blk0 + b, k * nc + c))

    def ws(rows, k):
        return pl.BlockSpec((rows, ct), lambda b, c: (0, k * nc + c))

    cs = pl.BlockSpec((L, ct), lambda b, c: (0, c))
    return pl.pallas_call(
        functools.partial(_hyena_kernel, L=L),
        out_shape=jax.ShapeDtypeStruct((nb * L, HY_CH), F32),
        grid=(nb, nc),
        in_specs=[xs(0), xs(1), xs(2), ws(3, 0), ws(3, 1), ws(3, 2), ws(1, 0), ws(1, 1), ws(1, 2), ws(1, 0),
                  pl.BlockSpec((2 * L, L), lambda b, c: (0, 0)),
                  pl.BlockSpec((L, 2 * L), lambda b, c: (0, 0)),
                  cs, cs, cs],
        out_specs=pl.BlockSpec((L, ct), lambda b, c: (b, c)),
        compiler_params=_cparams(("arbitrary", "arbitrary")),
    )(p_hy, p_hy, p_hy, short_w, short_w, short_w, short_b, short_b, short_b, bias,
      table_bf, table_t_bf, P, Q, S)


def _rope_tables(L):
    half = HEAD_DIM // 2
    pos = jnp.arange(L)
    r = (pos // GRID_W).astype(F32)
    col = (pos % GRID_W).astype(F32)
    inv = ROPE_BASE ** (-jnp.arange(0, half, 2, dtype=F32) / half)
    ar = r[:, None] * inv[None, :]
    ac = col[:, None] * inv[None, :]
    z = jnp.zeros_like(ar)
    cos = jnp.concatenate([jnp.cos(ar), jnp.cos(ar), jnp.cos(ac), jnp.cos(ac)], axis=1)
    s1 = jnp.concatenate([-jnp.sin(ar), z, -jnp.sin(ac), z], axis=1)
    s2 = jnp.concatenate([z, jnp.sin(ar), z, jnp.sin(ac)], axis=1)
    rep = LANES // HEAD_DIM
    return jnp.tile(cos, (1, rep)), jnp.tile(s1, (1, rep)), jnp.tile(s2, (1, rep))


def _attn_kernel(*refs, L, Tq, H, KVH, Lc, window, use_sink, use_norm, use_rope, emit_k):
    G = H // KVH
    hd = HEAD_DIM
    pad = WINDOW if window else 0
    it = iter(refs)
    q_ref = next(it)
    kv_ref = next(it)
    ck_ref = cv_ref = sink_ref = qn_ref = kn_ref = bd_ref = rc_ref = rs1_ref = rs2_ref = ko_ref = None
    if Lc:
        ck_ref = next(it)
        cv_ref = next(it)
    if use_sink:
        sink_ref = next(it)
    if use_norm:
        qn_ref = next(it)
        kn_ref = next(it)
        bd_ref = next(it)
    if use_rope:
        rc_ref = next(it)
        rs1_ref = next(it)
        rs2_ref = next(it)
    o_ref = next(it)
    if emit_k:
        ko_ref = next(it)
    k_scr = next(it)
    v_scr = next(it)
    q_scr = next(it)
    i = pl.program_id(1)
    lat0 = Lc + pad

    def head_norm(x, g_ref):
        hi, lo = _split(x * x)
        ms = _dot(hi, bd_ref[...]) + _dot(lo, bd_ref[...])
        return x * lax.rsqrt(ms + EPS) * g_ref[...]

    def rope(x, rows):
        return (x * rc_ref[rows, :] + pltpu.roll(x, LANES - 16, 1) * rs1_ref[rows, :]
                + pltpu.roll(x, 16, 1) * rs2_ref[rows, :])

    @pl.when(i == 0)
    def _():
        if window:
            zpad = jnp.zeros((KVH, pad, hd), BF16)
            k_scr[:, Lc:Lc + pad, :] = zpad
            k_scr[:, lat0 + L:lat0 + L + pad, :] = zpad
            v_scr[:, Lc:Lc + pad, :] = zpad
            v_scr[:, lat0 + L:lat0 + L + pad, :] = zpad
        for g in range(KVH * hd // LANES):
            kg = kv_ref[:, LANES * g:LANES * (g + 1)]
            if use_norm:
                kg = head_norm(kg, kn_ref)
                if emit_k:
                    ko_ref[:, LANES * g:LANES * (g + 1)] = kg
            if use_rope:
                kg = rope(kg, slice(None))
            kb = kg.astype(BF16)
            vb = kv_ref[:, KVH * hd + LANES * g:KVH * hd + LANES * (g + 1)].astype(BF16)
            for hh in range(LANES // hd):
                k_scr[2 * g + hh, lat0:lat0 + L, :] = kb[:, hd * hh:hd * (hh + 1)]
                v_scr[2 * g + hh, lat0:lat0 + L, :] = vb[:, hd * hh:hd * (hh + 1)]
        if Lc:
            ckb = ck_ref[0].astype(BF16)
            cvb = cv_ref[0].astype(BF16)
            for kvh in range(KVH):
                k_scr[kvh, 0:Lc, :] = ckb[:, hd * kvh:hd * (kvh + 1)]
                v_scr[kvh, 0:Lc, :] = cvb[:, hd * kvh:hd * (kvh + 1)]

    rows = pl.ds(pl.multiple_of(i * Tq, Tq), Tq)
    for g in range(H * hd // LANES):
        qg = q_ref[:, LANES * g:LANES * (g + 1)]
        if use_norm:
            qg = head_norm(qg, qn_ref)
        if use_rope:
            qg = rope(qg, rows)
        qb = (qg * (hd ** -0.5)).astype(BF16)
        for hh in range(LANES // hd):
            q_scr[2 * g + hh] = qb[:, hd * hh:hd * (hh + 1)]

    hrow = lax.broadcasted_iota(jnp.int32, (G * Tq, 1), 0) // Tq
    for kvh in range(KVH):
        qs = q_scr[kvh * G:(kvh + 1) * G].reshape(G * Tq, hd)
        if use_sink:
            sk = jnp.zeros((G * Tq, 1), F32)
            for g in range(G):
                sk = jnp.where(hrow == g, sink_ref[0, kvh * G + g], sk)
        if window:
            W = Tq + 2 * pad
            kc = k_scr[kvh, 0:Lc, :]
            vc = v_scr[kvh, 0:Lc, :]
            wrows = pl.ds(pl.multiple_of(Lc + i * Tq, Tq), W)
            kw = k_scr[kvh, wrows, :]
            vw = v_scr[kvh, wrows, :]
            s_c = _dot(qs, kc, _NT)
            s_w = _dot(qs, kw, _NT).reshape(G, Tq, W)
            r = lax.broadcasted_iota(jnp.int32, (Tq, W), 0)
            c = lax.broadcasted_iota(jnp.int32, (Tq, W), 1)
            kpos = i * Tq - pad + c
            valid = (kpos >= 0) & (kpos < L) & (jnp.abs(r + pad - c) <= WINDOW)
            s_w = jnp.where(valid[None], s_w, NEG_INF).reshape(G * Tq, W)
            m = jnp.maximum(jnp.max(s_c, axis=-1, keepdims=True), jnp.max(s_w, axis=-1, keepdims=True))
            if use_sink:
                m = jnp.maximum(m, sk)
            e_c = jnp.exp(s_c - m)
            e_w = jnp.exp(s_w - m)
            den = jnp.sum(e_c, axis=-1, keepdims=True) + jnp.sum(e_w, axis=-1, keepdims=True)
            o = _dot(e_c.astype(BF16), vc) + _dot(e_w.astype(BF16), vw)
        else:
            s = _dot(qs, k_scr[kvh], _NT)
            m = jnp.max(s, axis=-1, keepdims=True)
            if use_sink:
                m = jnp.maximum(m, sk)
            e = jnp.exp(s - m)
            den = jnp.sum(e, axis=-1, keepdims=True)
            o = _dot(e.astype(BF16), v_scr[kvh])
        if use_sink:
            den = den + jnp.exp(sk - m)
        o = o / den
        for g in range(G):
            h = kvh * G + g
            o_ref[:, hd * h:hd * (h + 1)] = o[g * Tq:(g + 1) * Tq]


def _attention(p, q_cols, kv_colblk, *, L, Tq, nb, row0, H, KVH, ctx=None, sink=None, norm=None, rope=None,
               window=False, emit_k=False):
    hd = HEAD_DIM
    nq = L // Tq
    Lc = 0 if ctx is None else ctx[0].shape[1]
    pad = WINDOW if window else 0
    Lk = Lc + L + 2 * pad
    kvw = 2 * KVH * hd
    args = [p, p]
    in_specs = [pl.BlockSpec((Tq, q_cols), lambda b, i: (row0 // Tq + b * nq + i, 0)),
                pl.BlockSpec((L, kvw), lambda b, i: (row0 // L + b, kv_colblk))]
    if ctx is not None:
        for a in ctx:
            args.append(a)
            in_specs.append(pl.BlockSpec((1, Lc, KVH * hd), lambda b, i: (b, 0, 0)))
    if sink is not None:
        args.append(sink.reshape(1, H))
        in_specs.append(pl.BlockSpec(memory_space=pltpu.SMEM))
    if norm is not None:
        bd = (jnp.arange(LANES)[:, None] // hd == jnp.arange(LANES)[None, :] // hd).astype(F32) / hd
        for a in (jnp.tile(norm[0], LANES // hd).reshape(1, LANES), jnp.tile(norm[1], LANES // hd).reshape(1, LANES)):
            args.append(a)
            in_specs.append(pl.BlockSpec((1, LANES), lambda b, i: (0, 0)))
        args.append(bd.astype(BF16))
        in_specs.append(pl.BlockSpec((LANES, LANES), lambda b, i: (0, 0)))
    if rope is not None:
        for a in rope:
            args.append(a)
            in_specs.append(pl.BlockSpec((L, LANES), lambda b, i: (0, 0)))
    out_shape = [jax.ShapeDtypeStruct((nb * L, H * hd), F32)]
    out_specs = [pl.BlockSpec((Tq, H * hd), lambda b, i: (b * nq + i, 0))]
    if emit_k:
        out_shape.append(jax.ShapeDtypeStruct((nb * L, KVH * hd), F32))
        out_specs.append(pl.BlockSpec((L, KVH * hd), lambda b, i: (b, 0)))
    kern = functools.partial(_attn_kernel, L=L, Tq=Tq, H=H, KVH=KVH, Lc=Lc, window=window,
                             use_sink=sink is not None, use_norm=norm is not None,
                             use_rope=rope is not None, emit_k=emit_k)
    call = pl.pallas_call(
        kern, out_shape=out_shape, grid=(nb, nq), in_specs=in_specs, out_specs=out_specs,
        scratch_shapes=[pltpu.VMEM((KVH, Lk, hd), BF16), pltpu.VMEM((KVH, Lk, hd), BF16),
                        pltpu.VMEM((H, Tq, hd), BF16)],
        compiler_params=_cparams(("arbitrary", "arbitrary")))
    return call, args


def _oproj_kernel(al_ref, ar_ref, w_ref, x_ref, g1_ref, gate_ref, g2_ref, sh_ref, sc_ref, wr_ref,
                  x1_ref, h2_ref, lg_ref):
    k = al_ref.shape[1]
    y = _dot(al_ref[...].astype(BF16), w_ref[0:k, :]) + _dot(ar_ref[...].astype(BF16), w_ref[k:2 * k, :])
    x1 = x_ref[...] + gate_ref[0] * _rms(y, g1_ref[...])
    x1_ref[...] = x1
    h2 = _rms(x1, g2_ref[...]) * (1.0 + sc_ref[0]) + sh_ref[0]
    h2_ref[...] = h2
    lg_ref[...] = _dot3(wr_ref[...], h2, _NT)


def _out_proj(a_l, l_blk, a_r, r_blk, w_bf, x, g1, g2, mod, layer, wr_t):
    tm = 512
    half = D_MODEL // 2
    return pl.pallas_call(
        _oproj_kernel,
        out_shape=[jax.ShapeDtypeStruct((NT, D_MODEL), F32), jax.ShapeDtypeStruct((NT, D_MODEL), F32),
                   jax.ShapeDtypeStruct((N_EXPERTS, NT), F32)],
        grid=(NT // tm,),
        in_specs=[pl.BlockSpec((tm, half), lambda i: (i, l_blk)),
                  pl.BlockSpec((tm, half), lambda i: (i, r_blk)),
                  pl.BlockSpec((D_MODEL, D_MODEL), lambda i: (0, 0)),
                  pl.BlockSpec((tm, D_MODEL), lambda i: (i, 0)),
                  pl.BlockSpec((1, D_MODEL), lambda i: (0, 0)),
                  _mod_spec(layer, 2, tm),
                  pl.BlockSpec((1, D_MODEL), lambda i: (0, 0)),
                  _mod_spec(layer, 3, tm), _mod_spec(layer, 4, tm),
                  pl.BlockSpec((N_EXPERTS, D_MODEL), lambda i: (0, 0))],
        out_specs=[pl.BlockSpec((tm, D_MODEL), lambda i: (i, 0)),
                   pl.BlockSpec((tm, D_MODEL), lambda i: (i, 0)),
                   pl.BlockSpec((N_EXPERTS, tm), lambda i: (0, i))],
        compiler_params=_cparams(("arbitrary",)),
    )(a_l, a_r, w_bf, x, g1.reshape(1, D_MODEL), mod, g2.reshape(1, D_MODEL), mod, mod, wr_t)


def _ffn_kernel(idx_ref, h_hbm, gate_ref, wg_ref, wu_ref, wd_ref, y_ref, xf_scr, xb_scr, acc_scr, sem, *, cap):
    e = pl.program_id(0)
    f = pl.program_id(1)

    def row_copy(src_row, p):
        return pltpu.make_async_copy(h_hbm.at[pl.ds(src_row, 1)], xf_scr.at[pl.ds(p, 1)], sem)

    @pl.when(f == 0)
    def _():
        def issue(p, carry):
            row_copy(idx_ref[e, p], p).start()
            return carry

        def drain(p, carry):
            row_copy(0, p).wait()
            return carry

        lax.fori_loop(0, cap, issue, 0)
        lax.fori_loop(0, cap, drain, 0)
        xb_scr[...] = xf_scr[...].astype(BF16)
        acc_scr[...] = jnp.zeros_like(acc_scr)

    xe = xb_scr[...]
    a = _dot(xe, wg_ref[0].astype(BF16))
    u = _dot(xe, wu_ref[0].astype(BF16))
    hmid = (jax.nn.silu(a) * u).astype(BF16)
    acc_scr[...] += _dot(hmid, wd_ref[0].astype(BF16))

    @pl.when(f == pl.num_programs(1) - 1)
    def _():
        y_ref[0] = acc_scr[...] * gate_ref[0]


def _expert_ffn(idx, gate, h2, w_gate, w_up, w_down):
    cap = idx.shape[1]
    fc = 512
    grid_spec = pltpu.PrefetchScalarGridSpec(
        num_scalar_prefetch=1,
        grid=(N_EXPERTS, EXPERT_FF // fc),
        in_specs=[pl.BlockSpec(memory_space=pl.ANY),
                  pl.BlockSpec((1, cap, 1), lambda e, f, idx: (e, 0, 0)),
                  pl.BlockSpec((1, D_MODEL, fc), lambda e, f, idx: (e, 0, f)),
                  pl.BlockSpec((1, D_MODEL, fc), lambda e, f, idx: (e, 0, f)),
                  pl.BlockSpec((1, fc, D_MODEL), lambda e, f, idx: (e, f, 0))],
        out_specs=pl.BlockSpec((1, cap, D_MODEL), lambda e, f, idx: (e, 0, 0)),
        scratch_shapes=[pltpu.VMEM((cap, D_MODEL), F32), pltpu.VMEM((cap, D_MODEL), BF16),
                        pltpu.VMEM((cap, D_MODEL), F32), pltpu.SemaphoreType.DMA(())])
    return pl.pallas_call(
        functools.partial(_ffn_kernel, cap=cap),
        out_shape=jax.ShapeDtypeStruct((N_EXPERTS, cap, D_MODEL), F32),
        grid_spec=grid_spec,
        compiler_params=_cparams(("arbitrary", "arbitrary")),
    )(idx, h2, gate.reshape(N_EXPERTS, cap, 1), w_gate, w_up, w_down)


def _resid_kernel(x_ref, f_ref, g_ref, gate_ref, o_ref):
    o_ref[...] = x_ref[...] + gate_ref[0] * _rms(f_ref[...], g_ref[...])


def _ffn_residual(x, f, g, mod, layer):
    tm = 512
    return pl.pallas_call(
        _resid_kernel,
        out_shape=jax.ShapeDtypeStruct((NT, D_MODEL), F32),
        grid=(NT // tm,),
        in_specs=[pl.BlockSpec((tm, D_MODEL), lambda i: (i, 0)),
                  pl.BlockSpec((tm, D_MODEL), lambda i: (i, 0)),
                  pl.BlockSpec((1, D_MODEL), lambda i: (0, 0)),
                  _mod_spec(layer, 5, tm)],
        out_specs=pl.BlockSpec((tm, D_MODEL), lambda i: (i, 0)),
        compiler_params=_cparams(("arbitrary",)),
    )(x, f, g.reshape(1, D_MODEL), mod)


def _route(logits_t, h2, w_gate, w_up, w_down):
    aff = jax.nn.softmax(logits_t, axis=0)
    f = jnp.zeros((NT, D_MODEL), F32)
    for lo, n in ((0, NS), (NS, NP)):
        cap = EC_CAPACITY * n // N_EXPERTS
        gate, idx = lax.top_k(aff[:, lo:lo + n], cap)
        idx = idx + lo
        y = _expert_ffn(idx, gate, h2, w_gate, w_up, w_down)
        f = f.at[idx.reshape(-1)].add(y.reshape(-1, D_MODEL))
    return f


def kernel(x_prompt, x_sample, cache_b_k, cache_b_v, cache_c_k, cache_c_v, c, c_ctx, ada_w, ada_b, norm_g,
           ab_w_in, hy_short_w, hy_short_b, hy_w1, hy_b1, hy_w2, hy_b2, hy_freq, hy_w3, hy_log_decay, hy_bias,
           b_sink, ab_w_out, c_w_qkv, c_q_norm, c_k_norm, c_w_out, ec_router, ec_w_gate, ec_w_up, ec_w_down):
    x = jnp.concatenate([x_sample.reshape(NS, D_MODEL), x_prompt.reshape(NP, D_MODEL)], axis=0)
    cond = jnp.concatenate([c, c_ctx[None], jnp.zeros((MOD_ROWS - DEC_BATCH - 1, D_MODEL), F32)], axis=0)
    mod = _adaln(cond, ada_w, ada_b).reshape(DEPTH * MOD_ROWS * 6, 1, D_MODEL)
    rope = _rope_tables(DEC_SEQ)
    hd = HEAD_DIM

    e = 0
    p_hy, p_at = _norm_mod_matmul(x, norm_g[0, 0], mod, 0, ab_w_in[e].astype(BF16), (3 * HY_CH, D_MODEL - HY_CH + 2 * B_KV * hd))
    ya = []
    for L, nb, row_blk0 in ((DEC_SEQ, DEC_BATCH, 0), (SEQ, BATCH, NS // SEQ)):
        table = _dft_table(L)
        h = _hyena_filters(L, hy_w1[e], hy_b1[e], hy_w2[e], hy_b2[e], hy_freq[e], hy_w3[e], hy_log_decay[e])
        h = h.at[0, HY_CH:].set(0.0)
        spec = _filter_spectrum(L, table, h)
        ya.append(_hyena(p_hy, hy_short_w[e], hy_short_b[e].reshape(1, -1), hy_bias[e].reshape(1, -1),
                         table.astype(BF16), table.T.astype(BF16), spec, L, nb, row_blk0))
    ya = jnp.concatenate(ya, axis=0)
    ctx_b = (cache_b_k[:, e].reshape(DEC_BATCH, PAST_LEN, B_KV * hd), cache_b_v[:, e].reshape(DEC_BATCH, PAST_LEN, B_KV * hd))
    call, args = _attention(p_at, B_HEADS * hd, 2, L=DEC_SEQ, Tq=256, nb=DEC_BATCH, row0=0, H=B_HEADS, KVH=B_KV,
                            ctx=ctx_b, sink=b_sink[e], rope=rope, window=True)
    yb_s = call(*args)[0]
    call, args = _attention(p_at, B_HEADS * hd, 2, L=SEQ, Tq=256, nb=BATCH, row0=NS, H=B_HEADS, KVH=B_KV,
                            sink=b_sink[e])
    yb_p = call(*args)[0]
    yb = jnp.concatenate([yb_s, yb_p], axis=0)
    state_b_k = p_at[NS:, B_HEADS * hd:(B_HEADS + B_KV) * hd].reshape(BATCH, 1, SEQ, B_KV, hd)
    state_b_v = p_at[NS:, (B_HEADS + B_KV) * hd:].reshape(BATCH, 1, SEQ, B_KV, hd)
    x, h2, lg = _out_proj(ya, 0, yb, 0, ab_w_out[e].astype(BF16), x, norm_g[0, 1], norm_g[0, 2], mod, 0, ec_router[0].T)
    f = _route(lg, h2, ec_w_gate[0], ec_w_up[0], ec_w_down[0])
    x = _ffn_residual(x, f, norm_g[0, 3], mod, 0)

    o = 0
    (p,) = _norm_mod_matmul(x, norm_g[1, 0], mod, 1, c_w_qkv[o].astype(BF16), ((C_HEADS + 2 * C_KV) * hd,))
    ctx_c = (cache_c_k[:, o].reshape(DEC_BATCH, PAST_LEN, C_KV * hd), cache_c_v[:, o].reshape(DEC_BATCH, PAST_LEN, C_KV * hd))
    nrm = (c_q_norm[o], c_k_norm[o])
    call, args = _attention(p, C_HEADS * hd, 2, L=DEC_SEQ, Tq=128, nb=DEC_BATCH, row0=0, H=C_HEADS, KVH=C_KV,
                            ctx=ctx_c, norm=nrm, rope=rope)
    yc_s = call(*args)[0]
    call, args = _attention(p, C_HEADS * hd, 2, L=SEQ, Tq=256, nb=BATCH, row0=NS, H=C_HEADS, KVH=C_KV,
                            norm=nrm, emit_k=True)
    yc_p, kn = call(*args)
    yc = jnp.concatenate([yc_s, yc_p], axis=0)
    state_c_k = kn.reshape(BATCH, 1, SEQ, C_KV, hd)
    state_c_v = p[NS:, (C_HEADS + C_KV) * hd:].reshape(BATCH, 1, SEQ, C_KV, hd)
    x, h2, lg = _out_proj(yc, 0, yc, 1, c_w_out[o].astype(BF16), x, norm_g[1, 1], norm_g[1, 2], mod, 1, ec_router[1].T)
    f = _route(lg, h2, ec_w_gate[1], ec_w_up[1], ec_w_down[1])
    x = _ffn_residual(x, f, norm_g[1, 3], mod, 1)

    y_sample = x[:NS].reshape(DEC_BATCH, DEC_SEQ, D_MODEL)
    y_prompt = x[NS:].reshape(BATCH, SEQ, D_MODEL)
    return (y_prompt, y_sample, state_b_k, state_b_v, state_c_k, state_c_v)
```

```python
import functools
import math

import jax
import jax.numpy as jnp
from jax import lax
from jax.experimental import pallas as pl
from jax.experimental.pallas import tpu as pltpu

F32 = jnp.float32
BF16 = jnp.bfloat16

D_MODEL = 1024
BATCH = 16
SEQ = 256
DEPTH = 2
DEC_BATCH = 8
DEC_SEQ = 1024
PAST_LEN = 512
GRID_W = 64
HEAD_DIM = 64
HY_CH = D_MODEL // 2
HY_BANDS = 8
B_HEADS = 8
B_KV = 2
WINDOW = 128
C_HEADS = 16
C_KV = 4
ROPE_BASE = 10000.0
N_EXPERTS = 16
EC_CAPACITY = 2
EXPERT_FF = 2 * D_MODEL
EPS = 1e-6
NEG_INF = -1e30

NS = DEC_BATCH * DEC_SEQ
NP = BATCH * SEQ
NT = NS + NP
MOD_ROWS = 16
LANES = 128
TOK_SUB = D_MODEL // LANES
VMEM_LIMIT = 56 * 1024 * 1024


def _cparams(sem):
    return pltpu.CompilerParams(dimension_semantics=sem, vmem_limit_bytes=VMEM_LIMIT)


def _split(x):
    hi = x.astype(BF16)
    lo = (x - hi.astype(F32)).astype(BF16)
    return hi, lo


_NN = (((1,), (0,)), ((), ()))
_NT = (((1,), (1,)), ((), ()))


def _dot(a, b, dims=_NN):
    return lax.dot_general(a, b, dims, preferred_element_type=F32)


def _dot3(a, b, dims=_NN):
    ah, al = _split(a)
    bh, bl = _split(b)
    return _dot(ah, bh, dims) + _dot(al, bh, dims) + _dot(ah, bl, dims)


def _rms(x, g):
    return x * lax.rsqrt(jnp.mean(x * x, axis=-1, keepdims=True) + EPS) * g


def _mod_row(tile_rows):
    def f(i):
        return jnp.minimum(i * tile_rows // DEC_SEQ, DEC_BATCH)
    return f


def _adaln_kernel(c_ref, w_ref, b_ref, o_ref):
    s = jax.nn.silu(c_ref[...])
    o_ref[0] = _dot3(s, w_ref[0]) + b_ref[0]


def _adaln(cond, ada_w, ada_b):
    tn = 1536
    return pl.pallas_call(
        _adaln_kernel,
        out_shape=jax.ShapeDtypeStruct((DEPTH, MOD_ROWS, 6 * D_MODEL), F32),
        grid=(DEPTH, 6 * D_MODEL // tn),
        in_specs=[pl.BlockSpec((MOD_ROWS, D_MODEL), lambda l, j: (0, 0)),
                  pl.BlockSpec((1, D_MODEL, tn), lambda l, j: (l, 0, j)),
                  pl.BlockSpec((1, 1, tn), lambda l, j: (l, 0, j))],
        out_specs=pl.BlockSpec((1, MOD_ROWS, tn), lambda l, j: (l, 0, j)),
        compiler_params=_cparams(("arbitrary", "arbitrary")),
    )(cond, ada_w, ada_b.reshape(DEPTH, 1, 6 * D_MODEL))


def _mod_spec(layer, which, tile_rows):
    row = _mod_row(tile_rows)
    return pl.BlockSpec((1, 1, D_MODEL), lambda i, *_: ((layer * MOD_ROWS + row(i)) * 6 + which, 0, 0))


def _nmm_kernel(x_ref, g_ref, sh_ref, sc_ref, w_ref, *o_refs):
    h = _rms(x_ref[...], g_ref[...]) * (1.0 + sc_ref[0]) + sh_ref[0]
    p = _dot(h.astype(BF16), w_ref[...])
    off = 0
    for o in o_refs:
        n = o.shape[1]
        o[...] = p[:, off:off + n]
        off += n


def _norm_mod_matmul(x, g, mod, layer, w_bf, splits):
    tm = 512
    n_out = w_bf.shape[1]
    return pl.pallas_call(
        _nmm_kernel,
        out_shape=[jax.ShapeDtypeStruct((NT, n), F32) for n in splits],
        grid=(NT // tm,),
        in_specs=[pl.BlockSpec((tm, D_MODEL), lambda i: (i, 0)),
                  pl.BlockSpec((1, D_MODEL), lambda i: (0, 0)),
                  _mod_spec(layer, 0, tm), _mod_spec(layer, 1, tm),
                  pl.BlockSpec((D_MODEL, n_out), lambda i: (0, 0))],
        out_specs=[pl.BlockSpec((tm, n), lambda i: (i, 0)) for n in splits],
        compiler_params=_cparams(("arbitrary",)),
    )(x, g.reshape(1, D_MODEL), mod, mod, w_bf)


def _hyena_filters(L, w1, b1, w2, b2, freq, w3, log_decay):
    hp = lax.Precision.HIGHEST
    t = jnp.linspace(0.0, 1.0, L, dtype=F32)[:, None]
    w = (2.0 * math.pi / L) * jnp.arange(L, dtype=F32)[:, None]
    bands = jnp.linspace(1e-4, HY_BANDS - 1, HY_BANDS, dtype=F32)[None, :]
    z = jnp.concatenate([t, jnp.cos(bands * w), -jnp.sin(bands * w)], axis=-1)
    h = jnp.sin(freq[0] * (jnp.dot(z, w1, precision=hp) + b1))
    h = jnp.sin(freq[1] * (jnp.dot(h, w2, precision=hp) + b2))
    return jnp.dot(h, w3, precision=hp) * jnp.exp(-t * jnp.exp(log_decay))


def _dft_table(L):
    n = 2 * L
    f = jnp.arange(L, dtype=jnp.int32)[:, None]
    t = jnp.arange(L, dtype=jnp.int32)[None, :]
    ang = ((f * t) % n).astype(F32) * (2.0 * math.pi / n)
    nyq = jnp.where(t % 2 == 0, 1.0, -1.0).astype(F32)
    fs = jnp.where(f == 0, nyq, -jnp.sin(ang))
    return jnp.concatenate([jnp.cos(ang), fs], axis=0)


def _filt_kernel(f_ref, hf_ref, hb_ref, p_ref, q_ref, s_ref, *, L):
    hf = hf_ref[...]
    hb = hb_ref[...]
    ct = hf.shape[1]
    nrm = lax.rsqrt(jnp.sum(hf * hf, axis=0, keepdims=True) + jnp.sum(hb * hb, axis=0, keepdims=True) + EPS)
    fh = _dot3(f_ref[...], jnp.concatenate([hf, hb], axis=1))
    row = lax.broadcasted_iota(jnp.int32, (L, ct), 0)
    w = jnp.where(row == 0, 1.0 / (2 * L), 2.0 / (2 * L)) * nrm
    re = fh[:L, :ct] + fh[:L, ct:]
    p_ref[...] = re * w
    q_ref[...] = jnp.where(row == 0, 0.0, fh[L:, :ct] - fh[L:, ct:]) * w
    s_ref[...] = jnp.where(row == 0, fh[L:, :ct] + fh[L:, ct:], re) * w


def _filter_spectrum(L, table, h):
    ct = 128
    nc = HY_CH // ct
    shp = jax.ShapeDtypeStruct((L, HY_CH), F32)
    return pl.pallas_call(
        functools.partial(_filt_kernel, L=L),
        out_shape=[shp, shp, shp],
        grid=(nc,),
        in_specs=[pl.BlockSpec((2 * L, L), lambda c: (0, 0)),
                  pl.BlockSpec((L, ct), lambda c: (0, c)),
                  pl.BlockSpec((L, ct), lambda c: (0, nc + c))],
        out_specs=[pl.BlockSpec((L, ct), lambda c: (0, c))] * 3,
        compiler_params=_cparams(("arbitrary",)),
    )(table, h, h)


def _hyena_kernel(x0_ref, x1_ref, v_ref, w0_ref, w1_ref, wv_ref, b0_ref, b1_ref, bv_ref, bias_ref,
                  f_ref, g_ref, p_ref, q_ref, s_ref, o_ref, *, L):
    ct = x0_ref.shape[1]
    row = lax.broadcasted_iota(jnp.int32, (L, ct), 0)

    def sconv(x_ref, w_ref, b_ref):
        x = x_ref[...]
        xm = jnp.where(row == 0, 0.0, pltpu.roll(x, 1, 0))
        xp = jnp.where(row == L - 1, 0.0, pltpu.roll(x, L - 1, 0))
        return xm * w_ref[0:1, :] + x * w_ref[1:2, :] + xp * w_ref[2:3, :] + b_ref[...]

    z = sconv(v_ref, wv_ref, bv_ref) * sconv(x1_ref, w1_ref, b1_ref)
    zf = _dot(f_ref[...], z.astype(BF16))
    a = zf[:L]
    b = zf[L:]
    p = p_ref[...]
    q = q_ref[...]
    yre = a * p - b * q
    yim = a * q + b * s_ref[...]
    ycat = jnp.concatenate([yre, yim], axis=0).astype(BF16)
    y = _dot(g_ref[...], ycat) + z * bias_ref[...]
    o_ref[...] = sconv(x0_ref, w0_ref, b0_ref) * y


def _hyena(p_hy, short_w, short_b, bias, table_bf, table_t_bf, spec, L, nb, row_blk0):
    ct = 256
    nc = HY_CH // ct
    P, Q, S = spec

    def xs(k):
        return pl.BlockSpec((L, ct), lambda b, c: (row_blk0 + b, k * nc + c))

    def ws(rows, k):
        return pl.BlockSpec((rows, ct), lambda b, c: (0, k * nc + c))

    cs = pl.BlockSpec((L, ct), lambda b, c: (0, c))
    return pl.pallas_call(
        functools.partial(_hyena_kernel, L=L),
        out_shape=jax.ShapeDtypeStruct((nb * L, HY_CH), F32),
        grid=(nb, nc),
        in_specs=[xs(0), xs(1), xs(2), ws(3, 0), ws(3, 1), ws(3, 2), ws(1, 0), ws(1, 1), ws(1, 2), ws(1, 0),
                  pl.BlockSpec((2 * L, L), lambda b, c: (0, 0)),
                  pl.BlockSpec((L, 2 * L), lambda b, c: (0, 0)),
                  cs, cs, cs],
        out_specs=pl.BlockSpec((L, ct), lambda b, c: (b, c)),
        compiler_params=_cparams(("arbitrary", "arbitrary")),
    )(p_hy, p_hy, p_hy, short_w, short_w, short_w, short_b, short_b, short_b, bias,
      table_bf, table_t_bf, P, Q, S)


def _rope_tables(L):
    half = HEAD_DIM // 2
    pos = jnp.arange(L)
    r = (pos // GRID_W).astype(F32)
    col = (pos % GRID_W).astype(F32)
    inv = ROPE_BASE ** (-jnp.arange(0, half, 2, dtype=F32) / half)
    ar = r[:, None] * inv[None, :]
    ac = col[:, None] * inv[None, :]
    z = jnp.zeros_like(ar)
    cos = jnp.concatenate([jnp.cos(ar), jnp.cos(ar), jnp.cos(ac), jnp.cos(ac)], axis=1)
    s1 = jnp.concatenate([-jnp.sin(ar), z, -jnp.sin(ac), z], axis=1)
    s2 = jnp.concatenate([z, jnp.sin(ar), z, jnp.sin(ac)], axis=1)
    rep = LANES // HEAD_DIM
    return jnp.tile(cos, (1, rep)), jnp.tile(s1, (1, rep)), jnp.tile(s2, (1, rep))


def _attn_kernel(*refs, L, Tq, H, KVH, Lc, window, use_sink, use_norm, use_rope, emit_k):
    G = H // KVH
    hd = HEAD_DIM
    pad = WINDOW if window else 0
    it = iter(refs)
    q_ref = next(it)
    kv_ref = next(it)
    ck_ref = cv_ref = sink_ref = qn_ref = kn_ref = bd_ref = rc_ref = rs1_ref = rs2_ref = ko_ref = None
    if Lc:
        ck_ref = next(it)
        cv_ref = next(it)
    if use_sink:
        sink_ref = next(it)
    if use_norm:
        qn_ref = next(it)
        kn_ref = next(it)
        bd_ref = next(it)
    if use_rope:
        rc_ref = next(it)
        rs1_ref = next(it)
        rs2_ref = next(it)
    o_ref = next(it)
    if emit_k:
        ko_ref = next(it)
    k_scr = next(it)
    v_scr = next(it)
    q_scr = next(it)
    i = pl.program_id(1)
    lat0 = Lc + pad

    def head_norm(x, g_ref):
        hi, lo = _split(x * x)
        ms = _dot(hi, bd_ref[...]) + _dot(lo, bd_ref[...])
        return x * lax.rsqrt(ms + EPS) * g_ref[...]

    def rope(x, rows):
        return (x * rc_ref[rows, :] + pltpu.roll(x, LANES - 16, 1) * rs1_ref[rows, :]
                + pltpu.roll(x, 16, 1) * rs2_ref[rows, :])

    @pl.when(i == 0)
    def _():
        if window:
            zpad = jnp.zeros((KVH, pad, hd), BF16)
            k_scr[:, Lc:Lc + pad, :] = zpad
            k_scr[:, lat0 + L:lat0 + L + pad, :] = zpad
            v_scr[:, Lc:Lc + pad, :] = zpad
            v_scr[:, lat0 + L:lat0 + L + pad, :] = zpad
        for g in range(KVH * hd // LANES):
            kg = kv_ref[:, LANES * g:LANES * (g + 1)]
            if use_norm:
                kg = head_norm(kg, kn_ref)
                if emit_k:
                    ko_ref[:, LANES * g:LANES * (g + 1)] = kg
            if use_rope:
                kg = rope(kg, slice(None))
            kb = kg.astype(BF16)
            vb = kv_ref[:, KVH * hd + LANES * g:KVH * hd + LANES * (g + 1)].astype(BF16)
            for hh in range(LANES // hd):
                k_scr[2 * g + hh, lat0:lat0 + L, :] = kb[:, hd * hh:hd * (hh + 1)]
                v_scr[2 * g + hh, lat0:lat0 + L, :] = vb[:, hd * hh:hd * (hh + 1)]
        if Lc:
            ckb = ck_ref[0].astype(BF16)
            cvb = cv_ref[0].astype(BF16)
            for kvh in range(KVH):
                k_scr[kvh, 0:Lc, :] = ckb[:, hd * kvh:hd * (kvh + 1)]
                v_scr[kvh, 0:Lc, :] = cvb[:, hd * kvh:hd * (kvh + 1)]

    rows = pl.ds(pl.multiple_of(i * Tq, Tq), Tq)
    for g in range(H * hd // LANES):
        qg = q_ref[:, LANES * g:LANES * (g + 1)]
        if use_norm:
            qg = head_norm(qg, qn_ref)
        if use_rope:
            qg = rope(qg, rows)
        qb = (qg * (hd ** -0.5)).astype(BF16)
        for hh in range(LANES // hd):
            q_scr[2 * g + hh] = qb[:, hd * hh:hd * (hh + 1)]

    hrow = lax.broadcasted_iota(jnp.int32, (G * Tq, 1), 0) // Tq
    for kvh in range(KVH):
        qs = q_scr[kvh * G:(kvh + 1) * G].reshape(G * Tq, hd)
        if use_sink:
            sk = jnp.zeros((G * Tq, 1), F32)
            for g in range(G):
                sk = jnp.where(hrow == g, sink_ref[0, kvh * G + g], sk)
        if window:
            W = Tq + 2 * pad
            kc = k_scr[kvh, 0:Lc, :]
            vc = v_scr[kvh, 0:Lc, :]
            wrows = pl.ds(pl.multiple_of(Lc + i * Tq, Tq), W)
            kw = k_scr[kvh, wrows, :]
            vw = v_scr[kvh, wrows, :]
            s_c = _dot(qs, kc, _NT)
            s_w = _dot(qs, kw, _NT).reshape(G, Tq, W)
            r = lax.broadcasted_iota(jnp.int32, (Tq, W), 0)
            c = lax.broadcasted_iota(jnp.int32, (Tq, W), 1)
            kpos = i * Tq - pad + c
            valid = (kpos >= 0) & (kpos < L) & (jnp.abs(r + pad - c) <= WINDOW)
            s_w = jnp.where(valid[None], s_w, NEG_INF).reshape(G * Tq, W)
            m = jnp.maximum(jnp.max(s_c, axis=-1, keepdims=True), jnp.max(s_w, axis=-1, keepdims=True))
            if use_sink:
                m = jnp.maximum(m, sk)
            e_c = jnp.exp(s_c - m)
            e_w = jnp.exp(s_w - m)
            den = jnp.sum(e_c, axis=-1, keepdims=True) + jnp.sum(e_w, axis=-1, keepdims=True)
            o = _dot(e_c.astype(BF16), vc) + _dot(e_w.astype(BF16), vw)
        else:
            s = _dot(qs, k_scr[kvh], _NT)
            m = jnp.max(s, axis=-1, keepdims=True)
            if use_sink:
                m = jnp.maximum(m, sk)
            e = jnp.exp(s - m)
            den = jnp.sum(e, axis=-1, keepdims=True)
            o = _dot(e.astype(BF16), v_scr[kvh])
        if use_sink:
            den = den + jnp.exp(sk - m)
        o = o / den
        for g in range(G):
            h = kvh * G + g
            o_ref[:, hd * h:hd * (h + 1)] = o[g * Tq:(g + 1) * Tq]


def _attention(p, q_cols, kv_colblk, *, L, Tq, nb, row0, H, KVH, ctx=None, sink=None, norm=None, rope=None,
               window=False, emit_k=False):
    hd = HEAD_DIM
    nq = L // Tq
    Lc = 0 if ctx is None else ctx[0].shape[1]
    pad = WINDOW if window else 0
    Lk = Lc + L + 2 * pad
    kvw = 2 * KVH * hd
    args = [p, p]
    in_specs = [pl.BlockSpec((Tq, q_cols), lambda b, i: (row0 // Tq + b * nq + i, 0)),
                pl.BlockSpec((L, kvw), lambda b, i: (row0 // L + b, kv_colblk))]
    if ctx is not None:
        for a in ctx:
            args.append(a)
            in_specs.append(pl.BlockSpec((1, Lc, KVH * hd), lambda b, i: (b, 0, 0)))
    if sink is not None:
        args.append(sink.reshape(1, H))
        in_specs.append(pl.BlockSpec(memory_space=pltpu.SMEM))
    if norm is not None:
        bd = (jnp.arange(LANES)[:, None] // hd == jnp.arange(LANES)[None, :] // hd).astype(F32) / hd
        for a in (jnp.tile(norm[0], LANES // hd).reshape(1, LANES), jnp.tile(norm[1], LANES // hd).reshape(1, LANES)):
            args.append(a)
            in_specs.append(pl.BlockSpec((1, LANES), lambda b, i: (0, 0)))
        args.append(bd.astype(BF16))
        in_specs.append(pl.BlockSpec((LANES, LANES), lambda b, i: (0, 0)))
    if rope is not None:
        for a in rope:
            args.append(a)
            in_specs.append(pl.BlockSpec((L, LANES), lambda b, i: (0, 0)))
    out_shape = [jax.ShapeDtypeStruct((nb * L, H * hd), F32)]
    out_specs = [pl.BlockSpec((Tq, H * hd), lambda b, i: (b * nq + i, 0))]
    if emit_k:
        out_shape.append(jax.ShapeDtypeStruct((nb * L, KVH * hd), F32))
        out_specs.append(pl.BlockSpec((L, KVH * hd), lambda b, i: (b, 0)))
    kern = functools.partial(_attn_kernel, L=L, Tq=Tq, H=H, KVH=KVH, Lc=Lc, window=window,
                             use_sink=sink is not None, use_norm=norm is not None,
                             use_rope=rope is not None, emit_k=emit_k)
    call = pl.pallas_call(
        kern, out_shape=out_shape, grid=(nb, nq), in_specs=in_specs, out_specs=out_specs,
        scratch_shapes=[pltpu.VMEM((KVH, Lk, hd), BF16), pltpu.VMEM((KVH, Lk, hd), BF16),
                        pltpu.VMEM((H, Tq, hd), BF16)],
        compiler_params=_cparams(("arbitrary", "arbitrary")))
    return call, args


def _oproj_kernel(al_ref, ar_ref, w_ref, x_ref, g1_ref, gate_ref, g2_ref, sh_ref, sc_ref, wr_ref,
                  x1_ref, h2_ref, lg_ref):
    k = al_ref.shape[1]
    y = _dot(al_ref[...].astype(BF16), w_ref[0:k, :]) + _dot(ar_ref[...].astype(BF16), w_ref[k:2 * k, :])
    x1 = x_ref[...] + gate_ref[0] * _rms(y, g1_ref[...])
    x1_ref[...] = x1
    h2 = _rms(x1, g2_ref[...]) * (1.0 + sc_ref[0]) + sh_ref[0]
    rows = h2.shape[0]
    for s in range(TOK_SUB):
        h2_ref[pl.ds(s, rows, stride=TOK_SUB), :] = h2[:, LANES * s:LANES * (s + 1)]
    lg_ref[...] = _dot3(wr_ref[...], h2, _NT)


def _out_proj(a_l, l_blk, a_r, r_blk, w_bf, x, g1, g2, mod, layer, wr_t):
    tm = 512
    half = D_MODEL // 2
    return pl.pallas_call(
        _oproj_kernel,
        out_shape=[jax.ShapeDtypeStruct((NT, D_MODEL), F32), jax.ShapeDtypeStruct((NT * TOK_SUB, LANES), F32),
                   jax.ShapeDtypeStruct((N_EXPERTS, NT), F32)],
        grid=(NT // tm,),
        in_specs=[pl.BlockSpec((tm, half), lambda i: (i, l_blk)),
                  pl.BlockSpec((tm, half), lambda i: (i, r_blk)),
                  pl.BlockSpec((D_MODEL, D_MODEL), lambda i: (0, 0)),
                  pl.BlockSpec((tm, D_MODEL), lambda i: (i, 0)),
                  pl.BlockSpec((1, D_MODEL), lambda i: (0, 0)),
                  _mod_spec(layer, 2, tm),
                  pl.BlockSpec((1, D_MODEL), lambda i: (0, 0)),
                  _mod_spec(layer, 3, tm), _mod_spec(layer, 4, tm),
                  pl.BlockSpec((N_EXPERTS, D_MODEL), lambda i: (0, 0))],
        out_specs=[pl.BlockSpec((tm, D_MODEL), lambda i: (i, 0)),
                   pl.BlockSpec((tm * TOK_SUB, LANES), lambda i: (i, 0)),
                   pl.BlockSpec((N_EXPERTS, tm), lambda i: (0, i))],
        compiler_params=_cparams(("arbitrary",)),
    )(a_l, a_r, w_bf, x, g1.reshape(1, D_MODEL), mod, g2.reshape(1, D_MODEL), mod, mod, wr_t)


def _ffn_kernel(idx_ref, h_hbm, gate_ref, wg_ref, wu_ref, wd_ref, y_ref, xf_scr, xb_scr, sem, *, cap, nf):
    e = pl.program_id(0)
    f = pl.program_id(1)
    per_step = cap // nf

    def issue_rows(expert, first, count):
        base = expert * cap + first

        def body(j, carry):
            src = pl.multiple_of(idx_ref[base + j], TOK_SUB)
            dst = pl.multiple_of((first + j) * TOK_SUB, TOK_SUB)
            pltpu.make_async_copy(h_hbm.at[pl.ds(src, TOK_SUB)], xf_scr.at[pl.ds(dst, TOK_SUB)], sem).start()
            return carry
        lax.fori_loop(0, count, body, 0, unroll=8)

    @pl.when((e == 0) & (f == 0))
    def _():
        issue_rows(0, 0, cap)

    @pl.when(f == 0)
    def _():
        pltpu.make_async_copy(h_hbm.at[pl.ds(0, cap * TOK_SUB)], xf_scr, sem).wait()
        for s in range(TOK_SUB):
            xb_scr[:, LANES * s:LANES * (s + 1)] = xf_scr[pl.ds(s, cap, stride=TOK_SUB), :].astype(BF16)

    @pl.when(e + 1 < pl.num_programs(0))
    def _():
        issue_rows(e + 1, f * per_step, per_step)

    wg = wg_ref[0, 0].astype(BF16)
    wu = wu_ref[0, 0].astype(BF16)
    wd = wd_ref[0, 0].astype(BF16)
    half = cap // 2
    for r in range(2):
        rows = slice(r * half, (r + 1) * half)
        xe = xb_scr[rows, :]
        hmid = (jax.nn.silu(_dot(xe, wg)) * _dot(xe, wu)).astype(BF16)
        part = _dot(hmid, wd)

        @pl.when(f == 0)
        def _():
            y_ref[0, rows, :] = part

        @pl.when(f != 0)
        def _():
            y_ref[0, rows, :] += part

    @pl.when(f == nf - 1)
    def _():
        y_ref[0] = y_ref[0] * gate_ref[0]


def _expert_ffn(idx, gate, h2, layer, w_gate, w_up, w_down):
    cap = idx.shape[1]
    fc = 512
    nf = EXPERT_FF // fc
    grid_spec = pltpu.PrefetchScalarGridSpec(
        num_scalar_prefetch=1,
        grid=(N_EXPERTS, nf),
        in_specs=[pl.BlockSpec(memory_space=pl.ANY),
                  pl.BlockSpec((1, cap, 1), lambda e, f, idx: (e, 0, 0)),
                  pl.BlockSpec((1, 1, D_MODEL, fc), lambda e, f, idx: (layer, e, 0, f)),
                  pl.BlockSpec((1, 1, D_MODEL, fc), lambda e, f, idx: (layer, e, 0, f)),
                  pl.BlockSpec((1, 1, fc, D_MODEL), lambda e, f, idx: (layer, e, f, 0))],
        out_specs=pl.BlockSpec((1, cap, D_MODEL), lambda e, f, idx: (e, 0, 0)),
        scratch_shapes=[pltpu.VMEM((cap * TOK_SUB, LANES), F32), pltpu.VMEM((cap, D_MODEL), BF16),
                        pltpu.SemaphoreType.DMA(())])
    return pl.pallas_call(
        functools.partial(_ffn_kernel, cap=cap, nf=nf),
        out_shape=jax.ShapeDtypeStruct((N_EXPERTS, cap, D_MODEL), F32),
        grid_spec=grid_spec,
        compiler_params=_cparams(("arbitrary", "arbitrary")),
    )((idx * TOK_SUB).reshape(-1), h2, gate.reshape(N_EXPERTS, cap, 1), w_gate, w_up, w_down)


def _resid_kernel(x_ref, f_ref, g_ref, gate_ref, o_ref):
    o_ref[...] = x_ref[...] + gate_ref[0] * _rms(f_ref[...], g_ref[...])


def _ffn_residual(x, f, g, mod, layer):
    tm = 512
    return pl.pallas_call(
        _resid_kernel,
        out_shape=jax.ShapeDtypeStruct((NT, D_MODEL), F32),
        grid=(NT // tm,),
        in_specs=[pl.BlockSpec((tm, D_MODEL), lambda i: (i, 0)),
                  pl.BlockSpec((tm, D_MODEL), lambda i: (i, 0)),
                  pl.BlockSpec((1, D_MODEL), lambda i: (0, 0)),
                  _mod_spec(layer, 5, tm)],
        out_specs=pl.BlockSpec((tm, D_MODEL), lambda i: (i, 0)),
        compiler_params=_cparams(("arbitrary",)),
    )(x, f, g.reshape(1, D_MODEL), mod)


def _route(logits_t, h2, layer, w_gate, w_up, w_down):
    aff = jax.nn.softmax(logits_t, axis=0)
    gates, idxs = [], []
    for lo, n in ((0, NS), (NS, NP)):
        gate, idx = lax.top_k(aff[:, lo:lo + n], EC_CAPACITY * n // N_EXPERTS)
        gates.append(gate)
        idxs.append(idx + lo)
    idx = jnp.concatenate(idxs, axis=1)
    y = _expert_ffn(idx, jnp.concatenate(gates, axis=1), h2, layer, w_gate, w_up, w_down)
    return jnp.zeros((NT, D_MODEL), F32).at[idx.reshape(-1)].add(y.reshape(-1, D_MODEL))


def kernel(x_prompt, x_sample, cache_b_k, cache_b_v, cache_c_k, cache_c_v, c, c_ctx, ada_w, ada_b, norm_g,
           ab_w_in, hy_short_w, hy_short_b, hy_w1, hy_b1, hy_w2, hy_b2, hy_freq, hy_w3, hy_log_decay, hy_bias,
           b_sink, ab_w_out, c_w_qkv, c_q_norm, c_k_norm, c_w_out, ec_router, ec_w_gate, ec_w_up, ec_w_down):
    x = jnp.concatenate([x_sample.reshape(NS, D_MODEL), x_prompt.reshape(NP, D_MODEL)], axis=0)
    cond = jnp.concatenate([c, c_ctx[None], jnp.zeros((MOD_ROWS - DEC_BATCH - 1, D_MODEL), F32)], axis=0)
    mod = _adaln(cond, ada_w, ada_b).reshape(DEPTH * MOD_ROWS * 6, 1, D_MODEL)
    rope = _rope_tables(DEC_SEQ)
    hd = HEAD_DIM

    e = 0
    p_hy, p_at = _norm_mod_matmul(x, norm_g[0, 0], mod, 0, ab_w_in[e].astype(BF16), (3 * HY_CH, D_MODEL - HY_CH + 2 * B_KV * hd))
    ya = []
    for L, nb, row_blk0 in ((DEC_SEQ, DEC_BATCH, 0), (SEQ, BATCH, NS // SEQ)):
        table = _dft_table(L)
        h = _hyena_filters(L, hy_w1[e], hy_b1[e], hy_w2[e], hy_b2[e], hy_freq[e], hy_w3[e], hy_log_decay[e])
        h = h.at[0, HY_CH:].set(0.0)
        spec = _filter_spectrum(L, table, h)
        ya.append(_hyena(p_hy, hy_short_w[e], hy_short_b[e].reshape(1, -1), hy_bias[e].reshape(1, -1),
                         table.astype(BF16), table.T.astype(BF16), spec, L, nb, row_blk0))
    ya = jnp.concatenate(ya, axis=0)
    ctx_b = (cache_b_k[:, e].reshape(DEC_BATCH, PAST_LEN, B_KV * hd), cache_b_v[:, e].reshape(DEC_BATCH, PAST_LEN, B_KV * hd))
    call, args = _attention(p_at, B_HEADS * hd, 2, L=DEC_SEQ, Tq=256, nb=DEC_BATCH, row0=0, H=B_HEADS, KVH=B_KV,
                            ctx=ctx_b, sink=b_sink[e], rope=rope, window=True)
    yb_s = call(*args)[0]
    call, args = _attention(p_at, B_HEADS * hd, 2, L=SEQ, Tq=256, nb=BATCH, row0=NS, H=B_HEADS, KVH=B_KV,
                            sink=b_sink[e])
    yb_p = call(*args)[0]
    yb = jnp.concatenate([yb_s, yb_p], axis=0)
    state_b_k = p_at[NS:, B_HEADS * hd:(B_HEADS + B_KV) * hd].reshape(BATCH, 1, SEQ, B_KV, hd)
    state_b_v = p_at[NS:, (B_HEADS + B_KV) * hd:].reshape(BATCH, 1, SEQ, B_KV, hd)
    x, h2, lg = _out_proj(ya, 0, yb, 0, ab_w_out[e].astype(BF16), x, norm_g[0, 1], norm_g[0, 2], mod, 0, ec_router[0].T)
    f = _route(lg, h2, 0, ec_w_gate, ec_w_up, ec_w_down)
    x = _ffn_residual(x, f, norm_g[0, 3], mod, 0)

    o = 0
    (p,) = _norm_mod_matmul(x, norm_g[1, 0], mod, 1, c_w_qkv[o].astype(BF16), ((C_HEADS + 2 * C_KV) * hd,))
    ctx_c = (cache_c_k[:, o].reshape(DEC_BATCH, PAST_LEN, C_KV * hd), cache_c_v[:, o].reshape(DEC_BATCH, PAST_LEN, C_KV * hd))
    nrm = (c_q_norm[o], c_k_norm[o])
    call, args = _attention(p, C_HEADS * hd, 2, L=DEC_SEQ, Tq=128, nb=DEC_BATCH, row0=0, H=C_HEADS, KVH=C_KV,
                            ctx=ctx_c, norm=nrm, rope=rope)
    yc_s = call(*args)[0]
    call, args = _attention(p, C_HEADS * hd, 2, L=SEQ, Tq=256, nb=BATCH, row0=NS, H=C_HEADS, KVH=C_KV,
                            norm=nrm, emit_k=True)
    yc_p, kn = call(*args)
    yc = jnp.concatenate([yc_s, yc_p], axis=0)
    state_c_k = kn.reshape(BATCH, 1, SEQ, C_KV, hd)
    state_c_v = p[NS:, (C_HEADS + C_KV) * hd:].reshape(BATCH, 1, SEQ, C_KV, hd)
    x, h2, lg = _out_proj(yc, 0, yc, 1, c_w_out[o].astype(BF16), x, norm_g[1, 1], norm_g[1, 2], mod, 1, ec_router[1].T)
    f = _route(lg, h2, 1, ec_w_gate, ec_w_up, ec_w_down)
    x = _ffn_residual(x, f, norm_g[1, 3], mod, 1)

    y_sample = x[:NS].reshape(DEC_BATCH, DEC_SEQ, D_MODEL)
    y_prompt = x[NS:].reshape(BATCH, SEQ, D_MODEL)
    return (y_prompt, y_sample, state_b_k, state_b_v, state_c_k, state_c_v)
```

```python
import functools
import math

import jax
import jax.numpy as jnp
from jax import lax
from jax.experimental import pallas as pl
from jax.experimental.pallas import tpu as pltpu

F32 = jnp.float32
BF16 = jnp.bfloat16

D_MODEL = 1024
BATCH = 16
SEQ = 256
DEPTH = 2
DEC_BATCH = 8
DEC_SEQ = 1024
PAST_LEN = 512
GRID_W = 64
HEAD_DIM = 64
HY_CH = D_MODEL // 2
HY_BANDS = 8
B_HEADS = 8
B_KV = 2
WINDOW = 128
C_HEADS = 16
C_KV = 4
ROPE_BASE = 10000.0
N_EXPERTS = 16
EC_CAPACITY = 2
EXPERT_FF = 2 * D_MODEL
EPS = 1e-6
NEG_INF = -1e30

NS = DEC_BATCH * DEC_SEQ
NP = BATCH * SEQ
NT = NS + NP
MOD_ROWS = 16
LANES = 128
TOK_SUB = D_MODEL // LANES
VMEM_LIMIT = 56 * 1024 * 1024


def _cparams(sem):
    return pltpu.CompilerParams(dimension_semantics=sem, vmem_limit_bytes=VMEM_LIMIT)


def _split(x):
    hi = x.astype(BF16)
    lo = (x - hi.astype(F32)).astype(BF16)
    return hi, lo


_NN = (((1,), (0,)), ((), ()))
_NT = (((1,), (1,)), ((), ()))


def _dot(a, b, dims=_NN):
    return lax.dot_general(a, b, dims, preferred_element_type=F32)


def _dot3(a, b, dims=_NN):
    ah, al = _split(a)
    bh, bl = _split(b)
    return _dot(ah, bh, dims) + _dot(al, bh, dims) + _dot(ah, bl, dims)


def _rms(x, g):
    return x * lax.rsqrt(jnp.mean(x * x, axis=-1, keepdims=True) + EPS) * g


def _mod_row(tile_rows):
    def f(i):
        return jnp.minimum(i * tile_rows // DEC_SEQ, DEC_BATCH)
    return f


def _adaln_kernel(c_ref, w_ref, b_ref, o_ref):
    s = jax.nn.silu(c_ref[...])
    o_ref[0] = _dot3(s, w_ref[0]) + b_ref[0]


def _adaln(cond, ada_w, ada_b):
    tn = 1536
    return pl.pallas_call(
        _adaln_kernel,
        out_shape=jax.ShapeDtypeStruct((DEPTH, MOD_ROWS, 6 * D_MODEL), F32),
        grid=(DEPTH, 6 * D_MODEL // tn),
        in_specs=[pl.BlockSpec((MOD_ROWS, D_MODEL), lambda l, j: (0, 0)),
                  pl.BlockSpec((1, D_MODEL, tn), lambda l, j: (l, 0, j)),
                  pl.BlockSpec((1, 1, tn), lambda l, j: (l, 0, j))],
        out_specs=pl.BlockSpec((1, MOD_ROWS, tn), lambda l, j: (l, 0, j)),
        compiler_params=_cparams(("arbitrary", "arbitrary")),
    )(cond, ada_w, ada_b.reshape(DEPTH, 1, 6 * D_MODEL))


def _mod_spec(layer, which, tile_rows):
    row = _mod_row(tile_rows)
    return pl.BlockSpec((1, 1, D_MODEL), lambda i, *_: ((layer * MOD_ROWS + row(i)) * 6 + which, 0, 0))


def _nmm_kernel(x_ref, g_ref, sh_ref, sc_ref, w_ref, *o_refs):
    h = _rms(x_ref[...], g_ref[...]) * (1.0 + sc_ref[0]) + sh_ref[0]
    p = _dot(h.astype(BF16), w_ref[...])
    off = 0
    for o in o_refs:
        n = o.shape[1]
        o[...] = p[:, off:off + n]
        off += n


def _norm_mod_matmul(x, g, mod, layer, w_bf, splits):
    tm = 512
    n_out = w_bf.shape[1]
    return pl.pallas_call(
        _nmm_kernel,
        out_shape=[jax.ShapeDtypeStruct((NT, n), F32) for n in splits],
        grid=(NT // tm,),
        in_specs=[pl.BlockSpec((tm, D_MODEL), lambda i: (i, 0)),
                  pl.BlockSpec((1, D_MODEL), lambda i: (0, 0)),
                  _mod_spec(layer, 0, tm), _mod_spec(layer, 1, tm),
                  pl.BlockSpec((D_MODEL, n_out), lambda i: (0, 0))],
        out_specs=[pl.BlockSpec((tm, n), lambda i: (i, 0)) for n in splits],
        compiler_params=_cparams(("arbitrary",)),
    )(x, g.reshape(1, D_MODEL), mod, mod, w_bf)


def _hyena_filters(L, w1, b1, w2, b2, freq, w3, log_decay):
    hp = lax.Precision.HIGHEST
    t = jnp.linspace(0.0, 1.0, L, dtype=F32)[:, None]
    w = (2.0 * math.pi / L) * jnp.arange(L, dtype=F32)[:, None]
    bands = jnp.linspace(1e-4, HY_BANDS - 1, HY_BANDS, dtype=F32)[None, :]
    z = jnp.concatenate([t, jnp.cos(bands * w), -jnp.sin(bands * w)], axis=-1)
    h = jnp.sin(freq[0] * (jnp.dot(z, w1, precision=hp) + b1))
    h = jnp.sin(freq[1] * (jnp.dot(h, w2, precision=hp) + b2))
    return jnp.dot(h, w3, precision=hp) * jnp.exp(-t * jnp.exp(log_decay))


def _dft_table(L):
    n = 2 * L
    f = jnp.arange(L, dtype=jnp.int32)[:, None]
    t = jnp.arange(L, dtype=jnp.int32)[None, :]
    ang = ((f * t) % n).astype(F32) * (2.0 * math.pi / n)
    nyq = jnp.where(t % 2 == 0, 1.0, -1.0).astype(F32)
    fs = jnp.where(f == 0, nyq, -jnp.sin(ang))
    return jnp.concatenate([jnp.cos(ang), fs], axis=0)


def _filt_kernel(f_ref, hf_ref, hb_ref, p_ref, q_ref, s_ref, *, L):
    hf = hf_ref[...]
    hb = hb_ref[...]
    ct = hf.shape[1]
    nrm = lax.rsqrt(jnp.sum(hf * hf, axis=0, keepdims=True) + jnp.sum(hb * hb, axis=0, keepdims=True) + EPS)
    fh = _dot3(f_ref[...], jnp.concatenate([hf, hb], axis=1))
    row = lax.broadcasted_iota(jnp.int32, (L, ct), 0)
    w = jnp.where(row == 0, 1.0 / (2 * L), 2.0 / (2 * L)) * nrm
    re = fh[:L, :ct] + fh[:L, ct:]
    p_ref[...] = re * w
    q_ref[...] = jnp.where(row == 0, 0.0, fh[L:, :ct] - fh[L:, ct:]) * w
    s_ref[...] = jnp.where(row == 0, fh[L:, :ct] + fh[L:, ct:], re) * w


def _filter_spectrum(L, table, h):
    ct = 128
    nc = HY_CH // ct
    shp = jax.ShapeDtypeStruct((L, HY_CH), F32)
    return pl.pallas_call(
        functools.partial(_filt_kernel, L=L),
        out_shape=[shp, shp, shp],
        grid=(nc,),
        in_specs=[pl.BlockSpec((2 * L, L), lambda c: (0, 0)),
                  pl.BlockSpec((L, ct), lambda c: (0, c)),
                  pl.BlockSpec((L, ct), lambda c: (0, nc + c))],
        out_specs=[pl.BlockSpec((L, ct), lambda c: (0, c))] * 3,
        compiler_params=_cparams(("arbitrary",)),
    )(table, h, h)


def _hyena_kernel(x0_ref, x1_ref, v_ref, w0_ref, w1_ref, wv_ref, b0_ref, b1_ref, bv_ref, bias_ref,
                  f_ref, g_ref, p_ref, q_ref, s_ref, o_ref, *, L):
    ct = x0_ref.shape[1]
    row = lax.broadcasted_iota(jnp.int32, (L, ct), 0)

    def sconv(x_ref, w_ref, b_ref):
        x = x_ref[...]
        xm = jnp.where(row == 0, 0.0, pltpu.roll(x, 1, 0))
        xp = jnp.where(row == L - 1, 0.0, pltpu.roll(x, L - 1, 0))
        return xm * w_ref[0:1, :] + x * w_ref[1:2, :] + xp * w_ref[2:3, :] + b_ref[...]

    z = sconv(v_ref, wv_ref, bv_ref) * sconv(x1_ref, w1_ref, b1_ref)
    zf = _dot(f_ref[...], z.astype(BF16))
    a = zf[:L]
    b = zf[L:]
    p = p_ref[...]
    q = q_ref[...]
    yre = a * p - b * q
    yim = a * q + b * s_ref[...]
    ycat = jnp.concatenate([yre, yim], axis=0).astype(BF16)
    y = _dot(g_ref[...], ycat) + z * bias_ref[...]
    o_ref[...] = sconv(x0_ref, w0_ref, b0_ref) * y


def _hyena(p_hy, short_w, short_b, bias, table_bf, table_t_bf, spec, L, nb, row_blk0):
    ct = 256
    nc = HY_CH // ct
    P, Q, S = spec

    def xs(k):
        return pl.BlockSpec((L, ct), lambda b, c: (row_blk0 + b, k * nc + c))

    def ws(rows, k):
        return pl.BlockSpec((rows, ct), lambda b, c: (0, k * nc + c))

    cs = pl.BlockSpec((L, ct), lambda b, c: (0, c))
    return pl.pallas_call(
        functools.partial(_hyena_kernel, L=L),
        out_shape=jax.ShapeDtypeStruct((nb * L, HY_CH), F32),
        grid=(nb, nc),
        in_specs=[xs(0), xs(1), xs(2), ws(3, 0), ws(3, 1), ws(3, 2), ws(1, 0), ws(1, 1), ws(1, 2), ws(1, 0),
                  pl.BlockSpec((2 * L, L), lambda b, c: (0, 0)),
                  pl.BlockSpec((L, 2 * L), lambda b, c: (0, 0)),
                  cs, cs, cs],
        out_specs=pl.BlockSpec((L, ct), lambda b, c: (b, c)),
        compiler_params=_cparams(("arbitrary", "arbitrary")),
    )(p_hy, p_hy, p_hy, short_w, short_w, short_w, short_b, short_b, short_b, bias,
      table_bf, table_t_bf, P, Q, S)


def _rope_tables(L):
    half = HEAD_DIM // 2
    pos = jnp.arange(L)
    r = (pos // GRID_W).astype(F32)
    col = (pos % GRID_W).astype(F32)
    inv = ROPE_BASE ** (-jnp.arange(0, half, 2, dtype=F32) / half)
    ar = r[:, None] * inv[None, :]
    ac = col[:, None] * inv[None, :]
    z = jnp.zeros_like(ar)
    cos = jnp.concatenate([jnp.cos(ar), jnp.cos(ar), jnp.cos(ac), jnp.cos(ac)], axis=1)
    s1 = jnp.concatenate([-jnp.sin(ar), z, -jnp.sin(ac), z], axis=1)
    s2 = jnp.concatenate([z, jnp.sin(ar), z, jnp.sin(ac)], axis=1)
    rep = LANES // HEAD_DIM
    return jnp.tile(cos, (1, rep)), jnp.tile(s1, (1, rep)), jnp.tile(s2, (1, rep))


def _attn_kernel(*refs, L, Tq, H, KVH, Lc, window, use_sink, use_norm, use_rope, emit_k):
    G = H // KVH
    hd = HEAD_DIM
    pad = WINDOW if window else 0
    it = iter(refs)
    q_ref = next(it)
    kv_ref = next(it)
    ck_ref = cv_ref = sink_ref = qn_ref = kn_ref = bd_ref = rc_ref = rs1_ref = rs2_ref = ko_ref = None
    if Lc:
        ck_ref = next(it)
        cv_ref = next(it)
    if use_sink:
        sink_ref = next(it)
    if use_norm:
        qn_ref = next(it)
        kn_ref = next(it)
        bd_ref = next(it)
    if use_rope:
        rc_ref = next(it)
        rs1_ref = next(it)
        rs2_ref = next(it)
    o_ref = next(it)
    if emit_k:
        ko_ref = next(it)
    k_scr = next(it)
    v_scr = next(it)
    q_scr = next(it)
    i = pl.program_id(1)
    lat0 = Lc + pad

    def head_norm(x, g_ref):
        hi, lo = _split(x * x)
        ms = _dot(hi, bd_ref[...]) + _dot(lo, bd_ref[...])
        return x * lax.rsqrt(ms + EPS) * g_ref[...]

    def rope(x, rows):
        return (x * rc_ref[rows, :] + pltpu.roll(x, LANES - 16, 1) * rs1_ref[rows, :]
                + pltpu.roll(x, 16, 1) * rs2_ref[rows, :])

    @pl.when(i == 0)
    def _():
        if window:
            zpad = jnp.zeros((KVH, pad, hd), BF16)
            k_scr[:, Lc:Lc + pad, :] = zpad
            k_scr[:, lat0 + L:lat0 + L + pad, :] = zpad
            v_scr[:, Lc:Lc + pad, :] = zpad
            v_scr[:, lat0 + L:lat0 + L + pad, :] = zpad
        for g in range(KVH * hd // LANES):
            kg = kv_ref[:, LANES * g:LANES * (g + 1)]
            if use_norm:
                kg = head_norm(kg, kn_ref)
                if emit_k:
                    ko_ref[:, LANES * g:LANES * (g + 1)] = kg
            if use_rope:
                kg = rope(kg, slice(None))
            kb = kg.astype(BF16)
            vb = kv_ref[:, KVH * hd + LANES * g:KVH * hd + LANES * (g + 1)].astype(BF16)
            for hh in range(LANES // hd):
                k_scr[2 * g + hh, lat0:lat0 + L, :] = kb[:, hd * hh:hd * (hh + 1)]
                v_scr[2 * g + hh, lat0:lat0 + L, :] = vb[:, hd * hh:hd * (hh + 1)]
        if Lc:
            ckb = ck_ref[0].astype(BF16)
            cvb = cv_ref[0].astype(BF16)
            for kvh in range(KVH):
                k_scr[kvh, 0:Lc, :] = ckb[:, hd * kvh:hd * (kvh + 1)]
                v_scr[kvh, 0:Lc, :] = cvb[:, hd * kvh:hd * (kvh + 1)]

    rows = pl.ds(pl.multiple_of(i * Tq, Tq), Tq)
    for g in range(H * hd // LANES):
        qg = q_ref[:, LANES * g:LANES * (g + 1)]
        if use_norm:
            qg = head_norm(qg, qn_ref)
        if use_rope:
            qg = rope(qg, rows)
        qb = (qg * (hd ** -0.5)).astype(BF16)
        for hh in range(LANES // hd):
            q_scr[2 * g + hh] = qb[:, hd * hh:hd * (hh + 1)]

    hrow = lax.broadcasted_iota(jnp.int32, (G * Tq, 1), 0) // Tq
    for kvh in range(KVH):
        qs = q_scr[kvh * G:(kvh + 1) * G].reshape(G * Tq, hd)
        if use_sink:
            sk = jnp.zeros((G * Tq, 1), F32)
            for g in range(G):
                sk = jnp.where(hrow == g, sink_ref[0, kvh * G + g], sk)
        if window:
            W = Tq + 2 * pad
            kc = k_scr[kvh, 0:Lc, :]
            vc = v_scr[kvh, 0:Lc, :]
            wrows = pl.ds(pl.multiple_of(Lc + i * Tq, Tq), W)
            kw = k_scr[kvh, wrows, :]
            vw = v_scr[kvh, wrows, :]
            s_c = _dot(qs, kc, _NT)
            s_w = _dot(qs, kw, _NT).reshape(G, Tq, W)
            r = lax.broadcasted_iota(jnp.int32, (Tq, W), 0)
            c = lax.broadcasted_iota(jnp.int32, (Tq, W), 1)
            kpos = i * Tq - pad + c
            valid = (kpos >= 0) & (kpos < L) & (jnp.abs(r + pad - c) <= WINDOW)
            s_w = jnp.where(valid[None], s_w, NEG_INF).reshape(G * Tq, W)
            m = jnp.maximum(jnp.max(s_c, axis=-1, keepdims=True), jnp.max(s_w, axis=-1, keepdims=True))
            if use_sink:
                m = jnp.maximum(m, sk)
            e_c = jnp.exp(s_c - m)
            e_w = jnp.exp(s_w - m)
            den = jnp.sum(e_c, axis=-1, keepdims=True) + jnp.sum(e_w, axis=-1, keepdims=True)
            o = _dot(e_c.astype(BF16), vc) + _dot(e_w.astype(BF16), vw)
        else:
            s = _dot(qs, k_scr[kvh], _NT)
            m = jnp.max(s, axis=-1, keepdims=True)
            if use_sink:
                m = jnp.maximum(m, sk)
            e = jnp.exp(s - m)
            den = jnp.sum(e, axis=-1, keepdims=True)
            o = _dot(e.astype(BF16), v_scr[kvh])
        if use_sink:
            den = den + jnp.exp(sk - m)
        o = o / den
        for g in range(G):
            h = kvh * G + g
            o_ref[:, hd * h:hd * (h + 1)] = o[g * Tq:(g + 1) * Tq]


def _attention(p, q_cols, kv_colblk, *, L, Tq, nb, row0, H, KVH, ctx=None, sink=None, norm=None, rope=None,
               window=False, emit_k=False):
    hd = HEAD_DIM
    nq = L // Tq
    Lc = 0 if ctx is None else ctx[0].shape[1]
    pad = WINDOW if window else 0
    Lk = Lc + L + 2 * pad
    kvw = 2 * KVH * hd
    args = [p, p]
    in_specs = [pl.BlockSpec((Tq, q_cols), lambda b, i: (row0 // Tq + b * nq + i, 0)),
                pl.BlockSpec((L, kvw), lambda b, i: (row0 // L + b, kv_colblk))]
    if ctx is not None:
        for a in ctx:
            args.append(a)
            in_specs.append(pl.BlockSpec((1, Lc, KVH * hd), lambda b, i: (b, 0, 0)))
    if sink is not None:
        args.append(sink.reshape(1, H))
        in_specs.append(pl.BlockSpec(memory_space=pltpu.SMEM))
    if norm is not None:
        bd = (jnp.arange(LANES)[:, None] // hd == jnp.arange(LANES)[None, :] // hd).astype(F32) / hd
        for a in (jnp.tile(norm[0], LANES // hd).reshape(1, LANES), jnp.tile(norm[1], LANES // hd).reshape(1, LANES)):
            args.append(a)
            in_specs.append(pl.BlockSpec((1, LANES), lambda b, i: (0, 0)))
        args.append(bd.astype(BF16))
        in_specs.append(pl.BlockSpec((LANES, LANES), lambda b, i: (0, 0)))
    if rope is not None:
        for a in rope:
            args.append(a)
            in_specs.append(pl.BlockSpec((L, LANES), lambda b, i: (0, 0)))
    out_shape = [jax.ShapeDtypeStruct((nb * L, H * hd), F32)]
    out_specs = [pl.BlockSpec((Tq, H * hd), lambda b, i: (b * nq + i, 0))]
    if emit_k:
        out_shape.append(jax.ShapeDtypeStruct((nb * L, KVH * hd), F32))
        out_specs.append(pl.BlockSpec((L, KVH * hd), lambda b, i: (b, 0)))
    kern = functools.partial(_attn_kernel, L=L, Tq=Tq, H=H, KVH=KVH, Lc=Lc, window=window,
                             use_sink=sink is not None, use_norm=norm is not None,
                             use_rope=rope is not None, emit_k=emit_k)
    call = pl.pallas_call(
        kern, out_shape=out_shape, grid=(nb, nq), in_specs=in_specs, out_specs=out_specs,
        scratch_shapes=[pltpu.VMEM((KVH, Lk, hd), BF16), pltpu.VMEM((KVH, Lk, hd), BF16),
                        pltpu.VMEM((H, Tq, hd), BF16)],
        compiler_params=_cparams(("arbitrary", "arbitrary")))
    return call, args


def _oproj_kernel(al_ref, ar_ref, w_ref, x_ref, g1_ref, gate_ref, g2_ref, sh_ref, sc_ref, wr_ref,
                  x1_ref, h2_ref, lg_ref):
    k = al_ref.shape[1]
    y = _dot(al_ref[...].astype(BF16), w_ref[0:k, :]) + _dot(ar_ref[...].astype(BF16), w_ref[k:2 * k, :])
    x1 = x_ref[...] + gate_ref[0] * _rms(y, g1_ref[...])
    x1_ref[...] = x1
    h2 = _rms(x1, g2_ref[...]) * (1.0 + sc_ref[0]) + sh_ref[0]
    rows = h2.shape[0]
    for s in range(TOK_SUB):
        h2_ref[pl.ds(s, rows, stride=TOK_SUB), :] = h2[:, LANES * s:LANES * (s + 1)]
    lg_ref[...] = _dot3(wr_ref[...], h2, _NT)


def _out_proj(a_l, l_blk, a_r, r_blk, w_bf, x, g1, g2, mod, layer, wr_t):
    tm = 512
    half = D_MODEL // 2
    return pl.pallas_call(
        _oproj_kernel,
        out_shape=[jax.ShapeDtypeStruct((NT, D_MODEL), F32), jax.ShapeDtypeStruct((NT * TOK_SUB, LANES), F32),
                   jax.ShapeDtypeStruct((N_EXPERTS, NT), F32)],
        grid=(NT // tm,),
        in_specs=[pl.BlockSpec((tm, half), lambda i: (i, l_blk)),
                  pl.BlockSpec((tm, half), lambda i: (i, r_blk)),
                  pl.BlockSpec((D_MODEL, D_MODEL), lambda i: (0, 0)),
                  pl.BlockSpec((tm, D_MODEL), lambda i: (i, 0)),
                  pl.BlockSpec((1, D_MODEL), lambda i: (0, 0)),
                  _mod_spec(layer, 2, tm),
                  pl.BlockSpec((1, D_MODEL), lambda i: (0, 0)),
                  _mod_spec(layer, 3, tm), _mod_spec(layer, 4, tm),
                  pl.BlockSpec((N_EXPERTS, D_MODEL), lambda i: (0, 0))],
        out_specs=[pl.BlockSpec((tm, D_MODEL), lambda i: (i, 0)),
                   pl.BlockSpec((tm * TOK_SUB, LANES), lambda i: (i, 0)),
                   pl.BlockSpec((N_EXPERTS, tm), lambda i: (0, i))],
        compiler_params=_cparams(("arbitrary",)),
    )(a_l, a_r, w_bf, x, g1.reshape(1, D_MODEL), mod, g2.reshape(1, D_MODEL), mod, mod, wr_t)


def _ffn_kernel(idx_ref, h_hbm, gate_ref, wg_ref, wu_ref, wd_ref, y_ref, xf_scr, xb_scr, acc_scr, sem, *, cap, nf):
    e = pl.program_id(0)
    f = pl.program_id(1)
    per_step = cap // nf

    def issue_rows(expert, first, count):
        base = expert * cap + first

        def body(j, carry):
            src = pl.multiple_of(idx_ref[base + j], TOK_SUB)
            dst = pl.multiple_of((first + j) * TOK_SUB, TOK_SUB)
            pltpu.make_async_copy(h_hbm.at[pl.ds(src, TOK_SUB)], xf_scr.at[pl.ds(dst, TOK_SUB)], sem).start()
            return carry
        lax.fori_loop(0, count, body, 0, unroll=8)

    @pl.when((e == 0) & (f == 0))
    def _():
        issue_rows(0, 0, cap)

    @pl.when(f == 0)
    def _():
        pltpu.make_async_copy(h_hbm.at[pl.ds(0, cap * TOK_SUB)], xf_scr, sem).wait()
        for s in range(TOK_SUB):
            xb_scr[:, LANES * s:LANES * (s + 1)] = xf_scr[pl.ds(s, cap, stride=TOK_SUB), :].astype(BF16)

    @pl.when(e + 1 < pl.num_programs(0))
    def _():
        issue_rows(e + 1, f * per_step, per_step)

    wg = wg_ref[0, 0].astype(BF16)
    wu = wu_ref[0, 0].astype(BF16)
    wd = wd_ref[0, 0].astype(BF16)
    half = cap // 2
    for r in range(2):
        rows = slice(r * half, (r + 1) * half)
        xe = xb_scr[rows, :]
        hmid = (jax.nn.silu(_dot(xe, wg)) * _dot(xe, wu)).astype(BF16)
        part = _dot(hmid, wd)

        @pl.when(f == 0)
        def _():
            acc_scr[rows, :] = part

        @pl.when(f != 0)
        def _():
            acc_scr[rows, :] += part

    @pl.when(f == nf - 1)
    def _():
        y_ref[...] = (acc_scr[...] * gate_ref[0]).astype(BF16)


def _expert_ffn(idx, gate, h2, layer, w_gate, w_up, w_down):
    cap = idx.shape[1]
    fc = 512
    nf = EXPERT_FF // fc
    grid_spec = pltpu.PrefetchScalarGridSpec(
        num_scalar_prefetch=1,
        grid=(N_EXPERTS, nf),
        in_specs=[pl.BlockSpec(memory_space=pl.ANY),
                  pl.BlockSpec((1, cap, 1), lambda e, f, idx: (e, 0, 0)),
                  pl.BlockSpec((1, 1, D_MODEL, fc), lambda e, f, idx: (layer, e, 0, f)),
                  pl.BlockSpec((1, 1, D_MODEL, fc), lambda e, f, idx: (layer, e, 0, f)),
                  pl.BlockSpec((1, 1, fc, D_MODEL), lambda e, f, idx: (layer, e, f, 0))],
        out_specs=pl.BlockSpec((cap, D_MODEL), lambda e, f, idx: (e, 0)),
        scratch_shapes=[pltpu.VMEM((cap * TOK_SUB, LANES), F32), pltpu.VMEM((cap, D_MODEL), BF16),
                        pltpu.VMEM((cap, D_MODEL), F32), pltpu.SemaphoreType.DMA(())])
    return pl.pallas_call(
        functools.partial(_ffn_kernel, cap=cap, nf=nf),
        out_shape=jax.ShapeDtypeStruct((N_EXPERTS * cap, D_MODEL), BF16),
        grid_spec=grid_spec,
        compiler_params=_cparams(("arbitrary", "arbitrary")),
    )((idx * TOK_SUB).reshape(-1), h2, gate.reshape(N_EXPERTS, cap, 1), w_gate, w_up, w_down)


COMB_ROWS = 512
COMB_WIN = 128
BF16_SUB = 16


def _combine_kernel(a0_ref, s1_ref, slot_ref, y_hbm, x_ref, g_ref, gate_ref, o_ref,
                    win_scr, ovf_scr, f_scr, sem, osem, *, cap, nblk):
    j = pl.program_id(0)
    E, R, Tb = N_EXPERTS, COMB_WIN, COMB_ROWS

    def window(e, start, dst, s):
        return pltpu.make_async_copy(y_hbm.at[pl.ds(e * cap + start, R)], dst, s)

    def issue(blk, slot):
        for e in range(E):
            a0 = pl.multiple_of(a0_ref[e * nblk + blk], BF16_SUB)
            window(e, a0, win_scr.at[slot, pl.ds(e * R, R)], sem.at[slot]).start()

    @pl.when(j == 0)
    def _():
        issue(0, 0)

    slot = j % 2
    pltpu.make_async_copy(y_hbm.at[pl.ds(0, E * R)], win_scr.at[slot], sem.at[slot]).wait()

    @pl.when(j + 1 < nblk)
    def _():
        issue(j + 1, 1 - slot)

    sl = slot_ref[...]
    q = lax.broadcasted_iota(jnp.int32, (Tb, R), 1)
    onehot = jnp.concatenate(
        [(sl[:, e:e + 1] - a0_ref[e * nblk + j] == q) for e in range(E)], axis=1).astype(F32).astype(BF16)
    f_scr[...] = _dot(onehot, win_scr[slot])

    for e in range(E):
        end = s1_ref[e * nblk + j]

        def more(c):
            return c < end

        def extra(c, e=e):
            c0 = pl.multiple_of(jnp.minimum(c, cap - R), BF16_SUB)
            cp = window(e, c0, ovf_scr, osem)
            cp.start()
            cp.wait()
            hit = (sl[:, e:e + 1] - c0 == q) & (q + c0 >= c)
            f_scr[...] += _dot(hit.astype(F32).astype(BF16), ovf_scr[...])
            return c + R

        lax.while_loop(more, extra, a0_ref[e * nblk + j] + R)

    o_ref[...] = x_ref[...] + gate_ref[0] * _rms(f_scr[...], g_ref[...])


def _combine(a0, s1, slot_t, y, x, g, mod, layer):
    cap = y.shape[0] // N_EXPERTS
    nblk = NT // COMB_ROWS
    grid_spec = pltpu.PrefetchScalarGridSpec(
        num_scalar_prefetch=2,
        grid=(nblk,),
        in_specs=[pl.BlockSpec((COMB_ROWS, N_EXPERTS), lambda i, *_: (i, 0)),
                  pl.BlockSpec(memory_space=pl.ANY),
                  pl.BlockSpec((COMB_ROWS, D_MODEL), lambda i, *_: (i, 0)),
                  pl.BlockSpec((1, D_MODEL), lambda i, *_: (0, 0)),
                  _mod_spec(layer, 5, COMB_ROWS)],
        out_specs=pl.BlockSpec((COMB_ROWS, D_MODEL), lambda i, *_: (i, 0)),
        scratch_shapes=[pltpu.VMEM((2, N_EXPERTS * COMB_WIN, D_MODEL), BF16),
                        pltpu.VMEM((COMB_WIN, D_MODEL), BF16),
                        pltpu.VMEM((COMB_ROWS, D_MODEL), F32),
                        pltpu.SemaphoreType.DMA((2,)), pltpu.SemaphoreType.DMA(())])
    return pl.pallas_call(
        functools.partial(_combine_kernel, cap=cap, nblk=nblk),
        out_shape=jax.ShapeDtypeStruct((NT, D_MODEL), F32),
        grid_spec=grid_spec,
        compiler_params=_cparams(("arbitrary",)),
    )(a0.reshape(-1), s1.reshape(-1), slot_t, y, x, g.reshape(1, D_MODEL), mod)


def _route(logits_t, h2, x, g, mod, layer, w_gate, w_up, w_down):
    aff = jax.nn.softmax(logits_t, axis=0)
    gates, idxs = [], []
    for lo, n in ((0, NS), (NS, NP)):
        gate, idx = lax.top_k(aff[:, lo:lo + n], EC_CAPACITY * n // N_EXPERTS)
        order = jnp.argsort(idx, axis=1)
        gates.append(jnp.take_along_axis(gate, order, axis=1))
        idxs.append(jnp.take_along_axis(idx, order, axis=1) + lo)
    idx = jnp.concatenate(idxs, axis=1)
    cap = idx.shape[1]
    y = _expert_ffn(idx, jnp.concatenate(gates, axis=1), h2, layer, w_gate, w_up, w_down)
    bounds = jnp.arange(0, NT + 1, COMB_ROWS, dtype=jnp.int32)
    s0 = jnp.sum(idx[:, :, None] < bounds[None, None, :], axis=1, dtype=jnp.int32)
    a0 = jnp.minimum(s0[:, :-1] // BF16_SUB * BF16_SUB, cap - COMB_WIN)
    slot = jnp.full((N_EXPERTS, NT), -1, jnp.int32).at[jnp.arange(N_EXPERTS)[:, None], idx].set(
        jnp.arange(cap, dtype=jnp.int32)[None, :])
    return _combine(a0, s0[:, 1:], slot.T, y, x, g, mod, layer)


def kernel(x_prompt, x_sample, cache_b_k, cache_b_v, cache_c_k, cache_c_v, c, c_ctx, ada_w, ada_b, norm_g,
           ab_w_in, hy_short_w, hy_short_b, hy_w1, hy_b1, hy_w2, hy_b2, hy_freq, hy_w3, hy_log_decay, hy_bias,
           b_sink, ab_w_out, c_w_qkv, c_q_norm, c_k_norm, c_w_out, ec_router, ec_w_gate, ec_w_up, ec_w_down):
    x = jnp.concatenate([x_sample.reshape(NS, D_MODEL), x_prompt.reshape(NP, D_MODEL)], axis=0)
    cond = jnp.concatenate([c, c_ctx[None], jnp.zeros((MOD_ROWS - DEC_BATCH - 1, D_MODEL), F32)], axis=0)
    mod = _adaln(cond, ada_w, ada_b).reshape(DEPTH * MOD_ROWS * 6, 1, D_MODEL)
    rope = _rope_tables(DEC_SEQ)
    hd = HEAD_DIM

    e = 0
    p_hy, p_at = _norm_mod_matmul(x, norm_g[0, 0], mod, 0, ab_w_in[e].astype(BF16), (3 * HY_CH, D_MODEL - HY_CH + 2 * B_KV * hd))
    ya = []
    for L, nb, row_blk0 in ((DEC_SEQ, DEC_BATCH, 0), (SEQ, BATCH, NS // SEQ)):
        table = _dft_table(L)
        h = _hyena_filters(L, hy_w1[e], hy_b1[e], hy_w2[e], hy_b2[e], hy_freq[e], hy_w3[e], hy_log_decay[e])
        h = h.at[0, HY_CH:].set(0.0)
        spec = _filter_spectrum(L, table, h)
        ya.append(_hyena(p_hy, hy_short_w[e], hy_short_b[e].reshape(1, -1), hy_bias[e].reshape(1, -1),
                         table.astype(BF16), table.T.astype(BF16), spec, L, nb, row_blk0))
    ya = jnp.concatenate(ya, axis=0)
    ctx_b = (cache_b_k[:, e].reshape(DEC_BATCH, PAST_LEN, B_KV * hd), cache_b_v[:, e].reshape(DEC_BATCH, PAST_LEN, B_KV * hd))
    call, args = _attention(p_at, B_HEADS * hd, 2, L=DEC_SEQ, Tq=256, nb=DEC_BATCH, row0=0, H=B_HEADS, KVH=B_KV,
                            ctx=ctx_b, sink=b_sink[e], rope=rope, window=True)
    yb_s = call(*args)[0]
    call, args = _attention(p_at, B_HEADS * hd, 2, L=SEQ, Tq=256, nb=BATCH, row0=NS, H=B_HEADS, KVH=B_KV,
                            sink=b_sink[e])
    yb_p = call(*args)[0]
    yb = jnp.concatenate([yb_s, yb_p], axis=0)
    state_b_k = p_at[NS:, B_HEADS * hd:(B_HEADS + B_KV) * hd].reshape(BATCH, 1, SEQ, B_KV, hd)
    state_b_v = p_at[NS:, (B_HEADS + B_KV) * hd:].reshape(BATCH, 1, SEQ, B_KV, hd)
    x, h2, lg = _out_proj(ya, 0, yb, 0, ab_w_out[e].astype(BF16), x, norm_g[0, 1], norm_g[0, 2], mod, 0, ec_router[0].T)
    x = _route(lg, h2, x, norm_g[0, 3], mod, 0, ec_w_gate, ec_w_up, ec_w_down)

    o = 0
    (p,) = _norm_mod_matmul(x, norm_g[1, 0], mod, 1, c_w_qkv[o].astype(BF16), ((C_HEADS + 2 * C_KV) * hd,))
    ctx_c = (cache_c_k[:, o].reshape(DEC_BATCH, PAST_LEN, C_KV * hd), cache_c_v[:, o].reshape(DEC_BATCH, PAST_LEN, C_KV * hd))
    nrm = (c_q_norm[o], c_k_norm[o])
    call, args = _attention(p, C_HEADS * hd, 2, L=DEC_SEQ, Tq=128, nb=DEC_BATCH, row0=0, H=C_HEADS, KVH=C_KV,
                            ctx=ctx_c, norm=nrm, rope=rope)
    yc_s = call(*args)[0]
    call, args = _attention(p, C_HEADS * hd, 2, L=SEQ, Tq=256, nb=BATCH, row0=NS, H=C_HEADS, KVH=C_KV,
                            norm=nrm, emit_k=True)
    yc_p, kn = call(*args)
    yc = jnp.concatenate([yc_s, yc_p], axis=0)
    state_c_k = kn.reshape(BATCH, 1, SEQ, C_KV, hd)
    state_c_v = p[NS:, (C_HEADS + C_KV) * hd:].reshape(BATCH, 1, SEQ, C_KV, hd)
    x, h2, lg = _out_proj(yc, 0, yc, 1, c_w_out[o].astype(BF16), x, norm_g[1, 1], norm_g[1, 2], mod, 1, ec_router[1].T)
    x = _route(lg, h2, x, norm_g[1, 3], mod, 1, ec_w_gate, ec_w_up, ec_w_down)

    y_sample = x[:NS].reshape(DEC_BATCH, DEC_SEQ, D_MODEL)
    y_prompt = x[NS:].reshape(BATCH, SEQ, D_MODEL)
    return (y_prompt, y_sample, state_b_k, state_b_v, state_c_k, state_c_v)
```

```python
import functools
import math

import jax
import jax.numpy as jnp
from jax import lax
from jax.experimental import pallas as pl
from jax.experimental.pallas import tpu as pltpu

F32 = jnp.float32
BF16 = jnp.bfloat16

D_MODEL = 1024
BATCH = 16
SEQ = 256
DEPTH = 2
DEC_BATCH = 8
DEC_SEQ = 1024
PAST_LEN = 512
GRID_W = 64
HEAD_DIM = 64
HY_CH = D_MODEL // 2
HY_BANDS = 8
B_HEADS = 8
B_KV = 2
WINDOW = 128
C_HEADS = 16
C_KV = 4
ROPE_BASE = 10000.0
N_EXPERTS = 16
EC_CAPACITY = 2
EXPERT_FF = 2 * D_MODEL
EPS = 1e-6
NEG_INF = -1e30

NS = DEC_BATCH * DEC_SEQ
NP = BATCH * SEQ
NT = NS + NP
MOD_ROWS = 16
LANES = 128
VMEM_LIMIT = 56 * 1024 * 1024


def _cparams(sem):
    return pltpu.CompilerParams(dimension_semantics=sem, vmem_limit_bytes=VMEM_LIMIT)


def _split(x):
    hi = x.astype(BF16)
    lo = (x - hi.astype(F32)).astype(BF16)
    return hi, lo


_NN = (((1,), (0,)), ((), ()))
_NT = (((1,), (1,)), ((), ()))


def _dot(a, b, dims=_NN):
    return lax.dot_general(a, b, dims, preferred_element_type=F32)


def _dot3(a, b, dims=_NN):
    ah, al = _split(a)
    bh, bl = _split(b)
    return _dot(ah, bh, dims) + _dot(al, bh, dims) + _dot(ah, bl, dims)


def _rms(x, g):
    return x * lax.rsqrt(jnp.mean(x * x, axis=-1, keepdims=True) + EPS) * g


def _mod_row(tile_rows):
    def f(i):
        return jnp.minimum(i * tile_rows // DEC_SEQ, DEC_BATCH)
    return f


def _adaln_kernel(c_ref, w_ref, b_ref, o_ref):
    s = jax.nn.silu(c_ref[...])
    o_ref[0] = _dot3(s, w_ref[0]) + b_ref[0]


def _adaln(cond, ada_w, ada_b):
    tn = 1536
    return pl.pallas_call(
        _adaln_kernel,
        out_shape=jax.ShapeDtypeStruct((DEPTH, MOD_ROWS, 6 * D_MODEL), F32),
        grid=(DEPTH, 6 * D_MODEL // tn),
        in_specs=[pl.BlockSpec((MOD_ROWS, D_MODEL), lambda l, j: (0, 0)),
                  pl.BlockSpec((1, D_MODEL, tn), lambda l, j: (l, 0, j)),
                  pl.BlockSpec((1, 1, tn), lambda l, j: (l, 0, j))],
        out_specs=pl.BlockSpec((1, MOD_ROWS, tn), lambda l, j: (l, 0, j)),
        compiler_params=_cparams(("arbitrary", "arbitrary")),
    )(cond, ada_w, ada_b.reshape(DEPTH, 1, 6 * D_MODEL))


def _mod_spec(layer, which, tile_rows):
    row = _mod_row(tile_rows)
    return pl.BlockSpec((1, 1, D_MODEL), lambda i, *_: ((layer * MOD_ROWS + row(i)) * 6 + which, 0, 0))


def _nmm_kernel(x_ref, g_ref, sh_ref, sc_ref, w_ref, *o_refs):
    h = _rms(x_ref[...], g_ref[...]) * (1.0 + sc_ref[0]) + sh_ref[0]
    p = _dot(h.astype(BF16), w_ref[...])
    off = 0
    for o in o_refs:
        n = o.shape[1]
        o[...] = p[:, off:off + n]
        off += n


def _norm_mod_matmul(x, g, mod, layer, w_bf, splits):
    tm = 512
    n_out = w_bf.shape[1]
    return pl.pallas_call(
        _nmm_kernel,
        out_shape=[jax.ShapeDtypeStruct((NT, n), F32) for n in splits],
        grid=(NT // tm,),
        in_specs=[pl.BlockSpec((tm, D_MODEL), lambda i: (i, 0)),
                  pl.BlockSpec((1, D_MODEL), lambda i: (0, 0)),
                  _mod_spec(layer, 0, tm), _mod_spec(layer, 1, tm),
                  pl.BlockSpec((D_MODEL, n_out), lambda i: (0, 0))],
        out_specs=[pl.BlockSpec((tm, n), lambda i: (i, 0)) for n in splits],
        compiler_params=_cparams(("arbitrary",)),
    )(x, g.reshape(1, D_MODEL), mod, mod, w_bf)


def _hyena_filters(L, w1, b1, w2, b2, freq, w3, log_decay):
    hp = lax.Precision.HIGHEST
    t = jnp.linspace(0.0, 1.0, L, dtype=F32)[:, None]
    w = (2.0 * math.pi / L) * jnp.arange(L, dtype=F32)[:, None]
    bands = jnp.linspace(1e-4, HY_BANDS - 1, HY_BANDS, dtype=F32)[None, :]
    z = jnp.concatenate([t, jnp.cos(bands * w), -jnp.sin(bands * w)], axis=-1)
    h = jnp.sin(freq[0] * (jnp.dot(z, w1, precision=hp) + b1))
    h = jnp.sin(freq[1] * (jnp.dot(h, w2, precision=hp) + b2))
    return jnp.dot(h, w3, precision=hp) * jnp.exp(-t * jnp.exp(log_decay))


def _dft_table(L):
    n = 2 * L
    f = jnp.arange(L, dtype=jnp.int32)[:, None]
    t = jnp.arange(L, dtype=jnp.int32)[None, :]
    ang = ((f * t) % n).astype(F32) * (2.0 * math.pi / n)
    nyq = jnp.where(t % 2 == 0, 1.0, -1.0).astype(F32)
    fs = jnp.where(f == 0, nyq, -jnp.sin(ang))
    return jnp.concatenate([jnp.cos(ang), fs], axis=0)


def _filt_kernel(f_ref, hf_ref, hb_ref, p_ref, q_ref, s_ref, *, L):
    hf = hf_ref[...]
    hb = hb_ref[...]
    ct = hf.shape[1]
    nrm = lax.rsqrt(jnp.sum(hf * hf, axis=0, keepdims=True) + jnp.sum(hb * hb, axis=0, keepdims=True) + EPS)
    fh = _dot3(f_ref[...], jnp.concatenate([hf, hb], axis=1))
    row = lax.broadcasted_iota(jnp.int32, (L, ct), 0)
    w = jnp.where(row == 0, 1.0 / (2 * L), 2.0 / (2 * L)) * nrm
    re = fh[:L, :ct] + fh[:L, ct:]
    p_ref[...] = re * w
    q_ref[...] = jnp.where(row == 0, 0.0, fh[L:, :ct] - fh[L:, ct:]) * w
    s_ref[...] = jnp.where(row == 0, fh[L:, :ct] + fh[L:, ct:], re) * w


def _filter_spectrum(L, table, h):
    ct = 128
    nc = HY_CH // ct
    shp = jax.ShapeDtypeStruct((L, HY_CH), F32)
    return pl.pallas_call(
        functools.partial(_filt_kernel, L=L),
        out_shape=[shp, shp, shp],
        grid=(nc,),
        in_specs=[pl.BlockSpec((2 * L, L), lambda c: (0, 0)),
                  pl.BlockSpec((L, ct), lambda c: (0, c)),
                  pl.BlockSpec((L, ct), lambda c: (0, nc + c))],
        out_specs=[pl.BlockSpec((L, ct), lambda c: (0, c))] * 3,
        compiler_params=_cparams(("arbitrary",)),
    )(table, h, h)


def _hyena_kernel(x0_ref, x1_ref, v_ref, w0_ref, w1_ref, wv_ref, b0_ref, b1_ref, bv_ref, bias_ref,
                  f_ref, g_ref, p_ref, q_ref, s_ref, o_ref, *, L):
    ct = x0_ref.shape[1]
    row = lax.broadcasted_iota(jnp.int32, (L, ct), 0)

    def sconv(x_ref, w_ref, b_ref):
        x = x_ref[...]
        xm = jnp.where(row == 0, 0.0, pltpu.roll(x, 1, 0))
        xp = jnp.where(row == L - 1, 0.0, pltpu.roll(x, L - 1, 0))
        return xm * w_ref[0:1, :] + x * w_ref[1:2, :] + xp * w_ref[2:3, :] + b_ref[...]

    z = sconv(v_ref, wv_ref, bv_ref) * sconv(x1_ref, w1_ref, b1_ref)
    zf = _dot(f_ref[...], z.astype(BF16))
    a = zf[:L]
    b = zf[L:]
    p = p_ref[...]
    q = q_ref[...]
    yre = a * p - b * q
    yim = a * q + b * s_ref[...]
    ycat = jnp.concatenate([yre, yim], axis=0).astype(BF16)
    y = _dot(g_ref[...], ycat) + z * bias_ref[...]
    o_ref[...] = sconv(x0_ref, w0_ref, b0_ref) * y


def _hyena(p_hy, short_w, short_b, bias, table_bf, table_t_bf, spec, L, nb, row_blk0):
    ct = 256
    nc = HY_CH // ct
    P, Q, S = spec

    def xs(k):
        return pl.BlockSpec((L, ct), lambda b, c: (row_blk0 + b, k * nc + c))

    def ws(rows, k):
        return pl.BlockSpec((rows, ct), lambda b, c: (0, k * nc + c))

    cs = pl.BlockSpec((L, ct), lambda b, c: (0, c))
    return pl.pallas_call(
        functools.partial(_hyena_kernel, L=L),
        out_shape=jax.ShapeDtypeStruct((nb * L, HY_CH), F32),
        grid=(nb, nc),
        in_specs=[xs(0), xs(1), xs(2), ws(3, 0), ws(3, 1), ws(3, 2), ws(1, 0), ws(1, 1), ws(1, 2), ws(1, 0),
                  pl.BlockSpec((2 * L, L), lambda b, c: (0, 0)),
                  pl.BlockSpec((L, 2 * L), lambda b, c: (0, 0)),
                  cs, cs, cs],
        out_specs=pl.BlockSpec((L, ct), lambda b, c: (b, c)),
        compiler_params=_cparams(("arbitrary", "arbitrary")),
    )(p_hy, p_hy, p_hy, short_w, short_w, short_w, short_b, short_b, short_b, bias,
      table_bf, table_t_bf, P, Q, S)


def _rope_tables(L):
    half = HEAD_DIM // 2
    pos = jnp.arange(L)
    r = (pos // GRID_W).astype(F32)
    col = (pos % GRID_W).astype(F32)
    inv = ROPE_BASE ** (-jnp.arange(0, half, 2, dtype=F32) / half)
    ar = r[:, None] * inv[None, :]
    ac = col[:, None] * inv[None, :]
    z = jnp.zeros_like(ar)
    cos = jnp.concatenate([jnp.cos(ar), jnp.cos(ar), jnp.cos(ac), jnp.cos(ac)], axis=1)
    s1 = jnp.concatenate([-jnp.sin(ar), z, -jnp.sin(ac), z], axis=1)
    s2 = jnp.concatenate([z, jnp.sin(ar), z, jnp.sin(ac)], axis=1)
    rep = LANES // HEAD_DIM
    return jnp.tile(cos, (1, rep)), jnp.tile(s1, (1, rep)), jnp.tile(s2, (1, rep))


def _attn_kernel(*refs, L, Tq, H, KVH, Lc, window, use_sink, use_norm, use_rope, emit_k):
    G = H // KVH
    hd = HEAD_DIM
    pad = WINDOW if window else 0
    it = iter(refs)
    q_ref = next(it)
    kv_ref = next(it)
    ck_ref = cv_ref = sink_ref = qn_ref = kn_ref = bd_ref = rc_ref = rs1_ref = rs2_ref = ko_ref = None
    if Lc:
        ck_ref = next(it)
        cv_ref = next(it)
    if use_sink:
        sink_ref = next(it)
    if use_norm:
        qn_ref = next(it)
        kn_ref = next(it)
        bd_ref = next(it)
    if use_rope:
        rc_ref = next(it)
        rs1_ref = next(it)
        rs2_ref = next(it)
    o_ref = next(it)
    if emit_k:
        ko_ref = next(it)
    k_scr = next(it)
    v_scr = next(it)
    q_scr = next(it)
    i = pl.program_id(1)
    lat0 = Lc + pad

    def head_norm(x, g_ref):
        hi, lo = _split(x * x)
        ms = _dot(hi, bd_ref[...]) + _dot(lo, bd_ref[...])
        return x * lax.rsqrt(ms + EPS) * g_ref[...]

    def rope(x, rows):
        return (x * rc_ref[rows, :] + pltpu.roll(x, LANES - 16, 1) * rs1_ref[rows, :]
                + pltpu.roll(x, 16, 1) * rs2_ref[rows, :])

    @pl.when(i == 0)
    def _():
        if window:
            zpad = jnp.zeros((KVH, pad, hd), BF16)
            k_scr[:, Lc:Lc + pad, :] = zpad
            k_scr[:, lat0 + L:lat0 + L + pad, :] = zpad
            v_scr[:, Lc:Lc + pad, :] = zpad
            v_scr[:, lat0 + L:lat0 + L + pad, :] = zpad
        for g in range(KVH * hd // LANES):
            kg = kv_ref[:, LANES * g:LANES * (g + 1)]
            if use_norm:
                kg = head_norm(kg, kn_ref)
                if emit_k:
                    ko_ref[:, LANES * g:LANES * (g + 1)] = kg
            if use_rope:
                kg = rope(kg, slice(None))
            kb = kg.astype(BF16)
            vb = kv_ref[:, KVH * hd + LANES * g:KVH * hd + LANES * (g + 1)].astype(BF16)
            for hh in range(LANES // hd):
                k_scr[2 * g + hh, lat0:lat0 + L, :] = kb[:, hd * hh:hd * (hh + 1)]
                v_scr[2 * g + hh, lat0:lat0 + L, :] = vb[:, hd * hh:hd * (hh + 1)]
        if Lc:
            ckb = ck_ref[0].astype(BF16)
            cvb = cv_ref[0].astype(BF16)
            for kvh in range(KVH):
                k_scr[kvh, 0:Lc, :] = ckb[:, hd * kvh:hd * (kvh + 1)]
                v_scr[kvh, 0:Lc, :] = cvb[:, hd * kvh:hd * (kvh + 1)]

    rows = pl.ds(pl.multiple_of(i * Tq, Tq), Tq)
    for g in range(H * hd // LANES):
        qg = q_ref[:, LANES * g:LANES * (g + 1)]
        if use_norm:
            qg = head_norm(qg, qn_ref)
        if use_rope:
            qg = rope(qg, rows)
        qb = (qg * (hd ** -0.5)).astype(BF16)
        for hh in range(LANES // hd):
            q_scr[2 * g + hh] = qb[:, hd * hh:hd * (hh + 1)]

    hrow = lax.broadcasted_iota(jnp.int32, (G * Tq, 1), 0) // Tq
    for kvh in range(KVH):
        qs = q_scr[kvh * G:(kvh + 1) * G].reshape(G * Tq, hd)
        if use_sink:
            sk = jnp.zeros((G * Tq, 1), F32)
            for g in range(G):
                sk = jnp.where(hrow == g, sink_ref[0, kvh * G + g], sk)
        if window:
            W = Tq + 2 * pad
            kc = k_scr[kvh, 0:Lc, :]
            vc = v_scr[kvh, 0:Lc, :]
            wrows = pl.ds(pl.multiple_of(Lc + i * Tq, Tq), W)
            kw = k_scr[kvh, wrows, :]
            vw = v_scr[kvh, wrows, :]
            s_c = _dot(qs, kc, _NT)
            s_w = _dot(qs, kw, _NT).reshape(G, Tq, W)
            r = lax.broadcasted_iota(jnp.int32, (Tq, W), 0)
            c = lax.broadcasted_iota(jnp.int32, (Tq, W), 1)
            kpos = i * Tq - pad + c
            valid = (kpos >= 0) & (kpos < L) & (jnp.abs(r + pad - c) <= WINDOW)
            s_w = jnp.where(valid[None], s_w, NEG_INF).reshape(G * Tq, W)
            m = jnp.maximum(jnp.max(s_c, axis=-1, keepdims=True), jnp.max(s_w, axis=-1, keepdims=True))
            if use_sink:
                m = jnp.maximum(m, sk)
            e_c = jnp.exp(s_c - m)
            e_w = jnp.exp(s_w - m)
            den = jnp.sum(e_c, axis=-1, keepdims=True) + jnp.sum(e_w, axis=-1, keepdims=True)
            o = _dot(e_c.astype(BF16), vc) + _dot(e_w.astype(BF16), vw)
        else:
            s = _dot(qs, k_scr[kvh], _NT)
            m = jnp.max(s, axis=-1, keepdims=True)
            if use_sink:
                m = jnp.maximum(m, sk)
            e = jnp.exp(s - m)
            den = jnp.sum(e, axis=-1, keepdims=True)
            o = _dot(e.astype(BF16), v_scr[kvh])
        if use_sink:
            den = den + jnp.exp(sk - m)
        o = o / den
        for g in range(G):
            h = kvh * G + g
            o_ref[:, hd * h:hd * (h + 1)] = o[g * Tq:(g + 1) * Tq]


def _attention(p, q_cols, kv_colblk, *, L, Tq, nb, row0, H, KVH, ctx=None, sink=None, norm=None, rope=None,
               window=False, emit_k=False):
    hd = HEAD_DIM
    nq = L // Tq
    Lc = 0 if ctx is None else ctx[0].shape[1]
    pad = WINDOW if window else 0
    Lk = Lc + L + 2 * pad
    kvw = 2 * KVH * hd
    args = [p, p]
    in_specs = [pl.BlockSpec((Tq, q_cols), lambda b, i: (row0 // Tq + b * nq + i, 0)),
                pl.BlockSpec((L, kvw), lambda b, i: (row0 // L + b, kv_colblk))]
    if ctx is not None:
        for a in ctx:
            args.append(a)
            in_specs.append(pl.BlockSpec((1, Lc, KVH * hd), lambda b, i: (b, 0, 0)))
    if sink is not None:
        args.append(sink.reshape(1, H))
        in_specs.append(pl.BlockSpec(memory_space=pltpu.SMEM))
    if norm is not None:
        bd = (jnp.arange(LANES)[:, None] // hd == jnp.arange(LANES)[None, :] // hd).astype(F32) / hd
        for a in (jnp.tile(norm[0], LANES // hd).reshape(1, LANES), jnp.tile(norm[1], LANES // hd).reshape(1, LANES)):
            args.append(a)
            in_specs.append(pl.BlockSpec((1, LANES), lambda b, i: (0, 0)))
        args.append(bd.astype(BF16))
        in_specs.append(pl.BlockSpec((LANES, LANES), lambda b, i: (0, 0)))
    if rope is not None:
        for a in rope:
            args.append(a)
            in_specs.append(pl.BlockSpec((L, LANES), lambda b, i: (0, 0)))
    out_shape = [jax.ShapeDtypeStruct((nb * L, H * hd), F32)]
    out_specs = [pl.BlockSpec((Tq, H * hd), lambda b, i: (b * nq + i, 0))]
    if emit_k:
        out_shape.append(jax.ShapeDtypeStruct((nb * L, KVH * hd), F32))
        out_specs.append(pl.BlockSpec((L, KVH * hd), lambda b, i: (b, 0)))
    kern = functools.partial(_attn_kernel, L=L, Tq=Tq, H=H, KVH=KVH, Lc=Lc, window=window,
                             use_sink=sink is not None, use_norm=norm is not None,
                             use_rope=rope is not None, emit_k=emit_k)
    call = pl.pallas_call(
        kern, out_shape=out_shape, grid=(nb, nq), in_specs=in_specs, out_specs=out_specs,
        scratch_shapes=[pltpu.VMEM((KVH, Lk, hd), BF16), pltpu.VMEM((KVH, Lk, hd), BF16),
                        pltpu.VMEM((H, Tq, hd), BF16)],
        compiler_params=_cparams(("arbitrary", "arbitrary")))
    return call, args


def _oproj_kernel(al_ref, ar_ref, w_ref, x_ref, g1_ref, gate_ref, g2_ref, sh_ref, sc_ref, wr_ref,
                  x1_ref, h2_ref, lg_ref):
    k = al_ref.shape[1]
    y = _dot(al_ref[...].astype(BF16), w_ref[0:k, :]) + _dot(ar_ref[...].astype(BF16), w_ref[k:2 * k, :])
    x1 = x_ref[...] + gate_ref[0] * _rms(y, g1_ref[...])
    x1_ref[...] = x1
    h2 = _rms(x1, g2_ref[...]) * (1.0 + sc_ref[0]) + sh_ref[0]
    h2_ref[...] = h2.astype(BF16)
    lg_ref[...] = _dot3(wr_ref[...], h2, _NT)


def _out_proj(a_l, l_blk, a_r, r_blk, w_bf, x, g1, g2, mod, layer, wr_t):
    tm = 512
    half = D_MODEL // 2
    return pl.pallas_call(
        _oproj_kernel,
        out_shape=[jax.ShapeDtypeStruct((NT, D_MODEL), F32), jax.ShapeDtypeStruct((NT, D_MODEL), BF16),
                   jax.ShapeDtypeStruct((N_EXPERTS, NT), F32)],
        grid=(NT // tm,),
        in_specs=[pl.BlockSpec((tm, half), lambda i: (i, l_blk)),
                  pl.BlockSpec((tm, half), lambda i: (i, r_blk)),
                  pl.BlockSpec((D_MODEL, D_MODEL), lambda i: (0, 0)),
                  pl.BlockSpec((tm, D_MODEL), lambda i: (i, 0)),
                  pl.BlockSpec((1, D_MODEL), lambda i: (0, 0)),
                  _mod_spec(layer, 2, tm),
                  pl.BlockSpec((1, D_MODEL), lambda i: (0, 0)),
                  _mod_spec(layer, 3, tm), _mod_spec(layer, 4, tm),
                  pl.BlockSpec((N_EXPERTS, D_MODEL), lambda i: (0, 0))],
        out_specs=[pl.BlockSpec((tm, D_MODEL), lambda i: (i, 0)),
                   pl.BlockSpec((tm, D_MODEL), lambda i: (i, 0)),
                   pl.BlockSpec((N_EXPERTS, tm), lambda i: (0, i))],
        compiler_params=_cparams(("arbitrary",)),
    )(a_l, a_r, w_bf, x, g1.reshape(1, D_MODEL), mod, g2.reshape(1, D_MODEL), mod, mod, wr_t)


COMB_ROWS = 512
COMB_WIN = 128
BF16_SUB = 16


def _ffn_kernel(a0_ref, s1_ref, slot_ref, aff_ref, h_ref, wg_ref, wu_ref, wd_ref, y_ref,
                xn_scr, gn_scr, xc_scr, gc_scr, acc_scr, *, cap, nf, nblk):
    grp = pl.program_id(0)
    f = pl.program_id(1)
    R = COMB_WIN
    per_step = nblk // nf

    def add_window(jb, c0, hit_of):
        cols = slice(jb * COMB_ROWS, (jb + 1) * COMB_ROWS)
        q = lax.broadcasted_iota(jnp.int32, (R, COMB_ROWS), 0)
        hit = hit_of(slot_ref[0, :, cols].astype(jnp.int32), q)
        rows = pl.ds(c0, R)
        piece = _dot(hit.astype(F32).astype(BF16), h_ref[cols, :])
        xn_scr[rows, :] += piece.astype(BF16)
        gn_scr[rows, :] += jnp.sum(jnp.where(hit, aff_ref[0, :, cols], 0.0), axis=1, keepdims=True)

    def window_start(jb):
        return pl.multiple_of(a0_ref[grp * nblk + f * per_step + jb], BF16_SUB)

    def gather():
        for jb in range(per_step):
            a0 = window_start(jb)
            add_window(jb, a0, lambda srow, q, a0=a0: srow - a0 == q)

    def gather_more():
        for jb in range(per_step):
            end = s1_ref[grp * nblk + f * per_step + jb]

            def extra(c, jb=jb):
                c0 = pl.multiple_of(jnp.minimum(c, cap - R), BF16_SUB)
                add_window(jb, c0, lambda srow, q: (srow - c0 == q) & (q + c0 >= c))
                return c + R

            lax.while_loop(lambda c, end=end: c < end, extra, window_start(jb) + R)

    def compute():
        wg = wg_ref[0, 0].astype(BF16)
        wu = wu_ref[0, 0].astype(BF16)
        wd = wd_ref[0, 0].astype(BF16)
        part_rows = cap // 4
        for r in range(4):
            rows = slice(r * part_rows, (r + 1) * part_rows)
            xe = xc_scr[rows, :]
            hmid = (jax.nn.silu(_dot(xe, wg)) * _dot(xe, wu)).astype(BF16)
            acc_scr[rows, :] += _dot(hmid, wd)

    @pl.when(f == 0)
    def _():
        xn_scr[...] = jnp.zeros_like(xn_scr)
        gn_scr[...] = jnp.zeros_like(gn_scr)
        acc_scr[...] = jnp.zeros_like(acc_scr)

    @pl.when(grp == 0)
    def _():
        gather()

    @pl.when((grp > 0) & (grp < N_EXPERTS))
    def _():
        gather()
        compute()

    @pl.when(grp == N_EXPERTS)
    def _():
        compute()

    @pl.when(grp < N_EXPERTS)
    def _():
        gather_more()

    @pl.when(f == nf - 1)
    def _():
        @pl.when(grp > 0)
        def _():
            y_ref[...] = (acc_scr[...] * gc_scr[...]).astype(BF16)

        xc_scr[...] = xn_scr[...]
        gc_scr[...] = gn_scr[...]


def _expert_ffn(a0, s1, slot, aff, h2, layer, w_gate, w_up, w_down):
    cap = EC_CAPACITY * NT // N_EXPERTS
    fc = 512
    nf = EXPERT_FF // fc
    nblk = NT // COMB_ROWS
    tok = NT // nf

    def prev(g):
        return jnp.maximum(g - 1, 0)

    def this(g):
        return jnp.minimum(g, N_EXPERTS - 1)

    grid_spec = pltpu.PrefetchScalarGridSpec(
        num_scalar_prefetch=2,
        grid=(N_EXPERTS + 1, nf),
        in_specs=[pl.BlockSpec((1, 1, tok), lambda g, f, *_: (this(g), 0, f)),
                  pl.BlockSpec((1, 1, tok), lambda g, f, *_: (this(g), 0, f)),
                  pl.BlockSpec((tok, D_MODEL), lambda g, f, *_: (f, 0)),
                  pl.BlockSpec((1, 1, D_MODEL, fc), lambda g, f, *_: (layer, prev(g), 0, f)),
                  pl.BlockSpec((1, 1, D_MODEL, fc), lambda g, f, *_: (layer, prev(g), 0, f)),
                  pl.BlockSpec((1, 1, fc, D_MODEL), lambda g, f, *_: (layer, prev(g), f, 0))],
        out_specs=pl.BlockSpec((cap, D_MODEL), lambda g, f, *_: (prev(g), 0)),
        scratch_shapes=[pltpu.VMEM((cap, D_MODEL), BF16), pltpu.VMEM((cap, 1), F32),
                        pltpu.VMEM((cap, D_MODEL), BF16), pltpu.VMEM((cap, 1), F32),
                        pltpu.VMEM((cap, D_MODEL), F32)])
    return pl.pallas_call(
        functools.partial(_ffn_kernel, cap=cap, nf=nf, nblk=nblk),
        out_shape=jax.ShapeDtypeStruct((N_EXPERTS * cap, D_MODEL), BF16),
        grid_spec=grid_spec,
        compiler_params=_cparams(("arbitrary", "arbitrary")),
    )(a0.reshape(-1), s1.reshape(-1), slot.reshape(N_EXPERTS, 1, NT), aff.reshape(N_EXPERTS, 1, NT), h2,
      w_gate, w_up, w_down)


def _combine_kernel(a0_ref, s1_ref, slot_ref, y_hbm, x_ref, g_ref, gate_ref, o_ref,
                    win_scr, ovf_scr, f_scr, sem, osem, *, cap, nblk):
    j = pl.program_id(0)
    E, R, Tb = N_EXPERTS, COMB_WIN, COMB_ROWS

    def window(e, start, dst, s):
        return pltpu.make_async_copy(y_hbm.at[pl.ds(e * cap + start, R)], dst, s)

    def issue(blk, slot):
        for e in range(E):
            a0 = pl.multiple_of(a0_ref[e * nblk + blk], BF16_SUB)
            window(e, a0, win_scr.at[slot, pl.ds(e * R, R)], sem.at[slot]).start()

    @pl.when(j == 0)
    def _():
        issue(0, 0)

    slot = j % 2
    pltpu.make_async_copy(y_hbm.at[pl.ds(0, E * R)], win_scr.at[slot], sem.at[slot]).wait()

    @pl.when(j + 1 < nblk)
    def _():
        issue(j + 1, 1 - slot)

    sl = slot_ref[...].astype(jnp.int32)
    q = lax.broadcasted_iota(jnp.int32, (Tb, R), 1)
    onehot = jnp.concatenate(
        [(sl[:, e:e + 1] - a0_ref[e * nblk + j] == q) for e in range(E)], axis=1).astype(F32).astype(BF16)
    f_scr[...] = _dot(onehot, win_scr[slot])

    for e in range(E):
        end = s1_ref[e * nblk + j]

        def more(c):
            return c < end

        def extra(c, e=e):
            c0 = pl.multiple_of(jnp.minimum(c, cap - R), BF16_SUB)
            cp = window(e, c0, ovf_scr, osem)
            cp.start()
            cp.wait()
            hit = (sl[:, e:e + 1] - c0 == q) & (q + c0 >= c)
            f_scr[...] += _dot(hit.astype(F32).astype(BF16), ovf_scr[...])
            return c + R

        lax.while_loop(more, extra, a0_ref[e * nblk + j] + R)

    o_ref[...] = x_ref[...] + gate_ref[0] * _rms(f_scr[...], g_ref[...])


def _combine(a0, s1, slot_t, y, x, g, mod, layer):
    cap = y.shape[0] // N_EXPERTS
    nblk = NT // COMB_ROWS
    grid_spec = pltpu.PrefetchScalarGridSpec(
        num_scalar_prefetch=2,
        grid=(nblk,),
        in_specs=[pl.BlockSpec((COMB_ROWS, LANES), lambda i, *_: (i, 0)),
                  pl.BlockSpec(memory_space=pl.ANY),
                  pl.BlockSpec((COMB_ROWS, D_MODEL), lambda i, *_: (i, 0)),
                  pl.BlockSpec((1, D_MODEL), lambda i, *_: (0, 0)),
                  _mod_spec(layer, 5, COMB_ROWS)],
        out_specs=pl.BlockSpec((COMB_ROWS, D_MODEL), lambda i, *_: (i, 0)),
        scratch_shapes=[pltpu.VMEM((2, N_EXPERTS * COMB_WIN, D_MODEL), BF16),
                        pltpu.VMEM((COMB_WIN, D_MODEL), BF16),
                        pltpu.VMEM((COMB_ROWS, D_MODEL), F32),
                        pltpu.SemaphoreType.DMA((2,)), pltpu.SemaphoreType.DMA(())])
    return pl.pallas_call(
        functools.partial(_combine_kernel, cap=cap, nblk=nblk),
        out_shape=jax.ShapeDtypeStruct((NT, D_MODEL), F32),
        grid_spec=grid_spec,
        compiler_params=_cparams(("arbitrary",)),
    )(a0.reshape(-1), s1.reshape(-1), slot_t, y, x, g.reshape(1, D_MODEL), mod)


CUM_BLK = 256
SCALE_STEP = 2.0 ** -16
SCALE_ITERS = 10
BISECT_ITERS = 64


def _prefix_count(mask):
    n = mask.shape[1]
    tri = (lax.broadcasted_iota(jnp.int32, (CUM_BLK, CUM_BLK), 0)
           <= lax.broadcasted_iota(jnp.int32, (CUM_BLK, CUM_BLK), 1)).astype(F32).astype(BF16)
    carry = jnp.zeros((mask.shape[0], 1), F32)
    outs = []
    for b in range(n // CUM_BLK):
        c = _dot(mask[:, b * CUM_BLK:(b + 1) * CUM_BLK].astype(F32).astype(BF16), tri) + carry
        outs.append(c)
        carry = c[:, CUM_BLK - 1:CUM_BLK]
    return jnp.concatenate(outs, axis=1)


def _route_kernel(lg_ref, aff_ref, slot_ref, slott_ref, s0_ref):
    x = lg_ref[...]
    ex = jnp.exp(x - jnp.max(x, axis=0, keepdims=True))
    aff = ex / jnp.sum(ex, axis=0, keepdims=True)
    aff_ref[...] = aff
    streams = ((0, NS), (NS, NP))
    caps = [float(EC_CAPACITY * n // N_EXPERTS) for _, n in streams]
    parts = [aff[:, lo:lo + n] for lo, n in streams]

    def enough(k, thr):
        return jnp.sum((parts[k] >= thr).astype(F32), axis=1, keepdims=True) >= caps[k]

    def scale_step(_, carry):
        out = []
        for k, (lo_v, hi_v, found_f) in enumerate(carry):
            found = found_f > 0.5
            mid = hi_v * SCALE_STEP
            ge = enough(k, mid)
            out.append((jnp.where(found | ~ge, lo_v, mid), jnp.where(found | ge, hi_v, mid),
                        jnp.where(found | ge, 1.0, 0.0)))
        return tuple(out)

    def bisect_step(_, carry):
        out = []
        for k, (lo_v, hi_v) in enumerate(carry):
            mid = 0.5 * (lo_v + hi_v)
            ge = enough(k, mid)
            out.append((jnp.where(ge, mid, lo_v), jnp.where(ge, hi_v, mid)))
        return tuple(out)

    col = lambda v: jnp.full((N_EXPERTS, 1), v, F32)
    start = tuple((col(0.0), col(2.0), col(0.0)) for _ in streams)
    scaled = lax.fori_loop(0, SCALE_ITERS, scale_step, start)
    bounds = lax.fori_loop(0, BISECT_ITERS, bisect_step, tuple((lo_v, hi_v) for lo_v, hi_v, _ in scaled))
    slots = []
    off = 0.0
    for k, (lo_v, hi_v) in enumerate(bounds):
        above = parts[k] >= hi_v
        edge = (parts[k] >= lo_v) & ~above
        need = caps[k] - jnp.sum(above.astype(F32), axis=1, keepdims=True)
        edge_rank = _prefix_count(edge) - edge.astype(F32)
        sel = above | (edge & (edge_rank < need))
        slots.append(jnp.where(sel, _prefix_count(sel) - 1.0 + off, -1.0))
        off += caps[k]
    slot = jnp.concatenate(slots, axis=1)
    slot_ref[...] = slot
    pad = jnp.full((LANES - N_EXPERTS, NT), -1.0, F32)
    slott_ref[...] = jnp.concatenate([slot, pad], axis=0).T
    tok = lax.broadcasted_iota(jnp.int32, (NT, LANES), 0)
    edge = lax.broadcasted_iota(jnp.int32, (NT, LANES), 1) * COMB_ROWS
    s0_ref[...] = _dot((slot >= 0.0).astype(F32).astype(BF16), (tok < edge).astype(F32).astype(BF16))


def _route_tables(logits_t):
    return pl.pallas_call(
        _route_kernel,
        out_shape=[jax.ShapeDtypeStruct((N_EXPERTS, NT), F32), jax.ShapeDtypeStruct((N_EXPERTS, NT), F32),
                   jax.ShapeDtypeStruct((NT, LANES), F32), jax.ShapeDtypeStruct((N_EXPERTS, LANES), F32)],
        compiler_params=pltpu.CompilerParams(vmem_limit_bytes=VMEM_LIMIT),
    )(logits_t)


def _route(logits_t, h2, x, g, mod, layer, w_gate, w_up, w_down):
    aff, slot, slot_t, s0 = _route_tables(logits_t)
    cap = EC_CAPACITY * NT // N_EXPERTS
    s0 = s0[:, :NT // COMB_ROWS + 1].astype(jnp.int32)
    a0 = jnp.minimum(s0[:, :-1] // BF16_SUB * BF16_SUB, cap - COMB_WIN)
    y = _expert_ffn(a0, s0[:, 1:], slot, aff, h2, layer, w_gate, w_up, w_down)
    return _combine(a0, s0[:, 1:], slot_t, y, x, g, mod, layer)


def kernel(x_prompt, x_sample, cache_b_k, cache_b_v, cache_c_k, cache_c_v, c, c_ctx, ada_w, ada_b, norm_g,
           ab_w_in, hy_short_w, hy_short_b, hy_w1, hy_b1, hy_w2, hy_b2, hy_freq, hy_w3, hy_log_decay, hy_bias,
           b_sink, ab_w_out, c_w_qkv, c_q_norm, c_k_norm, c_w_out, ec_router, ec_w_gate, ec_w_up, ec_w_down):
    x = jnp.concatenate([x_sample.reshape(NS, D_MODEL), x_prompt.reshape(NP, D_MODEL)], axis=0)
    cond = jnp.concatenate([c, c_ctx[None], jnp.zeros((MOD_ROWS - DEC_BATCH - 1, D_MODEL), F32)], axis=0)
    mod = _adaln(cond, ada_w, ada_b).reshape(DEPTH * MOD_ROWS * 6, 1, D_MODEL)
    rope = _rope_tables(DEC_SEQ)
    hd = HEAD_DIM

    e = 0
    p_hy, p_at = _norm_mod_matmul(x, norm_g[0, 0], mod, 0, ab_w_in[e].astype(BF16), (3 * HY_CH, D_MODEL - HY_CH + 2 * B_KV * hd))
    ya = []
    for L, nb, row_blk0 in ((DEC_SEQ, DEC_BATCH, 0), (SEQ, BATCH, NS // SEQ)):
        table = _dft_table(L)
        h = _hyena_filters(L, hy_w1[e], hy_b1[e], hy_w2[e], hy_b2[e], hy_freq[e], hy_w3[e], hy_log_decay[e])
        h = h.at[0, HY_CH:].set(0.0)
        spec = _filter_spectrum(L, table, h)
        ya.append(_hyena(p_hy, hy_short_w[e], hy_short_b[e].reshape(1, -1), hy_bias[e].reshape(1, -1),
                         table.astype(BF16), table.T.astype(BF16), spec, L, nb, row_blk0))
    ya = jnp.concatenate(ya, axis=0)
    ctx_b = (cache_b_k[:, e].reshape(DEC_BATCH, PAST_LEN, B_KV * hd), cache_b_v[:, e].reshape(DEC_BATCH, PAST_LEN, B_KV * hd))
    call, args = _attention(p_at, B_HEADS * hd, 2, L=DEC_SEQ, Tq=256, nb=DEC_BATCH, row0=0, H=B_HEADS, KVH=B_KV,
                            ctx=ctx_b, sink=b_sink[e], rope=rope, window=True)
    yb_s = call(*args)[0]
    call, args = _attention(p_at, B_HEADS * hd, 2, L=SEQ, Tq=256, nb=BATCH, row0=NS, H=B_HEADS, KVH=B_KV,
                            sink=b_sink[e])
    yb_p = call(*args)[0]
    yb = jnp.concatenate([yb_s, yb_p], axis=0)
    state_b_k = p_at[NS:, B_HEADS * hd:(B_HEADS + B_KV) * hd].reshape(BATCH, 1, SEQ, B_KV, hd)
    state_b_v = p_at[NS:, (B_HEADS + B_KV) * hd:].reshape(BATCH, 1, SEQ, B_KV, hd)
    x, h2, lg = _out_proj(ya, 0, yb, 0, ab_w_out[e].astype(BF16), x, norm_g[0, 1], norm_g[0, 2], mod, 0, ec_router[0].T)
    x = _route(lg, h2, x, norm_g[0, 3], mod, 0, ec_w_gate, ec_w_up, ec_w_down)

    o = 0
    (p,) = _norm_mod_matmul(x, norm_g[1, 0], mod, 1, c_w_qkv[o].astype(BF16), ((C_HEADS + 2 * C_KV) * hd,))
    ctx_c = (cache_c_k[:, o].reshape(DEC_BATCH, PAST_LEN, C_KV * hd), cache_c_v[:, o].reshape(DEC_BATCH, PAST_LEN, C_KV * hd))
    nrm = (c_q_norm[o], c_k_norm[o])
    call, args = _attention(p, C_HEADS * hd, 2, L=DEC_SEQ, Tq=128, nb=DEC_BATCH, row0=0, H=C_HEADS, KVH=C_KV,
                            ctx=ctx_c, norm=nrm, rope=rope)
    yc_s = call(*args)[0]
    call, args = _attention(p, C_HEADS * hd, 2, L=SEQ, Tq=256, nb=BATCH, row0=NS, H=C_HEADS, KVH=C_KV,
                            norm=nrm, emit_k=True)
    yc_p, kn = call(*args)
    yc = jnp.concatenate([yc_s, yc_p], axis=0)
    state_c_k = kn.reshape(BATCH, 1, SEQ, C_KV, hd)
    state_c_v = p[NS:, (C_HEADS + C_KV) * hd:].reshape(BATCH, 1, SEQ, C_KV, hd)
    x, h2, lg = _out_proj(yc, 0, yc, 1, c_w_out[o].astype(BF16), x, norm_g[1, 1], norm_g[1, 2], mod, 1, ec_router[1].T)
    x = _route(lg, h2, x, norm_g[1, 3], mod, 1, ec_w_gate, ec_w_up, ec_w_down)

    y_sample = x[:NS].reshape(DEC_BATCH, DEC_SEQ, D_MODEL)
    y_prompt = x[NS:].reshape(BATCH, SEQ, D_MODEL)
    return (y_prompt, y_sample, state_b_k, state_b_v, state_c_k, state_c_v)
```

```python
import functools
import math

import jax
import jax.numpy as jnp
from jax import lax
from jax.experimental import pallas as pl
from jax.experimental.pallas import tpu as pltpu

F32 = jnp.float32
BF16 = jnp.bfloat16

D_MODEL = 1024
BATCH = 16
SEQ = 256
DEPTH = 2
DEC_BATCH = 8
DEC_SEQ = 1024
PAST_LEN = 512
GRID_W = 64
HEAD_DIM = 64
HY_CH = D_MODEL // 2
HY_BANDS = 8
B_HEADS = 8
B_KV = 2
WINDOW = 128
C_HEADS = 16
C_KV = 4
ROPE_BASE = 10000.0
N_EXPERTS = 16
EC_CAPACITY = 2
EXPERT_FF = 2 * D_MODEL
EPS = 1e-6
NEG_INF = -1e30

NS = DEC_BATCH * DEC_SEQ
NP = BATCH * SEQ
NT = NS + NP
MOD_ROWS = 16
LANES = 128
VMEM_LIMIT = 56 * 1024 * 1024


def _cparams(sem):
    return pltpu.CompilerParams(dimension_semantics=sem, vmem_limit_bytes=VMEM_LIMIT)


def _split(x):
    hi = x.astype(BF16)
    lo = (x - hi.astype(F32)).astype(BF16)
    return hi, lo


_NN = (((1,), (0,)), ((), ()))
_NT = (((1,), (1,)), ((), ()))


def _dot(a, b, dims=_NN):
    return lax.dot_general(a, b, dims, preferred_element_type=F32)


def _dot3(a, b, dims=_NN):
    ah, al = _split(a)
    bh, bl = _split(b)
    return _dot(ah, bh, dims) + _dot(al, bh, dims) + _dot(ah, bl, dims)


def _rms(x, g):
    return x * lax.rsqrt(jnp.mean(x * x, axis=-1, keepdims=True) + EPS) * g


def _mod_row(tile_rows):
    def f(i):
        return jnp.minimum(i * tile_rows // DEC_SEQ, DEC_BATCH)
    return f


def _adaln_kernel(c_ref, w_ref, b_ref, o_ref):
    s = jax.nn.silu(c_ref[...])
    o_ref[0] = _dot3(s, w_ref[0]) + b_ref[0]


def _adaln(cond, ada_w, ada_b):
    tn = 1536
    return pl.pallas_call(
        _adaln_kernel,
        out_shape=jax.ShapeDtypeStruct((DEPTH, MOD_ROWS, 6 * D_MODEL), F32),
        grid=(DEPTH, 6 * D_MODEL // tn),
        in_specs=[pl.BlockSpec((MOD_ROWS, D_MODEL), lambda l, j: (0, 0)),
                  pl.BlockSpec((1, D_MODEL, tn), lambda l, j: (l, 0, j)),
                  pl.BlockSpec((1, 1, tn), lambda l, j: (l, 0, j))],
        out_specs=pl.BlockSpec((1, MOD_ROWS, tn), lambda l, j: (l, 0, j)),
        compiler_params=_cparams(("arbitrary", "arbitrary")),
    )(cond, ada_w, ada_b.reshape(DEPTH, 1, 6 * D_MODEL))


def _mod_spec(layer, which, tile_rows):
    row = _mod_row(tile_rows)
    return pl.BlockSpec((1, 1, D_MODEL), lambda i, *_: ((layer * MOD_ROWS + row(i)) * 6 + which, 0, 0))


ROW_TILE = 512
NS_TILES = NS // ROW_TILE


def _stream_specs(width, col=0):
    return [pl.BlockSpec((ROW_TILE, width), lambda i, *_: (jnp.minimum(i, NS_TILES - 1), col)),
            pl.BlockSpec((ROW_TILE, width), lambda i, *_: (jnp.maximum(i - NS_TILES, 0), col))]


def _row_specs(a, width, col=0):
    if isinstance(a, tuple):
        return _stream_specs(width, col), list(a)
    return [pl.BlockSpec((ROW_TILE, width), lambda i, *_: (i, col))], [a]


def _take(refs, paired):
    if paired:
        return jnp.where(pl.program_id(0) < NS_TILES, refs[0][...], refs[1][...]), refs[2:]
    return refs[0][...], refs[1:]


def _nmm_kernel(*refs, paired):
    x, (g_ref, sh_ref, sc_ref, w_ref, *o_refs) = _take(refs, paired)
    h = _rms(x, g_ref[...]) * (1.0 + sc_ref[0]) + sh_ref[0]
    p = _dot(h.astype(BF16), w_ref[...])
    off = 0
    for o in o_refs:
        n = o.shape[1]
        o[...] = p[:, off:off + n]
        off += n


def _norm_mod_matmul(x, g, mod, layer, w_bf, splits):
    tm = ROW_TILE
    n_out = w_bf.shape[1]
    x_specs, x_args = _row_specs(x, D_MODEL)
    return pl.pallas_call(
        functools.partial(_nmm_kernel, paired=isinstance(x, tuple)),
        out_shape=[jax.ShapeDtypeStruct((NT, n), F32) for n in splits],
        grid=(NT // tm,),
        in_specs=x_specs + [pl.BlockSpec((1, D_MODEL), lambda i: (0, 0)),
                            _mod_spec(layer, 0, tm), _mod_spec(layer, 1, tm),
                            pl.BlockSpec((D_MODEL, n_out), lambda i: (0, 0))],
        out_specs=[pl.BlockSpec((tm, n), lambda i: (i, 0)) for n in splits],
        compiler_params=_cparams(("arbitrary",)),
    )(*x_args, g.reshape(1, D_MODEL), mod, mod, w_bf)


def _hyena_filters(L, w1, b1, w2, b2, freq, w3, log_decay):
    hp = lax.Precision.HIGHEST
    t = jnp.linspace(0.0, 1.0, L, dtype=F32)[:, None]
    w = (2.0 * math.pi / L) * jnp.arange(L, dtype=F32)[:, None]
    bands = jnp.linspace(1e-4, HY_BANDS - 1, HY_BANDS, dtype=F32)[None, :]
    z = jnp.concatenate([t, jnp.cos(bands * w), -jnp.sin(bands * w)], axis=-1)
    h = jnp.sin(freq[0] * (jnp.dot(z, w1, precision=hp) + b1))
    h = jnp.sin(freq[1] * (jnp.dot(h, w2, precision=hp) + b2))
    return jnp.dot(h, w3, precision=hp) * jnp.exp(-t * jnp.exp(log_decay))


def _dft_table(L):
    n = 2 * L
    f = jnp.arange(L, dtype=jnp.int32)[:, None]
    t = jnp.arange(L, dtype=jnp.int32)[None, :]
    ang = ((f * t) % n).astype(F32) * (2.0 * math.pi / n)
    nyq = jnp.where(t % 2 == 0, 1.0, -1.0).astype(F32)
    fs = jnp.where(f == 0, nyq, -jnp.sin(ang))
    return jnp.concatenate([jnp.cos(ang), fs], axis=0)


def _filt_kernel(f_ref, hf_ref, hb_ref, p_ref, q_ref, s_ref, *, L):
    hf = hf_ref[...]
    hb = hb_ref[...]
    ct = hf.shape[1]
    nrm = lax.rsqrt(jnp.sum(hf * hf, axis=0, keepdims=True) + jnp.sum(hb * hb, axis=0, keepdims=True) + EPS)
    fh = _dot3(f_ref[...], jnp.concatenate([hf, hb], axis=1))
    row = lax.broadcasted_iota(jnp.int32, (L, ct), 0)
    w = jnp.where(row == 0, 1.0 / (2 * L), 2.0 / (2 * L)) * nrm
    re = fh[:L, :ct] + fh[:L, ct:]
    p_ref[...] = re * w
    q_ref[...] = jnp.where(row == 0, 0.0, fh[L:, :ct] - fh[L:, ct:]) * w
    s_ref[...] = jnp.where(row == 0, fh[L:, :ct] + fh[L:, ct:], re) * w


def _filter_spectrum(L, table, h):
    ct = 128
    nc = HY_CH // ct
    shp = jax.ShapeDtypeStruct((L, HY_CH), F32)
    return pl.pallas_call(
        functools.partial(_filt_kernel, L=L),
        out_shape=[shp, shp, shp],
        grid=(nc,),
        in_specs=[pl.BlockSpec((2 * L, L), lambda c: (0, 0)),
                  pl.BlockSpec((L, ct), lambda c: (0, c)),
                  pl.BlockSpec((L, ct), lambda c: (0, nc + c))],
        out_specs=[pl.BlockSpec((L, ct), lambda c: (0, c))] * 3,
        compiler_params=_cparams(("arbitrary",)),
    )(table, h, h)


def _hyena_kernel(x0_ref, x1_ref, v_ref, w0_ref, w1_ref, wv_ref, b0_ref, b1_ref, bv_ref, bias_ref,
                  f_ref, g_ref, p_ref, q_ref, s_ref, o_ref, *, L):
    ct = x0_ref.shape[1]
    row = lax.broadcasted_iota(jnp.int32, (L, ct), 0)

    def sconv(x_ref, w_ref, b_ref):
        x = x_ref[...]
        xm = jnp.where(row == 0, 0.0, pltpu.roll(x, 1, 0))
        xp = jnp.where(row == L - 1, 0.0, pltpu.roll(x, L - 1, 0))
        return xm * w_ref[0:1, :] + x * w_ref[1:2, :] + xp * w_ref[2:3, :] + b_ref[...]

    z = sconv(v_ref, wv_ref, bv_ref) * sconv(x1_ref, w1_ref, b1_ref)
    zf = _dot(f_ref[...], z.astype(BF16))
    a = zf[:L]
    b = zf[L:]
    p = p_ref[...]
    q = q_ref[...]
    yre = a * p - b * q
    yim = a * q + b * s_ref[...]
    ycat = jnp.concatenate([yre, yim], axis=0).astype(BF16)
    y = _dot(g_ref[...], ycat) + z * bias_ref[...]
    o_ref[...] = sconv(x0_ref, w0_ref, b0_ref) * y


def _hyena(p_hy, short_w, short_b, bias, table_bf, table_t_bf, spec, L, nb, row_blk0):
    ct = 256
    nc = HY_CH // ct
    P, Q, S = spec

    def xs(k):
        return pl.BlockSpec((L, ct), lambda b, c: (row_blk0 + b, k * nc + c))

    def ws(rows, k):
        return pl.BlockSpec((rows, ct), lambda b, c: (0, k * nc + c))

    cs = pl.BlockSpec((L, ct), lambda b, c: (0, c))
    return pl.pallas_call(
        functools.partial(_hyena_kernel, L=L),
        out_shape=jax.ShapeDtypeStruct((nb * L, HY_CH), F32),
        grid=(nb, nc),
        in_specs=[xs(0), xs(1), xs(2), ws(3, 0), ws(3, 1), ws(3, 2), ws(1, 0), ws(1, 1), ws(1, 2), ws(1, 0),
                  pl.BlockSpec((2 * L, L), lambda b, c: (0, 0)),
                  pl.BlockSpec((L, 2 * L), lambda b, c: (0, 0)),
                  cs, cs, cs],
        out_specs=pl.BlockSpec((L, ct), lambda b, c: (b, c)),
        compiler_params=_cparams(("arbitrary", "arbitrary")),
    )(p_hy, p_hy, p_hy, short_w, short_w, short_w, short_b, short_b, short_b, bias,
      table_bf, table_t_bf, P, Q, S)


def _rope_tables(L):
    half = HEAD_DIM // 2
    pos = jnp.arange(L)
    r = (pos // GRID_W).astype(F32)
    col = (pos % GRID_W).astype(F32)
    inv = ROPE_BASE ** (-jnp.arange(0, half, 2, dtype=F32) / half)
    ar = r[:, None] * inv[None, :]
    ac = col[:, None] * inv[None, :]
    z = jnp.zeros_like(ar)
    cos = jnp.concatenate([jnp.cos(ar), jnp.cos(ar), jnp.cos(ac), jnp.cos(ac)], axis=1)
    s1 = jnp.concatenate([-jnp.sin(ar), z, -jnp.sin(ac), z], axis=1)
    s2 = jnp.concatenate([z, jnp.sin(ar), z, jnp.sin(ac)], axis=1)
    rep = LANES // HEAD_DIM
    return jnp.tile(cos, (1, rep)), jnp.tile(s1, (1, rep)), jnp.tile(s2, (1, rep))


def _attn_kernel(*refs, L, Tq, H, KVH, Lc, window, use_sink, use_norm, use_rope, emit_k):
    G = H // KVH
    hd = HEAD_DIM
    pad = WINDOW if window else 0
    it = iter(refs)
    q_ref = next(it)
    kv_ref = next(it)
    ck_ref = cv_ref = sink_ref = qn_ref = kn_ref = bd_ref = rc_ref = rs1_ref = rs2_ref = ko_ref = None
    if Lc:
        ck_ref = next(it)
        cv_ref = next(it)
    if use_sink:
        sink_ref = next(it)
    if use_norm:
        qn_ref = next(it)
        kn_ref = next(it)
        bd_ref = next(it)
    if use_rope:
        rc_ref = next(it)
        rs1_ref = next(it)
        rs2_ref = next(it)
    o_ref = next(it)
    if emit_k:
        ko_ref = next(it)
    k_scr = next(it)
    v_scr = next(it)
    q_scr = next(it)
    i = pl.program_id(1)
    lat0 = Lc + pad

    def head_norm(x, g_ref):
        hi, lo = _split(x * x)
        ms = _dot(hi, bd_ref[...]) + _dot(lo, bd_ref[...])
        return x * lax.rsqrt(ms + EPS) * g_ref[...]

    def rope(x, rows):
        return (x * rc_ref[rows, :] + pltpu.roll(x, LANES - 16, 1) * rs1_ref[rows, :]
                + pltpu.roll(x, 16, 1) * rs2_ref[rows, :])

    @pl.when(i == 0)
    def _():
        if window:
            zpad = jnp.zeros((KVH, pad, hd), BF16)
            k_scr[:, Lc:Lc + pad, :] = zpad
            k_scr[:, lat0 + L:lat0 + L + pad, :] = zpad
            v_scr[:, Lc:Lc + pad, :] = zpad
            v_scr[:, lat0 + L:lat0 + L + pad, :] = zpad
        for g in range(KVH * hd // LANES):
            kg = kv_ref[:, LANES * g:LANES * (g + 1)]
            if use_norm:
                kg = head_norm(kg, kn_ref)
                if emit_k:
                    ko_ref[:, LANES * g:LANES * (g + 1)] = kg
            if use_rope:
                kg = rope(kg, slice(None))
            kb = kg.astype(BF16)
            vb = kv_ref[:, KVH * hd + LANES * g:KVH * hd + LANES * (g + 1)].astype(BF16)
            for hh in range(LANES // hd):
                k_scr[2 * g + hh, lat0:lat0 + L, :] = kb[:, hd * hh:hd * (hh + 1)]
                v_scr[2 * g + hh, lat0:lat0 + L, :] = vb[:, hd * hh:hd * (hh + 1)]
        if Lc:
            ckb = ck_ref[0].astype(BF16)
            cvb = cv_ref[0].astype(BF16)
            for kvh in range(KVH):
                k_scr[kvh, 0:Lc, :] = ckb[:, hd * kvh:hd * (kvh + 1)]
                v_scr[kvh, 0:Lc, :] = cvb[:, hd * kvh:hd * (kvh + 1)]

    rows = pl.ds(pl.multiple_of(i * Tq, Tq), Tq)
    for g in range(H * hd // LANES):
        qg = q_ref[:, LANES * g:LANES * (g + 1)]
        if use_norm:
            qg = head_norm(qg, qn_ref)
        if use_rope:
            qg = rope(qg, rows)
        qb = (qg * (hd ** -0.5)).astype(BF16)
        for hh in range(LANES // hd):
            q_scr[2 * g + hh] = qb[:, hd * hh:hd * (hh + 1)]

    hrow = lax.broadcasted_iota(jnp.int32, (G * Tq, 1), 0) // Tq
    for kvh in range(KVH):
        qs = q_scr[kvh * G:(kvh + 1) * G].reshape(G * Tq, hd)
        if use_sink:
            sk = jnp.zeros((G * Tq, 1), F32)
            for g in range(G):
                sk = jnp.where(hrow == g, sink_ref[0, kvh * G + g], sk)
        if window:
            W = Tq + 2 * pad
            kc = k_scr[kvh, 0:Lc, :]
            vc = v_scr[kvh, 0:Lc, :]
            wrows = pl.ds(pl.multiple_of(Lc + i * Tq, Tq), W)
            kw = k_scr[kvh, wrows, :]
            vw = v_scr[kvh, wrows, :]
            s_c = _dot(qs, kc, _NT)
            s_w = _dot(qs, kw, _NT).reshape(G, Tq, W)
            r = lax.broadcasted_iota(jnp.int32, (Tq, W), 0)
            c = lax.broadcasted_iota(jnp.int32, (Tq, W), 1)
            kpos = i * Tq - pad + c
            valid = (kpos >= 0) & (kpos < L) & (jnp.abs(r + pad - c) <= WINDOW)
            s_w = jnp.where(valid[None], s_w, NEG_INF).reshape(G * Tq, W)
            m = jnp.maximum(jnp.max(s_c, axis=-1, keepdims=True), jnp.max(s_w, axis=-1, keepdims=True))
            if use_sink:
                m = jnp.maximum(m, sk)
            e_c = jnp.exp(s_c - m)
            e_w = jnp.exp(s_w - m)
            den = jnp.sum(e_c, axis=-1, keepdims=True) + jnp.sum(e_w, axis=-1, keepdims=True)
            o = _dot(e_c.astype(BF16), vc) + _dot(e_w.astype(BF16), vw)
        else:
            s = _dot(qs, k_scr[kvh], _NT)
            m = jnp.max(s, axis=-1, keepdims=True)
            if use_sink:
                m = jnp.maximum(m, sk)
            e = jnp.exp(s - m)
            den = jnp.sum(e, axis=-1, keepdims=True)
            o = _dot(e.astype(BF16), v_scr[kvh])
        if use_sink:
            den = den + jnp.exp(sk - m)
        o = o / den
        for g in range(G):
            h = kvh * G + g
            o_ref[:, hd * h:hd * (h + 1)] = o[g * Tq:(g + 1) * Tq]


def _attention(p, q_cols, kv_colblk, *, L, Tq, nb, row0, H, KVH, ctx=None, sink=None, norm=None, rope=None,
               window=False, emit_k=False):
    hd = HEAD_DIM
    nq = L // Tq
    Lc = 0 if ctx is None else ctx[0].shape[1]
    pad = WINDOW if window else 0
    Lk = Lc + L + 2 * pad
    kvw = 2 * KVH * hd
    args = [p, p]
    in_specs = [pl.BlockSpec((Tq, q_cols), lambda b, i: (row0 // Tq + b * nq + i, 0)),
                pl.BlockSpec((L, kvw), lambda b, i: (row0 // L + b, kv_colblk))]
    if ctx is not None:
        for a in ctx:
            args.append(a)
            in_specs.append(pl.BlockSpec((1, Lc, KVH * hd), lambda b, i: (b, 0, 0)))
    if sink is not None:
        args.append(sink.reshape(1, H))
        in_specs.append(pl.BlockSpec(memory_space=pltpu.SMEM))
    if norm is not None:
        bd = (jnp.arange(LANES)[:, None] // hd == jnp.arange(LANES)[None, :] // hd).astype(F32) / hd
        for a in (jnp.tile(norm[0], LANES // hd).reshape(1, LANES), jnp.tile(norm[1], LANES // hd).reshape(1, LANES)):
            args.append(a)
            in_specs.append(pl.BlockSpec((1, LANES), lambda b, i: (0, 0)))
        args.append(bd.astype(BF16))
        in_specs.append(pl.BlockSpec((LANES, LANES), lambda b, i: (0, 0)))
    if rope is not None:
        for a in rope:
            args.append(a)
            in_specs.append(pl.BlockSpec((L, LANES), lambda b, i: (0, 0)))
    out_shape = [jax.ShapeDtypeStruct((nb * L, H * hd), F32)]
    out_specs = [pl.BlockSpec((Tq, H * hd), lambda b, i: (b * nq + i, 0))]
    if emit_k:
        out_shape.append(jax.ShapeDtypeStruct((nb * L, KVH * hd), F32))
        out_specs.append(pl.BlockSpec((L, KVH * hd), lambda b, i: (b, 0)))
    kern = functools.partial(_attn_kernel, L=L, Tq=Tq, H=H, KVH=KVH, Lc=Lc, window=window,
                             use_sink=sink is not None, use_norm=norm is not None,
                             use_rope=rope is not None, emit_k=emit_k)
    call = pl.pallas_call(
        kern, out_shape=out_shape, grid=(nb, nq), in_specs=in_specs, out_specs=out_specs,
        scratch_shapes=[pltpu.VMEM((KVH, Lk, hd), BF16), pltpu.VMEM((KVH, Lk, hd), BF16),
                        pltpu.VMEM((H, Tq, hd), BF16)],
        compiler_params=_cparams(("arbitrary", "arbitrary")))
    return call, args


def _oproj_kernel(*refs, paired):
    al, refs = _take(refs, paired[0])
    ar, refs = _take(refs, paired[1])
    x, refs = _take(refs, paired[2])
    w_ref, g1_ref, gate_ref, g2_ref, sh_ref, sc_ref, wr_ref, x1_ref, h2_ref, lg_ref = refs
    k = al.shape[1]
    y = _dot(al.astype(BF16), w_ref[0:k, :]) + _dot(ar.astype(BF16), w_ref[k:2 * k, :])
    x1 = x + gate_ref[0] * _rms(y, g1_ref[...])
    x1_ref[...] = x1
    h2 = _rms(x1, g2_ref[...]) * (1.0 + sc_ref[0]) + sh_ref[0]
    h2_ref[...] = h2.astype(BF16)
    lg_ref[...] = _dot3(wr_ref[...], h2, _NT)


def _out_proj(a_l, l_blk, a_r, r_blk, w_bf, x, g1, g2, mod, layer, wr_t):
    tm = ROW_TILE
    half = D_MODEL // 2
    l_specs, l_args = _row_specs(a_l, half, l_blk)
    r_specs, r_args = _row_specs(a_r, half, r_blk)
    x_specs, x_args = _row_specs(x, D_MODEL)
    paired = tuple(isinstance(a, tuple) for a in (a_l, a_r, x))
    return pl.pallas_call(
        functools.partial(_oproj_kernel, paired=paired),
        out_shape=[jax.ShapeDtypeStruct((NT, D_MODEL), F32), jax.ShapeDtypeStruct((NT, D_MODEL), BF16),
                   jax.ShapeDtypeStruct((N_EXPERTS, NT), F32)],
        grid=(NT // tm,),
        in_specs=l_specs + r_specs + x_specs + [
            pl.BlockSpec((D_MODEL, D_MODEL), lambda i: (0, 0)),
            pl.BlockSpec((1, D_MODEL), lambda i: (0, 0)),
            _mod_spec(layer, 2, tm),
            pl.BlockSpec((1, D_MODEL), lambda i: (0, 0)),
            _mod_spec(layer, 3, tm), _mod_spec(layer, 4, tm),
            pl.BlockSpec((N_EXPERTS, D_MODEL), lambda i: (0, 0))],
        out_specs=[pl.BlockSpec((tm, D_MODEL), lambda i: (i, 0)),
                   pl.BlockSpec((tm, D_MODEL), lambda i: (i, 0)),
                   pl.BlockSpec((N_EXPERTS, tm), lambda i: (0, i))],
        compiler_params=_cparams(("arbitrary",)),
    )(*l_args, *r_args, *x_args, w_bf, g1.reshape(1, D_MODEL), mod, g2.reshape(1, D_MODEL), mod, mod, wr_t)


COMB_ROWS = 512
COMB_WIN = 128
BF16_SUB = 16


def _ffn_kernel(a0_ref, s1_ref, slot_ref, aff_ref, h_ref, wg_ref, wu_ref, wd_ref, y_ref,
                xn_scr, gn_scr, xc_scr, gc_scr, acc_scr, *, cap, nf, nblk):
    grp = pl.program_id(0)
    f = pl.program_id(1)
    R = COMB_WIN
    per_step = nblk // nf

    def add_window(jb, c0, hit_of):
        cols = slice(jb * COMB_ROWS, (jb + 1) * COMB_ROWS)
        q = lax.broadcasted_iota(jnp.int32, (R, COMB_ROWS), 0)
        hit = hit_of(slot_ref[0, :, cols].astype(jnp.int32), q)
        rows = pl.ds(c0, R)
        piece = _dot(hit.astype(F32).astype(BF16), h_ref[cols, :])
        xn_scr[rows, :] += piece.astype(BF16)
        gn_scr[rows, :] += jnp.sum(jnp.where(hit, aff_ref[0, :, cols], 0.0), axis=1, keepdims=True)

    def window_start(jb):
        return pl.multiple_of(a0_ref[grp * nblk + f * per_step + jb], BF16_SUB)

    def gather():
        for jb in range(per_step):
            a0 = window_start(jb)
            add_window(jb, a0, lambda srow, q, a0=a0: srow - a0 == q)

    def gather_more():
        for jb in range(per_step):
            end = s1_ref[grp * nblk + f * per_step + jb]

            def extra(c, jb=jb):
                c0 = pl.multiple_of(jnp.minimum(c, cap - R), BF16_SUB)
                add_window(jb, c0, lambda srow, q: (srow - c0 == q) & (q + c0 >= c))
                return c + R

            lax.while_loop(lambda c, end=end: c < end, extra, window_start(jb) + R)

    def compute():
        wg = wg_ref[0, 0].astype(BF16)
        wu = wu_ref[0, 0].astype(BF16)
        wd = wd_ref[0, 0].astype(BF16)
        part_rows = cap // 4
        for r in range(4):
            rows = slice(r * part_rows, (r + 1) * part_rows)
            xe = xc_scr[rows, :]
            hmid = (jax.nn.silu(_dot(xe, wg)) * _dot(xe, wu)).astype(BF16)
            acc_scr[rows, :] += _dot(hmid, wd)

    @pl.when(f == 0)
    def _():
        xn_scr[...] = jnp.zeros_like(xn_scr)
        gn_scr[...] = jnp.zeros_like(gn_scr)
        acc_scr[...] = jnp.zeros_like(acc_scr)

    @pl.when(grp == 0)
    def _():
        gather()

    @pl.when((grp > 0) & (grp < N_EXPERTS))
    def _():
        gather()
        compute()

    @pl.when(grp == N_EXPERTS)
    def _():
        compute()

    @pl.when(grp < N_EXPERTS)
    def _():
        gather_more()

    @pl.when(f == nf - 1)
    def _():
        @pl.when(grp > 0)
        def _():
            y_ref[...] = (acc_scr[...] * gc_scr[...]).astype(BF16)

        xc_scr[...] = xn_scr[...]
        gc_scr[...] = gn_scr[...]


def _expert_ffn(a0, s1, slot, aff, h2, layer, w_gate, w_up, w_down):
    cap = EC_CAPACITY * NT // N_EXPERTS
    fc = 512
    nf = EXPERT_FF // fc
    nblk = NT // COMB_ROWS
    tok = NT // nf

    def prev(g):
        return jnp.maximum(g - 1, 0)

    def this(g):
        return jnp.minimum(g, N_EXPERTS - 1)

    grid_spec = pltpu.PrefetchScalarGridSpec(
        num_scalar_prefetch=2,
        grid=(N_EXPERTS + 1, nf),
        in_specs=[pl.BlockSpec((1, 1, tok), lambda g, f, *_: (this(g), 0, f)),
                  pl.BlockSpec((1, 1, tok), lambda g, f, *_: (this(g), 0, f)),
                  pl.BlockSpec((tok, D_MODEL), lambda g, f, *_: (f, 0)),
                  pl.BlockSpec((1, 1, D_MODEL, fc), lambda g, f, *_: (layer, prev(g), 0, f)),
                  pl.BlockSpec((1, 1, D_MODEL, fc), lambda g, f, *_: (layer, prev(g), 0, f)),
                  pl.BlockSpec((1, 1, fc, D_MODEL), lambda g, f, *_: (layer, prev(g), f, 0))],
        out_specs=pl.BlockSpec((cap, D_MODEL), lambda g, f, *_: (prev(g), 0)),
        scratch_shapes=[pltpu.VMEM((cap, D_MODEL), BF16), pltpu.VMEM((cap, 1), F32),
                        pltpu.VMEM((cap, D_MODEL), BF16), pltpu.VMEM((cap, 1), F32),
                        pltpu.VMEM((cap, D_MODEL), F32)])
    return pl.pallas_call(
        functools.partial(_ffn_kernel, cap=cap, nf=nf, nblk=nblk),
        out_shape=jax.ShapeDtypeStruct((N_EXPERTS * cap, D_MODEL), BF16),
        grid_spec=grid_spec,
        compiler_params=_cparams(("arbitrary", "arbitrary")),
    )(a0.reshape(-1), s1.reshape(-1), slot.reshape(N_EXPERTS, 1, NT), aff.reshape(N_EXPERTS, 1, NT), h2,
      w_gate, w_up, w_down)


def _combine_kernel(a0_ref, s1_ref, slot_ref, y_hbm, x_ref, g_ref, gate_ref, *rest, cap, nblk):
    *o_refs, win_scr, ovf_scr, f_scr, sem, osem = rest
    j = pl.program_id(0)
    E, R, Tb = N_EXPERTS, COMB_WIN, COMB_ROWS

    def window(e, start, dst, s):
        return pltpu.make_async_copy(y_hbm.at[pl.ds(e * cap + start, R)], dst, s)

    def issue(blk, slot):
        for e in range(E):
            a0 = pl.multiple_of(a0_ref[e * nblk + blk], BF16_SUB)
            window(e, a0, win_scr.at[slot, pl.ds(e * R, R)], sem.at[slot]).start()

    @pl.when(j == 0)
    def _():
        issue(0, 0)

    slot = j % 2
    pltpu.make_async_copy(y_hbm.at[pl.ds(0, E * R)], win_scr.at[slot], sem.at[slot]).wait()

    @pl.when(j + 1 < nblk)
    def _():
        issue(j + 1, 1 - slot)

    sl = slot_ref[...].astype(jnp.int32)
    q = lax.broadcasted_iota(jnp.int32, (Tb, R), 1)
    onehot = jnp.concatenate(
        [(sl[:, e:e + 1] - a0_ref[e * nblk + j] == q) for e in range(E)], axis=1).astype(F32).astype(BF16)
    f_scr[...] = _dot(onehot, win_scr[slot])

    for e in range(E):
        end = s1_ref[e * nblk + j]

        def more(c):
            return c < end

        def extra(c, e=e):
            c0 = pl.multiple_of(jnp.minimum(c, cap - R), BF16_SUB)
            cp = window(e, c0, ovf_scr, osem)
            cp.start()
            cp.wait()
            hit = (sl[:, e:e + 1] - c0 == q) & (q + c0 >= c)
            f_scr[...] += _dot(hit.astype(F32).astype(BF16), ovf_scr[...])
            return c + R

        lax.while_loop(more, extra, a0_ref[e * nblk + j] + R)

    res = x_ref[...] + gate_ref[0] * _rms(f_scr[...], g_ref[...])
    if len(o_refs) == 1:
        o_refs[0][...] = res
    else:
        @pl.when(j < NS_TILES)
        def _():
            o_refs[0][...] = res

        @pl.when(j >= NS_TILES)
        def _():
            o_refs[1][...] = res


def _combine(a0, s1, slot_t, y, x, g, mod, layer, split_out=False):
    cap = y.shape[0] // N_EXPERTS
    nblk = NT // COMB_ROWS
    if split_out:
        out_shape = [jax.ShapeDtypeStruct((NS, D_MODEL), F32), jax.ShapeDtypeStruct((NP, D_MODEL), F32)]
        out_specs = _stream_specs(D_MODEL)
    else:
        out_shape = jax.ShapeDtypeStruct((NT, D_MODEL), F32)
        out_specs = pl.BlockSpec((COMB_ROWS, D_MODEL), lambda i, *_: (i, 0))
    grid_spec = pltpu.PrefetchScalarGridSpec(
        num_scalar_prefetch=2,
        grid=(nblk,),
        in_specs=[pl.BlockSpec((COMB_ROWS, LANES), lambda i, *_: (i, 0)),
                  pl.BlockSpec(memory_space=pl.ANY),
                  pl.BlockSpec((COMB_ROWS, D_MODEL), lambda i, *_: (i, 0)),
                  pl.BlockSpec((1, D_MODEL), lambda i, *_: (0, 0)),
                  _mod_spec(layer, 5, COMB_ROWS)],
        out_specs=out_specs,
        scratch_shapes=[pltpu.VMEM((2, N_EXPERTS * COMB_WIN, D_MODEL), BF16),
                        pltpu.VMEM((COMB_WIN, D_MODEL), BF16),
                        pltpu.VMEM((COMB_ROWS, D_MODEL), F32),
                        pltpu.SemaphoreType.DMA((2,)), pltpu.SemaphoreType.DMA(())])
    return pl.pallas_call(
        functools.partial(_combine_kernel, cap=cap, nblk=nblk),
        out_shape=out_shape,
        grid_spec=grid_spec,
        compiler_params=_cparams(("arbitrary",)),
    )(a0.reshape(-1), s1.reshape(-1), slot_t, y, x, g.reshape(1, D_MODEL), mod)


CUM_BLK = 256
SCALE_STEP = 2.0 ** -16
SCALE_ITERS = 10
BISECT_ITERS = 64


def _prefix_count(mask):
    n = mask.shape[1]
    tri = (lax.broadcasted_iota(jnp.int32, (CUM_BLK, CUM_BLK), 0)
           <= lax.broadcasted_iota(jnp.int32, (CUM_BLK, CUM_BLK), 1)).astype(F32).astype(BF16)
    carry = jnp.zeros((mask.shape[0], 1), F32)
    outs = []
    for b in range(n // CUM_BLK):
        c = _dot(mask[:, b * CUM_BLK:(b + 1) * CUM_BLK].astype(F32).astype(BF16), tri) + carry
        outs.append(c)
        carry = c[:, CUM_BLK - 1:CUM_BLK]
    return jnp.concatenate(outs, axis=1)


def _route_kernel(lg_ref, aff_ref, slot_ref, slott_ref, s0_ref):
    x = lg_ref[...]
    ex = jnp.exp(x - jnp.max(x, axis=0, keepdims=True))
    aff = ex / jnp.sum(ex, axis=0, keepdims=True)
    aff_ref[...] = aff
    streams = ((0, NS), (NS, NP))
    caps = [float(EC_CAPACITY * n // N_EXPERTS) for _, n in streams]
    parts = [aff[:, lo:lo + n] for lo, n in streams]

    def enough(k, thr):
        return jnp.sum((parts[k] >= thr).astype(F32), axis=1, keepdims=True) >= caps[k]

    def scale_step(_, carry):
        out = []
        for k, (lo_v, hi_v, found_f) in enumerate(carry):
            found = found_f > 0.5
            mid = hi_v * SCALE_STEP
            ge = enough(k, mid)
            out.append((jnp.where(found | ~ge, lo_v, mid), jnp.where(found | ge, hi_v, mid),
                        jnp.where(found | ge, 1.0, 0.0)))
        return tuple(out)

    def bisect_step(_, carry):
        out = []
        for k, (lo_v, hi_v) in enumerate(carry):
            mid = 0.5 * (lo_v + hi_v)
            ge = enough(k, mid)
            out.append((jnp.where(ge, mid, lo_v), jnp.where(ge, hi_v, mid)))
        return tuple(out)

    col = lambda v: jnp.full((N_EXPERTS, 1), v, F32)
    start = tuple((col(0.0), col(2.0), col(0.0)) for _ in streams)
    scaled = lax.fori_loop(0, SCALE_ITERS, scale_step, start)
    bounds = lax.fori_loop(0, BISECT_ITERS, bisect_step, tuple((lo_v, hi_v) for lo_v, hi_v, _ in scaled))
    slots = []
    off = 0.0
    for k, (lo_v, hi_v) in enumerate(bounds):
        above = parts[k] >= hi_v
        edge = (parts[k] >= lo_v) & ~above
        need = caps[k] - jnp.sum(above.astype(F32), axis=1, keepdims=True)
        edge_rank = _prefix_count(edge) - edge.astype(F32)
        sel = above | (edge & (edge_rank < need))
        slots.append(jnp.where(sel, _prefix_count(sel) - 1.0 + off, -1.0))
        off += caps[k]
    slot = jnp.concatenate(slots, axis=1)
    slot_ref[...] = slot
    pad = jnp.full((LANES - N_EXPERTS, NT), -1.0, F32)
    slott_ref[...] = jnp.concatenate([slot, pad], axis=0).T
    tok = lax.broadcasted_iota(jnp.int32, (NT, LANES), 0)
    edge = lax.broadcasted_iota(jnp.int32, (NT, LANES), 1) * COMB_ROWS
    s0_ref[...] = _dot((slot >= 0.0).astype(F32).astype(BF16), (tok < edge).astype(F32).astype(BF16))


def _route_tables(logits_t):
    return pl.pallas_call(
        _route_kernel,
        out_shape=[jax.ShapeDtypeStruct((N_EXPERTS, NT), F32), jax.ShapeDtypeStruct((N_EXPERTS, NT), F32),
                   jax.ShapeDtypeStruct((NT, LANES), F32), jax.ShapeDtypeStruct((N_EXPERTS, LANES), F32)],
        compiler_params=pltpu.CompilerParams(vmem_limit_bytes=VMEM_LIMIT),
    )(logits_t)


def _route(logits_t, h2, x, g, mod, layer, w_gate, w_up, w_down, split_out=False):
    aff, slot, slot_t, s0 = _route_tables(logits_t)
    cap = EC_CAPACITY * NT // N_EXPERTS
    s0 = s0[:, :NT // COMB_ROWS + 1].astype(jnp.int32)
    a0 = jnp.minimum(s0[:, :-1] // BF16_SUB * BF16_SUB, cap - COMB_WIN)
    y = _expert_ffn(a0, s0[:, 1:], slot, aff, h2, layer, w_gate, w_up, w_down)
    return _combine(a0, s0[:, 1:], slot_t, y, x, g, mod, layer, split_out)


def kernel(x_prompt, x_sample, cache_b_k, cache_b_v, cache_c_k, cache_c_v, c, c_ctx, ada_w, ada_b, norm_g,
           ab_w_in, hy_short_w, hy_short_b, hy_w1, hy_b1, hy_w2, hy_b2, hy_freq, hy_w3, hy_log_decay, hy_bias,
           b_sink, ab_w_out, c_w_qkv, c_q_norm, c_k_norm, c_w_out, ec_router, ec_w_gate, ec_w_up, ec_w_down):
    x = (x_sample.reshape(NS, D_MODEL), x_prompt.reshape(NP, D_MODEL))
    cond = jnp.concatenate([c, c_ctx[None], jnp.zeros((MOD_ROWS - DEC_BATCH - 1, D_MODEL), F32)], axis=0)
    mod = _adaln(cond, ada_w, ada_b).reshape(DEPTH * MOD_ROWS * 6, 1, D_MODEL)
    rope = _rope_tables(DEC_SEQ)
    hd = HEAD_DIM

    e = 0
    p_hy, p_at = _norm_mod_matmul(x, norm_g[0, 0], mod, 0, ab_w_in[e].astype(BF16), (3 * HY_CH, D_MODEL - HY_CH + 2 * B_KV * hd))
    ya = []
    for L, nb, row_blk0 in ((DEC_SEQ, DEC_BATCH, 0), (SEQ, BATCH, NS // SEQ)):
        table = _dft_table(L)
        h = _hyena_filters(L, hy_w1[e], hy_b1[e], hy_w2[e], hy_b2[e], hy_freq[e], hy_w3[e], hy_log_decay[e])
        h = h.at[0, HY_CH:].set(0.0)
        spec = _filter_spectrum(L, table, h)
        ya.append(_hyena(p_hy, hy_short_w[e], hy_short_b[e].reshape(1, -1), hy_bias[e].reshape(1, -1),
                         table.astype(BF16), table.T.astype(BF16), spec, L, nb, row_blk0))
    ya = tuple(ya)
    ctx_b = (cache_b_k[:, e].reshape(DEC_BATCH, PAST_LEN, B_KV * hd), cache_b_v[:, e].reshape(DEC_BATCH, PAST_LEN, B_KV * hd))
    call, args = _attention(p_at, B_HEADS * hd, 2, L=DEC_SEQ, Tq=256, nb=DEC_BATCH, row0=0, H=B_HEADS, KVH=B_KV,
                            ctx=ctx_b, sink=b_sink[e], rope=rope, window=True)
    yb_s = call(*args)[0]
    call, args = _attention(p_at, B_HEADS * hd, 2, L=SEQ, Tq=256, nb=BATCH, row0=NS, H=B_HEADS, KVH=B_KV,
                            sink=b_sink[e])
    yb_p = call(*args)[0]
    yb = (yb_s, yb_p)
    state_b_k = p_at[NS:, B_HEADS * hd:(B_HEADS + B_KV) * hd].reshape(BATCH, 1, SEQ, B_KV, hd)
    state_b_v = p_at[NS:, (B_HEADS + B_KV) * hd:].reshape(BATCH, 1, SEQ, B_KV, hd)
    x, h2, lg = _out_proj(ya, 0, yb, 0, ab_w_out[e].astype(BF16), x, norm_g[0, 1], norm_g[0, 2], mod, 0, ec_router[0].T)
    x = _route(lg, h2, x, norm_g[0, 3], mod, 0, ec_w_gate, ec_w_up, ec_w_down)

    o = 0
    (p,) = _norm_mod_matmul(x, norm_g[1, 0], mod, 1, c_w_qkv[o].astype(BF16), ((C_HEADS + 2 * C_KV) * hd,))
    ctx_c = (cache_c_k[:, o].reshape(DEC_BATCH, PAST_LEN, C_KV * hd), cache_c_v[:, o].reshape(DEC_BATCH, PAST_LEN, C_KV * hd))
    nrm = (c_q_norm[o], c_k_norm[o])
    call, args = _attention(p, C_HEADS * hd, 2, L=DEC_SEQ, Tq=128, nb=DEC_BATCH, row0=0, H=C_HEADS, KVH=C_KV,
                            ctx=ctx_c, norm=nrm, rope=rope)
    yc_s = call(*args)[0]
    call, args = _attention(p, C_HEADS * hd, 2, L=SEQ, Tq=256, nb=BATCH, row0=NS, H=C_HEADS, KVH=C_KV,
                            norm=nrm, emit_k=True)
    yc_p, kn = call(*args)
    yc = (yc_s, yc_p)
    state_c_k = kn.reshape(BATCH, 1, SEQ, C_KV, hd)
    state_c_v = p[NS:, (C_HEADS + C_KV) * hd:].reshape(BATCH, 1, SEQ, C_KV, hd)
    x, h2, lg = _out_proj(yc, 0, yc, 1, c_w_out[o].astype(BF16), x, norm_g[1, 1], norm_g[1, 2], mod, 1, ec_router[1].T)
    xs, xp = _route(lg, h2, x, norm_g[1, 3], mod, 1, ec_w_gate, ec_w_up, ec_w_down, split_out=True)
    y_sample = xs.reshape(DEC_BATCH, DEC_SEQ, D_MODEL)
    y_prompt = xp.reshape(BATCH, SEQ, D_MODEL)
    return (y_prompt, y_sample, state_b_k, state_b_v, state_c_k, state_c_v)
```

```python
import functools
import math

import jax
import jax.numpy as jnp
from jax import lax
from jax.experimental import pallas as pl
from jax.experimental.pallas import tpu as pltpu

F32 = jnp.float32
BF16 = jnp.bfloat16

D_MODEL = 1024
BATCH = 16
SEQ = 256
DEPTH = 2
DEC_BATCH = 8
DEC_SEQ = 1024
PAST_LEN = 512
GRID_W = 64
HEAD_DIM = 64
HY_CH = D_MODEL // 2
HY_BANDS = 8
B_HEADS = 8
B_KV = 2
WINDOW = 128
C_HEADS = 16
C_KV = 4
ROPE_BASE = 10000.0
N_EXPERTS = 16
EC_CAPACITY = 2
EXPERT_FF = 2 * D_MODEL
EPS = 1e-6
NEG_INF = -1e30

NS = DEC_BATCH * DEC_SEQ
NP = BATCH * SEQ
NT = NS + NP
MOD_ROWS = 16
LANES = 128
VMEM_LIMIT = 56 * 1024 * 1024


def _cparams(sem):
    return pltpu.CompilerParams(dimension_semantics=sem, vmem_limit_bytes=VMEM_LIMIT)


def _split(x):
    hi = x.astype(BF16)
    lo = (x - hi.astype(F32)).astype(BF16)
    return hi, lo


_NN = (((1,), (0,)), ((), ()))
_NT = (((1,), (1,)), ((), ()))


def _dot(a, b, dims=_NN):
    return lax.dot_general(a, b, dims, preferred_element_type=F32)


def _dot3(a, b, dims=_NN):
    ah, al = _split(a)
    bh, bl = _split(b)
    return _dot(ah, bh, dims) + _dot(al, bh, dims) + _dot(ah, bl, dims)


def _rms(x, g):
    return x * lax.rsqrt(jnp.mean(x * x, axis=-1, keepdims=True) + EPS) * g


def _mod_row(tile_rows):
    def f(i):
        return jnp.minimum(i * tile_rows // DEC_SEQ, DEC_BATCH)
    return f


def _adaln_kernel(c_ref, w_ref, b_ref, o_ref):
    s = jax.nn.silu(c_ref[...])
    o_ref[0] = _dot3(s, w_ref[0]) + b_ref[0]


def _adaln(cond, ada_w, ada_b):
    tn = 1536
    return pl.pallas_call(
        _adaln_kernel,
        out_shape=jax.ShapeDtypeStruct((DEPTH, MOD_ROWS, 6 * D_MODEL), F32),
        grid=(DEPTH, 6 * D_MODEL // tn),
        in_specs=[pl.BlockSpec((MOD_ROWS, D_MODEL), lambda l, j: (0, 0)),
                  pl.BlockSpec((1, D_MODEL, tn), lambda l, j: (l, 0, j)),
                  pl.BlockSpec((1, 1, tn), lambda l, j: (l, 0, j))],
        out_specs=pl.BlockSpec((1, MOD_ROWS, tn), lambda l, j: (l, 0, j)),
        compiler_params=_cparams(("arbitrary", "arbitrary")),
    )(cond, ada_w, ada_b.reshape(DEPTH, 1, 6 * D_MODEL))


def _mod_spec(layer, which, tile_rows):
    row = _mod_row(tile_rows)
    return pl.BlockSpec((1, 1, D_MODEL), lambda i, *_: ((layer * MOD_ROWS + row(i)) * 6 + which, 0, 0))


ROW_TILE = 512
NS_TILES = NS // ROW_TILE


def _stream_specs(width, col=0):
    return [pl.BlockSpec((ROW_TILE, width), lambda i, *_: (jnp.minimum(i, NS_TILES - 1), col)),
            pl.BlockSpec((ROW_TILE, width), lambda i, *_: (jnp.maximum(i - NS_TILES, 0), col))]


def _row_specs(a, width, col=0):
    if isinstance(a, tuple):
        return _stream_specs(width, col), list(a)
    return [pl.BlockSpec((ROW_TILE, width), lambda i, *_: (i, col))], [a]


ROW_CHUNK = 128


def _take_rows(refs, paired, rows):
    if paired:
        return jnp.where(pl.program_id(0) < NS_TILES, refs[0][rows, :], refs[1][rows, :]), refs[2:]
    return refs[0][rows, :], refs[1:]


def _nmm_kernel(*refs, paired):
    for c in range(ROW_TILE // ROW_CHUNK):
        rows = slice(c * ROW_CHUNK, (c + 1) * ROW_CHUNK)
        x, (g_ref, sh_ref, sc_ref, w_ref, *o_refs) = _take_rows(refs, paired, rows)
        h = _rms(x, g_ref[...]) * (1.0 + sc_ref[0]) + sh_ref[0]
        p = _dot(h.astype(BF16), w_ref[...])
        off = 0
        for o in o_refs:
            n = o.shape[1]
            o[rows, :] = p[:, off:off + n]
            off += n


def _norm_mod_matmul(x, g, mod, layer, w_bf, splits):
    tm = ROW_TILE
    n_out = w_bf.shape[1]
    x_specs, x_args = _row_specs(x, D_MODEL)
    return pl.pallas_call(
        functools.partial(_nmm_kernel, paired=isinstance(x, tuple)),
        out_shape=[jax.ShapeDtypeStruct((NT, n), F32) for n in splits],
        grid=(NT // tm,),
        in_specs=x_specs + [pl.BlockSpec((1, D_MODEL), lambda i: (0, 0)),
                            _mod_spec(layer, 0, tm), _mod_spec(layer, 1, tm),
                            pl.BlockSpec((D_MODEL, n_out), lambda i: (0, 0))],
        out_specs=[pl.BlockSpec((tm, n), lambda i: (i, 0)) for n in splits],
        compiler_params=_cparams(("arbitrary",)),
    )(*x_args, g.reshape(1, D_MODEL), mod, mod, w_bf)


def _hyena_filters(L, w1, b1, w2, b2, freq, w3, log_decay):
    hp = lax.Precision.HIGHEST
    t = jnp.linspace(0.0, 1.0, L, dtype=F32)[:, None]
    w = (2.0 * math.pi / L) * jnp.arange(L, dtype=F32)[:, None]
    bands = jnp.linspace(1e-4, HY_BANDS - 1, HY_BANDS, dtype=F32)[None, :]
    z = jnp.concatenate([t, jnp.cos(bands * w), -jnp.sin(bands * w)], axis=-1)
    h = jnp.sin(freq[0] * (jnp.dot(z, w1, precision=hp) + b1))
    h = jnp.sin(freq[1] * (jnp.dot(h, w2, precision=hp) + b2))
    return jnp.dot(h, w3, precision=hp) * jnp.exp(-t * jnp.exp(log_decay))


def _dft_kernel(f_ref, fb_ref, gb_ref, *, L):
    n = 2 * L
    r = lax.broadcasted_iota(jnp.int32, (L, L), 0)
    c = lax.broadcasted_iota(jnp.int32, (L, L), 1)
    ang = ((r * c) & (n - 1)).astype(F32) * (2.0 * math.pi / n)
    cs = jnp.cos(ang)
    sn = -jnp.sin(ang)
    by_col = jnp.where((c & 1) == 0, 1.0, -1.0)
    by_row = jnp.where((r & 1) == 0, 1.0, -1.0)
    fs = jnp.where(r == 0, by_col, sn)
    f_ref[0:L, :] = cs
    f_ref[L:2 * L, :] = fs
    fb_ref[0:L, :] = cs.astype(BF16)
    fb_ref[L:2 * L, :] = fs.astype(BF16)
    gb_ref[:, 0:L] = cs.astype(BF16)
    gb_ref[:, L:2 * L] = jnp.where(c == 0, by_row, sn).astype(BF16)


def _dft_tables(L):
    assert L & (L - 1) == 0
    return pl.pallas_call(
        functools.partial(_dft_kernel, L=L),
        out_shape=[jax.ShapeDtypeStruct((2 * L, L), F32), jax.ShapeDtypeStruct((2 * L, L), BF16),
                   jax.ShapeDtypeStruct((L, 2 * L), BF16)],
        compiler_params=pltpu.CompilerParams(vmem_limit_bytes=VMEM_LIMIT),
    )()


def _filt_kernel(f_ref, hf_ref, hb_ref, p_ref, q_ref, s_ref, *, L):
    hf = hf_ref[...]
    hb = hb_ref[...]
    ct = hf.shape[1]
    nrm = lax.rsqrt(jnp.sum(hf * hf, axis=0, keepdims=True) + jnp.sum(hb * hb, axis=0, keepdims=True) + EPS)
    fh = _dot3(f_ref[...], jnp.concatenate([hf, hb], axis=1))
    row = lax.broadcasted_iota(jnp.int32, (L, ct), 0)
    w = jnp.where(row == 0, 1.0 / (2 * L), 2.0 / (2 * L)) * nrm
    re = fh[:L, :ct] + fh[:L, ct:]
    p_ref[...] = re * w
    q_ref[...] = jnp.where(row == 0, 0.0, fh[L:, :ct] - fh[L:, ct:]) * w
    s_ref[...] = jnp.where(row == 0, fh[L:, :ct] + fh[L:, ct:], re) * w


def _filter_spectrum(L, table, h):
    ct = 128
    nc = HY_CH // ct
    shp = jax.ShapeDtypeStruct((L, HY_CH), F32)
    return pl.pallas_call(
        functools.partial(_filt_kernel, L=L),
        out_shape=[shp, shp, shp],
        grid=(nc,),
        in_specs=[pl.BlockSpec((2 * L, L), lambda c: (0, 0)),
                  pl.BlockSpec((L, ct), lambda c: (0, c)),
                  pl.BlockSpec((L, ct), lambda c: (0, nc + c))],
        out_specs=[pl.BlockSpec((L, ct), lambda c: (0, c))] * 3,
        compiler_params=_cparams(("arbitrary",)),
    )(table, h, h)


def _hyena_kernel(x0_ref, x1_ref, v_ref, w0_ref, w1_ref, wv_ref, b0_ref, b1_ref, bv_ref, bias_ref,
                  f_ref, g_ref, p_ref, q_ref, s_ref, o_ref, *, L):
    ct = x0_ref.shape[1]
    row = lax.broadcasted_iota(jnp.int32, (L, ct), 0)

    def sconv(x_ref, w_ref, b_ref):
        x = x_ref[...]
        xm = jnp.where(row == 0, 0.0, pltpu.roll(x, 1, 0))
        xp = jnp.where(row == L - 1, 0.0, pltpu.roll(x, L - 1, 0))
        return xm * w_ref[0:1, :] + x * w_ref[1:2, :] + xp * w_ref[2:3, :] + b_ref[...]

    z = sconv(v_ref, wv_ref, bv_ref) * sconv(x1_ref, w1_ref, b1_ref)
    zf = _dot(f_ref[...], z.astype(BF16))
    a = zf[:L]
    b = zf[L:]
    p = p_ref[...]
    q = q_ref[...]
    yre = a * p - b * q
    yim = a * q + b * s_ref[...]
    ycat = jnp.concatenate([yre, yim], axis=0).astype(BF16)
    y = _dot(g_ref[...], ycat) + z * bias_ref[...]
    o_ref[...] = sconv(x0_ref, w0_ref, b0_ref) * y


def _hyena(p_hy, short_w, short_b, bias, table_bf, table_t_bf, spec, L, nb, row_blk0):
    ct = 256
    nc = HY_CH // ct
    P, Q, S = spec

    def xs(k):
        return pl.BlockSpec((L, ct), lambda b, c: (row_blk0 + b, k * nc + c))

    def ws(rows, k):
        return pl.BlockSpec((rows, ct), lambda b, c: (0, k * nc + c))

    cs = pl.BlockSpec((L, ct), lambda b, c: (0, c))
    return pl.pallas_call(
        functools.partial(_hyena_kernel, L=L),
        out_shape=jax.ShapeDtypeStruct((nb * L, HY_CH), F32),
        grid=(nb, nc),
        in_specs=[xs(0), xs(1), xs(2), ws(3, 0), ws(3, 1), ws(3, 2), ws(1, 0), ws(1, 1), ws(1, 2), ws(1, 0),
                  pl.BlockSpec((2 * L, L), lambda b, c: (0, 0)),
                  pl.BlockSpec((L, 2 * L), lambda b, c: (0, 0)),
                  cs, cs, cs],
        out_specs=pl.BlockSpec((L, ct), lambda b, c: (b, c)),
        compiler_params=_cparams(("arbitrary", "arbitrary")),
    )(p_hy, p_hy, p_hy, short_w, short_w, short_w, short_b, short_b, short_b, bias,
      table_bf, table_t_bf, P, Q, S)


def _rope_tables(L):
    half = HEAD_DIM // 2
    pos = jnp.arange(L)
    r = (pos // GRID_W).astype(F32)
    col = (pos % GRID_W).astype(F32)
    inv = ROPE_BASE ** (-jnp.arange(0, half, 2, dtype=F32) / half)
    ar = r[:, None] * inv[None, :]
    ac = col[:, None] * inv[None, :]
    z = jnp.zeros_like(ar)
    cos = jnp.concatenate([jnp.cos(ar), jnp.cos(ar), jnp.cos(ac), jnp.cos(ac)], axis=1)
    s1 = jnp.concatenate([-jnp.sin(ar), z, -jnp.sin(ac), z], axis=1)
    s2 = jnp.concatenate([z, jnp.sin(ar), z, jnp.sin(ac)], axis=1)
    rep = LANES // HEAD_DIM
    return jnp.tile(cos, (1, rep)), jnp.tile(s1, (1, rep)), jnp.tile(s2, (1, rep))


def _attn_kernel(*refs, L, Tq, H, KVH, Lc, window, use_sink, use_norm, use_rope, emit_k):
    G = H // KVH
    hd = HEAD_DIM
    per = LANES // hd
    pad = WINDOW if window else 0
    it = iter(refs)
    q_ref = next(it)
    kv_ref = next(it)
    ck_ref = cv_ref = sink_ref = qn_ref = kn_ref = bd_ref = rc_ref = rs1_ref = rs2_ref = ko_ref = None
    if Lc:
        ck_ref = next(it)
        cv_ref = next(it)
    if use_sink:
        sink_ref = next(it)
    if use_norm:
        qn_ref = next(it)
        kn_ref = next(it)
        bd_ref = next(it)
    if use_rope:
        rc_ref = next(it)
        rs1_ref = next(it)
        rs2_ref = next(it)
    o_ref = next(it)
    if emit_k:
        ko_ref = next(it)
    k_scr = next(it)
    v_scr = next(it)
    i = pl.program_id(1)
    lat0 = Lc + pad

    def head_norm(x, g_ref):
        hi, lo = _split(x * x)
        ms = _dot(hi, bd_ref[...]) + _dot(lo, bd_ref[...])
        return x * lax.rsqrt(ms + EPS) * g_ref[...]

    def rope(x, rows):
        return (x * rc_ref[rows, :] + pltpu.roll(x, LANES - 16, 1) * rs1_ref[rows, :]
                + pltpu.roll(x, 16, 1) * rs2_ref[rows, :])

    def put(scr, g, rows, x):
        scr[g, 0, rows, :] = x.astype(BF16)
        scr[g, 1, rows, :] = pltpu.roll(x, hd, 1).astype(BF16)

    @pl.when(i == 0)
    def _():
        n_kg = KVH // per
        if window:
            zpad = jnp.zeros((n_kg, per, pad, LANES), BF16)
            for scr in (k_scr, v_scr):
                scr[:, :, Lc:Lc + pad, :] = zpad
                scr[:, :, lat0 + L:lat0 + L + pad, :] = zpad
        for g in range(n_kg):
            cols = slice(LANES * g, LANES * (g + 1))
            kg = kv_ref[:, cols]
            if use_norm:
                kg = head_norm(kg, kn_ref)
                if emit_k:
                    ko_ref[:, cols] = kg
            if use_rope:
                kg = rope(kg, slice(None))
            put(k_scr, g, slice(lat0, lat0 + L), kg)
            put(v_scr, g, slice(lat0, lat0 + L), kv_ref[:, KVH * hd + LANES * g:KVH * hd + LANES * (g + 1)])
            if Lc:
                put(k_scr, g, slice(0, Lc), ck_ref[0][:, cols])
                put(v_scr, g, slice(0, Lc), cv_ref[0][:, cols])

    rows = pl.ds(pl.multiple_of(i * Tq, Tq), Tq)
    low = lax.broadcasted_iota(jnp.int32, (Tq, LANES), 1) < hd
    groups_per_kv = G // per
    hrow = lax.broadcasted_iota(jnp.int32, (groups_per_kv * Tq, 1), 0) // Tq
    for kvh in range(KVH):
        qgs = []
        for g in range(kvh * groups_per_kv, (kvh + 1) * groups_per_kv):
            qg = q_ref[:, LANES * g:LANES * (g + 1)]
            if use_norm:
                qg = head_norm(qg, qn_ref)
            if use_rope:
                qg = rope(qg, rows)
            qgs.append((qg * (hd ** -0.5)).astype(BF16))
        outs = []
        for half in range(per):
            keep = low if half == 0 else ~low
            qs = jnp.concatenate([jnp.where(keep, qg, jnp.zeros_like(qg)) for qg in qgs], axis=0)
            var = (kvh % per) ^ half
            kg = kvh // per
            if use_sink:
                sk = jnp.zeros((groups_per_kv * Tq, 1), F32)
                for n in range(groups_per_kv):
                    sk = jnp.where(hrow == n, sink_ref[0, (kvh * groups_per_kv + n) * per + half], sk)
            if window:
                W = Tq + 2 * pad
                wrows = pl.ds(pl.multiple_of(Lc + i * Tq, Tq), W)
                s_c = _dot(qs, k_scr[kg, var, 0:Lc, :], _NT)
                s_w = _dot(qs, k_scr[kg, var, wrows, :], _NT).reshape(groups_per_kv, Tq, W)
                r = lax.broadcasted_iota(jnp.int32, (Tq, W), 0)
                c = lax.broadcasted_iota(jnp.int32, (Tq, W), 1)
                kpos = i * Tq - pad + c
                valid = (kpos >= 0) & (kpos < L) & (jnp.abs(r + pad - c) <= WINDOW)
                s_w = jnp.where(valid[None], s_w, NEG_INF).reshape(groups_per_kv * Tq, W)
                m = jnp.maximum(jnp.max(s_c, axis=-1, keepdims=True), jnp.max(s_w, axis=-1, keepdims=True))
                if use_sink:
                    m = jnp.maximum(m, sk)
                e_c = jnp.exp(s_c - m)
                e_w = jnp.exp(s_w - m)
                den = jnp.sum(e_c, axis=-1, keepdims=True) + jnp.sum(e_w, axis=-1, keepdims=True)
                o = (_dot(e_c.astype(BF16), v_scr[kg, var, 0:Lc, :])
                     + _dot(e_w.astype(BF16), v_scr[kg, var, wrows, :]))
            else:
                s = _dot(qs, k_scr[kg, var], _NT)
                m = jnp.max(s, axis=-1, keepdims=True)
                if use_sink:
                    m = jnp.maximum(m, sk)
                e = jnp.exp(s - m)
                den = jnp.sum(e, axis=-1, keepdims=True)
                o = _dot(e.astype(BF16), v_scr[kg, var])
            if use_sink:
                den = den + jnp.exp(sk - m)
            outs.append(o / den)
        for n in range(groups_per_kv):
            g = kvh * groups_per_kv + n
            o_ref[:, LANES * g:LANES * (g + 1)] = jnp.where(low, outs[0][n * Tq:(n + 1) * Tq],
                                                            outs[1][n * Tq:(n + 1) * Tq])


def _attention(p, q_cols, kv_colblk, *, L, Tq, nb, row0, H, KVH, ctx=None, sink=None, norm=None, rope=None,
               window=False, emit_k=False):
    hd = HEAD_DIM
    nq = L // Tq
    Lc = 0 if ctx is None else ctx[0].shape[1]
    pad = WINDOW if window else 0
    Lk = Lc + L + 2 * pad
    kvw = 2 * KVH * hd
    args = [p, p]
    in_specs = [pl.BlockSpec((Tq, q_cols), lambda b, i: (row0 // Tq + b * nq + i, 0)),
                pl.BlockSpec((L, kvw), lambda b, i: (row0 // L + b, kv_colblk))]
    if ctx is not None:
        for a in ctx:
            args.append(a)
            in_specs.append(pl.BlockSpec((1, Lc, KVH * hd), lambda b, i: (b, 0, 0)))
    if sink is not None:
        args.append(sink.reshape(1, H))
        in_specs.append(pl.BlockSpec(memory_space=pltpu.SMEM))
    if norm is not None:
        bd = (jnp.arange(LANES)[:, None] // hd == jnp.arange(LANES)[None, :] // hd).astype(F32) / hd
        for a in (jnp.tile(norm[0], LANES // hd).reshape(1, LANES), jnp.tile(norm[1], LANES // hd).reshape(1, LANES)):
            args.append(a)
            in_specs.append(pl.BlockSpec((1, LANES), lambda b, i: (0, 0)))
        args.append(bd.astype(BF16))
        in_specs.append(pl.BlockSpec((LANES, LANES), lambda b, i: (0, 0)))
    if rope is not None:
        for a in rope:
            args.append(a)
            in_specs.append(pl.BlockSpec((L, LANES), lambda b, i: (0, 0)))
    out_shape = [jax.ShapeDtypeStruct((nb * L, H * hd), F32)]
    out_specs = [pl.BlockSpec((Tq, H * hd), lambda b, i: (b * nq + i, 0))]
    if emit_k:
        out_shape.append(jax.ShapeDtypeStruct((nb * L, KVH * hd), F32))
        out_specs.append(pl.BlockSpec((L, KVH * hd), lambda b, i: (b, 0)))
    kern = functools.partial(_attn_kernel, L=L, Tq=Tq, H=H, KVH=KVH, Lc=Lc, window=window,
                             use_sink=sink is not None, use_norm=norm is not None,
                             use_rope=rope is not None, emit_k=emit_k)
    call = pl.pallas_call(
        kern, out_shape=out_shape, grid=(nb, nq), in_specs=in_specs, out_specs=out_specs,
        scratch_shapes=[pltpu.VMEM((KVH * hd // LANES, LANES // hd, Lk, LANES), BF16),
                        pltpu.VMEM((KVH * hd // LANES, LANES // hd, Lk, LANES), BF16)],
        compiler_params=_cparams(("arbitrary", "arbitrary")))
    return call, args


def _oproj_kernel(*refs, paired):
    rows = slice(None)
    al, rest = _take_rows(refs, paired[0], rows)
    ar, rest = _take_rows(rest, paired[1], rows)
    x, rest = _take_rows(rest, paired[2], rows)
    w_ref, g1_ref, gate_ref, g2_ref, sh_ref, sc_ref, wr_ref, x1_ref, h2_ref, lg_ref = rest
    k = al.shape[1]
    y = _dot(al.astype(BF16), w_ref[0:k, :]) + _dot(ar.astype(BF16), w_ref[k:2 * k, :])
    x1 = x + gate_ref[0] * _rms(y, g1_ref[...])
    x1_ref[...] = x1
    h2 = _rms(x1, g2_ref[...]) * (1.0 + sc_ref[0]) + sh_ref[0]
    h2_ref[...] = h2.astype(BF16)
    lg_ref[...] = _dot3(wr_ref[...], h2, _NT)


def _out_proj(a_l, l_blk, a_r, r_blk, w_bf, x, g1, g2, mod, layer, wr_t):
    tm = ROW_TILE
    half = D_MODEL // 2
    l_specs, l_args = _row_specs(a_l, half, l_blk)
    r_specs, r_args = _row_specs(a_r, half, r_blk)
    x_specs, x_args = _row_specs(x, D_MODEL)
    paired = tuple(isinstance(a, tuple) for a in (a_l, a_r, x))
    return pl.pallas_call(
        functools.partial(_oproj_kernel, paired=paired),
        out_shape=[jax.ShapeDtypeStruct((NT, D_MODEL), F32), jax.ShapeDtypeStruct((NT, D_MODEL), BF16),
                   jax.ShapeDtypeStruct((N_EXPERTS, NT), F32)],
        grid=(NT // tm,),
        in_specs=l_specs + r_specs + x_specs + [
            pl.BlockSpec((D_MODEL, D_MODEL), lambda i: (0, 0)),
            pl.BlockSpec((1, D_MODEL), lambda i: (0, 0)),
            _mod_spec(layer, 2, tm),
            pl.BlockSpec((1, D_MODEL), lambda i: (0, 0)),
            _mod_spec(layer, 3, tm), _mod_spec(layer, 4, tm),
            pl.BlockSpec((N_EXPERTS, D_MODEL), lambda i: (0, 0))],
        out_specs=[pl.BlockSpec((tm, D_MODEL), lambda i: (i, 0)),
                   pl.BlockSpec((tm, D_MODEL), lambda i: (i, 0)),
                   pl.BlockSpec((N_EXPERTS, tm), lambda i: (0, i))],
        compiler_params=_cparams(("arbitrary",)),
    )(*l_args, *r_args, *x_args, w_bf, g1.reshape(1, D_MODEL), mod, g2.reshape(1, D_MODEL), mod, mod, wr_t)


COMB_ROWS = 512
COMB_WIN = 128
BF16_SUB = 16


def _ffn_kernel(a0_ref, s1_ref, slot_ref, aff_ref, h_ref, wg_ref, wu_ref, wd_ref, y_ref,
                xn_scr, gn_scr, xc_scr, gc_scr, acc_scr, *, cap, nf, nblk):
    grp = pl.program_id(0)
    f = pl.program_id(1)
    R = COMB_WIN
    per_step = nblk // nf

    def add_window(jb, c0, hit_of):
        cols = slice(jb * COMB_ROWS, (jb + 1) * COMB_ROWS)
        q = lax.broadcasted_iota(jnp.int32, (R, COMB_ROWS), 0)
        hit = hit_of(slot_ref[0, :, cols].astype(jnp.int32), q)
        rows = pl.ds(c0, R)
        piece = _dot(hit.astype(F32).astype(BF16), h_ref[cols, :])
        xn_scr[rows, :] += piece.astype(BF16)
        gn_scr[rows, :] += jnp.sum(jnp.where(hit, aff_ref[0, :, cols], 0.0), axis=1, keepdims=True)

    def window_start(jb):
        return pl.multiple_of(a0_ref[grp * nblk + f * per_step + jb], BF16_SUB)

    def gather():
        for jb in range(per_step):
            a0 = window_start(jb)
            add_window(jb, a0, lambda srow, q, a0=a0: srow - a0 == q)

    def gather_more():
        for jb in range(per_step):
            end = s1_ref[grp * nblk + f * per_step + jb]

            def extra(c, jb=jb):
                c0 = pl.multiple_of(jnp.minimum(c, cap - R), BF16_SUB)
                add_window(jb, c0, lambda srow, q: (srow - c0 == q) & (q + c0 >= c))
                return c + R

            lax.while_loop(lambda c, end=end: c < end, extra, window_start(jb) + R)

    def compute():
        wg = wg_ref[0, 0].astype(BF16)
        wu = wu_ref[0, 0].astype(BF16)
        wd = wd_ref[0, 0].astype(BF16)
        part_rows = cap // 4
        for r in range(4):
            rows = slice(r * part_rows, (r + 1) * part_rows)
            xe = xc_scr[rows, :]
            hmid = (jax.nn.silu(_dot(xe, wg)) * _dot(xe, wu)).astype(BF16)
            acc_scr[rows, :] += _dot(hmid, wd)

    @pl.when(f == 0)
    def _():
        xn_scr[...] = jnp.zeros_like(xn_scr)
        gn_scr[...] = jnp.zeros_like(gn_scr)
        acc_scr[...] = jnp.zeros_like(acc_scr)

    @pl.when(grp == 0)
    def _():
        gather()

    @pl.when((grp > 0) & (grp < N_EXPERTS))
    def _():
        gather()
        compute()

    @pl.when(grp == N_EXPERTS)
    def _():
        compute()

    @pl.when(grp < N_EXPERTS)
    def _():
        gather_more()

    @pl.when(f == nf - 1)
    def _():
        @pl.when(grp > 0)
        def _():
            y_ref[...] = (acc_scr[...] * gc_scr[...]).astype(BF16)

        xc_scr[...] = xn_scr[...]
        gc_scr[...] = gn_scr[...]


def _expert_ffn(a0, s1, slot, aff, h2, layer, w_gate, w_up, w_down):
    cap = EC_CAPACITY * NT // N_EXPERTS
    fc = 512
    nf = EXPERT_FF // fc
    nblk = NT // COMB_ROWS
    tok = NT // nf

    def prev(g):
        return jnp.maximum(g - 1, 0)

    def this(g):
        return jnp.minimum(g, N_EXPERTS - 1)

    grid_spec = pltpu.PrefetchScalarGridSpec(
        num_scalar_prefetch=2,
        grid=(N_EXPERTS + 1, nf),
        in_specs=[pl.BlockSpec((1, 1, tok), lambda g, f, *_: (this(g), 0, f)),
                  pl.BlockSpec((1, 1, tok), lambda g, f, *_: (this(g), 0, f)),
                  pl.BlockSpec((tok, D_MODEL), lambda g, f, *_: (f, 0)),
                  pl.BlockSpec((1, 1, D_MODEL, fc), lambda g, f, *_: (layer, prev(g), 0, f)),
                  pl.BlockSpec((1, 1, D_MODEL, fc), lambda g, f, *_: (layer, prev(g), 0, f)),
                  pl.BlockSpec((1, 1, fc, D_MODEL), lambda g, f, *_: (layer, prev(g), f, 0))],
        out_specs=pl.BlockSpec((cap, D_MODEL), lambda g, f, *_: (prev(g), 0)),
        scratch_shapes=[pltpu.VMEM((cap, D_MODEL), BF16), pltpu.VMEM((cap, 1), F32),
                        pltpu.VMEM((cap, D_MODEL), BF16), pltpu.VMEM((cap, 1), F32),
                        pltpu.VMEM((cap, D_MODEL), F32)])
    return pl.pallas_call(
        functools.partial(_ffn_kernel, cap=cap, nf=nf, nblk=nblk),
        out_shape=jax.ShapeDtypeStruct((N_EXPERTS * cap, D_MODEL), BF16),
        grid_spec=grid_spec,
        compiler_params=_cparams(("arbitrary", "arbitrary")),
    )(a0.reshape(-1), s1.reshape(-1), slot.reshape(N_EXPERTS, 1, NT), aff.reshape(N_EXPERTS, 1, NT), h2,
      w_gate, w_up, w_down)


def _combine_kernel(a0_ref, s1_ref, slot_ref, y_hbm, x_ref, g_ref, gate_ref, *rest, cap, nblk):
    *o_refs, win_scr, ovf_scr, f_scr, sem, osem = rest
    j = pl.program_id(0)
    E, R, Tb = N_EXPERTS, COMB_WIN, COMB_ROWS

    def window(e, start, dst, s):
        return pltpu.make_async_copy(y_hbm.at[pl.ds(e * cap + start, R)], dst, s)

    def issue(blk, slot):
        for e in range(E):
            a0 = pl.multiple_of(a0_ref[e * nblk + blk], BF16_SUB)
            window(e, a0, win_scr.at[slot, pl.ds(e * R, R)], sem.at[slot]).start()

    @pl.when(j == 0)
    def _():
        issue(0, 0)

    slot = j % 2
    pltpu.make_async_copy(y_hbm.at[pl.ds(0, E * R)], win_scr.at[slot], sem.at[slot]).wait()

    @pl.when(j + 1 < nblk)
    def _():
        issue(j + 1, 1 - slot)

    sl = slot_ref[...].astype(jnp.int32)
    q = lax.broadcasted_iota(jnp.int32, (Tb, R), 1)
    onehot = jnp.concatenate(
        [(sl[:, e:e + 1] - a0_ref[e * nblk + j] == q) for e in range(E)], axis=1).astype(F32).astype(BF16)
    f_scr[...] = _dot(onehot, win_scr[slot])

    for e in range(E):
        end = s1_ref[e * nblk + j]

        def more(c):
            return c < end

        def extra(c, e=e):
            c0 = pl.multiple_of(jnp.minimum(c, cap - R), BF16_SUB)
            cp = window(e, c0, ovf_scr, osem)
            cp.start()
            cp.wait()
            hit = (sl[:, e:e + 1] - c0 == q) & (q + c0 >= c)
            f_scr[...] += _dot(hit.astype(F32).astype(BF16), ovf_scr[...])
            return c + R

        lax.while_loop(more, extra, a0_ref[e * nblk + j] + R)

    res = x_ref[...] + gate_ref[0] * _rms(f_scr[...], g_ref[...])
    if len(o_refs) == 1:
        o_refs[0][...] = res
    else:
        @pl.when(j < NS_TILES)
        def _():
            o_refs[0][...] = res

        @pl.when(j >= NS_TILES)
        def _():
            o_refs[1][...] = res


def _combine(a0, s1, slot_t, y, x, g, mod, layer, split_out=False):
    cap = y.shape[0] // N_EXPERTS
    nblk = NT // COMB_ROWS
    if split_out:
        out_shape = [jax.ShapeDtypeStruct((NS, D_MODEL), F32), jax.ShapeDtypeStruct((NP, D_MODEL), F32)]
        out_specs = _stream_specs(D_MODEL)
    else:
        out_shape = jax.ShapeDtypeStruct((NT, D_MODEL), F32)
        out_specs = pl.BlockSpec((COMB_ROWS, D_MODEL), lambda i, *_: (i, 0))
    grid_spec = pltpu.PrefetchScalarGridSpec(
        num_scalar_prefetch=2,
        grid=(nblk,),
        in_specs=[pl.BlockSpec((COMB_ROWS, LANES), lambda i, *_: (i, 0)),
                  pl.BlockSpec(memory_space=pl.ANY),
                  pl.BlockSpec((COMB_ROWS, D_MODEL), lambda i, *_: (i, 0)),
                  pl.BlockSpec((1, D_MODEL), lambda i, *_: (0, 0)),
                  _mod_spec(layer, 5, COMB_ROWS)],
        out_specs=out_specs,
        scratch_shapes=[pltpu.VMEM((2, N_EXPERTS * COMB_WIN, D_MODEL), BF16),
                        pltpu.VMEM((COMB_WIN, D_MODEL), BF16),
                        pltpu.VMEM((COMB_ROWS, D_MODEL), F32),
                        pltpu.SemaphoreType.DMA((2,)), pltpu.SemaphoreType.DMA(())])
    return pl.pallas_call(
        functools.partial(_combine_kernel, cap=cap, nblk=nblk),
        out_shape=out_shape,
        grid_spec=grid_spec,
        compiler_params=_cparams(("arbitrary",)),
    )(a0.reshape(-1), s1.reshape(-1), slot_t, y, x, g.reshape(1, D_MODEL), mod)


CUM_BLK = 256
SCALE_STEP = 2.0 ** -16
SCALE_ITERS = 10
BISECT_ITERS = 64


def _prefix_count(mask):
    n = mask.shape[1]
    tri = (lax.broadcasted_iota(jnp.int32, (CUM_BLK, CUM_BLK), 0)
           <= lax.broadcasted_iota(jnp.int32, (CUM_BLK, CUM_BLK), 1)).astype(F32).astype(BF16)
    carry = jnp.zeros((mask.shape[0], 1), F32)
    outs = []
    for b in range(n // CUM_BLK):
        c = _dot(mask[:, b * CUM_BLK:(b + 1) * CUM_BLK].astype(F32).astype(BF16), tri) + carry
        outs.append(c)
        carry = c[:, CUM_BLK - 1:CUM_BLK]
    return jnp.concatenate(outs, axis=1)


def _route_kernel(lg_ref, aff_ref, slot_ref, slott_ref, s0_ref):
    x = lg_ref[...]
    ex = jnp.exp(x - jnp.max(x, axis=0, keepdims=True))
    aff = ex / jnp.sum(ex, axis=0, keepdims=True)
    aff_ref[...] = aff
    streams = ((0, NS), (NS, NP))
    caps = [float(EC_CAPACITY * n // N_EXPERTS) for _, n in streams]
    parts = [aff[:, lo:lo + n] for lo, n in streams]

    def enough(k, thr):
        return jnp.sum((parts[k] >= thr).astype(F32), axis=1, keepdims=True) >= caps[k]

    def scale_step(_, carry):
        out = []
        for k, (lo_v, hi_v, found_f) in enumerate(carry):
            found = found_f > 0.5
            mid = hi_v * SCALE_STEP
            ge = enough(k, mid)
            out.append((jnp.where(found | ~ge, lo_v, mid), jnp.where(found | ge, hi_v, mid),
                        jnp.where(found | ge, 1.0, 0.0)))
        return tuple(out)

    def bisect_step(_, carry):
        out = []
        for k, (lo_v, hi_v) in enumerate(carry):
            mid = 0.5 * (lo_v + hi_v)
            ge = enough(k, mid)
            out.append((jnp.where(ge, mid, lo_v), jnp.where(ge, hi_v, mid)))
        return tuple(out)

    col = lambda v: jnp.full((N_EXPERTS, 1), v, F32)
    start = tuple((col(0.0), col(2.0), col(0.0)) for _ in streams)
    scaled = lax.fori_loop(0, SCALE_ITERS, scale_step, start)
    bounds = lax.fori_loop(0, BISECT_ITERS, bisect_step, tuple((lo_v, hi_v) for lo_v, hi_v, _ in scaled))
    slots = []
    off = 0.0
    for k, (lo_v, hi_v) in enumerate(bounds):
        above = parts[k] >= hi_v
        edge = (parts[k] >= lo_v) & ~above
        need = caps[k] - jnp.sum(above.astype(F32), axis=1, keepdims=True)
        edge_rank = _prefix_count(edge) - edge.astype(F32)
        sel = above | (edge & (edge_rank < need))
        slots.append(jnp.where(sel, _prefix_count(sel) - 1.0 + off, -1.0))
        off += caps[k]
    slot = jnp.concatenate(slots, axis=1)
    slot_ref[...] = slot
    pad = jnp.full((LANES - N_EXPERTS, NT), -1.0, F32)
    slott_ref[...] = jnp.concatenate([slot, pad], axis=0).T
    tok = lax.broadcasted_iota(jnp.int32, (NT, LANES), 0)
    edge = lax.broadcasted_iota(jnp.int32, (NT, LANES), 1) * COMB_ROWS
    s0_ref[...] = _dot((slot >= 0.0).astype(F32).astype(BF16), (tok < edge).astype(F32).astype(BF16))


def _route_tables(logits_t):
    return pl.pallas_call(
        _route_kernel,
        out_shape=[jax.ShapeDtypeStruct((N_EXPERTS, NT), F32), jax.ShapeDtypeStruct((N_EXPERTS, NT), F32),
                   jax.ShapeDtypeStruct((NT, LANES), F32), jax.ShapeDtypeStruct((N_EXPERTS, LANES), F32)],
        compiler_params=pltpu.CompilerParams(vmem_limit_bytes=VMEM_LIMIT),
    )(logits_t)


def _route(logits_t, h2, x, g, mod, layer, w_gate, w_up, w_down, split_out=False):
    aff, slot, slot_t, s0 = _route_tables(logits_t)
    cap = EC_CAPACITY * NT // N_EXPERTS
    s0 = s0[:, :NT // COMB_ROWS + 1].astype(jnp.int32)
    a0 = jnp.minimum(s0[:, :-1] // BF16_SUB * BF16_SUB, cap - COMB_WIN)
    y = _expert_ffn(a0, s0[:, 1:], slot, aff, h2, layer, w_gate, w_up, w_down)
    return _combine(a0, s0[:, 1:], slot_t, y, x, g, mod, layer, split_out)


def kernel(x_prompt, x_sample, cache_b_k, cache_b_v, cache_c_k, cache_c_v, c, c_ctx, ada_w, ada_b, norm_g,
           ab_w_in, hy_short_w, hy_short_b, hy_w1, hy_b1, hy_w2, hy_b2, hy_freq, hy_w3, hy_log_decay, hy_bias,
           b_sink, ab_w_out, c_w_qkv, c_q_norm, c_k_norm, c_w_out, ec_router, ec_w_gate, ec_w_up, ec_w_down):
    x = (x_sample.reshape(NS, D_MODEL), x_prompt.reshape(NP, D_MODEL))
    cond = jnp.concatenate([c, c_ctx[None], jnp.zeros((MOD_ROWS - DEC_BATCH - 1, D_MODEL), F32)], axis=0)
    mod = _adaln(cond, ada_w, ada_b).reshape(DEPTH * MOD_ROWS * 6, 1, D_MODEL)
    rope = _rope_tables(DEC_SEQ)
    hd = HEAD_DIM

    e = 0
    p_hy, p_at = _norm_mod_matmul(x, norm_g[0, 0], mod, 0, ab_w_in[e].astype(BF16), (3 * HY_CH, D_MODEL - HY_CH + 2 * B_KV * hd))
    ya = []
    for L, nb, row_blk0 in ((DEC_SEQ, DEC_BATCH, 0), (SEQ, BATCH, NS // SEQ)):
        table, table_bf, table_t_bf = _dft_tables(L)
        h = _hyena_filters(L, hy_w1[e], hy_b1[e], hy_w2[e], hy_b2[e], hy_freq[e], hy_w3[e], hy_log_decay[e])
        h = h.at[0, HY_CH:].set(0.0)
        spec = _filter_spectrum(L, table, h)
        ya.append(_hyena(p_hy, hy_short_w[e], hy_short_b[e].reshape(1, -1), hy_bias[e].reshape(1, -1),
                         table_bf, table_t_bf, spec, L, nb, row_blk0))
    ya = tuple(ya)
    ctx_b = (cache_b_k[:, e].reshape(DEC_BATCH, PAST_LEN, B_KV * hd), cache_b_v[:, e].reshape(DEC_BATCH, PAST_LEN, B_KV * hd))
    call, args = _attention(p_at, B_HEADS * hd, 2, L=DEC_SEQ, Tq=256, nb=DEC_BATCH, row0=0, H=B_HEADS, KVH=B_KV,
                            ctx=ctx_b, sink=b_sink[e], rope=rope, window=True)
    yb_s = call(*args)[0]
    call, args = _attention(p_at, B_HEADS * hd, 2, L=SEQ, Tq=256, nb=BATCH, row0=NS, H=B_HEADS, KVH=B_KV,
                            sink=b_sink[e])
    yb_p = call(*args)[0]
    yb = (yb_s, yb_p)
    state_b_k = p_at[NS:, B_HEADS * hd:(B_HEADS + B_KV) * hd].reshape(BATCH, 1, SEQ, B_KV, hd)
    state_b_v = p_at[NS:, (B_HEADS + B_KV) * hd:].reshape(BATCH, 1, SEQ, B_KV, hd)
    x, h2, lg = _out_proj(ya, 0, yb, 0, ab_w_out[e].astype(BF16), x, norm_g[0, 1], norm_g[0, 2], mod, 0, ec_router[0].T)
    x = _route(lg, h2, x, norm_g[0, 3], mod, 0, ec_w_gate, ec_w_up, ec_w_down)

    o = 0
    (p,) = _norm_mod_matmul(x, norm_g[1, 0], mod, 1, c_w_qkv[o].astype(BF16), ((C_HEADS + 2 * C_KV) * hd,))
    ctx_c = (cache_c_k[:, o].reshape(DEC_BATCH, PAST_LEN, C_KV * hd), cache_c_v[:, o].reshape(DEC_BATCH, PAST_LEN, C_KV * hd))
    nrm = (c_q_norm[o], c_k_norm[o])
    call, args = _attention(p, C_HEADS * hd, 2, L=DEC_SEQ, Tq=256, nb=DEC_BATCH, row0=0, H=C_HEADS, KVH=C_KV,
                            ctx=ctx_c, norm=nrm, rope=rope)
    yc_s = call(*args)[0]
    call, args = _attention(p, C_HEADS * hd, 2, L=SEQ, Tq=256, nb=BATCH, row0=NS, H=C_HEADS, KVH=C_KV,
                            norm=nrm, emit_k=True)
    yc_p, kn = call(*args)
    yc = (yc_s, yc_p)
    state_c_k = kn.reshape(BATCH, 1, SEQ, C_KV, hd)
    state_c_v = p[NS:, (C_HEADS + C_KV) * hd:].reshape(BATCH, 1, SEQ, C_KV, hd)
    x, h2, lg = _out_proj(yc, 0, yc, 1, c_w_out[o].astype(BF16), x, norm_g[1, 1], norm_g[1, 2], mod, 1, ec_router[1].T)
    xs, xp = _route(lg, h2, x, norm_g[1, 3], mod, 1, ec_w_gate, ec_w_up, ec_w_down, split_out=True)
    y_sample = xs.reshape(DEC_BATCH, DEC_SEQ, D_MODEL)
    y_prompt = xp.reshape(BATCH, SEQ, D_MODEL)
    return (y_prompt, y_sample, state_b_k, state_b_v, state_c_k, state_c_v)
```

```python
import functools
import math

import jax
import jax.numpy as jnp
from jax import lax
from jax.experimental import pallas as pl
from jax.experimental.pallas import tpu as pltpu

F32 = jnp.float32
BF16 = jnp.bfloat16

D_MODEL = 1024
BATCH = 16
SEQ = 256
DEPTH = 2
DEC_BATCH = 8
DEC_SEQ = 1024
PAST_LEN = 512
GRID_W = 64
HEAD_DIM = 64
HY_CH = D_MODEL // 2
HY_BANDS = 8
B_HEADS = 8
B_KV = 2
WINDOW = 128
C_HEADS = 16
C_KV = 4
ROPE_BASE = 10000.0
N_EXPERTS = 16
EC_CAPACITY = 2
EXPERT_FF = 2 * D_MODEL
EPS = 1e-6
NEG_INF = -1e30

NS = DEC_BATCH * DEC_SEQ
NP = BATCH * SEQ
NT = NS + NP
MOD_ROWS = 16
LANES = 128
VMEM_LIMIT = 56 * 1024 * 1024


def _cparams(sem):
    return pltpu.CompilerParams(dimension_semantics=sem, vmem_limit_bytes=VMEM_LIMIT)


def _split(x):
    hi = x.astype(BF16)
    lo = (x - hi.astype(F32)).astype(BF16)
    return hi, lo


_NN = (((1,), (0,)), ((), ()))
_NT = (((1,), (1,)), ((), ()))


def _dot(a, b, dims=_NN):
    return lax.dot_general(a, b, dims, preferred_element_type=F32)


def _dot3(a, b, dims=_NN):
    ah, al = _split(a)
    bh, bl = _split(b)
    return _dot(ah, bh, dims) + _dot(al, bh, dims) + _dot(ah, bl, dims)


def _rms(x, g):
    return x * lax.rsqrt(jnp.mean(x * x, axis=-1, keepdims=True) + EPS) * g


def _mod_row(tile_rows):
    def f(i):
        return jnp.minimum(i * tile_rows // DEC_SEQ, DEC_BATCH)
    return f


def _adaln_kernel(c_ref, w_ref, b_ref, o_ref):
    s = jax.nn.silu(c_ref[...])
    o_ref[0] = _dot3(s, w_ref[0]) + b_ref[0]


def _adaln(cond, ada_w, ada_b):
    tn = 1536
    return pl.pallas_call(
        _adaln_kernel,
        out_shape=jax.ShapeDtypeStruct((DEPTH, MOD_ROWS, 6 * D_MODEL), F32),
        grid=(DEPTH, 6 * D_MODEL // tn),
        in_specs=[pl.BlockSpec((MOD_ROWS, D_MODEL), lambda l, j: (0, 0)),
                  pl.BlockSpec((1, D_MODEL, tn), lambda l, j: (l, 0, j)),
                  pl.BlockSpec((1, 1, tn), lambda l, j: (l, 0, j))],
        out_specs=pl.BlockSpec((1, MOD_ROWS, tn), lambda l, j: (l, 0, j)),
        compiler_params=_cparams(("arbitrary", "arbitrary")),
    )(cond, ada_w, ada_b.reshape(DEPTH, 1, 6 * D_MODEL))


def _mod_spec(layer, which, tile_rows):
    row = _mod_row(tile_rows)
    return pl.BlockSpec((1, 1, D_MODEL), lambda i, *_: ((layer * MOD_ROWS + row(i)) * 6 + which, 0, 0))


ROW_TILE = 512
NS_TILES = NS // ROW_TILE


def _stream_specs(width, col=0):
    return [pl.BlockSpec((ROW_TILE, width), lambda i, *_: (jnp.minimum(i, NS_TILES - 1), col)),
            pl.BlockSpec((ROW_TILE, width), lambda i, *_: (jnp.maximum(i - NS_TILES, 0), col))]


def _row_specs(a, width, col=0):
    if isinstance(a, tuple):
        return _stream_specs(width, col), list(a)
    return [pl.BlockSpec((ROW_TILE, width), lambda i, *_: (i, col))], [a]


ROW_CHUNK = 128


def _take_rows(refs, paired, rows):
    if paired:
        return jnp.where(pl.program_id(0) < NS_TILES, refs[0][rows, :], refs[1][rows, :]), refs[2:]
    return refs[0][rows, :], refs[1:]


def _nmm_kernel(*refs, paired):
    for c in range(ROW_TILE // ROW_CHUNK):
        rows = slice(c * ROW_CHUNK, (c + 1) * ROW_CHUNK)
        x, (g_ref, sh_ref, sc_ref, w_ref, *o_refs) = _take_rows(refs, paired, rows)
        h = _rms(x, g_ref[...]) * (1.0 + sc_ref[0]) + sh_ref[0]
        p = _dot(h.astype(BF16), w_ref[...])
        off = 0
        for o in o_refs:
            n = o.shape[1]
            o[rows, :] = p[:, off:off + n]
            off += n


def _norm_mod_matmul(x, g, mod, layer, w_bf, splits):
    tm = ROW_TILE
    n_out = w_bf.shape[1]
    x_specs, x_args = _row_specs(x, D_MODEL)
    return pl.pallas_call(
        functools.partial(_nmm_kernel, paired=isinstance(x, tuple)),
        out_shape=[jax.ShapeDtypeStruct((NT, n), F32) for n in splits],
        grid=(NT // tm,),
        in_specs=x_specs + [pl.BlockSpec((1, D_MODEL), lambda i: (0, 0)),
                            _mod_spec(layer, 0, tm), _mod_spec(layer, 1, tm),
                            pl.BlockSpec((D_MODEL, n_out), lambda i: (0, 0))],
        out_specs=[pl.BlockSpec((tm, n), lambda i: (i, 0)) for n in splits],
        compiler_params=_cparams(("arbitrary",)),
    )(*x_args, g.reshape(1, D_MODEL), mod, mod, w_bf)


def _hyena_filters(L, w1, b1, w2, b2, freq, w3, log_decay):
    hp = lax.Precision.HIGHEST
    t = jnp.linspace(0.0, 1.0, L, dtype=F32)[:, None]
    w = (2.0 * math.pi / L) * jnp.arange(L, dtype=F32)[:, None]
    bands = jnp.linspace(1e-4, HY_BANDS - 1, HY_BANDS, dtype=F32)[None, :]
    z = jnp.concatenate([t, jnp.cos(bands * w), -jnp.sin(bands * w)], axis=-1)
    h = jnp.sin(freq[0] * (jnp.dot(z, w1, precision=hp) + b1))
    h = jnp.sin(freq[1] * (jnp.dot(h, w2, precision=hp) + b2))
    return jnp.dot(h, w3, precision=hp) * jnp.exp(-t * jnp.exp(log_decay))


def _dft_kernel(f_ref, fb_ref, gb_ref, *, L):
    n = 2 * L
    r = lax.broadcasted_iota(jnp.int32, (L, L), 0)
    c = lax.broadcasted_iota(jnp.int32, (L, L), 1)
    ang = ((r * c) & (n - 1)).astype(F32) * (2.0 * math.pi / n)
    cs = jnp.cos(ang)
    sn = -jnp.sin(ang)
    by_col = jnp.where((c & 1) == 0, 1.0, -1.0)
    by_row = jnp.where((r & 1) == 0, 1.0, -1.0)
    fs = jnp.where(r == 0, by_col, sn)
    f_ref[0:L, :] = cs
    f_ref[L:2 * L, :] = fs
    fb_ref[0:L, :] = cs.astype(BF16)
    fb_ref[L:2 * L, :] = fs.astype(BF16)
    gb_ref[:, 0:L] = cs.astype(BF16)
    gb_ref[:, L:2 * L] = jnp.where(c == 0, by_row, sn).astype(BF16)


def _dft_tables(L):
    assert L & (L - 1) == 0
    return pl.pallas_call(
        functools.partial(_dft_kernel, L=L),
        out_shape=[jax.ShapeDtypeStruct((2 * L, L), F32), jax.ShapeDtypeStruct((2 * L, L), BF16),
                   jax.ShapeDtypeStruct((L, 2 * L), BF16)],
        compiler_params=pltpu.CompilerParams(vmem_limit_bytes=VMEM_LIMIT),
    )()


def _filt_kernel(f_ref, hf_ref, hb_ref, p_ref, q_ref, s_ref, *, L):
    hf = hf_ref[...]
    hb = hb_ref[...]
    ct = hf.shape[1]
    nrm = lax.rsqrt(jnp.sum(hf * hf, axis=0, keepdims=True) + jnp.sum(hb * hb, axis=0, keepdims=True) + EPS)
    fh = _dot3(f_ref[...], jnp.concatenate([hf, hb], axis=1))
    row = lax.broadcasted_iota(jnp.int32, (L, ct), 0)
    w = jnp.where(row == 0, 1.0 / (2 * L), 2.0 / (2 * L)) * nrm
    re = fh[:L, :ct] + fh[:L, ct:]
    p_ref[...] = re * w
    q_ref[...] = jnp.where(row == 0, 0.0, fh[L:, :ct] - fh[L:, ct:]) * w
    s_ref[...] = jnp.where(row == 0, fh[L:, :ct] + fh[L:, ct:], re) * w


def _filter_spectrum(L, table, h):
    ct = 128
    nc = HY_CH // ct
    shp = jax.ShapeDtypeStruct((L, HY_CH), F32)
    return pl.pallas_call(
        functools.partial(_filt_kernel, L=L),
        out_shape=[shp, shp, shp],
        grid=(nc,),
        in_specs=[pl.BlockSpec((2 * L, L), lambda c: (0, 0)),
                  pl.BlockSpec((L, ct), lambda c: (0, c)),
                  pl.BlockSpec((L, ct), lambda c: (0, nc + c))],
        out_specs=[pl.BlockSpec((L, ct), lambda c: (0, c))] * 3,
        compiler_params=_cparams(("arbitrary",)),
    )(table, h, h)


def _hyena_kernel(x0_ref, x1_ref, v_ref, w0_ref, w1_ref, wv_ref, b0_ref, b1_ref, bv_ref, bias_ref,
                  f_ref, g_ref, p_ref, q_ref, s_ref, o_ref, *, L, seqs):
    ct = x0_ref.shape[1]
    row = lax.broadcasted_iota(jnp.int32, (L, ct), 0)
    for n in range(seqs):
        rows = slice(n * L, (n + 1) * L)

        def sconv(x_ref, w_ref, b_ref, rows=rows):
            x = x_ref[rows, :]
            xm = jnp.where(row == 0, 0.0, pltpu.roll(x, 1, 0))
            xp = jnp.where(row == L - 1, 0.0, pltpu.roll(x, L - 1, 0))
            return xm * w_ref[0:1, :] + x * w_ref[1:2, :] + xp * w_ref[2:3, :] + b_ref[...]

        z = sconv(v_ref, wv_ref, bv_ref) * sconv(x1_ref, w1_ref, b1_ref)
        zf = _dot(f_ref[...], z.astype(BF16))
        a = zf[:L]
        b = zf[L:]
        p = p_ref[...]
        q = q_ref[...]
        yre = a * p - b * q
        yim = a * q + b * s_ref[...]
        ycat = jnp.concatenate([yre, yim], axis=0).astype(BF16)
        y = _dot(g_ref[...], ycat) + z * bias_ref[...]
        o_ref[rows, :] = sconv(x0_ref, w0_ref, b0_ref) * y


def _hyena(p_hy, short_w, short_b, bias, table_bf, table_t_bf, spec, L, nb, row0, seqs):
    ct = 256
    nc = HY_CH // ct
    P, Q, S = spec
    blk = seqs * L

    def xs(k):
        return pl.BlockSpec((blk, ct), lambda b, c: (row0 // blk + b, k * nc + c))

    def ws(rows, k):
        return pl.BlockSpec((rows, ct), lambda b, c: (0, k * nc + c))

    cs = pl.BlockSpec((L, ct), lambda b, c: (0, c))
    return pl.pallas_call(
        functools.partial(_hyena_kernel, L=L, seqs=seqs),
        out_shape=jax.ShapeDtypeStruct((nb * L, HY_CH), F32),
        grid=(nb // seqs, nc),
        in_specs=[xs(0), xs(1), xs(2), ws(3, 0), ws(3, 1), ws(3, 2), ws(1, 0), ws(1, 1), ws(1, 2), ws(1, 0),
                  pl.BlockSpec((2 * L, L), lambda b, c: (0, 0)),
                  pl.BlockSpec((L, 2 * L), lambda b, c: (0, 0)),
                  cs, cs, cs],
        out_specs=pl.BlockSpec((blk, ct), lambda b, c: (b, c)),
        compiler_params=_cparams(("arbitrary", "arbitrary")),
    )(p_hy, p_hy, p_hy, short_w, short_w, short_w, short_b, short_b, short_b, bias,
      table_bf, table_t_bf, P, Q, S)


def _rope_tables(L):
    half = HEAD_DIM // 2
    pos = jnp.arange(L)
    r = (pos // GRID_W).astype(F32)
    col = (pos % GRID_W).astype(F32)
    inv = ROPE_BASE ** (-jnp.arange(0, half, 2, dtype=F32) / half)
    ar = r[:, None] * inv[None, :]
    ac = col[:, None] * inv[None, :]
    z = jnp.zeros_like(ar)
    cos = jnp.concatenate([jnp.cos(ar), jnp.cos(ar), jnp.cos(ac), jnp.cos(ac)], axis=1)
    s1 = jnp.concatenate([-jnp.sin(ar), z, -jnp.sin(ac), z], axis=1)
    s2 = jnp.concatenate([z, jnp.sin(ar), z, jnp.sin(ac)], axis=1)
    rep = LANES // HEAD_DIM
    return jnp.tile(cos, (1, rep)), jnp.tile(s1, (1, rep)), jnp.tile(s2, (1, rep))


def _attn_kernel(*refs, L, Tq, H, KVH, Lc, window, use_sink, use_norm, use_rope, emit_k):
    G = H // KVH
    hd = HEAD_DIM
    per = LANES // hd
    pad = WINDOW if window else 0
    it = iter(refs)
    q_ref = next(it)
    kv_ref = next(it)
    ck_ref = cv_ref = sink_ref = qn_ref = kn_ref = bd_ref = rc_ref = rs1_ref = rs2_ref = ko_ref = None
    if Lc:
        ck_ref = next(it)
        cv_ref = next(it)
    if use_sink:
        sink_ref = next(it)
    if use_norm:
        qn_ref = next(it)
        kn_ref = next(it)
        bd_ref = next(it)
    if use_rope:
        rc_ref = next(it)
        rs1_ref = next(it)
        rs2_ref = next(it)
    o_ref = next(it)
    if emit_k:
        ko_ref = next(it)
    k_scr = next(it)
    v_scr = next(it)
    i = pl.program_id(1)
    lat0 = Lc + pad

    def head_norm(x, g_ref):
        hi, lo = _split(x * x)
        ms = _dot(hi, bd_ref[...]) + _dot(lo, bd_ref[...])
        return x * lax.rsqrt(ms + EPS) * g_ref[...]

    def rope(x, rows):
        return (x * rc_ref[rows, :] + pltpu.roll(x, LANES - 16, 1) * rs1_ref[rows, :]
                + pltpu.roll(x, 16, 1) * rs2_ref[rows, :])

    def put(scr, g, rows, x):
        scr[g, 0, rows, :] = x.astype(BF16)
        scr[g, 1, rows, :] = pltpu.roll(x, hd, 1).astype(BF16)

    @pl.when(i == 0)
    def _():
        n_kg = KVH // per
        if window:
            zpad = jnp.zeros((n_kg, per, pad, LANES), BF16)
            for scr in (k_scr, v_scr):
                scr[:, :, Lc:Lc + pad, :] = zpad
                scr[:, :, lat0 + L:lat0 + L + pad, :] = zpad
        for g in range(n_kg):
            cols = slice(LANES * g, LANES * (g + 1))
            kg = kv_ref[:, cols]
            if use_norm:
                kg = head_norm(kg, kn_ref)
                if emit_k:
                    ko_ref[:, cols] = kg
            if use_rope:
                kg = rope(kg, slice(None))
            put(k_scr, g, slice(lat0, lat0 + L), kg)
            put(v_scr, g, slice(lat0, lat0 + L), kv_ref[:, KVH * hd + LANES * g:KVH * hd + LANES * (g + 1)])
            if Lc:
                put(k_scr, g, slice(0, Lc), ck_ref[0][:, cols])
                put(v_scr, g, slice(0, Lc), cv_ref[0][:, cols])

    rows = pl.ds(pl.multiple_of(i * Tq, Tq), Tq)
    low = lax.broadcasted_iota(jnp.int32, (Tq, LANES), 1) < hd
    groups_per_kv = G // per
    hrow = lax.broadcasted_iota(jnp.int32, (groups_per_kv * Tq, 1), 0) // Tq
    for kvh in range(KVH):
        qgs = []
        for g in range(kvh * groups_per_kv, (kvh + 1) * groups_per_kv):
            qg = q_ref[:, LANES * g:LANES * (g + 1)]
            if use_norm:
                qg = head_norm(qg, qn_ref)
            if use_rope:
                qg = rope(qg, rows)
            qgs.append((qg * (hd ** -0.5)).astype(BF16))
        outs = []
        for half in range(per):
            keep = low if half == 0 else ~low
            qs = jnp.concatenate([jnp.where(keep, qg, jnp.zeros_like(qg)) for qg in qgs], axis=0)
            var = (kvh % per) ^ half
            kg = kvh // per
            if use_sink:
                sk = jnp.zeros((groups_per_kv * Tq, 1), F32)
                for n in range(groups_per_kv):
                    sk = jnp.where(hrow == n, sink_ref[0, (kvh * groups_per_kv + n) * per + half], sk)
            if window:
                W = Tq + 2 * pad
                wrows = pl.ds(pl.multiple_of(Lc + i * Tq, Tq), W)
                s_c = _dot(qs, k_scr[kg, var, 0:Lc, :], _NT)
                s_w = _dot(qs, k_scr[kg, var, wrows, :], _NT).reshape(groups_per_kv, Tq, W)
                r = lax.broadcasted_iota(jnp.int32, (Tq, W), 0)
                c = lax.broadcasted_iota(jnp.int32, (Tq, W), 1)
                kpos = i * Tq - pad + c
                valid = (kpos >= 0) & (kpos < L) & (jnp.abs(r + pad - c) <= WINDOW)
                s_w = jnp.where(valid[None], s_w, NEG_INF).reshape(groups_per_kv * Tq, W)
                m = jnp.maximum(jnp.max(s_c, axis=-1, keepdims=True), jnp.max(s_w, axis=-1, keepdims=True))
                if use_sink:
                    m = jnp.maximum(m, sk)
                e_c = jnp.exp(s_c - m)
                e_w = jnp.exp(s_w - m)
                den = jnp.sum(e_c, axis=-1, keepdims=True) + jnp.sum(e_w, axis=-1, keepdims=True)
                o = (_dot(e_c.astype(BF16), v_scr[kg, var, 0:Lc, :])
                     + _dot(e_w.astype(BF16), v_scr[kg, var, wrows, :]))
            else:
                s = _dot(qs, k_scr[kg, var], _NT)
                m = jnp.max(s, axis=-1, keepdims=True)
                if use_sink:
                    m = jnp.maximum(m, sk)
                e = jnp.exp(s - m)
                den = jnp.sum(e, axis=-1, keepdims=True)
                o = _dot(e.astype(BF16), v_scr[kg, var])
            if use_sink:
                den = den + jnp.exp(sk - m)
            outs.append(o / den)
        for n in range(groups_per_kv):
            g = kvh * groups_per_kv + n
            o_ref[:, LANES * g:LANES * (g + 1)] = jnp.where(low, outs[0][n * Tq:(n + 1) * Tq],
                                                            outs[1][n * Tq:(n + 1) * Tq])


def _attention(p, q_cols, kv_colblk, *, L, Tq, nb, row0, H, KVH, ctx=None, sink=None, norm=None, rope=None,
               window=False, emit_k=False):
    hd = HEAD_DIM
    nq = L // Tq
    Lc = 0 if ctx is None else ctx[0].shape[1]
    pad = WINDOW if window else 0
    Lk = Lc + L + 2 * pad
    kvw = 2 * KVH * hd
    args = [p, p]
    in_specs = [pl.BlockSpec((Tq, q_cols), lambda b, i: (row0 // Tq + b * nq + i, 0)),
                pl.BlockSpec((L, kvw), lambda b, i: (row0 // L + b, kv_colblk))]
    if ctx is not None:
        for a in ctx:
            args.append(a)
            in_specs.append(pl.BlockSpec((1, Lc, KVH * hd), lambda b, i: (b, 0, 0)))
    if sink is not None:
        args.append(sink.reshape(1, H))
        in_specs.append(pl.BlockSpec(memory_space=pltpu.SMEM))
    if norm is not None:
        bd = (jnp.arange(LANES)[:, None] // hd == jnp.arange(LANES)[None, :] // hd).astype(F32) / hd
        for a in (jnp.tile(norm[0], LANES // hd).reshape(1, LANES), jnp.tile(norm[1], LANES // hd).reshape(1, LANES)):
            args.append(a)
            in_specs.append(pl.BlockSpec((1, LANES), lambda b, i: (0, 0)))
        args.append(bd.astype(BF16))
        in_specs.append(pl.BlockSpec((LANES, LANES), lambda b, i: (0, 0)))
    if rope is not None:
        for a in rope:
            args.append(a)
            in_specs.append(pl.BlockSpec((L, LANES), lambda b, i: (0, 0)))
    out_shape = [jax.ShapeDtypeStruct((nb * L, H * hd), F32)]
    out_specs = [pl.BlockSpec((Tq, H * hd), lambda b, i: (b * nq + i, 0))]
    if emit_k:
        out_shape.append(jax.ShapeDtypeStruct((nb * L, KVH * hd), F32))
        out_specs.append(pl.BlockSpec((L, KVH * hd), lambda b, i: (b, 0)))
    kern = functools.partial(_attn_kernel, L=L, Tq=Tq, H=H, KVH=KVH, Lc=Lc, window=window,
                             use_sink=sink is not None, use_norm=norm is not None,
                             use_rope=rope is not None, emit_k=emit_k)
    call = pl.pallas_call(
        kern, out_shape=out_shape, grid=(nb, nq), in_specs=in_specs, out_specs=out_specs,
        scratch_shapes=[pltpu.VMEM((KVH * hd // LANES, LANES // hd, Lk, LANES), BF16),
                        pltpu.VMEM((KVH * hd // LANES, LANES // hd, Lk, LANES), BF16)],
        compiler_params=_cparams(("arbitrary", "arbitrary")))
    return call, args


def _oproj_kernel(*refs, paired):
    rows = slice(None)
    al, rest = _take_rows(refs, paired[0], rows)
    ar, rest = _take_rows(rest, paired[1], rows)
    x, rest = _take_rows(rest, paired[2], rows)
    w_ref, g1_ref, gate_ref, g2_ref, sh_ref, sc_ref, wr_ref, x1_ref, h2_ref, lg_ref = rest
    k = al.shape[1]
    y = _dot(al.astype(BF16), w_ref[0:k, :]) + _dot(ar.astype(BF16), w_ref[k:2 * k, :])
    x1 = x + gate_ref[0] * _rms(y, g1_ref[...])
    x1_ref[...] = x1
    h2 = _rms(x1, g2_ref[...]) * (1.0 + sc_ref[0]) + sh_ref[0]
    h2_ref[...] = h2.astype(BF16)
    lg_ref[...] = _dot3(wr_ref[...], h2, _NT)


def _out_proj(a_l, l_blk, a_r, r_blk, w_bf, x, g1, g2, mod, layer, wr_t):
    tm = ROW_TILE
    half = D_MODEL // 2
    l_specs, l_args = _row_specs(a_l, half, l_blk)
    r_specs, r_args = _row_specs(a_r, half, r_blk)
    x_specs, x_args = _row_specs(x, D_MODEL)
    paired = tuple(isinstance(a, tuple) for a in (a_l, a_r, x))
    return pl.pallas_call(
        functools.partial(_oproj_kernel, paired=paired),
        out_shape=[jax.ShapeDtypeStruct((NT, D_MODEL), F32), jax.ShapeDtypeStruct((NT, D_MODEL), BF16),
                   jax.ShapeDtypeStruct((N_EXPERTS, NT), F32)],
        grid=(NT // tm,),
        in_specs=l_specs + r_specs + x_specs + [
            pl.BlockSpec((D_MODEL, D_MODEL), lambda i: (0, 0)),
            pl.BlockSpec((1, D_MODEL), lambda i: (0, 0)),
            _mod_spec(layer, 2, tm),
            pl.BlockSpec((1, D_MODEL), lambda i: (0, 0)),
            _mod_spec(layer, 3, tm), _mod_spec(layer, 4, tm),
            pl.BlockSpec((N_EXPERTS, D_MODEL), lambda i: (0, 0))],
        out_specs=[pl.BlockSpec((tm, D_MODEL), lambda i: (i, 0)),
                   pl.BlockSpec((tm, D_MODEL), lambda i: (i, 0)),
                   pl.BlockSpec((N_EXPERTS, tm), lambda i: (0, i))],
        compiler_params=_cparams(("arbitrary",)),
    )(*l_args, *r_args, *x_args, w_bf, g1.reshape(1, D_MODEL), mod, g2.reshape(1, D_MODEL), mod, mod, wr_t)


COMB_ROWS = 512
COMB_WIN = 128
BF16_SUB = 16
F32_SUB = 8


def _ffn_kernel(a0_ref, s1_ref, slot_ref, aff_ref, h_ref, wg_ref, wu_ref, wd_ref, y_ref,
                xn_scr, gn_scr, xc_scr, gc_scr, acc_scr, *, cap, nf, nblk):
    grp = pl.program_id(0)
    f = pl.program_id(1)
    R = COMB_WIN
    per_step = nblk // nf
    erow = jnp.minimum(grp, N_EXPERTS - 1) % F32_SUB

    def add_window(jb, c0, hit_of):
        cols = slice(jb * COMB_ROWS, (jb + 1) * COMB_ROWS)
        q = lax.broadcasted_iota(jnp.int32, (R, COMB_ROWS), 0)
        hit = hit_of(slot_ref[pl.ds(erow, 1), cols].astype(jnp.int32), q)
        rows = pl.ds(c0, R)
        piece = _dot(hit.astype(F32).astype(BF16), h_ref[cols, :])
        xn_scr[rows, :] += piece.astype(BF16)
        gn_scr[rows, :] += jnp.sum(jnp.where(hit, aff_ref[pl.ds(erow, 1), cols], 0.0), axis=1, keepdims=True)

    def window_start(jb):
        return pl.multiple_of(a0_ref[grp * nblk + f * per_step + jb], BF16_SUB)

    def gather():
        for jb in range(per_step):
            a0 = window_start(jb)
            add_window(jb, a0, lambda srow, q, a0=a0: srow - a0 == q)

    def gather_more():
        for jb in range(per_step):
            end = s1_ref[grp * nblk + f * per_step + jb]

            def extra(c, jb=jb):
                c0 = pl.multiple_of(jnp.minimum(c, cap - R), BF16_SUB)
                add_window(jb, c0, lambda srow, q: (srow - c0 == q) & (q + c0 >= c))
                return c + R

            lax.while_loop(lambda c, end=end: c < end, extra, window_start(jb) + R)

    def compute():
        wg = wg_ref[0, 0].astype(BF16)
        wu = wu_ref[0, 0].astype(BF16)
        wd = wd_ref[0, 0].astype(BF16)
        part_rows = cap // 4
        for r in range(4):
            rows = slice(r * part_rows, (r + 1) * part_rows)
            xe = xc_scr[rows, :]
            hmid = (jax.nn.silu(_dot(xe, wg)) * _dot(xe, wu)).astype(BF16)
            acc_scr[rows, :] += _dot(hmid, wd)

    @pl.when(f == 0)
    def _():
        xn_scr[...] = jnp.zeros_like(xn_scr)
        gn_scr[...] = jnp.zeros_like(gn_scr)
        acc_scr[...] = jnp.zeros_like(acc_scr)

    @pl.when(grp == 0)
    def _():
        gather()

    @pl.when((grp > 0) & (grp < N_EXPERTS))
    def _():
        gather()
        compute()

    @pl.when(grp == N_EXPERTS)
    def _():
        compute()

    @pl.when(grp < N_EXPERTS)
    def _():
        gather_more()

    @pl.when(f == nf - 1)
    def _():
        @pl.when(grp > 0)
        def _():
            y_ref[...] = (acc_scr[...] * gc_scr[...]).astype(BF16)

        xc_scr[...] = xn_scr[...]
        gc_scr[...] = gn_scr[...]


def _expert_ffn(a0, s1, slot, aff, h2, layer, w_gate, w_up, w_down):
    cap = EC_CAPACITY * NT // N_EXPERTS
    fc = 512
    nf = EXPERT_FF // fc
    nblk = NT // COMB_ROWS
    tok = NT // nf

    def prev(g):
        return jnp.maximum(g - 1, 0)

    def this(g):
        return jnp.minimum(g, N_EXPERTS - 1)

    grid_spec = pltpu.PrefetchScalarGridSpec(
        num_scalar_prefetch=2,
        grid=(N_EXPERTS + 1, nf),
        in_specs=[pl.BlockSpec((F32_SUB, tok), lambda g, f, *_: (this(g) // F32_SUB, f)),
                  pl.BlockSpec((F32_SUB, tok), lambda g, f, *_: (this(g) // F32_SUB, f)),
                  pl.BlockSpec((tok, D_MODEL), lambda g, f, *_: (f, 0)),
                  pl.BlockSpec((1, 1, D_MODEL, fc), lambda g, f, *_: (layer, prev(g), 0, f)),
                  pl.BlockSpec((1, 1, D_MODEL, fc), lambda g, f, *_: (layer, prev(g), 0, f)),
                  pl.BlockSpec((1, 1, fc, D_MODEL), lambda g, f, *_: (layer, prev(g), f, 0))],
        out_specs=pl.BlockSpec((cap, D_MODEL), lambda g, f, *_: (prev(g), 0)),
        scratch_shapes=[pltpu.VMEM((cap, D_MODEL), BF16), pltpu.VMEM((cap, 1), F32),
                        pltpu.VMEM((cap, D_MODEL), BF16), pltpu.VMEM((cap, 1), F32),
                        pltpu.VMEM((cap, D_MODEL), F32)])
    return pl.pallas_call(
        functools.partial(_ffn_kernel, cap=cap, nf=nf, nblk=nblk),
        out_shape=jax.ShapeDtypeStruct((N_EXPERTS * cap, D_MODEL), BF16),
        grid_spec=grid_spec,
        compiler_params=_cparams(("arbitrary", "arbitrary")),
    )(a0.reshape(-1), s1.reshape(-1), slot, aff, h2,
      w_gate, w_up, w_down)


def _combine_kernel(a0_ref, s1_ref, slot_ref, y_hbm, x_ref, g_ref, gate_ref, *rest, cap, nblk):
    *o_refs, win_scr, ovf_scr, f_scr, sem, osem = rest
    j = pl.program_id(0)
    E, R, Tb = N_EXPERTS, COMB_WIN, COMB_ROWS

    def window(e, start, dst, s):
        return pltpu.make_async_copy(y_hbm.at[pl.ds(e * cap + start, R)], dst, s)

    def issue(blk, slot):
        for e in range(E):
            a0 = pl.multiple_of(a0_ref[e * nblk + blk], BF16_SUB)
            window(e, a0, win_scr.at[slot, pl.ds(e * R, R)], sem.at[slot]).start()

    @pl.when(j == 0)
    def _():
        issue(0, 0)

    slot = j % 2
    pltpu.make_async_copy(y_hbm.at[pl.ds(0, E * R)], win_scr.at[slot], sem.at[slot]).wait()

    @pl.when(j + 1 < nblk)
    def _():
        issue(j + 1, 1 - slot)

    sl = slot_ref[...].astype(jnp.int32)
    q = lax.broadcasted_iota(jnp.int32, (Tb, R), 1)
    onehot = jnp.concatenate(
        [(sl[:, e:e + 1] - a0_ref[e * nblk + j] == q) for e in range(E)], axis=1).astype(F32).astype(BF16)
    f_scr[...] = _dot(onehot, win_scr[slot])

    for e in range(E):
        end = s1_ref[e * nblk + j]

        def more(c):
            return c < end

        def extra(c, e=e):
            c0 = pl.multiple_of(jnp.minimum(c, cap - R), BF16_SUB)
            cp = window(e, c0, ovf_scr, osem)
            cp.start()
            cp.wait()
            hit = (sl[:, e:e + 1] - c0 == q) & (q + c0 >= c)
            f_scr[...] += _dot(hit.astype(F32).astype(BF16), ovf_scr[...])
            return c + R

        lax.while_loop(more, extra, a0_ref[e * nblk + j] + R)

    res = x_ref[...] + gate_ref[0] * _rms(f_scr[...], g_ref[...])
    if len(o_refs) == 1:
        o_refs[0][...] = res
    else:
        @pl.when(j < NS_TILES)
        def _():
            o_refs[0][...] = res

        @pl.when(j >= NS_TILES)
        def _():
            o_refs[1][...] = res


def _combine(a0, s1, slot_t, y, x, g, mod, layer, split_out=False):
    cap = y.shape[0] // N_EXPERTS
    nblk = NT // COMB_ROWS
    if split_out:
        out_shape = [jax.ShapeDtypeStruct((NS, D_MODEL), F32), jax.ShapeDtypeStruct((NP, D_MODEL), F32)]
        out_specs = _stream_specs(D_MODEL)
    else:
        out_shape = jax.ShapeDtypeStruct((NT, D_MODEL), F32)
        out_specs = pl.BlockSpec((COMB_ROWS, D_MODEL), lambda i, *_: (i, 0))
    grid_spec = pltpu.PrefetchScalarGridSpec(
        num_scalar_prefetch=2,
        grid=(nblk,),
        in_specs=[pl.BlockSpec((COMB_ROWS, LANES), lambda i, *_: (i, 0)),
                  pl.BlockSpec(memory_space=pl.ANY),
                  pl.BlockSpec((COMB_ROWS, D_MODEL), lambda i, *_: (i, 0)),
                  pl.BlockSpec((1, D_MODEL), lambda i, *_: (0, 0)),
                  _mod_spec(layer, 5, COMB_ROWS)],
        out_specs=out_specs,
        scratch_shapes=[pltpu.VMEM((2, N_EXPERTS * COMB_WIN, D_MODEL), BF16),
                        pltpu.VMEM((COMB_WIN, D_MODEL), BF16),
                        pltpu.VMEM((COMB_ROWS, D_MODEL), F32),
                        pltpu.SemaphoreType.DMA((2,)), pltpu.SemaphoreType.DMA(())])
    return pl.pallas_call(
        functools.partial(_combine_kernel, cap=cap, nblk=nblk),
        out_shape=out_shape,
        grid_spec=grid_spec,
        compiler_params=_cparams(("arbitrary",)),
    )(a0.reshape(-1), s1.reshape(-1), slot_t, y, x, g.reshape(1, D_MODEL), mod)


CUM_BLK = 256
SCALE_STEP = 2.0 ** -16
SCALE_ITERS = 10
BISECT_ITERS = 64


def _prefix_count(mask):
    n = mask.shape[1]
    tri = (lax.broadcasted_iota(jnp.int32, (CUM_BLK, CUM_BLK), 0)
           <= lax.broadcasted_iota(jnp.int32, (CUM_BLK, CUM_BLK), 1)).astype(F32).astype(BF16)
    carry = jnp.zeros((mask.shape[0], 1), F32)
    outs = []
    for b in range(n // CUM_BLK):
        c = _dot(mask[:, b * CUM_BLK:(b + 1) * CUM_BLK].astype(F32).astype(BF16), tri) + carry
        outs.append(c)
        carry = c[:, CUM_BLK - 1:CUM_BLK]
    return jnp.concatenate(outs, axis=1)


def _route_kernel(lg_ref, aff_ref, slot_ref, slott_ref, s0_ref):
    x = lg_ref[...]
    ex = jnp.exp(x - jnp.max(x, axis=0, keepdims=True))
    aff = ex / jnp.sum(ex, axis=0, keepdims=True)
    aff_ref[...] = aff
    streams = ((0, NS), (NS, NP))
    caps = [float(EC_CAPACITY * n // N_EXPERTS) for _, n in streams]
    parts = [aff[:, lo:lo + n] for lo, n in streams]

    def enough(k, thr):
        return jnp.sum((parts[k] >= thr).astype(F32), axis=1, keepdims=True) >= caps[k]

    def scale_step(_, carry):
        out = []
        for k, (lo_v, hi_v, found_f) in enumerate(carry):
            found = found_f > 0.5
            mid = hi_v * SCALE_STEP
            ge = enough(k, mid)
            out.append((jnp.where(found | ~ge, lo_v, mid), jnp.where(found | ge, hi_v, mid),
                        jnp.where(found | ge, 1.0, 0.0)))
        return tuple(out)

    def bisect_step(_, carry):
        out = []
        for k, (lo_v, hi_v) in enumerate(carry):
            mid = 0.5 * (lo_v + hi_v)
            ge = enough(k, mid)
            out.append((jnp.where(ge, mid, lo_v), jnp.where(ge, hi_v, mid)))
        return tuple(out)

    col = lambda v: jnp.full((N_EXPERTS, 1), v, F32)
    start = tuple((col(0.0), col(2.0), col(0.0)) for _ in streams)
    scaled = lax.fori_loop(0, SCALE_ITERS, scale_step, start)
    bounds = lax.fori_loop(0, BISECT_ITERS, bisect_step, tuple((lo_v, hi_v) for lo_v, hi_v, _ in scaled))
    slots = []
    off = 0.0
    for k, (lo_v, hi_v) in enumerate(bounds):
        above = parts[k] >= hi_v
        edge = (parts[k] >= lo_v) & ~above
        need = caps[k] - jnp.sum(above.astype(F32), axis=1, keepdims=True)
        edge_rank = _prefix_count(edge) - edge.astype(F32)
        sel = above | (edge & (edge_rank < need))
        slots.append(jnp.where(sel, _prefix_count(sel) - 1.0 + off, -1.0))
        off += caps[k]
    slot = jnp.concatenate(slots, axis=1)
    slot_ref[...] = slot
    pad = jnp.full((LANES - N_EXPERTS, NT), -1.0, F32)
    slott_ref[...] = jnp.concatenate([slot, pad], axis=0).T
    tok = lax.broadcasted_iota(jnp.int32, (NT, LANES), 0)
    edge = lax.broadcasted_iota(jnp.int32, (NT, LANES), 1) * COMB_ROWS
    s0_ref[...] = _dot((slot >= 0.0).astype(F32).astype(BF16), (tok < edge).astype(F32).astype(BF16))


def _route_tables(logits_t):
    return pl.pallas_call(
        _route_kernel,
        out_shape=[jax.ShapeDtypeStruct((N_EXPERTS, NT), F32), jax.ShapeDtypeStruct((N_EXPERTS, NT), F32),
                   jax.ShapeDtypeStruct((NT, LANES), F32), jax.ShapeDtypeStruct((N_EXPERTS, LANES), F32)],
        compiler_params=pltpu.CompilerParams(vmem_limit_bytes=VMEM_LIMIT),
    )(logits_t)


def _route(logits_t, h2, x, g, mod, layer, w_gate, w_up, w_down, split_out=False):
    aff, slot, slot_t, s0 = _route_tables(logits_t)
    cap = EC_CAPACITY * NT // N_EXPERTS
    s0 = s0[:, :NT // COMB_ROWS + 1].astype(jnp.int32)
    a0 = jnp.minimum(s0[:, :-1] // BF16_SUB * BF16_SUB, cap - COMB_WIN)
    y = _expert_ffn(a0, s0[:, 1:], slot, aff, h2, layer, w_gate, w_up, w_down)
    return _combine(a0, s0[:, 1:], slot_t, y, x, g, mod, layer, split_out)


def kernel(x_prompt, x_sample, cache_b_k, cache_b_v, cache_c_k, cache_c_v, c, c_ctx, ada_w, ada_b, norm_g,
           ab_w_in, hy_short_w, hy_short_b, hy_w1, hy_b1, hy_w2, hy_b2, hy_freq, hy_w3, hy_log_decay, hy_bias,
           b_sink, ab_w_out, c_w_qkv, c_q_norm, c_k_norm, c_w_out, ec_router, ec_w_gate, ec_w_up, ec_w_down):
    x = (x_sample.reshape(NS, D_MODEL), x_prompt.reshape(NP, D_MODEL))
    cond = jnp.concatenate([c, c_ctx[None], jnp.zeros((MOD_ROWS - DEC_BATCH - 1, D_MODEL), F32)], axis=0)
    mod = _adaln(cond, ada_w, ada_b).reshape(DEPTH * MOD_ROWS * 6, 1, D_MODEL)
    rope = _rope_tables(DEC_SEQ)
    hd = HEAD_DIM

    e = 0
    p_hy, p_at = _norm_mod_matmul(x, norm_g[0, 0], mod, 0, ab_w_in[e].astype(BF16), (3 * HY_CH, D_MODEL - HY_CH + 2 * B_KV * hd))
    ya = []
    for L, nb, row0, seqs in ((DEC_SEQ, DEC_BATCH, 0, 2), (SEQ, BATCH, NS, 4)):
        table, table_bf, table_t_bf = _dft_tables(L)
        h = _hyena_filters(L, hy_w1[e], hy_b1[e], hy_w2[e], hy_b2[e], hy_freq[e], hy_w3[e], hy_log_decay[e])
        h = h.at[0, HY_CH:].set(0.0)
        spec = _filter_spectrum(L, table, h)
        ya.append(_hyena(p_hy, hy_short_w[e], hy_short_b[e].reshape(1, -1), hy_bias[e].reshape(1, -1),
                         table_bf, table_t_bf, spec, L, nb, row0, seqs))
    ya = tuple(ya)
    ctx_b = (cache_b_k[:, e].reshape(DEC_BATCH, PAST_LEN, B_KV * hd), cache_b_v[:, e].reshape(DEC_BATCH, PAST_LEN, B_KV * hd))
    call, args = _attention(p_at, B_HEADS * hd, 2, L=DEC_SEQ, Tq=256, nb=DEC_BATCH, row0=0, H=B_HEADS, KVH=B_KV,
                            ctx=ctx_b, sink=b_sink[e], rope=rope, window=True)
    yb_s = call(*args)[0]
    call, args = _attention(p_at, B_HEADS * hd, 2, L=SEQ, Tq=256, nb=BATCH, row0=NS, H=B_HEADS, KVH=B_KV,
                            sink=b_sink[e])
    yb_p = call(*args)[0]
    yb = (yb_s, yb_p)
    state_b_k = p_at[NS:, B_HEADS * hd:(B_HEADS + B_KV) * hd].reshape(BATCH, 1, SEQ, B_KV, hd)
    state_b_v = p_at[NS:, (B_HEADS + B_KV) * hd:].reshape(BATCH, 1, SEQ, B_KV, hd)
    x, h2, lg = _out_proj(ya, 0, yb, 0, ab_w_out[e].astype(BF16), x, norm_g[0, 1], norm_g[0, 2], mod, 0, ec_router[0].T)
    x = _route(lg, h2, x, norm_g[0, 3], mod, 0, ec_w_gate, ec_w_up, ec_w_down)

    o = 0
    (p,) = _norm_mod_matmul(x, norm_g[1, 0], mod, 1, c_w_qkv[o].astype(BF16), ((C_HEADS + 2 * C_KV) * hd,))
    ctx_c = (cache_c_k[:, o].reshape(DEC_BATCH, PAST_LEN, C_KV * hd), cache_c_v[:, o].reshape(DEC_BATCH, PAST_LEN, C_KV * hd))
    nrm = (c_q_norm[o], c_k_norm[o])
    call, args = _attention(p, C_HEADS * hd, 2, L=DEC_SEQ, Tq=256, nb=DEC_BATCH, row0=0, H=C_HEADS, KVH=C_KV,
                            ctx=ctx_c, norm=nrm, rope=rope)
    yc_s = call(*args)[0]
    call, args = _attention(p, C_HEADS * hd, 2, L=SEQ, Tq=256, nb=BATCH, row0=NS, H=C_HEADS, KVH=C_KV,
                            norm=nrm, emit_k=True)
    yc_p, kn = call(*args)
    yc = (yc_s, yc_p)
    state_c_k = kn.reshape(BATCH, 1, SEQ, C_KV, hd)
    state_c_v = p[NS:, (C_HEADS + C_KV) * hd:].reshape(BATCH, 1, SEQ, C_KV, hd)
    x, h2, lg = _out_proj(yc, 0, yc, 1, c_w_out[o].astype(BF16), x, norm_g[1, 1], norm_g[1, 2], mod, 1, ec_router[1].T)
    xs, xp = _route(lg, h2, x, norm_g[1, 3], mod, 1, ec_w_gate, ec_w_up, ec_w_down, split_out=True)
    y_sample = xs.reshape(DEC_BATCH, DEC_SEQ, D_MODEL)
    y_prompt = xp.reshape(BATCH, SEQ, D_MODEL)
    return (y_prompt, y_sample, state_b_k, state_b_v, state_c_k, state_c_v)
```

```python
import functools
import math

import jax
import jax.numpy as jnp
from jax import lax
from jax.experimental import pallas as pl
from jax.experimental.pallas import tpu as pltpu

F32 = jnp.float32
BF16 = jnp.bfloat16

D_MODEL = 1024
BATCH = 16
SEQ = 256
DEPTH = 2
DEC_BATCH = 8
DEC_SEQ = 1024
PAST_LEN = 512
GRID_W = 64
HEAD_DIM = 64
HY_CH = D_MODEL // 2
HY_BANDS = 8
B_HEADS = 8
B_KV = 2
WINDOW = 128
C_HEADS = 16
C_KV = 4
ROPE_BASE = 10000.0
N_EXPERTS = 16
EC_CAPACITY = 2
EXPERT_FF = 2 * D_MODEL
EPS = 1e-6
NEG_INF = -1e30

NS = DEC_BATCH * DEC_SEQ
NP = BATCH * SEQ
NT = NS + NP
MOD_ROWS = 16
LANES = 128
VMEM_LIMIT = 56 * 1024 * 1024


def _cparams(sem):
    return pltpu.CompilerParams(dimension_semantics=sem, vmem_limit_bytes=VMEM_LIMIT)


def _split(x):
    hi = x.astype(BF16)
    lo = (x - hi.astype(F32)).astype(BF16)
    return hi, lo


_NN = (((1,), (0,)), ((), ()))
_NT = (((1,), (1,)), ((), ()))


def _dot(a, b, dims=_NN):
    return lax.dot_general(a, b, dims, preferred_element_type=F32)


def _dot3(a, b, dims=_NN):
    ah, al = _split(a)
    bh, bl = _split(b)
    return _dot(ah, bh, dims) + _dot(al, bh, dims) + _dot(ah, bl, dims)


def _rms(x, g):
    return x * lax.rsqrt(jnp.mean(x * x, axis=-1, keepdims=True) + EPS) * g


def _mod_row(tile_rows):
    def f(i):
        return jnp.minimum(i * tile_rows // DEC_SEQ, DEC_BATCH)
    return f


def _adaln_kernel(c_ref, w_ref, b_ref, o_ref):
    s = jax.nn.silu(c_ref[...])
    o_ref[0] = _dot3(s, w_ref[0]) + b_ref[0]


def _adaln(cond, ada_w, ada_b):
    tn = 1536
    return pl.pallas_call(
        _adaln_kernel,
        out_shape=jax.ShapeDtypeStruct((DEPTH, MOD_ROWS, 6 * D_MODEL), F32),
        grid=(DEPTH, 6 * D_MODEL // tn),
        in_specs=[pl.BlockSpec((MOD_ROWS, D_MODEL), lambda l, j: (0, 0)),
                  pl.BlockSpec((1, D_MODEL, tn), lambda l, j: (l, 0, j)),
                  pl.BlockSpec((1, 1, tn), lambda l, j: (l, 0, j))],
        out_specs=pl.BlockSpec((1, MOD_ROWS, tn), lambda l, j: (l, 0, j)),
        compiler_params=_cparams(("arbitrary", "arbitrary")),
    )(cond, ada_w, ada_b.reshape(DEPTH, 1, 6 * D_MODEL))


def _mod_spec(layer, which, tile_rows):
    row = _mod_row(tile_rows)
    return pl.BlockSpec((1, 1, D_MODEL), lambda i, *_: ((layer * MOD_ROWS + row(i)) * 6 + which, 0, 0))


ROW_TILE = 512
NS_TILES = NS // ROW_TILE


def _stream_specs(width, col=0, tile=ROW_TILE):
    ns = NS // tile
    return [pl.BlockSpec((tile, width), lambda i, *_: (jnp.minimum(i, ns - 1), col)),
            pl.BlockSpec((tile, width), lambda i, *_: (jnp.maximum(i - ns, 0), col))]


def _row_specs(a, width, col=0):
    if isinstance(a, tuple):
        return _stream_specs(width, col), list(a)
    return [pl.BlockSpec((ROW_TILE, width), lambda i, *_: (i, col))], [a]


ROW_CHUNK = 128


def _take_rows(refs, paired, rows):
    if paired:
        return jnp.where(pl.program_id(0) < NS_TILES, refs[0][rows, :], refs[1][rows, :]), refs[2:]
    return refs[0][rows, :], refs[1:]


def _nmm_kernel(*refs, paired):
    for c in range(ROW_TILE // ROW_CHUNK):
        rows = slice(c * ROW_CHUNK, (c + 1) * ROW_CHUNK)
        x, (g_ref, sh_ref, sc_ref, w_ref, *o_refs) = _take_rows(refs, paired, rows)
        h = _rms(x, g_ref[...]) * (1.0 + sc_ref[0]) + sh_ref[0]
        p = _dot(h.astype(BF16), w_ref[...])
        off = 0
        for o in o_refs:
            n = o.shape[1]
            o[rows, :] = p[:, off:off + n]
            off += n


def _norm_mod_matmul(x, g, mod, layer, w_bf, splits):
    tm = ROW_TILE
    n_out = w_bf.shape[1]
    x_specs, x_args = _row_specs(x, D_MODEL)
    return pl.pallas_call(
        functools.partial(_nmm_kernel, paired=isinstance(x, tuple)),
        out_shape=[jax.ShapeDtypeStruct((NT, n), F32) for n in splits],
        grid=(NT // tm,),
        in_specs=x_specs + [pl.BlockSpec((1, D_MODEL), lambda i: (0, 0)),
                            _mod_spec(layer, 0, tm), _mod_spec(layer, 1, tm),
                            pl.BlockSpec((D_MODEL, n_out), lambda i: (0, 0))],
        out_specs=[pl.BlockSpec((tm, n), lambda i: (i, 0)) for n in splits],
        compiler_params=_cparams(("arbitrary",)),
    )(*x_args, g.reshape(1, D_MODEL), mod, mod, w_bf)


def _hyena_filters(L, w1, b1, w2, b2, freq, w3, log_decay):
    hp = lax.Precision.HIGHEST
    t = jnp.linspace(0.0, 1.0, L, dtype=F32)[:, None]
    w = (2.0 * math.pi / L) * jnp.arange(L, dtype=F32)[:, None]
    bands = jnp.linspace(1e-4, HY_BANDS - 1, HY_BANDS, dtype=F32)[None, :]
    z = jnp.concatenate([t, jnp.cos(bands * w), -jnp.sin(bands * w)], axis=-1)
    h = jnp.sin(freq[0] * (jnp.dot(z, w1, precision=hp) + b1))
    h = jnp.sin(freq[1] * (jnp.dot(h, w2, precision=hp) + b2))
    return jnp.dot(h, w3, precision=hp) * jnp.exp(-t * jnp.exp(log_decay))


def _dft_kernel(f_ref, fb_ref, gb_ref, *, L):
    n = 2 * L
    w = 2.0 * math.pi / n
    r = lax.broadcasted_iota(jnp.int32, (L, LANES), 0)
    b = lax.broadcasted_iota(jnp.int32, (L, LANES), 1)
    r1 = lax.broadcasted_iota(jnp.int32, (L, 1), 0)
    ang_b = ((r * b) & (n - 1)).astype(F32) * w
    cb = jnp.cos(ang_b)
    sb = jnp.sin(ang_b)
    by_col = jnp.where((b & 1) == 0, 1.0, -1.0)
    by_row = jnp.where((r & 1) == 0, 1.0, -1.0)
    for a in range(L // LANES):
        ang_a = ((r1 * (a * LANES)) & (n - 1)).astype(F32) * w
        ca = jnp.cos(ang_a)
        sa = jnp.sin(ang_a)
        cs = ca * cb - sa * sb
        sn = -(sa * cb + ca * sb)
        cols = slice(a * LANES, (a + 1) * LANES)
        fs = jnp.where(r == 0, by_col, sn)
        f_ref[0:L, cols] = cs
        f_ref[L:2 * L, cols] = fs
        fb_ref[0:L, cols] = cs.astype(BF16)
        fb_ref[L:2 * L, cols] = fs.astype(BF16)
        gb_ref[:, cols] = cs.astype(BF16)
        gs = jnp.where(b == 0, by_row, sn) if a == 0 else sn
        gb_ref[:, L + a * LANES:L + (a + 1) * LANES] = gs.astype(BF16)


def _dft_tables(L):
    assert L & (L - 1) == 0
    return pl.pallas_call(
        functools.partial(_dft_kernel, L=L),
        out_shape=[jax.ShapeDtypeStruct((2 * L, L), F32), jax.ShapeDtypeStruct((2 * L, L), BF16),
                   jax.ShapeDtypeStruct((L, 2 * L), BF16)],
        compiler_params=pltpu.CompilerParams(vmem_limit_bytes=VMEM_LIMIT),
    )()


def _filt_kernel(f_ref, hf_ref, hb_ref, p_ref, q_ref, s_ref, *, L):
    hf = hf_ref[...]
    hb = hb_ref[...]
    ct = hf.shape[1]
    nrm = lax.rsqrt(jnp.sum(hf * hf, axis=0, keepdims=True) + jnp.sum(hb * hb, axis=0, keepdims=True) + EPS)
    fh = _dot3(f_ref[...], jnp.concatenate([hf, hb], axis=1))
    row = lax.broadcasted_iota(jnp.int32, (L, ct), 0)
    w = jnp.where(row == 0, 1.0 / (2 * L), 2.0 / (2 * L)) * nrm
    re = fh[:L, :ct] + fh[:L, ct:]
    p_ref[...] = re * w
    q_ref[...] = jnp.where(row == 0, 0.0, fh[L:, :ct] - fh[L:, ct:]) * w
    s_ref[...] = jnp.where(row == 0, fh[L:, :ct] + fh[L:, ct:], re) * w


def _filter_spectrum(L, table, h):
    ct = 128
    nc = HY_CH // ct
    shp = jax.ShapeDtypeStruct((L, HY_CH), F32)
    return pl.pallas_call(
        functools.partial(_filt_kernel, L=L),
        out_shape=[shp, shp, shp],
        grid=(nc,),
        in_specs=[pl.BlockSpec((2 * L, L), lambda c: (0, 0)),
                  pl.BlockSpec((L, ct), lambda c: (0, c)),
                  pl.BlockSpec((L, ct), lambda c: (0, nc + c))],
        out_specs=[pl.BlockSpec((L, ct), lambda c: (0, c))] * 3,
        compiler_params=_cparams(("arbitrary",)),
    )(table, h, h)


def _hyena_kernel(x0_ref, x1_ref, v_ref, w0_ref, w1_ref, wv_ref, b0_ref, b1_ref, bv_ref, bias_ref,
                  f_ref, g_ref, p_ref, q_ref, s_ref, o_ref, *, L, seqs):
    ct = x0_ref.shape[1]
    row = lax.broadcasted_iota(jnp.int32, (L, ct), 0)
    for n in range(seqs):
        rows = slice(n * L, (n + 1) * L)

        def sconv(x_ref, w_ref, b_ref, rows=rows):
            x = x_ref[rows, :]
            xm = jnp.where(row == 0, 0.0, pltpu.roll(x, 1, 0))
            xp = jnp.where(row == L - 1, 0.0, pltpu.roll(x, L - 1, 0))
            return xm * w_ref[0:1, :] + x * w_ref[1:2, :] + xp * w_ref[2:3, :] + b_ref[...]

        z = sconv(v_ref, wv_ref, bv_ref) * sconv(x1_ref, w1_ref, b1_ref)
        zf = _dot(f_ref[...], z.astype(BF16))
        a = zf[:L]
        b = zf[L:]
        p = p_ref[...]
        q = q_ref[...]
        yre = a * p - b * q
        yim = a * q + b * s_ref[...]
        ycat = jnp.concatenate([yre, yim], axis=0).astype(BF16)
        y = _dot(g_ref[...], ycat) + z * bias_ref[...]
        o_ref[rows, :] = sconv(x0_ref, w0_ref, b0_ref) * y


def _hyena(p_hy, short_w, short_b, bias, table_bf, table_t_bf, spec, L, nb, row0, seqs):
    ct = 256
    nc = HY_CH // ct
    P, Q, S = spec
    blk = seqs * L

    def xs(k):
        return pl.BlockSpec((blk, ct), lambda b, c: (row0 // blk + b, k * nc + c))

    def ws(rows, k):
        return pl.BlockSpec((rows, ct), lambda b, c: (0, k * nc + c))

    cs = pl.BlockSpec((L, ct), lambda b, c: (0, c))
    return pl.pallas_call(
        functools.partial(_hyena_kernel, L=L, seqs=seqs),
        out_shape=jax.ShapeDtypeStruct((nb * L, HY_CH), F32),
        grid=(nb // seqs, nc),
        in_specs=[xs(0), xs(1), xs(2), ws(3, 0), ws(3, 1), ws(3, 2), ws(1, 0), ws(1, 1), ws(1, 2), ws(1, 0),
                  pl.BlockSpec((2 * L, L), lambda b, c: (0, 0)),
                  pl.BlockSpec((L, 2 * L), lambda b, c: (0, 0)),
                  cs, cs, cs],
        out_specs=pl.BlockSpec((blk, ct), lambda b, c: (b, c)),
        compiler_params=_cparams(("arbitrary", "arbitrary")),
    )(p_hy, p_hy, p_hy, short_w, short_w, short_w, short_b, short_b, short_b, bias,
      table_bf, table_t_bf, P, Q, S)


def _rope_tables(L):
    half = HEAD_DIM // 2
    pos = jnp.arange(L)
    r = (pos // GRID_W).astype(F32)
    col = (pos % GRID_W).astype(F32)
    inv = ROPE_BASE ** (-jnp.arange(0, half, 2, dtype=F32) / half)
    ar = r[:, None] * inv[None, :]
    ac = col[:, None] * inv[None, :]
    z = jnp.zeros_like(ar)
    cos = jnp.concatenate([jnp.cos(ar), jnp.cos(ar), jnp.cos(ac), jnp.cos(ac)], axis=1)
    s1 = jnp.concatenate([-jnp.sin(ar), z, -jnp.sin(ac), z], axis=1)
    s2 = jnp.concatenate([z, jnp.sin(ar), z, jnp.sin(ac)], axis=1)
    rep = LANES // HEAD_DIM
    return jnp.tile(cos, (1, rep)), jnp.tile(s1, (1, rep)), jnp.tile(s2, (1, rep))


def _attn_kernel(*refs, L, Tq, H, KVH, Lc, window, use_sink, use_norm, use_rope, emit_k):
    G = H // KVH
    hd = HEAD_DIM
    per = LANES // hd
    pad = WINDOW if window else 0
    it = iter(refs)
    q_ref = next(it)
    kv_ref = next(it)
    ck_ref = cv_ref = sink_ref = qn_ref = kn_ref = bd_ref = rc_ref = rs1_ref = rs2_ref = ko_ref = None
    if Lc:
        ck_ref = next(it)
        cv_ref = next(it)
    if use_sink:
        sink_ref = next(it)
    if use_norm:
        qn_ref = next(it)
        kn_ref = next(it)
        bd_ref = next(it)
    if use_rope:
        rc_ref = next(it)
        rs1_ref = next(it)
        rs2_ref = next(it)
    o_ref = next(it)
    if emit_k:
        ko_ref = next(it)
    k_scr = next(it)
    v_scr = next(it)
    i = pl.program_id(1)
    lat0 = Lc + pad

    def head_norm(x, g_ref):
        hi, lo = _split(x * x)
        ms = _dot(hi, bd_ref[...]) + _dot(lo, bd_ref[...])
        return x * lax.rsqrt(ms + EPS) * g_ref[...]

    def rope(x, rows):
        return (x * rc_ref[rows, :] + pltpu.roll(x, LANES - 16, 1) * rs1_ref[rows, :]
                + pltpu.roll(x, 16, 1) * rs2_ref[rows, :])

    def put(scr, g, rows, x):
        scr[g, 0, rows, :] = x.astype(BF16)
        scr[g, 1, rows, :] = pltpu.roll(x, hd, 1).astype(BF16)

    @pl.when(i == 0)
    def _():
        n_kg = KVH // per
        if window:
            zpad = jnp.zeros((n_kg, per, pad, LANES), BF16)
            for scr in (k_scr, v_scr):
                scr[:, :, Lc:Lc + pad, :] = zpad
                scr[:, :, lat0 + L:lat0 + L + pad, :] = zpad
        for g in range(n_kg):
            cols = slice(LANES * g, LANES * (g + 1))
            kg = kv_ref[:, cols]
            if use_norm:
                kg = head_norm(kg, kn_ref)
                if emit_k:
                    ko_ref[:, cols] = kg
            if use_rope:
                kg = rope(kg, slice(None))
            put(k_scr, g, slice(lat0, lat0 + L), kg)
            put(v_scr, g, slice(lat0, lat0 + L), kv_ref[:, KVH * hd + LANES * g:KVH * hd + LANES * (g + 1)])
            if Lc:
                put(k_scr, g, slice(0, Lc), ck_ref[0][:, cols])
                put(v_scr, g, slice(0, Lc), cv_ref[0][:, cols])

    rows = pl.ds(pl.multiple_of(i * Tq, Tq), Tq)
    low = lax.broadcasted_iota(jnp.int32, (Tq, LANES), 1) < hd
    groups_per_kv = G // per
    hrow = lax.broadcasted_iota(jnp.int32, (groups_per_kv * Tq, 1), 0) // Tq
    for kvh in range(KVH):
        qgs = []
        for g in range(kvh * groups_per_kv, (kvh + 1) * groups_per_kv):
            qg = q_ref[:, LANES * g:LANES * (g + 1)]
            if use_norm:
                qg = head_norm(qg, qn_ref)
            if use_rope:
                qg = rope(qg, rows)
            qgs.append((qg * (hd ** -0.5)).astype(BF16))
        outs = []
        for half in range(per):
            keep = low if half == 0 else ~low
            qs = jnp.concatenate([jnp.where(keep, qg, jnp.zeros_like(qg)) for qg in qgs], axis=0)
            var = (kvh % per) ^ half
            kg = kvh // per
            if use_sink:
                sk = jnp.zeros((groups_per_kv * Tq, 1), F32)
                for n in range(groups_per_kv):
                    sk = jnp.where(hrow == n, sink_ref[0, (kvh * groups_per_kv + n) * per + half], sk)
            if window:
                W = Tq + 2 * pad
                wrows = pl.ds(pl.multiple_of(Lc + i * Tq, Tq), W)
                s_c = _dot(qs, k_scr[kg, var, 0:Lc, :], _NT)
                s_w = _dot(qs, k_scr[kg, var, wrows, :], _NT).reshape(groups_per_kv, Tq, W)
                r = lax.broadcasted_iota(jnp.int32, (Tq, W), 0)
                c = lax.broadcasted_iota(jnp.int32, (Tq, W), 1)
                kpos = i * Tq - pad + c
                valid = (kpos >= 0) & (kpos < L) & (jnp.abs(r + pad - c) <= WINDOW)
                s_w = jnp.where(valid[None], s_w, NEG_INF).reshape(groups_per_kv * Tq, W)
                m = jnp.maximum(jnp.max(s_c, axis=-1, keepdims=True), jnp.max(s_w, axis=-1, keepdims=True))
                if use_sink:
                    m = jnp.maximum(m, sk)
                e_c = jnp.exp(s_c - m)
                e_w = jnp.exp(s_w - m)
                den = jnp.sum(e_c, axis=-1, keepdims=True) + jnp.sum(e_w, axis=-1, keepdims=True)
                o = (_dot(e_c.astype(BF16), v_scr[kg, var, 0:Lc, :])
                     + _dot(e_w.astype(BF16), v_scr[kg, var, wrows, :]))
            else:
                s = _dot(qs, k_scr[kg, var], _NT)
                m = jnp.max(s, axis=-1, keepdims=True)
                if use_sink:
                    m = jnp.maximum(m, sk)
                e = jnp.exp(s - m)
                den = jnp.sum(e, axis=-1, keepdims=True)
                o = _dot(e.astype(BF16), v_scr[kg, var])
            if use_sink:
                den = den + jnp.exp(sk - m)
            outs.append(o / den)
        for n in range(groups_per_kv):
            g = kvh * groups_per_kv + n
            o_ref[:, LANES * g:LANES * (g + 1)] = jnp.where(low, outs[0][n * Tq:(n + 1) * Tq],
                                                            outs[1][n * Tq:(n + 1) * Tq])


def _attention(p, q_cols, kv_colblk, *, L, Tq, nb, row0, H, KVH, ctx=None, sink=None, norm=None, rope=None,
               window=False, emit_k=False):
    hd = HEAD_DIM
    nq = L // Tq
    Lc = 0 if ctx is None else ctx[0].shape[1]
    pad = WINDOW if window else 0
    Lk = Lc + L + 2 * pad
    kvw = 2 * KVH * hd
    args = [p, p]
    in_specs = [pl.BlockSpec((Tq, q_cols), lambda b, i: (row0 // Tq + b * nq + i, 0)),
                pl.BlockSpec((L, kvw), lambda b, i: (row0 // L + b, kv_colblk))]
    if ctx is not None:
        for a in ctx:
            args.append(a)
            in_specs.append(pl.BlockSpec((1, Lc, KVH * hd), lambda b, i: (b, 0, 0)))
    if sink is not None:
        args.append(sink.reshape(1, H))
        in_specs.append(pl.BlockSpec(memory_space=pltpu.SMEM))
    if norm is not None:
        bd = (jnp.arange(LANES)[:, None] // hd == jnp.arange(LANES)[None, :] // hd).astype(F32) / hd
        for a in (jnp.tile(norm[0], LANES // hd).reshape(1, LANES), jnp.tile(norm[1], LANES // hd).reshape(1, LANES)):
            args.append(a)
            in_specs.append(pl.BlockSpec((1, LANES), lambda b, i: (0, 0)))
        args.append(bd.astype(BF16))
        in_specs.append(pl.BlockSpec((LANES, LANES), lambda b, i: (0, 0)))
    if rope is not None:
        for a in rope:
            args.append(a)
            in_specs.append(pl.BlockSpec((L, LANES), lambda b, i: (0, 0)))
    out_shape = [jax.ShapeDtypeStruct((nb * L, H * hd), F32)]
    out_specs = [pl.BlockSpec((Tq, H * hd), lambda b, i: (b * nq + i, 0))]
    if emit_k:
        out_shape.append(jax.ShapeDtypeStruct((nb * L, KVH * hd), F32))
        out_specs.append(pl.BlockSpec((L, KVH * hd), lambda b, i: (b, 0)))
    kern = functools.partial(_attn_kernel, L=L, Tq=Tq, H=H, KVH=KVH, Lc=Lc, window=window,
                             use_sink=sink is not None, use_norm=norm is not None,
                             use_rope=rope is not None, emit_k=emit_k)
    call = pl.pallas_call(
        kern, out_shape=out_shape, grid=(nb, nq), in_specs=in_specs, out_specs=out_specs,
        scratch_shapes=[pltpu.VMEM((KVH * hd // LANES, LANES // hd, Lk, LANES), BF16),
                        pltpu.VMEM((KVH * hd // LANES, LANES // hd, Lk, LANES), BF16)],
        compiler_params=_cparams(("arbitrary", "arbitrary")))
    return call, args


def _oproj_kernel(*refs, paired):
    rows = slice(None)
    al, rest = _take_rows(refs, paired[0], rows)
    ar, rest = _take_rows(rest, paired[1], rows)
    x, rest = _take_rows(rest, paired[2], rows)
    w_ref, g1_ref, gate_ref, g2_ref, sh_ref, sc_ref, wr_ref, x1_ref, h2_ref, lg_ref = rest
    k = al.shape[1]
    y = _dot(al.astype(BF16), w_ref[0:k, :]) + _dot(ar.astype(BF16), w_ref[k:2 * k, :])
    x1 = x + gate_ref[0] * _rms(y, g1_ref[...])
    x1_ref[...] = x1
    h2 = _rms(x1, g2_ref[...]) * (1.0 + sc_ref[0]) + sh_ref[0]
    wh, wl = _split(wr_ref[...])
    hb = h2.astype(BF16)
    h2_ref[...] = hb
    lg_ref[...] = _dot(wh, hb, _NT) + _dot(wl, hb, _NT)


def _out_proj(a_l, l_blk, a_r, r_blk, w_bf, x, g1, g2, mod, layer, wr_t):
    tm = ROW_TILE
    half = D_MODEL // 2
    l_specs, l_args = _row_specs(a_l, half, l_blk)
    r_specs, r_args = _row_specs(a_r, half, r_blk)
    x_specs, x_args = _row_specs(x, D_MODEL)
    paired = tuple(isinstance(a, tuple) for a in (a_l, a_r, x))
    return pl.pallas_call(
        functools.partial(_oproj_kernel, paired=paired),
        out_shape=[jax.ShapeDtypeStruct((NT, D_MODEL), F32), jax.ShapeDtypeStruct((NT, D_MODEL), BF16),
                   jax.ShapeDtypeStruct((N_EXPERTS, NT), F32)],
        grid=(NT // tm,),
        in_specs=l_specs + r_specs + x_specs + [
            pl.BlockSpec((D_MODEL, D_MODEL), lambda i: (0, 0)),
            pl.BlockSpec((1, D_MODEL), lambda i: (0, 0)),
            _mod_spec(layer, 2, tm),
            pl.BlockSpec((1, D_MODEL), lambda i: (0, 0)),
            _mod_spec(layer, 3, tm), _mod_spec(layer, 4, tm),
            pl.BlockSpec((N_EXPERTS, D_MODEL), lambda i: (0, 0))],
        out_specs=[pl.BlockSpec((tm, D_MODEL), lambda i: (i, 0)),
                   pl.BlockSpec((tm, D_MODEL), lambda i: (i, 0)),
                   pl.BlockSpec((N_EXPERTS, tm), lambda i: (0, i))],
        compiler_params=_cparams(("arbitrary",)),
    )(*l_args, *r_args, *x_args, w_bf, g1.reshape(1, D_MODEL), mod, g2.reshape(1, D_MODEL), mod, mod, wr_t)


COMB_ROWS = 512
COMB_WIN = 128
BF16_SUB = 16
F32_SUB = 8


def _ffn_kernel(a0_ref, s1_ref, slot_ref, aff_ref, h_ref, wg_ref, wu_ref, wd_ref, y_ref,
                xn_scr, gn_scr, xc_scr, gc_scr, acc_scr, *, cap, nf, nblk):
    grp = pl.program_id(0)
    f = pl.program_id(1)
    R = COMB_WIN
    per_step = nblk // nf
    erow = jnp.minimum(grp, N_EXPERTS - 1) % F32_SUB

    def add_window(jb, c0, hit_of):
        cols = slice(jb * COMB_ROWS, (jb + 1) * COMB_ROWS)
        q = lax.broadcasted_iota(jnp.int32, (R, COMB_ROWS), 0)
        hit = hit_of(slot_ref[pl.ds(erow, 1), cols].astype(jnp.int32), q)
        rows = pl.ds(c0, R)
        piece = _dot(hit.astype(F32).astype(BF16), h_ref[cols, :])
        xn_scr[rows, :] += piece.astype(BF16)
        gn_scr[rows, :] += jnp.sum(jnp.where(hit, aff_ref[pl.ds(erow, 1), cols], 0.0), axis=1, keepdims=True)

    def window_start(jb):
        return pl.multiple_of(a0_ref[grp * nblk + f * per_step + jb], BF16_SUB)

    def gather():
        for jb in range(per_step):
            a0 = window_start(jb)
            add_window(jb, a0, lambda srow, q, a0=a0: srow - a0 == q)

    def gather_more():
        for jb in range(per_step):
            end = s1_ref[grp * nblk + f * per_step + jb]

            def extra(c, jb=jb):
                c0 = pl.multiple_of(jnp.minimum(c, cap - R), BF16_SUB)
                add_window(jb, c0, lambda srow, q: (srow - c0 == q) & (q + c0 >= c))
                return c + R

            lax.while_loop(lambda c, end=end: c < end, extra, window_start(jb) + R)

    def compute():
        wg = wg_ref[0, 0].astype(BF16)
        wu = wu_ref[0, 0].astype(BF16)
        wd = wd_ref[0, 0].astype(BF16)
        part_rows = cap // 4
        for r in range(4):
            rows = slice(r * part_rows, (r + 1) * part_rows)
            xe = xc_scr[rows, :]
            hmid = (jax.nn.silu(_dot(xe, wg)) * _dot(xe, wu)).astype(BF16)
            acc_scr[rows, :] += _dot(hmid, wd)

    @pl.when(f == 0)
    def _():
        xn_scr[...] = jnp.zeros_like(xn_scr)
        gn_scr[...] = jnp.zeros_like(gn_scr)
        acc_scr[...] = jnp.zeros_like(acc_scr)

    @pl.when(grp == 0)
    def _():
        gather()

    @pl.when((grp > 0) & (grp < N_EXPERTS))
    def _():
        gather()
        compute()

    @pl.when(grp == N_EXPERTS)
    def _():
        compute()

    @pl.when(grp < N_EXPERTS)
    def _():
        gather_more()

    @pl.when(f == nf - 1)
    def _():
        @pl.when(grp > 0)
        def _():
            y_ref[...] = (acc_scr[...] * gc_scr[...]).astype(BF16)

        xc_scr[...] = xn_scr[...]
        gc_scr[...] = gn_scr[...]


def _expert_ffn(a0, s1, slot, aff, h2, layer, w_gate, w_up, w_down):
    cap = EC_CAPACITY * NT // N_EXPERTS
    fc = 512
    nf = EXPERT_FF // fc
    nblk = NT // COMB_ROWS
    tok = NT // nf

    def prev(g):
        return jnp.maximum(g - 1, 0)

    def this(g):
        return jnp.minimum(g, N_EXPERTS - 1)

    grid_spec = pltpu.PrefetchScalarGridSpec(
        num_scalar_prefetch=2,
        grid=(N_EXPERTS + 1, nf),
        in_specs=[pl.BlockSpec((F32_SUB, tok), lambda g, f, *_: (this(g) // F32_SUB, f)),
                  pl.BlockSpec((F32_SUB, tok), lambda g, f, *_: (this(g) // F32_SUB, f)),
                  pl.BlockSpec((tok, D_MODEL), lambda g, f, *_: (f, 0)),
                  pl.BlockSpec((1, 1, D_MODEL, fc), lambda g, f, *_: (layer, prev(g), 0, f)),
                  pl.BlockSpec((1, 1, D_MODEL, fc), lambda g, f, *_: (layer, prev(g), 0, f)),
                  pl.BlockSpec((1, 1, fc, D_MODEL), lambda g, f, *_: (layer, prev(g), f, 0))],
        out_specs=pl.BlockSpec((cap, D_MODEL), lambda g, f, *_: (prev(g), 0)),
        scratch_shapes=[pltpu.VMEM((cap, D_MODEL), BF16), pltpu.VMEM((cap, 1), F32),
                        pltpu.VMEM((cap, D_MODEL), BF16), pltpu.VMEM((cap, 1), F32),
                        pltpu.VMEM((cap, D_MODEL), F32)])
    return pl.pallas_call(
        functools.partial(_ffn_kernel, cap=cap, nf=nf, nblk=nblk),
        out_shape=jax.ShapeDtypeStruct((N_EXPERTS * cap, D_MODEL), BF16),
        grid_spec=grid_spec,
        compiler_params=_cparams(("arbitrary", "arbitrary")),
    )(a0.reshape(-1), s1.reshape(-1), slot, aff, h2,
      w_gate, w_up, w_down)


CMB_ROWS = COMB_ROWS
CMB_WIN = LANES


def _combine_kernel(a0_ref, s1_ref, slot_ref, y_hbm, x_ref, g_ref, gate_ref, *rest, cap, nblk):
    *o_refs, win_scr, ovf_scr, f_scr, sem, osem = rest
    j = pl.program_id(0)
    E, R, Tb = N_EXPERTS, CMB_WIN, CMB_ROWS

    def window(e, start, dst, s):
        return pltpu.make_async_copy(y_hbm.at[pl.ds(e * cap + start, R)], dst, s)

    def issue(blk, slot):
        for e in range(E):
            a0 = pl.multiple_of(a0_ref[e * nblk + blk], BF16_SUB)
            window(e, a0, win_scr.at[slot, pl.ds(e * R, R)], sem.at[slot]).start()

    @pl.when(j == 0)
    def _():
        issue(0, 0)

    slot = j % 2
    pltpu.make_async_copy(y_hbm.at[pl.ds(0, E * R)], win_scr.at[slot], sem.at[slot]).wait()

    @pl.when(j + 1 < nblk)
    def _():
        issue(j + 1, 1 - slot)

    sl = slot_ref[...].astype(jnp.int32)
    q = lax.broadcasted_iota(jnp.int32, (Tb, R), 1)
    onehot = jnp.concatenate(
        [(sl[:, e:e + 1] - a0_ref[e * nblk + j] == q) for e in range(E)], axis=1).astype(F32).astype(BF16)
    f_scr[...] = _dot(onehot, win_scr[slot])

    for e in range(E):
        end = s1_ref[e * nblk + j]

        def more(c):
            return c < end

        def extra(c, e=e):
            c0 = pl.multiple_of(jnp.minimum(c, cap - R), BF16_SUB)
            cp = window(e, c0, ovf_scr, osem)
            cp.start()
            cp.wait()
            hit = (sl[:, e:e + 1] - c0 == q) & (q + c0 >= c)
            f_scr[...] += _dot(hit.astype(F32).astype(BF16), ovf_scr[...])
            return c + R

        lax.while_loop(more, extra, a0_ref[e * nblk + j] + R)

    res = x_ref[...] + gate_ref[0] * _rms(f_scr[...], g_ref[...])
    if len(o_refs) == 1:
        o_refs[0][...] = res
    else:
        @pl.when(j < NS // Tb)
        def _():
            o_refs[0][...] = res

        @pl.when(j >= NS // Tb)
        def _():
            o_refs[1][...] = res


def _combine(a0, s1, slot_t, y, x, g, mod, layer, split_out=False):
    cap = y.shape[0] // N_EXPERTS
    nblk = NT // CMB_ROWS
    if split_out:
        out_shape = [jax.ShapeDtypeStruct((NS, D_MODEL), F32), jax.ShapeDtypeStruct((NP, D_MODEL), F32)]
        out_specs = _stream_specs(D_MODEL, tile=CMB_ROWS)
    else:
        out_shape = jax.ShapeDtypeStruct((NT, D_MODEL), F32)
        out_specs = pl.BlockSpec((CMB_ROWS, D_MODEL), lambda i, *_: (i, 0))
    grid_spec = pltpu.PrefetchScalarGridSpec(
        num_scalar_prefetch=2,
        grid=(nblk,),
        in_specs=[pl.BlockSpec((CMB_ROWS, LANES), lambda i, *_: (i, 0)),
                  pl.BlockSpec(memory_space=pl.ANY),
                  pl.BlockSpec((CMB_ROWS, D_MODEL), lambda i, *_: (i, 0)),
                  pl.BlockSpec((1, D_MODEL), lambda i, *_: (0, 0)),
                  _mod_spec(layer, 5, CMB_ROWS)],
        out_specs=out_specs,
        scratch_shapes=[pltpu.VMEM((2, N_EXPERTS * CMB_WIN, D_MODEL), BF16),
                        pltpu.VMEM((CMB_WIN, D_MODEL), BF16),
                        pltpu.VMEM((CMB_ROWS, D_MODEL), F32),
                        pltpu.SemaphoreType.DMA((2,)), pltpu.SemaphoreType.DMA(())])
    return pl.pallas_call(
        functools.partial(_combine_kernel, cap=cap, nblk=nblk),
        out_shape=out_shape,
        grid_spec=grid_spec,
        compiler_params=_cparams(("arbitrary",)),
    )(a0.reshape(-1), s1.reshape(-1), slot_t, y, x, g.reshape(1, D_MODEL), mod)


CUM_BLK = 256
SCALE_STEP = 2.0 ** -16
SCALE_ITERS = 10
BISECT_ITERS = 64


def _prefix_count(mask):
    n = mask.shape[1]
    tri = (lax.broadcasted_iota(jnp.int32, (CUM_BLK, CUM_BLK), 0)
           <= lax.broadcasted_iota(jnp.int32, (CUM_BLK, CUM_BLK), 1)).astype(F32).astype(BF16)
    carry = jnp.zeros((mask.shape[0], 1), F32)
    outs = []
    for b in range(n // CUM_BLK):
        c = _dot(mask[:, b * CUM_BLK:(b + 1) * CUM_BLK].astype(F32).astype(BF16), tri) + carry
        outs.append(c)
        carry = c[:, CUM_BLK - 1:CUM_BLK]
    return jnp.concatenate(outs, axis=1)


def _route_kernel(lg_ref, aff_ref, slot_ref, slott_ref, s0_ref):
    x = lg_ref[...]
    ex = jnp.exp(x - jnp.max(x, axis=0, keepdims=True))
    aff = ex / jnp.sum(ex, axis=0, keepdims=True)
    aff_ref[...] = aff
    streams = ((0, NS), (NS, NP))
    caps = [float(EC_CAPACITY * n // N_EXPERTS) for _, n in streams]
    parts = [aff[:, lo:lo + n] for lo, n in streams]

    def enough(k, thr):
        return jnp.sum((parts[k] >= thr).astype(F32), axis=1, keepdims=True) >= caps[k]

    def scale_step(_, carry):
        out = []
        for k, (lo_v, hi_v, found_f) in enumerate(carry):
            found = found_f > 0.5
            mid = hi_v * SCALE_STEP
            ge = enough(k, mid)
            out.append((jnp.where(found | ~ge, lo_v, mid), jnp.where(found | ge, hi_v, mid),
                        jnp.where(found | ge, 1.0, 0.0)))
        return tuple(out)

    def bisect_step(_, carry):
        out = []
        for k, (lo_v, hi_v) in enumerate(carry):
            mid = 0.5 * (lo_v + hi_v)
            ge = enough(k, mid)
            out.append((jnp.where(ge, mid, lo_v), jnp.where(ge, hi_v, mid)))
        return tuple(out)

    col = lambda v: jnp.full((N_EXPERTS, 1), v, F32)
    start = tuple((col(0.0), col(2.0), col(0.0)) for _ in streams)
    scaled = lax.fori_loop(0, SCALE_ITERS, scale_step, start)
    bounds = lax.fori_loop(0, BISECT_ITERS, bisect_step, tuple((lo_v, hi_v) for lo_v, hi_v, _ in scaled))
    slots = []
    off = 0.0
    for k, (lo_v, hi_v) in enumerate(bounds):
        above = parts[k] >= hi_v
        edge = (parts[k] >= lo_v) & ~above
        need = caps[k] - jnp.sum(above.astype(F32), axis=1, keepdims=True)
        edge_rank = _prefix_count(edge) - edge.astype(F32)
        sel = above | (edge & (edge_rank < need))
        slots.append(jnp.where(sel, _prefix_count(sel) - 1.0 + off, -1.0))
        off += caps[k]
    slot = jnp.concatenate(slots, axis=1)
    slot_ref[...] = slot
    pad = jnp.full((LANES - N_EXPERTS, NT), -1.0, F32)
    slott_ref[...] = jnp.concatenate([slot, pad], axis=0).T
    tok = lax.broadcasted_iota(jnp.int32, (NT, LANES), 0)
    edge = lax.broadcasted_iota(jnp.int32, (NT, LANES), 1) * CMB_ROWS
    s0_ref[...] = _dot((slot >= 0.0).astype(F32).astype(BF16), (tok < edge).astype(F32).astype(BF16))


def _route_tables(logits_t):
    return pl.pallas_call(
        _route_kernel,
        out_shape=[jax.ShapeDtypeStruct((N_EXPERTS, NT), F32), jax.ShapeDtypeStruct((N_EXPERTS, NT), F32),
                   jax.ShapeDtypeStruct((NT, LANES), F32), jax.ShapeDtypeStruct((N_EXPERTS, LANES), F32)],
        compiler_params=pltpu.CompilerParams(vmem_limit_bytes=VMEM_LIMIT),
    )(logits_t)


def _route(logits_t, h2, x, g, mod, layer, w_gate, w_up, w_down, split_out=False):
    aff, slot, slot_t, s0 = _route_tables(logits_t)
    cap = EC_CAPACITY * NT // N_EXPERTS
    s0 = s0[:, :NT // CMB_ROWS + 1].astype(jnp.int32)

    def windows(starts, win):
        return jnp.minimum(starts[:, :-1] // BF16_SUB * BF16_SUB, cap - win), starts[:, 1:]

    a0g, s1g = windows(s0[:, ::COMB_ROWS // CMB_ROWS], COMB_WIN)
    y = _expert_ffn(a0g, s1g, slot, aff, h2, layer, w_gate, w_up, w_down)
    a0c, s1c = windows(s0, CMB_WIN)
    return _combine(a0c, s1c, slot_t, y, x, g, mod, layer, split_out)


def kernel(x_prompt, x_sample, cache_b_k, cache_b_v, cache_c_k, cache_c_v, c, c_ctx, ada_w, ada_b, norm_g,
           ab_w_in, hy_short_w, hy_short_b, hy_w1, hy_b1, hy_w2, hy_b2, hy_freq, hy_w3, hy_log_decay, hy_bias,
           b_sink, ab_w_out, c_w_qkv, c_q_norm, c_k_norm, c_w_out, ec_router, ec_w_gate, ec_w_up, ec_w_down):
    x = (x_sample.reshape(NS, D_MODEL), x_prompt.reshape(NP, D_MODEL))
    cond = jnp.concatenate([c, c_ctx[None], jnp.zeros((MOD_ROWS - DEC_BATCH - 1, D_MODEL), F32)], axis=0)
    mod = _adaln(cond, ada_w, ada_b).reshape(DEPTH * MOD_ROWS * 6, 1, D_MODEL)
    rope = _rope_tables(DEC_SEQ)
    hd = HEAD_DIM

    e = 0
    p_hy, p_at = _norm_mod_matmul(x, norm_g[0, 0], mod, 0, ab_w_in[e].astype(BF16), (3 * HY_CH, D_MODEL - HY_CH + 2 * B_KV * hd))
    ya = []
    for L, nb, row0, seqs in ((DEC_SEQ, DEC_BATCH, 0, 2), (SEQ, BATCH, NS, 4)):
        table, table_bf, table_t_bf = _dft_tables(L)
        h = _hyena_filters(L, hy_w1[e], hy_b1[e], hy_w2[e], hy_b2[e], hy_freq[e], hy_w3[e], hy_log_decay[e])
        h = h.at[0, HY_CH:].set(0.0)
        spec = _filter_spectrum(L, table, h)
        ya.append(_hyena(p_hy, hy_short_w[e], hy_short_b[e].reshape(1, -1), hy_bias[e].reshape(1, -1),
                         table_bf, table_t_bf, spec, L, nb, row0, seqs))
    ya = tuple(ya)
    ctx_b = (cache_b_k[:, e].reshape(DEC_BATCH, PAST_LEN, B_KV * hd), cache_b_v[:, e].reshape(DEC_BATCH, PAST_LEN, B_KV * hd))
    call, args = _attention(p_at, B_HEADS * hd, 2, L=DEC_SEQ, Tq=256, nb=DEC_BATCH, row0=0, H=B_HEADS, KVH=B_KV,
                            ctx=ctx_b, sink=b_sink[e], rope=rope, window=True)
    yb_s = call(*args)[0]
    call, args = _attention(p_at, B_HEADS * hd, 2, L=SEQ, Tq=256, nb=BATCH, row0=NS, H=B_HEADS, KVH=B_KV,
                            sink=b_sink[e])
    yb_p = call(*args)[0]
    yb = (yb_s, yb_p)
    state_b_k = p_at[NS:, B_HEADS * hd:(B_HEADS + B_KV) * hd].reshape(BATCH, 1, SEQ, B_KV, hd)
    state_b_v = p_at[NS:, (B_HEADS + B_KV) * hd:].reshape(BATCH, 1, SEQ, B_KV, hd)
    x, h2, lg = _out_proj(ya, 0, yb, 0, ab_w_out[e].astype(BF16), x, norm_g[0, 1], norm_g[0, 2], mod, 0, ec_router[0].T)
    x = _route(lg, h2, x, norm_g[0, 3], mod, 0, ec_w_gate, ec_w_up, ec_w_down)

    o = 0
    (p,) = _norm_mod_matmul(x, norm_g[1, 0], mod, 1, c_w_qkv[o].astype(BF16), ((C_HEADS + 2 * C_KV) * hd,))
    ctx_c = (cache_c_k[:, o].reshape(DEC_BATCH, PAST_LEN, C_KV * hd), cache_c_v[:, o].reshape(DEC_BATCH, PAST_LEN, C_KV * hd))
    nrm = (c_q_norm[o], c_k_norm[o])
    call, args = _attention(p, C_HEADS * hd, 2, L=DEC_SEQ, Tq=256, nb=DEC_BATCH, row0=0, H=C_HEADS, KVH=C_KV,
                            ctx=ctx_c, norm=nrm, rope=rope)
    yc_s = call(*args)[0]
    call, args = _attention(p, C_HEADS * hd, 2, L=SEQ, Tq=256, nb=BATCH, row0=NS, H=C_HEADS, KVH=C_KV,
                            norm=nrm, emit_k=True)
    yc_p, kn = call(*args)
    yc = (yc_s, yc_p)
    state_c_k = kn.reshape(BATCH, 1, SEQ, C_KV, hd)
    state_c_v = p[NS:, (C_HEADS + C_KV) * hd:].reshape(BATCH, 1, SEQ, C_KV, hd)
    x, h2, lg = _out_proj(yc, 0, yc, 1, c_w_out[o].astype(BF16), x, norm_g[1, 1], norm_g[1, 2], mod, 1, ec_router[1].T)
    xs, xp = _route(lg, h2, x, norm_g[1, 3], mod, 1, ec_w_gate, ec_w_up, ec_w_down, split_out=True)
    y_sample = xs.reshape(DEC_BATCH, DEC_SEQ, D_MODEL)
    y_prompt = xp.reshape(BATCH, SEQ, D_MODEL)
    return (y_prompt, y_sample, state_b_k, state_b_v, state_c_k, state_c_v)
```

```python
import functools
import math

import jax
import jax.numpy as jnp
from jax import lax
from jax.experimental import pallas as pl
from jax.experimental.pallas import tpu as pltpu

F32 = jnp.float32
BF16 = jnp.bfloat16

D_MODEL = 1024
BATCH = 16
SEQ = 256
DEPTH = 2
DEC_BATCH = 8
DEC_SEQ = 1024
PAST_LEN = 512
GRID_W = 64
HEAD_DIM = 64
HY_CH = D_MODEL // 2
HY_BANDS = 8
B_HEADS = 8
B_KV = 2
WINDOW = 128
C_HEADS = 16
C_KV = 4
ROPE_BASE = 10000.0
N_EXPERTS = 16
EC_CAPACITY = 2
EXPERT_FF = 2 * D_MODEL
EPS = 1e-6
NEG_INF = -1e30

NS = DEC_BATCH * DEC_SEQ
NP = BATCH * SEQ
NT = NS + NP
MOD_ROWS = 16
LANES = 128
VMEM_LIMIT = 56 * 1024 * 1024


def _cparams(sem):
    return pltpu.CompilerParams(dimension_semantics=sem, vmem_limit_bytes=VMEM_LIMIT)


def _split(x):
    hi = x.astype(BF16)
    lo = (x - hi.astype(F32)).astype(BF16)
    return hi, lo


_NN = (((1,), (0,)), ((), ()))
_NT = (((1,), (1,)), ((), ()))


def _dot(a, b, dims=_NN):
    return lax.dot_general(a, b, dims, preferred_element_type=F32)


def _dot3(a, b, dims=_NN):
    ah, al = _split(a)
    bh, bl = _split(b)
    return _dot(ah, bh, dims) + _dot(al, bh, dims) + _dot(ah, bl, dims)


def _rms(x, g):
    return x * lax.rsqrt(jnp.mean(x * x, axis=-1, keepdims=True) + EPS) * g


def _mod_row(tile_rows):
    def f(i):
        return jnp.minimum(i * tile_rows // DEC_SEQ, DEC_BATCH)
    return f


def _adaln_kernel(c_ref, w_ref, b_ref, o_ref):
    s = jax.nn.silu(c_ref[...])
    o_ref[0] = _dot3(s, w_ref[0]) + b_ref[0]


def _adaln(cond, ada_w, ada_b):
    tn = 1536
    return pl.pallas_call(
        _adaln_kernel,
        out_shape=jax.ShapeDtypeStruct((DEPTH, MOD_ROWS, 6 * D_MODEL), F32),
        grid=(DEPTH, 6 * D_MODEL // tn),
        in_specs=[pl.BlockSpec((MOD_ROWS, D_MODEL), lambda l, j: (0, 0)),
                  pl.BlockSpec((1, D_MODEL, tn), lambda l, j: (l, 0, j)),
                  pl.BlockSpec((1, 1, tn), lambda l, j: (l, 0, j))],
        out_specs=pl.BlockSpec((1, MOD_ROWS, tn), lambda l, j: (l, 0, j)),
        compiler_params=_cparams(("arbitrary", "arbitrary")),
    )(cond, ada_w, ada_b.reshape(DEPTH, 1, 6 * D_MODEL))


def _mod_spec(layer, which, tile_rows):
    row = _mod_row(tile_rows)
    return pl.BlockSpec((1, 1, D_MODEL), lambda i, *_: ((layer * MOD_ROWS + row(i)) * 6 + which, 0, 0))


ROW_TILE = 512
NS_TILES = NS // ROW_TILE


def _stream_specs(width, col=0, tile=ROW_TILE):
    ns = NS // tile
    return [pl.BlockSpec((tile, width), lambda i, *_: (jnp.minimum(i, ns - 1), col)),
            pl.BlockSpec((tile, width), lambda i, *_: (jnp.maximum(i - ns, 0), col))]


def _row_specs(a, width, col=0):
    if isinstance(a, tuple):
        return _stream_specs(width, col), list(a)
    return [pl.BlockSpec((ROW_TILE, width), lambda i, *_: (i, col))], [a]


ROW_CHUNK = 128


def _take_rows(refs, paired, rows):
    if paired:
        return jnp.where(pl.program_id(0) < NS_TILES, refs[0][rows, :], refs[1][rows, :]), refs[2:]
    return refs[0][rows, :], refs[1:]


def _nmm_kernel(*refs, paired):
    for c in range(ROW_TILE // ROW_CHUNK):
        rows = slice(c * ROW_CHUNK, (c + 1) * ROW_CHUNK)
        x, (g_ref, sh_ref, sc_ref, w_ref, *o_refs) = _take_rows(refs, paired, rows)
        h = _rms(x, g_ref[...]) * (1.0 + sc_ref[0]) + sh_ref[0]
        p = _dot(h.astype(BF16), w_ref[...])
        off = 0
        for o in o_refs:
            n = o.shape[1]
            o[rows, :] = p[:, off:off + n]
            off += n


def _norm_mod_matmul(x, g, mod, layer, w_bf, splits):
    tm = ROW_TILE
    n_out = w_bf.shape[1]
    x_specs, x_args = _row_specs(x, D_MODEL)
    return pl.pallas_call(
        functools.partial(_nmm_kernel, paired=isinstance(x, tuple)),
        out_shape=[jax.ShapeDtypeStruct((NT, n), F32) for n in splits],
        grid=(NT // tm,),
        in_specs=x_specs + [pl.BlockSpec((1, D_MODEL), lambda i: (0, 0)),
                            _mod_spec(layer, 0, tm), _mod_spec(layer, 1, tm),
                            pl.BlockSpec((D_MODEL, n_out), lambda i: (0, 0))],
        out_specs=[pl.BlockSpec((tm, n), lambda i: (i, 0)) for n in splits],
        compiler_params=_cparams(("arbitrary",)),
    )(*x_args, g.reshape(1, D_MODEL), mod, mod, w_bf)


def _hyena_filters(L, w1, b1, w2, b2, freq, w3, log_decay):
    hp = lax.Precision.HIGHEST
    t = jnp.linspace(0.0, 1.0, L, dtype=F32)[:, None]
    w = (2.0 * math.pi / L) * jnp.arange(L, dtype=F32)[:, None]
    bands = jnp.linspace(1e-4, HY_BANDS - 1, HY_BANDS, dtype=F32)[None, :]
    z = jnp.concatenate([t, jnp.cos(bands * w), -jnp.sin(bands * w)], axis=-1)
    h = jnp.sin(freq[0] * (jnp.dot(z, w1, precision=hp) + b1))
    h = jnp.sin(freq[1] * (jnp.dot(h, w2, precision=hp) + b2))
    return jnp.dot(h, w3, precision=hp) * jnp.exp(-t * jnp.exp(log_decay))


def _dft_kernel(fl_ref, fb_ref, gb_ref, *, L):
    n = 2 * L
    w = 2.0 * math.pi / n
    r = lax.broadcasted_iota(jnp.int32, (L, LANES), 0)
    b = lax.broadcasted_iota(jnp.int32, (L, LANES), 1)
    r1 = lax.broadcasted_iota(jnp.int32, (L, 1), 0)
    ang_b = ((r * b) & (n - 1)).astype(F32) * w
    cb = jnp.cos(ang_b)
    sb = jnp.sin(ang_b)
    by_col = jnp.where((b & 1) == 0, 1.0, -1.0)
    by_row = jnp.where((r & 1) == 0, 1.0, -1.0)
    for a in range(L // LANES):
        ang_a = ((r1 * (a * LANES)) & (n - 1)).astype(F32) * w
        ca = jnp.cos(ang_a)
        sa = jnp.sin(ang_a)
        cs = ca * cb - sa * sb
        sn = -(sa * cb + ca * sb)
        cols = slice(a * LANES, (a + 1) * LANES)
        fs = jnp.where(r == 0, by_col, sn)
        cs_hi, cs_lo = _split(cs)
        fs_hi, fs_lo = _split(fs)
        fb_ref[0:L, cols] = cs_hi
        fb_ref[L:2 * L, cols] = fs_hi
        fl_ref[0:L, cols] = cs_lo
        fl_ref[L:2 * L, cols] = fs_lo
        gb_ref[:, cols] = cs_hi
        gs = jnp.where(b == 0, by_row, sn) if a == 0 else sn
        gb_ref[:, L + a * LANES:L + (a + 1) * LANES] = gs.astype(BF16)


def _dft_tables(L):
    assert L & (L - 1) == 0
    return pl.pallas_call(
        functools.partial(_dft_kernel, L=L),
        out_shape=[jax.ShapeDtypeStruct((2 * L, L), BF16), jax.ShapeDtypeStruct((2 * L, L), BF16),
                   jax.ShapeDtypeStruct((L, 2 * L), BF16)],
        compiler_params=pltpu.CompilerParams(vmem_limit_bytes=VMEM_LIMIT),
    )()


def _filt_kernel(fb_ref, fl_ref, hf_ref, hb_ref, p_ref, q_ref, s_ref, *, L):
    hf = hf_ref[...]
    hb = hb_ref[...]
    ct = hf.shape[1]
    nrm = lax.rsqrt(jnp.sum(hf * hf, axis=0, keepdims=True) + jnp.sum(hb * hb, axis=0, keepdims=True) + EPS)
    hh, hl = _split(jnp.concatenate([hf, hb], axis=1))
    fb = fb_ref[...]
    fh = _dot(fb, hh) + _dot(fl_ref[...], hh) + _dot(fb, hl)
    row = lax.broadcasted_iota(jnp.int32, (L, ct), 0)
    w = jnp.where(row == 0, 1.0 / (2 * L), 2.0 / (2 * L)) * nrm
    re = fh[:L, :ct] + fh[:L, ct:]
    p_ref[...] = re * w
    q_ref[...] = jnp.where(row == 0, 0.0, fh[L:, :ct] - fh[L:, ct:]) * w
    s_ref[...] = jnp.where(row == 0, fh[L:, :ct] + fh[L:, ct:], re) * w


def _filter_spectrum(L, table_hi, table_lo, h):
    ct = 128
    nc = HY_CH // ct
    shp = jax.ShapeDtypeStruct((L, HY_CH), F32)
    return pl.pallas_call(
        functools.partial(_filt_kernel, L=L),
        out_shape=[shp, shp, shp],
        grid=(nc,),
        in_specs=[pl.BlockSpec((2 * L, L), lambda c: (0, 0)),
                  pl.BlockSpec((2 * L, L), lambda c: (0, 0)),
                  pl.BlockSpec((L, ct), lambda c: (0, c)),
                  pl.BlockSpec((L, ct), lambda c: (0, nc + c))],
        out_specs=[pl.BlockSpec((L, ct), lambda c: (0, c))] * 3,
        compiler_params=_cparams(("arbitrary",)),
    )(table_hi, table_lo, h, h)


def _hyena_kernel(x0_ref, x1_ref, v_ref, w0_ref, w1_ref, wv_ref, b0_ref, b1_ref, bv_ref, bias_ref,
                  f_ref, g_ref, p_ref, q_ref, s_ref, o_ref, *, L, seqs):
    ct = x0_ref.shape[1]
    row = lax.broadcasted_iota(jnp.int32, (L, ct), 0)
    for n in range(seqs):
        rows = slice(n * L, (n + 1) * L)

        def sconv(x_ref, w_ref, b_ref, rows=rows):
            x = x_ref[rows, :]
            xm = jnp.where(row == 0, 0.0, pltpu.roll(x, 1, 0))
            xp = jnp.where(row == L - 1, 0.0, pltpu.roll(x, L - 1, 0))
            return xm * w_ref[0:1, :] + x * w_ref[1:2, :] + xp * w_ref[2:3, :] + b_ref[...]

        z = sconv(v_ref, wv_ref, bv_ref) * sconv(x1_ref, w1_ref, b1_ref)
        zf = _dot(f_ref[...], z.astype(BF16))
        a = zf[:L]
        b = zf[L:]
        p = p_ref[...]
        q = q_ref[...]
        yre = a * p - b * q
        yim = a * q + b * s_ref[...]
        ycat = jnp.concatenate([yre, yim], axis=0).astype(BF16)
        y = _dot(g_ref[...], ycat) + z * bias_ref[...]
        o_ref[rows, :] = sconv(x0_ref, w0_ref, b0_ref) * y


def _hyena(p_hy, short_w, short_b, bias, table_bf, table_t_bf, spec, L, nb, row0, seqs):
    ct = 256
    nc = HY_CH // ct
    P, Q, S = spec
    blk = seqs * L

    def xs(k):
        return pl.BlockSpec((blk, ct), lambda b, c: (row0 // blk + b, k * nc + c))

    def ws(rows, k):
        return pl.BlockSpec((rows, ct), lambda b, c: (0, k * nc + c))

    cs = pl.BlockSpec((L, ct), lambda b, c: (0, c))
    return pl.pallas_call(
        functools.partial(_hyena_kernel, L=L, seqs=seqs),
        out_shape=jax.ShapeDtypeStruct((nb * L, HY_CH), F32),
        grid=(nb // seqs, nc),
        in_specs=[xs(0), xs(1), xs(2), ws(3, 0), ws(3, 1), ws(3, 2), ws(1, 0), ws(1, 1), ws(1, 2), ws(1, 0),
                  pl.BlockSpec((2 * L, L), lambda b, c: (0, 0)),
                  pl.BlockSpec((L, 2 * L), lambda b, c: (0, 0)),
                  cs, cs, cs],
        out_specs=pl.BlockSpec((blk, ct), lambda b, c: (b, c)),
        compiler_params=_cparams(("arbitrary", "arbitrary")),
    )(p_hy, p_hy, p_hy, short_w, short_w, short_w, short_b, short_b, short_b, bias,
      table_bf, table_t_bf, P, Q, S)


def _rope_tables(L):
    half = HEAD_DIM // 2
    pos = jnp.arange(L)
    r = (pos // GRID_W).astype(F32)
    col = (pos % GRID_W).astype(F32)
    inv = ROPE_BASE ** (-jnp.arange(0, half, 2, dtype=F32) / half)
    ar = r[:, None] * inv[None, :]
    ac = col[:, None] * inv[None, :]
    z = jnp.zeros_like(ar)
    cos = jnp.concatenate([jnp.cos(ar), jnp.cos(ar), jnp.cos(ac), jnp.cos(ac)], axis=1)
    s1 = jnp.concatenate([-jnp.sin(ar), z, -jnp.sin(ac), z], axis=1)
    s2 = jnp.concatenate([z, jnp.sin(ar), z, jnp.sin(ac)], axis=1)
    rep = LANES // HEAD_DIM
    return jnp.tile(cos, (1, rep)), jnp.tile(s1, (1, rep)), jnp.tile(s2, (1, rep))


def _attn_kernel(*refs, L, Tq, H, KVH, Lc, window, use_sink, use_norm, use_rope, emit_k):
    G = H // KVH
    hd = HEAD_DIM
    per = LANES // hd
    pad = WINDOW if window else 0
    it = iter(refs)
    q_ref = next(it)
    kv_ref = next(it)
    ck_ref = cv_ref = sink_ref = qn_ref = kn_ref = bd_ref = rc_ref = rs1_ref = rs2_ref = ko_ref = None
    if Lc:
        ck_ref = next(it)
        cv_ref = next(it)
    if use_sink:
        sink_ref = next(it)
    if use_norm:
        qn_ref = next(it)
        kn_ref = next(it)
        bd_ref = next(it)
    if use_rope:
        rc_ref = next(it)
        rs1_ref = next(it)
        rs2_ref = next(it)
    o_ref = next(it)
    if emit_k:
        ko_ref = next(it)
    k_scr = next(it)
    v_scr = next(it)
    i = pl.program_id(1)
    lat0 = Lc + pad

    def head_norm(x, g_ref):
        hi, lo = _split(x * x)
        ms = _dot(hi, bd_ref[...]) + _dot(lo, bd_ref[...])
        return x * lax.rsqrt(ms + EPS) * g_ref[...]

    def rope(x, rows):
        return (x * rc_ref[rows, :] + pltpu.roll(x, LANES - 16, 1) * rs1_ref[rows, :]
                + pltpu.roll(x, 16, 1) * rs2_ref[rows, :])

    def put(scr, g, rows, x):
        scr[g, 0, rows, :] = x.astype(BF16)
        scr[g, 1, rows, :] = pltpu.roll(x, hd, 1).astype(BF16)

    @pl.when(i == 0)
    def _():
        n_kg = KVH // per
        if window:
            zpad = jnp.zeros((n_kg, per, pad, LANES), BF16)
            for scr in (k_scr, v_scr):
                scr[:, :, Lc:Lc + pad, :] = zpad
                scr[:, :, lat0 + L:lat0 + L + pad, :] = zpad
        for g in range(n_kg):
            cols = slice(LANES * g, LANES * (g + 1))
            kg = kv_ref[:, cols]
            if use_norm:
                kg = head_norm(kg, kn_ref)
                if emit_k:
                    ko_ref[:, cols] = kg
            if use_rope:
                kg = rope(kg, slice(None))
            put(k_scr, g, slice(lat0, lat0 + L), kg)
            put(v_scr, g, slice(lat0, lat0 + L), kv_ref[:, KVH * hd + LANES * g:KVH * hd + LANES * (g + 1)])
            if Lc:
                put(k_scr, g, slice(0, Lc), ck_ref[0][:, cols])
                put(v_scr, g, slice(0, Lc), cv_ref[0][:, cols])

    rows = pl.ds(pl.multiple_of(i * Tq, Tq), Tq)
    low = lax.broadcasted_iota(jnp.int32, (Tq, LANES), 1) < hd
    groups_per_kv = G // per
    hrow = lax.broadcasted_iota(jnp.int32, (groups_per_kv * Tq, 1), 0) // Tq
    for kvh in range(KVH):
        qgs = []
        for g in range(kvh * groups_per_kv, (kvh + 1) * groups_per_kv):
            qg = q_ref[:, LANES * g:LANES * (g + 1)]
            if use_norm:
                qg = head_norm(qg, qn_ref)
            if use_rope:
                qg = rope(qg, rows)
            qgs.append((qg * (hd ** -0.5)).astype(BF16))
        outs = []
        for half in range(per):
            keep = low if half == 0 else ~low
            qs = jnp.concatenate([jnp.where(keep, qg, jnp.zeros_like(qg)) for qg in qgs], axis=0)
            var = (kvh % per) ^ half
            kg = kvh // per
            if use_sink:
                sk = jnp.zeros((groups_per_kv * Tq, 1), F32)
                for n in range(groups_per_kv):
                    sk = jnp.where(hrow == n, sink_ref[0, (kvh * groups_per_kv + n) * per + half], sk)
            if window:
                W = Tq + 2 * pad
                wrows = pl.ds(pl.multiple_of(Lc + i * Tq, Tq), W)
                s_c = _dot(qs, k_scr[kg, var, 0:Lc, :], _NT)
                s_w = _dot(qs, k_scr[kg, var, wrows, :], _NT).reshape(groups_per_kv, Tq, W)
                r = lax.broadcasted_iota(jnp.int32, (Tq, W), 0)
                c = lax.broadcasted_iota(jnp.int32, (Tq, W), 1)
                kpos = i * Tq - pad + c
                valid = (kpos >= 0) & (kpos < L) & (jnp.abs(r + pad - c) <= WINDOW)
                s_w = jnp.where(valid[None], s_w, NEG_INF).reshape(groups_per_kv * Tq, W)
                m = jnp.maximum(jnp.max(s_c, axis=-1, keepdims=True), jnp.max(s_w, axis=-1, keepdims=True))
                if use_sink:
                    m = jnp.maximum(m, sk)
                e_c = jnp.exp(s_c - m)
                e_w = jnp.exp(s_w - m)
                den = jnp.sum(e_c, axis=-1, keepdims=True) + jnp.sum(e_w, axis=-1, keepdims=True)
                o = (_dot(e_c.astype(BF16), v_scr[kg, var, 0:Lc, :])
                     + _dot(e_w.astype(BF16), v_scr[kg, var, wrows, :]))
            else:
                s = _dot(qs, k_scr[kg, var], _NT)
                m = jnp.max(s, axis=-1, keepdims=True)
                if use_sink:
                    m = jnp.maximum(m, sk)
                e = jnp.exp(s - m)
                den = jnp.sum(e, axis=-1, keepdims=True)
                o = _dot(e.astype(BF16), v_scr[kg, var])
            if use_sink:
                den = den + jnp.exp(sk - m)
            outs.append(o / den)
        for n in range(groups_per_kv):
            g = kvh * groups_per_kv + n
            o_ref[:, LANES * g:LANES * (g + 1)] = jnp.where(low, outs[0][n * Tq:(n + 1) * Tq],
                                                            outs[1][n * Tq:(n + 1) * Tq])


def _attention(p, q_cols, kv_colblk, *, L, Tq, nb, row0, H, KVH, ctx=None, sink=None, norm=None, rope=None,
               window=False, emit_k=False):
    hd = HEAD_DIM
    nq = L // Tq
    Lc = 0 if ctx is None else ctx[0].shape[1]
    pad = WINDOW if window else 0
    Lk = Lc + L + 2 * pad
    kvw = 2 * KVH * hd
    args = [p, p]
    in_specs = [pl.BlockSpec((Tq, q_cols), lambda b, i: (row0 // Tq + b * nq + i, 0)),
                pl.BlockSpec((L, kvw), lambda b, i: (row0 // L + b, kv_colblk))]
    if ctx is not None:
        for a in ctx:
            args.append(a)
            in_specs.append(pl.BlockSpec((1, Lc, KVH * hd), lambda b, i: (b, 0, 0)))
    if sink is not None:
        args.append(sink.reshape(1, H))
        in_specs.append(pl.BlockSpec(memory_space=pltpu.SMEM))
    if norm is not None:
        bd = (jnp.arange(LANES)[:, None] // hd == jnp.arange(LANES)[None, :] // hd).astype(F32) / hd
        for a in (jnp.tile(norm[0], LANES // hd).reshape(1, LANES), jnp.tile(norm[1], LANES // hd).reshape(1, LANES)):
            args.append(a)
            in_specs.append(pl.BlockSpec((1, LANES), lambda b, i: (0, 0)))
        args.append(bd.astype(BF16))
        in_specs.append(pl.BlockSpec((LANES, LANES), lambda b, i: (0, 0)))
    if rope is not None:
        for a in rope:
            args.append(a)
            in_specs.append(pl.BlockSpec((L, LANES), lambda b, i: (0, 0)))
    out_shape = [jax.ShapeDtypeStruct((nb * L, H * hd), F32)]
    out_specs = [pl.BlockSpec((Tq, H * hd), lambda b, i: (b * nq + i, 0))]
    if emit_k:
        out_shape.append(jax.ShapeDtypeStruct((nb * L, KVH * hd), F32))
        out_specs.append(pl.BlockSpec((L, KVH * hd), lambda b, i: (b, 0)))
    kern = functools.partial(_attn_kernel, L=L, Tq=Tq, H=H, KVH=KVH, Lc=Lc, window=window,
                             use_sink=sink is not None, use_norm=norm is not None,
                             use_rope=rope is not None, emit_k=emit_k)
    call = pl.pallas_call(
        kern, out_shape=out_shape, grid=(nb, nq), in_specs=in_specs, out_specs=out_specs,
        scratch_shapes=[pltpu.VMEM((KVH * hd // LANES, LANES // hd, Lk, LANES), BF16),
                        pltpu.VMEM((KVH * hd // LANES, LANES // hd, Lk, LANES), BF16)],
        compiler_params=_cparams(("arbitrary", "arbitrary")))
    return call, args


def _oproj_kernel(*refs, paired):
    rows = slice(None)
    al, rest = _take_rows(refs, paired[0], rows)
    ar, rest = _take_rows(rest, paired[1], rows)
    x, rest = _take_rows(rest, paired[2], rows)
    w_ref, g1_ref, gate_ref, g2_ref, sh_ref, sc_ref, wr_ref, x1_ref, h2_ref, lg_ref = rest
    k = al.shape[1]
    y = _dot(al.astype(BF16), w_ref[0:k, :]) + _dot(ar.astype(BF16), w_ref[k:2 * k, :])
    x1 = x + gate_ref[0] * _rms(y, g1_ref[...])
    x1_ref[...] = x1
    h2 = _rms(x1, g2_ref[...]) * (1.0 + sc_ref[0]) + sh_ref[0]
    wh, wl = _split(wr_ref[...])
    hb = h2.astype(BF16)
    h2_ref[...] = hb
    lg_ref[...] = _dot(wh, hb, _NT) + _dot(wl, hb, _NT)


def _out_proj(a_l, l_blk, a_r, r_blk, w_bf, x, g1, g2, mod, layer, wr_t):
    tm = ROW_TILE
    half = D_MODEL // 2
    l_specs, l_args = _row_specs(a_l, half, l_blk)
    r_specs, r_args = _row_specs(a_r, half, r_blk)
    x_specs, x_args = _row_specs(x, D_MODEL)
    paired = tuple(isinstance(a, tuple) for a in (a_l, a_r, x))
    return pl.pallas_call(
        functools.partial(_oproj_kernel, paired=paired),
        out_shape=[jax.ShapeDtypeStruct((NT, D_MODEL), F32), jax.ShapeDtypeStruct((NT, D_MODEL), BF16),
                   jax.ShapeDtypeStruct((N_EXPERTS, NT), F32)],
        grid=(NT // tm,),
        in_specs=l_specs + r_specs + x_specs + [
            pl.BlockSpec((D_MODEL, D_MODEL), lambda i: (0, 0)),
            pl.BlockSpec((1, D_MODEL), lambda i: (0, 0)),
            _mod_spec(layer, 2, tm),
            pl.BlockSpec((1, D_MODEL), lambda i: (0, 0)),
            _mod_spec(layer, 3, tm), _mod_spec(layer, 4, tm),
            pl.BlockSpec((N_EXPERTS, D_MODEL), lambda i: (0, 0))],
        out_specs=[pl.BlockSpec((tm, D_MODEL), lambda i: (i, 0)),
                   pl.BlockSpec((tm, D_MODEL), lambda i: (i, 0)),
                   pl.BlockSpec((N_EXPERTS, tm), lambda i: (0, i))],
        compiler_params=_cparams(("arbitrary",)),
    )(*l_args, *r_args, *x_args, w_bf, g1.reshape(1, D_MODEL), mod, g2.reshape(1, D_MODEL), mod, mod, wr_t)


COMB_ROWS = 512
COMB_WIN = 128
BF16_SUB = 16
F32_SUB = 8


def _ffn_kernel(a0_ref, s1_ref, slot_ref, aff_ref, h_ref, wg_ref, wu_ref, wd_ref, y_ref,
                xn_scr, gn_scr, xc_scr, gc_scr, acc_scr, *, cap, nf, nblk):
    grp = pl.program_id(0)
    f = pl.program_id(1)
    R = COMB_WIN
    per_step = nblk // nf
    erow = jnp.minimum(grp, N_EXPERTS - 1) % F32_SUB

    def add_window(jb, c0, hit_of):
        cols = slice(jb * COMB_ROWS, (jb + 1) * COMB_ROWS)
        q = lax.broadcasted_iota(jnp.int32, (R, COMB_ROWS), 0)
        hit = hit_of(slot_ref[pl.ds(erow, 1), cols].astype(jnp.int32), q)
        rows = pl.ds(c0, R)
        piece = _dot(hit.astype(F32).astype(BF16), h_ref[cols, :])
        xn_scr[rows, :] += piece.astype(BF16)
        gn_scr[rows, :] += jnp.sum(jnp.where(hit, aff_ref[pl.ds(erow, 1), cols], 0.0), axis=1, keepdims=True)

    def window_start(jb):
        return pl.multiple_of(a0_ref[grp * nblk + f * per_step + jb], BF16_SUB)

    def gather():
        for jb in range(per_step):
            a0 = window_start(jb)
            add_window(jb, a0, lambda srow, q, a0=a0: srow - a0 == q)

    def gather_more():
        for jb in range(per_step):
            end = s1_ref[grp * nblk + f * per_step + jb]

            def extra(c, jb=jb):
                c0 = pl.multiple_of(jnp.minimum(c, cap - R), BF16_SUB)
                add_window(jb, c0, lambda srow, q: (srow - c0 == q) & (q + c0 >= c))
                return c + R

            lax.while_loop(lambda c, end=end: c < end, extra, window_start(jb) + R)

    def compute(first):
        wg = wg_ref[0, 0].astype(BF16)
        wu = wu_ref[0, 0].astype(BF16)
        wd = wd_ref[0, 0].astype(BF16)
        part_rows = cap // 4
        for r in range(4):
            rows = slice(r * part_rows, (r + 1) * part_rows)
            xe = xc_scr[rows, :]
            hmid = (jax.nn.silu(_dot(xe, wg)) * _dot(xe, wu)).astype(BF16)
            part = _dot(hmid, wd)
            if first:
                acc_scr[rows, :] = part
            else:
                acc_scr[rows, :] += part

    @pl.when(f == 0)
    def _():
        xn_scr[...] = jnp.zeros_like(xn_scr)
        gn_scr[...] = jnp.zeros_like(gn_scr)

    @pl.when(grp == 0)
    def _():
        gather()

    for first in (True, False):
        first_step = (f == 0) if first else (f != 0)

        @pl.when((grp > 0) & (grp < N_EXPERTS) & first_step)
        def _():
            gather()
            compute(first)

        @pl.when((grp == N_EXPERTS) & first_step)
        def _():
            compute(first)

    @pl.when(grp < N_EXPERTS)
    def _():
        gather_more()

    @pl.when(f == nf - 1)
    def _():
        @pl.when(grp > 0)
        def _():
            y_ref[...] = (acc_scr[...] * gc_scr[...]).astype(BF16)

        xc_scr[...] = xn_scr[...]
        gc_scr[...] = gn_scr[...]


def _expert_ffn(a0, s1, slot, aff, h2, layer, w_gate, w_up, w_down):
    cap = EC_CAPACITY * NT // N_EXPERTS
    fc = 512
    nf = EXPERT_FF // fc
    nblk = NT // COMB_ROWS
    tok = NT // nf

    def prev(g):
        return jnp.maximum(g - 1, 0)

    def this(g):
        return jnp.minimum(g, N_EXPERTS - 1)

    grid_spec = pltpu.PrefetchScalarGridSpec(
        num_scalar_prefetch=2,
        grid=(N_EXPERTS + 1, nf),
        in_specs=[pl.BlockSpec((F32_SUB, tok), lambda g, f, *_: (this(g) // F32_SUB, f)),
                  pl.BlockSpec((F32_SUB, tok), lambda g, f, *_: (this(g) // F32_SUB, f)),
                  pl.BlockSpec((tok, D_MODEL), lambda g, f, *_: (f, 0)),
                  pl.BlockSpec((1, 1, D_MODEL, fc), lambda g, f, *_: (layer, prev(g), 0, f)),
                  pl.BlockSpec((1, 1, D_MODEL, fc), lambda g, f, *_: (layer, prev(g), 0, f)),
                  pl.BlockSpec((1, 1, fc, D_MODEL), lambda g, f, *_: (layer, prev(g), f, 0))],
        out_specs=pl.BlockSpec((cap, D_MODEL), lambda g, f, *_: (prev(g), 0)),
        scratch_shapes=[pltpu.VMEM((cap, D_MODEL), BF16), pltpu.VMEM((cap, 1), F32),
                        pltpu.VMEM((cap, D_MODEL), BF16), pltpu.VMEM((cap, 1), F32),
                        pltpu.VMEM((cap, D_MODEL), F32)])
    return pl.pallas_call(
        functools.partial(_ffn_kernel, cap=cap, nf=nf, nblk=nblk),
        out_shape=jax.ShapeDtypeStruct((N_EXPERTS * cap, D_MODEL), BF16),
        grid_spec=grid_spec,
        compiler_params=_cparams(("arbitrary", "arbitrary")),
    )(a0.reshape(-1), s1.reshape(-1), slot, aff, h2,
      w_gate, w_up, w_down)


CMB_ROWS = COMB_ROWS
CMB_WIN = LANES


def _combine_kernel(a0_ref, s1_ref, slot_ref, y_hbm, x_ref, g_ref, gate_ref, *rest, cap, nblk):
    *o_refs, win_scr, ovf_scr, f_scr, sem, osem = rest
    j = pl.program_id(0)
    E, R, Tb = N_EXPERTS, CMB_WIN, CMB_ROWS

    def window(e, start, dst, s):
        return pltpu.make_async_copy(y_hbm.at[pl.ds(e * cap + start, R)], dst, s)

    def issue(blk, slot):
        for e in range(E):
            a0 = pl.multiple_of(a0_ref[e * nblk + blk], BF16_SUB)
            window(e, a0, win_scr.at[slot, pl.ds(e * R, R)], sem.at[slot]).start()

    @pl.when(j == 0)
    def _():
        issue(0, 0)

    slot = j % 2
    pltpu.make_async_copy(y_hbm.at[pl.ds(0, E * R)], win_scr.at[slot], sem.at[slot]).wait()

    @pl.when(j + 1 < nblk)
    def _():
        issue(j + 1, 1 - slot)

    sl = slot_ref[...].astype(jnp.int32)
    q = lax.broadcasted_iota(jnp.int32, (Tb, R), 1)
    onehot = jnp.concatenate(
        [(sl[:, e:e + 1] - a0_ref[e * nblk + j] == q) for e in range(E)], axis=1).astype(F32).astype(BF16)
    f_scr[...] = _dot(onehot, win_scr[slot])

    for e in range(E):
        end = s1_ref[e * nblk + j]

        def more(c):
            return c < end

        def extra(c, e=e):
            c0 = pl.multiple_of(jnp.minimum(c, cap - R), BF16_SUB)
            cp = window(e, c0, ovf_scr, osem)
            cp.start()
            cp.wait()
            hit = (sl[:, e:e + 1] - c0 == q) & (q + c0 >= c)
            f_scr[...] += _dot(hit.astype(F32).astype(BF16), ovf_scr[...])
            return c + R

        lax.while_loop(more, extra, a0_ref[e * nblk + j] + R)

    res = x_ref[...] + gate_ref[0] * _rms(f_scr[...], g_ref[...])
    if len(o_refs) == 1:
        o_refs[0][...] = res
    else:
        @pl.when(j < NS // Tb)
        def _():
            o_refs[0][...] = res

        @pl.when(j >= NS // Tb)
        def _():
            o_refs[1][...] = res


def _combine(a0, s1, slot_t, y, x, g, mod, layer, split_out=False):
    cap = y.shape[0] // N_EXPERTS
    nblk = NT // CMB_ROWS
    if split_out:
        out_shape = [jax.ShapeDtypeStruct((NS, D_MODEL), F32), jax.ShapeDtypeStruct((NP, D_MODEL), F32)]
        out_specs = _stream_specs(D_MODEL, tile=CMB_ROWS)
    else:
        out_shape = jax.ShapeDtypeStruct((NT, D_MODEL), F32)
        out_specs = pl.BlockSpec((CMB_ROWS, D_MODEL), lambda i, *_: (i, 0))
    grid_spec = pltpu.PrefetchScalarGridSpec(
        num_scalar_prefetch=2,
        grid=(nblk,),
        in_specs=[pl.BlockSpec((CMB_ROWS, LANES), lambda i, *_: (i, 0)),
                  pl.BlockSpec(memory_space=pl.ANY),
                  pl.BlockSpec((CMB_ROWS, D_MODEL), lambda i, *_: (i, 0)),
                  pl.BlockSpec((1, D_MODEL), lambda i, *_: (0, 0)),
                  _mod_spec(layer, 5, CMB_ROWS)],
        out_specs=out_specs,
        scratch_shapes=[pltpu.VMEM((2, N_EXPERTS * CMB_WIN, D_MODEL), BF16),
                        pltpu.VMEM((CMB_WIN, D_MODEL), BF16),
                        pltpu.VMEM((CMB_ROWS, D_MODEL), F32),
                        pltpu.SemaphoreType.DMA((2,)), pltpu.SemaphoreType.DMA(())])
    return pl.pallas_call(
        functools.partial(_combine_kernel, cap=cap, nblk=nblk),
        out_shape=out_shape,
        grid_spec=grid_spec,
        compiler_params=_cparams(("arbitrary",)),
    )(a0.reshape(-1), s1.reshape(-1), slot_t, y, x, g.reshape(1, D_MODEL), mod)


CUM_BLK = 256
SCALE_STEP = 2.0 ** -16
SCALE_ITERS = 10
BISECT_ITERS = 64


def _prefix_count(mask):
    n = mask.shape[1]
    tri = (lax.broadcasted_iota(jnp.int32, (CUM_BLK, CUM_BLK), 0)
           <= lax.broadcasted_iota(jnp.int32, (CUM_BLK, CUM_BLK), 1)).astype(F32).astype(BF16)
    carry = jnp.zeros((mask.shape[0], 1), F32)
    outs = []
    for b in range(n // CUM_BLK):
        c = _dot(mask[:, b * CUM_BLK:(b + 1) * CUM_BLK].astype(F32).astype(BF16), tri) + carry
        outs.append(c)
        carry = c[:, CUM_BLK - 1:CUM_BLK]
    return jnp.concatenate(outs, axis=1)


def _route_kernel(lg_ref, aff_ref, slot_ref, slott_ref, s0_ref):
    x = lg_ref[...]
    ex = jnp.exp(x - jnp.max(x, axis=0, keepdims=True))
    aff = ex / jnp.sum(ex, axis=0, keepdims=True)
    aff_ref[...] = aff
    streams = ((0, NS), (NS, NP))
    caps = [float(EC_CAPACITY * n // N_EXPERTS) for _, n in streams]
    parts = [aff[:, lo:lo + n] for lo, n in streams]

    def enough(k, thr):
        return jnp.sum((parts[k] >= thr).astype(F32), axis=1, keepdims=True) >= caps[k]

    def scale_step(_, carry):
        out = []
        for k, (lo_v, hi_v, found_f) in enumerate(carry):
            found = found_f > 0.5
            mid = hi_v * SCALE_STEP
            ge = enough(k, mid)
            out.append((jnp.where(found | ~ge, lo_v, mid), jnp.where(found | ge, hi_v, mid),
                        jnp.where(found | ge, 1.0, 0.0)))
        return tuple(out)

    def bisect_step(_, carry):
        out = []
        for k, (lo_v, hi_v) in enumerate(carry):
            mid = 0.5 * (lo_v + hi_v)
            ge = enough(k, mid)
            out.append((jnp.where(ge, mid, lo_v), jnp.where(ge, hi_v, mid)))
        return tuple(out)

    col = lambda v: jnp.full((N_EXPERTS, 1), v, F32)
    start = tuple((col(0.0), col(2.0), col(0.0)) for _ in streams)
    scaled = lax.fori_loop(0, SCALE_ITERS, scale_step, start)
    bounds = lax.fori_loop(0, BISECT_ITERS, bisect_step, tuple((lo_v, hi_v) for lo_v, hi_v, _ in scaled))
    slots = []
    off = 0.0
    for k, (lo_v, hi_v) in enumerate(bounds):
        above = parts[k] >= hi_v
        edge = (parts[k] >= lo_v) & ~above
        need = caps[k] - jnp.sum(above.astype(F32), axis=1, keepdims=True)
        edge_rank = _prefix_count(edge) - edge.astype(F32)
        sel = above | (edge & (edge_rank < need))
        slots.append(jnp.where(sel, _prefix_count(sel) - 1.0 + off, -1.0))
        off += caps[k]
    slot = jnp.concatenate(slots, axis=1)
    slot_ref[...] = slot
    pad = jnp.full((LANES - N_EXPERTS, NT), -1.0, F32)
    slott_ref[...] = jnp.concatenate([slot, pad], axis=0).T
    tok = lax.broadcasted_iota(jnp.int32, (NT, LANES), 0)
    edge = lax.broadcasted_iota(jnp.int32, (NT, LANES), 1) * CMB_ROWS
    s0_ref[...] = _dot((slot >= 0.0).astype(F32).astype(BF16), (tok < edge).astype(F32).astype(BF16))


def _route_tables(logits_t):
    return pl.pallas_call(
        _route_kernel,
        out_shape=[jax.ShapeDtypeStruct((N_EXPERTS, NT), F32), jax.ShapeDtypeStruct((N_EXPERTS, NT), F32),
                   jax.ShapeDtypeStruct((NT, LANES), F32), jax.ShapeDtypeStruct((N_EXPERTS, LANES), F32)],
        compiler_params=pltpu.CompilerParams(vmem_limit_bytes=VMEM_LIMIT),
    )(logits_t)


def _route(logits_t, h2, x, g, mod, layer, w_gate, w_up, w_down, split_out=False):
    aff, slot, slot_t, s0 = _route_tables(logits_t)
    cap = EC_CAPACITY * NT // N_EXPERTS
    s0 = s0[:, :NT // CMB_ROWS + 1].astype(jnp.int32)

    def windows(starts, win):
        return jnp.minimum(starts[:, :-1] // BF16_SUB * BF16_SUB, cap - win), starts[:, 1:]

    a0g, s1g = windows(s0[:, ::COMB_ROWS // CMB_ROWS], COMB_WIN)
    y = _expert_ffn(a0g, s1g, slot, aff, h2, layer, w_gate, w_up, w_down)
    a0c, s1c = windows(s0, CMB_WIN)
    return _combine(a0c, s1c, slot_t, y, x, g, mod, layer, split_out)


def kernel(x_prompt, x_sample, cache_b_k, cache_b_v, cache_c_k, cache_c_v, c, c_ctx, ada_w, ada_b, norm_g,
           ab_w_in, hy_short_w, hy_short_b, hy_w1, hy_b1, hy_w2, hy_b2, hy_freq, hy_w3, hy_log_decay, hy_bias,
           b_sink, ab_w_out, c_w_qkv, c_q_norm, c_k_norm, c_w_out, ec_router, ec_w_gate, ec_w_up, ec_w_down):
    x = (x_sample.reshape(NS, D_MODEL), x_prompt.reshape(NP, D_MODEL))
    cond = jnp.concatenate([c, c_ctx[None], jnp.zeros((MOD_ROWS - DEC_BATCH - 1, D_MODEL), F32)], axis=0)
    mod = _adaln(cond, ada_w, ada_b).reshape(DEPTH * MOD_ROWS * 6, 1, D_MODEL)
    rope = _rope_tables(DEC_SEQ)
    hd = HEAD_DIM

    e = 0
    p_hy, p_at = _norm_mod_matmul(x, norm_g[0, 0], mod, 0, ab_w_in[e].astype(BF16), (3 * HY_CH, D_MODEL - HY_CH + 2 * B_KV * hd))
    ya = []
    for L, nb, row0, seqs in ((DEC_SEQ, DEC_BATCH, 0, 2), (SEQ, BATCH, NS, 4)):
        table_lo, table_bf, table_t_bf = _dft_tables(L)
        h = _hyena_filters(L, hy_w1[e], hy_b1[e], hy_w2[e], hy_b2[e], hy_freq[e], hy_w3[e], hy_log_decay[e])
        h = h.at[0, HY_CH:].set(0.0)
        spec = _filter_spectrum(L, table_bf, table_lo, h)
        ya.append(_hyena(p_hy, hy_short_w[e], hy_short_b[e].reshape(1, -1), hy_bias[e].reshape(1, -1),
                         table_bf, table_t_bf, spec, L, nb, row0, seqs))
    ya = tuple(ya)
    ctx_b = (cache_b_k[:, e].reshape(DEC_BATCH, PAST_LEN, B_KV * hd), cache_b_v[:, e].reshape(DEC_BATCH, PAST_LEN, B_KV * hd))
    call, args = _attention(p_at, B_HEADS * hd, 2, L=DEC_SEQ, Tq=256, nb=DEC_BATCH, row0=0, H=B_HEADS, KVH=B_KV,
                            ctx=ctx_b, sink=b_sink[e], rope=rope, window=True)
    yb_s = call(*args)[0]
    call, args = _attention(p_at, B_HEADS * hd, 2, L=SEQ, Tq=256, nb=BATCH, row0=NS, H=B_HEADS, KVH=B_KV,
                            sink=b_sink[e])
    yb_p = call(*args)[0]
    yb = (yb_s, yb_p)
    state_b_k = p_at[NS:, B_HEADS * hd:(B_HEADS + B_KV) * hd].reshape(BATCH, 1, SEQ, B_KV, hd)
    state_b_v = p_at[NS:, (B_HEADS + B_KV) * hd:].reshape(BATCH, 1, SEQ, B_KV, hd)
    x, h2, lg = _out_proj(ya, 0, yb, 0, ab_w_out[e].astype(BF16), x, norm_g[0, 1], norm_g[0, 2], mod, 0, ec_router[0].T)
    x = _route(lg, h2, x, norm_g[0, 3], mod, 0, ec_w_gate, ec_w_up, ec_w_down)

    o = 0
    (p,) = _norm_mod_matmul(x, norm_g[1, 0], mod, 1, c_w_qkv[o].astype(BF16), ((C_HEADS + 2 * C_KV) * hd,))
    ctx_c = (cache_c_k[:, o].reshape(DEC_BATCH, PAST_LEN, C_KV * hd), cache_c_v[:, o].reshape(DEC_BATCH, PAST_LEN, C_KV * hd))
    nrm = (c_q_norm[o], c_k_norm[o])
    call, args = _attention(p, C_HEADS * hd, 2, L=DEC_SEQ, Tq=256, nb=DEC_BATCH, row0=0, H=C_HEADS, KVH=C_KV,
                            ctx=ctx_c, norm=nrm, rope=rope)
    yc_s = call(*args)[0]
    call, args = _attention(p, C_HEADS * hd, 2, L=SEQ, Tq=256, nb=BATCH, row0=NS, H=C_HEADS, KVH=C_KV,
                            norm=nrm, emit_k=True)
    yc_p, kn = call(*args)
    yc = (yc_s, yc_p)
    state_c_k = kn.reshape(BATCH, 1, SEQ, C_KV, hd)
    state_c_v = p[NS:, (C_HEADS + C_KV) * hd:].reshape(BATCH, 1, SEQ, C_KV, hd)
    x, h2, lg = _out_proj(yc, 0, yc, 1, c_w_out[o].astype(BF16), x, norm_g[1, 1], norm_g[1, 2], mod, 1, ec_router[1].T)
    xs, xp = _route(lg, h2, x, norm_g[1, 3], mod, 1, ec_w_gate, ec_w_up, ec_w_down, split_out=True)
    y_sample = xs.reshape(DEC_BATCH, DEC_SEQ, D_MODEL)
    y_prompt = xp.reshape(BATCH, SEQ, D_MODEL)
    return (y_prompt, y_sample, state_b_k, state_b_v, state_c_k, state_c_v)
```

```python
import functools
import math

import jax
import jax.numpy as jnp
from jax import lax
from jax.experimental import pallas as pl
from jax.experimental.pallas import tpu as pltpu

F32 = jnp.float32
BF16 = jnp.bfloat16

D_MODEL = 1024
BATCH = 16
SEQ = 256
DEPTH = 2
DEC_BATCH = 8
DEC_SEQ = 1024
PAST_LEN = 512
GRID_W = 64
HEAD_DIM = 64
HY_CH = D_MODEL // 2
HY_BANDS = 8
B_HEADS = 8
B_KV = 2
WINDOW = 128
C_HEADS = 16
C_KV = 4
ROPE_BASE = 10000.0
N_EXPERTS = 16
EC_CAPACITY = 2
EXPERT_FF = 2 * D_MODEL
EPS = 1e-6
NEG_INF = -1e30

NS = DEC_BATCH * DEC_SEQ
NP = BATCH * SEQ
NT = NS + NP
MOD_ROWS = 16
LANES = 128
VMEM_LIMIT = 56 * 1024 * 1024


def _cparams(sem):
    return pltpu.CompilerParams(dimension_semantics=sem, vmem_limit_bytes=VMEM_LIMIT)


def _split(x):
    hi = x.astype(BF16)
    lo = (x - hi.astype(F32)).astype(BF16)
    return hi, lo


_NN = (((1,), (0,)), ((), ()))
_NT = (((1,), (1,)), ((), ()))


def _dot(a, b, dims=_NN):
    return lax.dot_general(a, b, dims, preferred_element_type=F32)


def _dot3(a, b, dims=_NN):
    ah, al = _split(a)
    bh, bl = _split(b)
    return _dot(ah, bh, dims) + _dot(al, bh, dims) + _dot(ah, bl, dims)


def _rms(x, g):
    return x * lax.rsqrt(jnp.mean(x * x, axis=-1, keepdims=True) + EPS) * g


def _mod_row(tile_rows):
    def f(i):
        return jnp.minimum(i * tile_rows // DEC_SEQ, DEC_BATCH)
    return f


def _adaln_kernel(c_ref, w_ref, b_ref, o_ref):
    s = jax.nn.silu(c_ref[...])
    o_ref[0] = _dot3(s, w_ref[0]) + b_ref[0]


def _adaln(cond, ada_w, ada_b):
    tn = 1536
    return pl.pallas_call(
        _adaln_kernel,
        out_shape=jax.ShapeDtypeStruct((DEPTH, MOD_ROWS, 6 * D_MODEL), F32),
        grid=(DEPTH, 6 * D_MODEL // tn),
        in_specs=[pl.BlockSpec((MOD_ROWS, D_MODEL), lambda l, j: (0, 0)),
                  pl.BlockSpec((1, D_MODEL, tn), lambda l, j: (l, 0, j)),
                  pl.BlockSpec((1, 1, tn), lambda l, j: (l, 0, j))],
        out_specs=pl.BlockSpec((1, MOD_ROWS, tn), lambda l, j: (l, 0, j)),
        compiler_params=_cparams(("arbitrary", "arbitrary")),
    )(cond, ada_w, ada_b.reshape(DEPTH, 1, 6 * D_MODEL))


def _mod_spec(layer, which, tile_rows):
    row = _mod_row(tile_rows)
    return pl.BlockSpec((1, 1, D_MODEL), lambda i, *_: ((layer * MOD_ROWS + row(i)) * 6 + which, 0, 0))


ROW_TILE = 512
NS_TILES = NS // ROW_TILE


def _stream_specs(width, col=0, tile=ROW_TILE):
    ns = NS // tile
    return [pl.BlockSpec((tile, width), lambda i, *_: (jnp.minimum(i, ns - 1), col)),
            pl.BlockSpec((tile, width), lambda i, *_: (jnp.maximum(i - ns, 0), col))]


def _row_specs(a, width, col=0):
    if isinstance(a, tuple):
        return _stream_specs(width, col), list(a)
    return [pl.BlockSpec((ROW_TILE, width), lambda i, *_: (i, col))], [a]


ROW_CHUNK = 128


def _take_rows(refs, paired, rows, stream):
    if paired:
        return refs[stream][rows, :], refs[2:]
    return refs[0][rows, :], refs[1:]


def _per_stream(any_paired, body):
    if not any_paired:
        body(0)
        return

    @pl.when(pl.program_id(0) < NS_TILES)
    def _():
        body(0)

    @pl.when(pl.program_id(0) >= NS_TILES)
    def _():
        body(1)


def _nmm_kernel(*refs, paired):
    def body(stream):
        for c in range(ROW_TILE // ROW_CHUNK):
            rows = slice(c * ROW_CHUNK, (c + 1) * ROW_CHUNK)
            x, (g_ref, sh_ref, sc_ref, w_ref, *o_refs) = _take_rows(refs, paired, rows, stream)
            h = _rms(x, g_ref[...]) * (1.0 + sc_ref[0]) + sh_ref[0]
            p = _dot(h.astype(BF16), w_ref[...])
            off = 0
            for o in o_refs:
                n = o.shape[1]
                o[rows, :] = p[:, off:off + n]
                off += n

    _per_stream(paired, body)


def _norm_mod_matmul(x, g, mod, layer, w_bf, splits):
    tm = ROW_TILE
    n_out = w_bf.shape[1]
    x_specs, x_args = _row_specs(x, D_MODEL)
    return pl.pallas_call(
        functools.partial(_nmm_kernel, paired=isinstance(x, tuple)),
        out_shape=[jax.ShapeDtypeStruct((NT, n), F32) for n in splits],
        grid=(NT // tm,),
        in_specs=x_specs + [pl.BlockSpec((1, D_MODEL), lambda i: (0, 0)),
                            _mod_spec(layer, 0, tm), _mod_spec(layer, 1, tm),
                            pl.BlockSpec((D_MODEL, n_out), lambda i: (0, 0))],
        out_specs=[pl.BlockSpec((tm, n), lambda i: (i, 0)) for n in splits],
        compiler_params=_cparams(("arbitrary",)),
    )(*x_args, g.reshape(1, D_MODEL), mod, mod, w_bf)


def _hyena_filters(L, w1, b1, w2, b2, freq, w3, log_decay):
    hp = lax.Precision.HIGHEST
    t = jnp.linspace(0.0, 1.0, L, dtype=F32)[:, None]
    w = (2.0 * math.pi / L) * jnp.arange(L, dtype=F32)[:, None]
    bands = jnp.linspace(1e-4, HY_BANDS - 1, HY_BANDS, dtype=F32)[None, :]
    z = jnp.concatenate([t, jnp.cos(bands * w), -jnp.sin(bands * w)], axis=-1)
    h = jnp.sin(freq[0] * (jnp.dot(z, w1, precision=hp) + b1))
    h = jnp.sin(freq[1] * (jnp.dot(h, w2, precision=hp) + b2))
    return jnp.dot(h, w3, precision=hp) * jnp.exp(-t * jnp.exp(log_decay))


def _dft_kernel(fl_ref, fb_ref, gb_ref, *, L):
    n = 2 * L
    w = 2.0 * math.pi / n
    r = lax.broadcasted_iota(jnp.int32, (L, LANES), 0)
    b = lax.broadcasted_iota(jnp.int32, (L, LANES), 1)
    r1 = lax.broadcasted_iota(jnp.int32, (L, 1), 0)
    ang_b = ((r * b) & (n - 1)).astype(F32) * w
    cb = jnp.cos(ang_b)
    sb = jnp.sin(ang_b)
    by_col = jnp.where((b & 1) == 0, 1.0, -1.0)
    by_row = jnp.where((r & 1) == 0, 1.0, -1.0)
    for a in range(L // LANES):
        ang_a = ((r1 * (a * LANES)) & (n - 1)).astype(F32) * w
        ca = jnp.cos(ang_a)
        sa = jnp.sin(ang_a)
        cs = ca * cb - sa * sb
        sn = -(sa * cb + ca * sb)
        cols = slice(a * LANES, (a + 1) * LANES)
        fs = jnp.where(r == 0, by_col, sn)
        cs_hi, cs_lo = _split(cs)
        fs_hi, fs_lo = _split(fs)
        fb_ref[0:L, cols] = cs_hi
        fb_ref[L:2 * L, cols] = fs_hi
        fl_ref[0:L, cols] = cs_lo
        fl_ref[L:2 * L, cols] = fs_lo
        gb_ref[:, cols] = cs_hi
        gs = jnp.where(b == 0, by_row, sn) if a == 0 else sn
        gb_ref[:, L + a * LANES:L + (a + 1) * LANES] = gs.astype(BF16)


def _dft_tables(L):
    assert L & (L - 1) == 0
    return pl.pallas_call(
        functools.partial(_dft_kernel, L=L),
        out_shape=[jax.ShapeDtypeStruct((2 * L, L), BF16), jax.ShapeDtypeStruct((2 * L, L), BF16),
                   jax.ShapeDtypeStruct((L, 2 * L), BF16)],
        compiler_params=pltpu.CompilerParams(vmem_limit_bytes=VMEM_LIMIT),
    )()


def _filt_kernel(fb_ref, fl_ref, hf_ref, hb_ref, p_ref, q_ref, s_ref, *, L):
    hf = hf_ref[...]
    hb = hb_ref[...]
    ct = hf.shape[1]
    nrm = lax.rsqrt(jnp.sum(hf * hf, axis=0, keepdims=True) + jnp.sum(hb * hb, axis=0, keepdims=True) + EPS)
    hh, hl = _split(jnp.concatenate([hf, hb], axis=1))
    fb = fb_ref[...]
    fh = _dot(fb, hh) + _dot(fl_ref[...], hh) + _dot(fb, hl)
    row = lax.broadcasted_iota(jnp.int32, (L, ct), 0)
    w = jnp.where(row == 0, 1.0 / (2 * L), 2.0 / (2 * L)) * nrm
    re = fh[:L, :ct] + fh[:L, ct:]
    p_ref[...] = re * w
    q_ref[...] = jnp.where(row == 0, 0.0, fh[L:, :ct] - fh[L:, ct:]) * w
    s_ref[...] = jnp.where(row == 0, fh[L:, :ct] + fh[L:, ct:], re) * w


def _filter_spectrum(L, table_hi, table_lo, h):
    ct = 128
    nc = HY_CH // ct
    shp = jax.ShapeDtypeStruct((L, HY_CH), F32)
    return pl.pallas_call(
        functools.partial(_filt_kernel, L=L),
        out_shape=[shp, shp, shp],
        grid=(nc,),
        in_specs=[pl.BlockSpec((2 * L, L), lambda c: (0, 0)),
                  pl.BlockSpec((2 * L, L), lambda c: (0, 0)),
                  pl.BlockSpec((L, ct), lambda c: (0, c)),
                  pl.BlockSpec((L, ct), lambda c: (0, nc + c))],
        out_specs=[pl.BlockSpec((L, ct), lambda c: (0, c))] * 3,
        compiler_params=_cparams(("arbitrary",)),
    )(table_hi, table_lo, h, h)


def _hyena_kernel(x0_ref, x1_ref, v_ref, w0_ref, w1_ref, wv_ref, b0_ref, b1_ref, bv_ref, bias_ref,
                  f_ref, g_ref, p_ref, q_ref, s_ref, o_ref, *, L, seqs):
    ct = x0_ref.shape[1]
    row = lax.broadcasted_iota(jnp.int32, (L, ct), 0)
    for n in range(seqs):
        rows = slice(n * L, (n + 1) * L)

        def sconv(x_ref, w_ref, b_ref, rows=rows):
            x = x_ref[rows, :]
            xm = jnp.where(row == 0, 0.0, pltpu.roll(x, 1, 0))
            xp = jnp.where(row == L - 1, 0.0, pltpu.roll(x, L - 1, 0))
            return xm * w_ref[0:1, :] + x * w_ref[1:2, :] + xp * w_ref[2:3, :] + b_ref[...]

        z = sconv(v_ref, wv_ref, bv_ref) * sconv(x1_ref, w1_ref, b1_ref)
        zf = _dot(f_ref[...], z.astype(BF16))
        a = zf[:L]
        b = zf[L:]
        p = p_ref[...]
        q = q_ref[...]
        yre = a * p - b * q
        yim = a * q + b * s_ref[...]
        ycat = jnp.concatenate([yre, yim], axis=0).astype(BF16)
        y = _dot(g_ref[...], ycat) + z * bias_ref[...]
        o_ref[rows, :] = sconv(x0_ref, w0_ref, b0_ref) * y


def _hyena(p_hy, short_w, short_b, bias, table_bf, table_t_bf, spec, L, nb, row0, seqs):
    ct = 256
    nc = HY_CH // ct
    P, Q, S = spec
    blk = seqs * L

    def xs(k):
        return pl.BlockSpec((blk, ct), lambda b, c: (row0 // blk + b, k * nc + c))

    def ws(rows, k):
        return pl.BlockSpec((rows, ct), lambda b, c: (0, k * nc + c))

    cs = pl.BlockSpec((L, ct), lambda b, c: (0, c))
    return pl.pallas_call(
        functools.partial(_hyena_kernel, L=L, seqs=seqs),
        out_shape=jax.ShapeDtypeStruct((nb * L, HY_CH), F32),
        grid=(nb // seqs, nc),
        in_specs=[xs(0), xs(1), xs(2), ws(3, 0), ws(3, 1), ws(3, 2), ws(1, 0), ws(1, 1), ws(1, 2), ws(1, 0),
                  pl.BlockSpec((2 * L, L), lambda b, c: (0, 0)),
                  pl.BlockSpec((L, 2 * L), lambda b, c: (0, 0)),
                  cs, cs, cs],
        out_specs=pl.BlockSpec((blk, ct), lambda b, c: (b, c)),
        compiler_params=_cparams(("arbitrary", "arbitrary")),
    )(p_hy, p_hy, p_hy, short_w, short_w, short_w, short_b, short_b, short_b, bias,
      table_bf, table_t_bf, P, Q, S)


def _rope_tables(L):
    half = HEAD_DIM // 2
    pos = jnp.arange(L)
    r = (pos // GRID_W).astype(F32)
    col = (pos % GRID_W).astype(F32)
    inv = ROPE_BASE ** (-jnp.arange(0, half, 2, dtype=F32) / half)
    ar = r[:, None] * inv[None, :]
    ac = col[:, None] * inv[None, :]
    z = jnp.zeros_like(ar)
    cos = jnp.concatenate([jnp.cos(ar), jnp.cos(ar), jnp.cos(ac), jnp.cos(ac)], axis=1)
    s1 = jnp.concatenate([-jnp.sin(ar), z, -jnp.sin(ac), z], axis=1)
    s2 = jnp.concatenate([z, jnp.sin(ar), z, jnp.sin(ac)], axis=1)
    rep = LANES // HEAD_DIM
    return jnp.tile(cos, (1, rep)), jnp.tile(s1, (1, rep)), jnp.tile(s2, (1, rep))


def _attn_kernel(*refs, L, Tq, H, KVH, Lc, window, use_sink, use_norm, use_rope, emit_k):
    G = H // KVH
    hd = HEAD_DIM
    per = LANES // hd
    pad = WINDOW if window else 0
    it = iter(refs)
    q_ref = next(it)
    kv_ref = next(it)
    ck_ref = cv_ref = sink_ref = qn_ref = kn_ref = bd_ref = rc_ref = rs1_ref = rs2_ref = ko_ref = None
    if Lc:
        ck_ref = next(it)
        cv_ref = next(it)
    if use_sink:
        sink_ref = next(it)
    if use_norm:
        qn_ref = next(it)
        kn_ref = next(it)
        bd_ref = next(it)
    if use_rope:
        rc_ref = next(it)
        rs1_ref = next(it)
        rs2_ref = next(it)
    o_ref = next(it)
    if emit_k:
        ko_ref = next(it)
    k_scr = next(it)
    v_scr = next(it)
    i = pl.program_id(1)
    lat0 = Lc + pad

    def head_norm(x, g_ref):
        hi, lo = _split(x * x)
        ms = _dot(hi, bd_ref[...]) + _dot(lo, bd_ref[...])
        return x * lax.rsqrt(ms + EPS) * g_ref[...]

    def rope(x, rows):
        return (x * rc_ref[rows, :] + pltpu.roll(x, LANES - 16, 1) * rs1_ref[rows, :]
                + pltpu.roll(x, 16, 1) * rs2_ref[rows, :])

    def put(scr, g, rows, x):
        scr[g, 0, rows, :] = x.astype(BF16)
        scr[g, 1, rows, :] = pltpu.roll(x, hd, 1).astype(BF16)

    @pl.when(i == 0)
    def _():
        n_kg = KVH // per
        if window:
            zpad = jnp.zeros((n_kg, per, pad, LANES), BF16)
            for scr in (k_scr, v_scr):
                scr[:, :, Lc:Lc + pad, :] = zpad
                scr[:, :, lat0 + L:lat0 + L + pad, :] = zpad
        for g in range(n_kg):
            cols = slice(LANES * g, LANES * (g + 1))
            kg = kv_ref[:, cols]
            if use_norm:
                kg = head_norm(kg, kn_ref)
                if emit_k:
                    ko_ref[:, cols] = kg
            if use_rope:
                kg = rope(kg, slice(None))
            put(k_scr, g, slice(lat0, lat0 + L), kg)
            put(v_scr, g, slice(lat0, lat0 + L), kv_ref[:, KVH * hd + LANES * g:KVH * hd + LANES * (g + 1)])
            if Lc:
                put(k_scr, g, slice(0, Lc), ck_ref[0][:, cols])
                put(v_scr, g, slice(0, Lc), cv_ref[0][:, cols])

    rows = pl.ds(pl.multiple_of(i * Tq, Tq), Tq)
    low = lax.broadcasted_iota(jnp.int32, (Tq, LANES), 1) < hd
    groups_per_kv = G // per
    hrow = lax.broadcasted_iota(jnp.int32, (groups_per_kv * Tq, 1), 0) // Tq
    for kvh in range(KVH):
        qgs = []
        for g in range(kvh * groups_per_kv, (kvh + 1) * groups_per_kv):
            qg = q_ref[:, LANES * g:LANES * (g + 1)]
            if use_norm:
                qg = head_norm(qg, qn_ref)
            if use_rope:
                qg = rope(qg, rows)
            qgs.append((qg * (hd ** -0.5)).astype(BF16))
        outs = []
        for half in range(per):
            keep = low if half == 0 else ~low
            qs = jnp.concatenate([jnp.where(keep, qg, jnp.zeros_like(qg)) for qg in qgs], axis=0)
            var = (kvh % per) ^ half
            kg = kvh // per
            if use_sink:
                sk = jnp.zeros((groups_per_kv * Tq, 1), F32)
                for n in range(groups_per_kv):
                    sk = jnp.where(hrow == n, sink_ref[0, (kvh * groups_per_kv + n) * per + half], sk)
            if window:
                W = Tq + 2 * pad
                wrows = pl.ds(pl.multiple_of(Lc + i * Tq, Tq), W)
                s_c = _dot(qs, k_scr[kg, var, 0:Lc, :], _NT)
                s_w = _dot(qs, k_scr[kg, var, wrows, :], _NT).reshape(groups_per_kv, Tq, W)
                r = lax.broadcasted_iota(jnp.int32, (Tq, W), 0)
                c = lax.broadcasted_iota(jnp.int32, (Tq, W), 1)
                kpos = i * Tq - pad + c
                valid = (kpos >= 0) & (kpos < L) & (jnp.abs(r + pad - c) <= WINDOW)
                s_w = jnp.where(valid[None], s_w, NEG_INF).reshape(groups_per_kv * Tq, W)
                m = jnp.maximum(jnp.max(s_c, axis=-1, keepdims=True), jnp.max(s_w, axis=-1, keepdims=True))
                if use_sink:
                    m = jnp.maximum(m, sk)
                e_c = jnp.exp(s_c - m)
                e_w = jnp.exp(s_w - m)
                den = jnp.sum(e_c, axis=-1, keepdims=True) + jnp.sum(e_w, axis=-1, keepdims=True)
                o = (_dot(e_c.astype(BF16), v_scr[kg, var, 0:Lc, :])
                     + _dot(e_w.astype(BF16), v_scr[kg, var, wrows, :]))
            else:
                s = _dot(qs, k_scr[kg, var], _NT)
                m = jnp.max(s, axis=-1, keepdims=True)
                if use_sink:
                    m = jnp.maximum(m, sk)
                e = jnp.exp(s - m)
                den = jnp.sum(e, axis=-1, keepdims=True)
                o = _dot(e.astype(BF16), v_scr[kg, var])
            if use_sink:
                den = den + jnp.exp(sk - m)
            outs.append(o / den)
        for n in range(groups_per_kv):
            g = kvh * groups_per_kv + n
            o_ref[:, LANES * g:LANES * (g + 1)] = jnp.where(low, outs[0][n * Tq:(n + 1) * Tq],
                                                            outs[1][n * Tq:(n + 1) * Tq])


def _attention(p, q_cols, kv_colblk, *, L, Tq, nb, row0, H, KVH, ctx=None, sink=None, norm=None, rope=None,
               window=False, emit_k=False):
    hd = HEAD_DIM
    nq = L // Tq
    Lc = 0 if ctx is None else ctx[0].shape[1]
    pad = WINDOW if window else 0
    Lk = Lc + L + 2 * pad
    kvw = 2 * KVH * hd
    args = [p, p]
    in_specs = [pl.BlockSpec((Tq, q_cols), lambda b, i: (row0 // Tq + b * nq + i, 0)),
                pl.BlockSpec((L, kvw), lambda b, i: (row0 // L + b, kv_colblk))]
    if ctx is not None:
        for a in ctx:
            args.append(a)
            in_specs.append(pl.BlockSpec((1, Lc, KVH * hd), lambda b, i: (b, 0, 0)))
    if sink is not None:
        args.append(sink.reshape(1, H))
        in_specs.append(pl.BlockSpec(memory_space=pltpu.SMEM))
    if norm is not None:
        bd = (jnp.arange(LANES)[:, None] // hd == jnp.arange(LANES)[None, :] // hd).astype(F32) / hd
        for a in (jnp.tile(norm[0], LANES // hd).reshape(1, LANES), jnp.tile(norm[1], LANES // hd).reshape(1, LANES)):
            args.append(a)
            in_specs.append(pl.BlockSpec((1, LANES), lambda b, i: (0, 0)))
        args.append(bd.astype(BF16))
        in_specs.append(pl.BlockSpec((LANES, LANES), lambda b, i: (0, 0)))
    if rope is not None:
        for a in rope:
            args.append(a)
            in_specs.append(pl.BlockSpec((L, LANES), lambda b, i: (0, 0)))
    out_shape = [jax.ShapeDtypeStruct((nb * L, H * hd), F32)]
    out_specs = [pl.BlockSpec((Tq, H * hd), lambda b, i: (b * nq + i, 0))]
    if emit_k:
        out_shape.append(jax.ShapeDtypeStruct((nb * L, KVH * hd), F32))
        out_specs.append(pl.BlockSpec((L, KVH * hd), lambda b, i: (b, 0)))
    kern = functools.partial(_attn_kernel, L=L, Tq=Tq, H=H, KVH=KVH, Lc=Lc, window=window,
                             use_sink=sink is not None, use_norm=norm is not None,
                             use_rope=rope is not None, emit_k=emit_k)
    call = pl.pallas_call(
        kern, out_shape=out_shape, grid=(nb, nq), in_specs=in_specs, out_specs=out_specs,
        scratch_shapes=[pltpu.VMEM((KVH * hd // LANES, LANES // hd, Lk, LANES), BF16),
                        pltpu.VMEM((KVH * hd // LANES, LANES // hd, Lk, LANES), BF16)],
        compiler_params=_cparams(("arbitrary", "arbitrary")))
    return call, args


def _oproj_kernel(*refs, paired):
    def body(stream):
        rows = slice(None)
        al, rest = _take_rows(refs, paired[0], rows, stream)
        ar, rest = _take_rows(rest, paired[1], rows, stream)
        x, rest = _take_rows(rest, paired[2], rows, stream)
        w_ref, g1_ref, gate_ref, g2_ref, sh_ref, sc_ref, wr_ref, x1_ref, h2_ref, lg_ref = rest
        k = al.shape[1]
        y = _dot(al.astype(BF16), w_ref[0:k, :]) + _dot(ar.astype(BF16), w_ref[k:2 * k, :])
        x1 = x + gate_ref[0] * _rms(y, g1_ref[...])
        x1_ref[...] = x1
        h2 = _rms(x1, g2_ref[...]) * (1.0 + sc_ref[0]) + sh_ref[0]
        wh, wl = _split(wr_ref[...])
        hb = h2.astype(BF16)
        h2_ref[...] = hb
        lg_ref[...] = _dot(wh, hb, _NT) + _dot(wl, hb, _NT)

    _per_stream(any(paired), body)


def _out_proj(a_l, l_blk, a_r, r_blk, w_bf, x, g1, g2, mod, layer, wr_t):
    tm = ROW_TILE
    half = D_MODEL // 2
    l_specs, l_args = _row_specs(a_l, half, l_blk)
    r_specs, r_args = _row_specs(a_r, half, r_blk)
    x_specs, x_args = _row_specs(x, D_MODEL)
    paired = tuple(isinstance(a, tuple) for a in (a_l, a_r, x))
    return pl.pallas_call(
        functools.partial(_oproj_kernel, paired=paired),
        out_shape=[jax.ShapeDtypeStruct((NT, D_MODEL), F32), jax.ShapeDtypeStruct((NT, D_MODEL), BF16),
                   jax.ShapeDtypeStruct((N_EXPERTS, NT), F32)],
        grid=(NT // tm,),
        in_specs=l_specs + r_specs + x_specs + [
            pl.BlockSpec((D_MODEL, D_MODEL), lambda i: (0, 0)),
            pl.BlockSpec((1, D_MODEL), lambda i: (0, 0)),
            _mod_spec(layer, 2, tm),
            pl.BlockSpec((1, D_MODEL), lambda i: (0, 0)),
            _mod_spec(layer, 3, tm), _mod_spec(layer, 4, tm),
            pl.BlockSpec((N_EXPERTS, D_MODEL), lambda i: (0, 0))],
        out_specs=[pl.BlockSpec((tm, D_MODEL), lambda i: (i, 0)),
                   pl.BlockSpec((tm, D_MODEL), lambda i: (i, 0)),
                   pl.BlockSpec((N_EXPERTS, tm), lambda i: (0, i))],
        compiler_params=_cparams(("arbitrary",)),
    )(*l_args, *r_args, *x_args, w_bf, g1.reshape(1, D_MODEL), mod, g2.reshape(1, D_MODEL), mod, mod, wr_t)


COMB_ROWS = 512
COMB_WIN = 128
BF16_SUB = 16
F32_SUB = 8
FFN_PARTS = 2


def _ffn_kernel(a0_ref, s1_ref, slot_ref, aff_ref, h_ref, wg_ref, wu_ref, wd_ref, y_ref,
                xn_scr, gn_scr, xc_scr, gc_scr, acc_scr, *, cap, nf, nblk):
    grp = pl.program_id(0)
    f = pl.program_id(1)
    R = COMB_WIN
    per_step = nblk // nf
    erow = jnp.minimum(grp, N_EXPERTS - 1) % F32_SUB

    def add_window(jb, c0, hit_of):
        cols = slice(jb * COMB_ROWS, (jb + 1) * COMB_ROWS)
        q = lax.broadcasted_iota(jnp.int32, (R, COMB_ROWS), 0)
        hit = hit_of(slot_ref[pl.ds(erow, 1), cols].astype(jnp.int32), q)
        rows = pl.ds(c0, R)
        piece = _dot(hit.astype(F32).astype(BF16), h_ref[cols, :])
        xn_scr[rows, :] += piece.astype(BF16)
        gn_scr[rows, :] += jnp.sum(jnp.where(hit, aff_ref[pl.ds(erow, 1), cols], 0.0), axis=1, keepdims=True)

    def window_start(jb):
        return pl.multiple_of(a0_ref[grp * nblk + f * per_step + jb], BF16_SUB)

    def gather():
        for jb in range(per_step):
            a0 = window_start(jb)
            add_window(jb, a0, lambda srow, q, a0=a0: srow - a0 == q)

    def gather_more():
        for jb in range(per_step):
            end = s1_ref[grp * nblk + f * per_step + jb]

            def extra(c, jb=jb):
                c0 = pl.multiple_of(jnp.minimum(c, cap - R), BF16_SUB)
                add_window(jb, c0, lambda srow, q: (srow - c0 == q) & (q + c0 >= c))
                return c + R

            lax.while_loop(lambda c, end=end: c < end, extra, window_start(jb) + R)

    def compute(first):
        wg = wg_ref[0, 0].astype(BF16)
        wu = wu_ref[0, 0].astype(BF16)
        wd = wd_ref[0, 0].astype(BF16)
        part_rows = cap // FFN_PARTS
        for r in range(FFN_PARTS):
            rows = slice(r * part_rows, (r + 1) * part_rows)
            xe = xc_scr[rows, :]
            hmid = (jax.nn.silu(_dot(xe, wg)) * _dot(xe, wu)).astype(BF16)
            part = _dot(hmid, wd)
            if first:
                acc_scr[rows, :] = part
            else:
                acc_scr[rows, :] += part

    @pl.when(f == 0)
    def _():
        xn_scr[...] = jnp.zeros_like(xn_scr)
        gn_scr[...] = jnp.zeros_like(gn_scr)

    @pl.when(grp == 0)
    def _():
        gather()

    for first in (True, False):
        first_step = (f == 0) if first else (f != 0)

        @pl.when((grp > 0) & (grp < N_EXPERTS) & first_step)
        def _():
            gather()
            compute(first)

        @pl.when((grp == N_EXPERTS) & first_step)
        def _():
            compute(first)

    @pl.when(grp < N_EXPERTS)
    def _():
        gather_more()

    @pl.when(f == nf - 1)
    def _():
        @pl.when(grp > 0)
        def _():
            y_ref[...] = (acc_scr[...] * gc_scr[...]).astype(BF16)

        xc_scr[...] = xn_scr[...]
        gc_scr[...] = gn_scr[...]


def _expert_ffn(a0, s1, slot, aff, h2, layer, w_gate, w_up, w_down):
    cap = EC_CAPACITY * NT // N_EXPERTS
    fc = 512
    nf = EXPERT_FF // fc
    nblk = NT // COMB_ROWS
    tok = NT // nf

    def prev(g):
        return jnp.maximum(g - 1, 0)

    def this(g):
        return jnp.minimum(g, N_EXPERTS - 1)

    grid_spec = pltpu.PrefetchScalarGridSpec(
        num_scalar_prefetch=2,
        grid=(N_EXPERTS + 1, nf),
        in_specs=[pl.BlockSpec((F32_SUB, tok), lambda g, f, *_: (this(g) // F32_SUB, f)),
                  pl.BlockSpec((F32_SUB, tok), lambda g, f, *_: (this(g) // F32_SUB, f)),
                  pl.BlockSpec((tok, D_MODEL), lambda g, f, *_: (f, 0)),
                  pl.BlockSpec((1, 1, D_MODEL, fc), lambda g, f, *_: (layer, prev(g), 0, f)),
                  pl.BlockSpec((1, 1, D_MODEL, fc), lambda g, f, *_: (layer, prev(g), 0, f)),
                  pl.BlockSpec((1, 1, fc, D_MODEL), lambda g, f, *_: (layer, prev(g), f, 0))],
        out_specs=pl.BlockSpec((cap, D_MODEL), lambda g, f, *_: (prev(g), 0)),
        scratch_shapes=[pltpu.VMEM((cap, D_MODEL), BF16), pltpu.VMEM((cap, 1), F32),
                        pltpu.VMEM((cap, D_MODEL), BF16), pltpu.VMEM((cap, 1), F32),
                        pltpu.VMEM((cap, D_MODEL), F32)])
    return pl.pallas_call(
        functools.partial(_ffn_kernel, cap=cap, nf=nf, nblk=nblk),
        out_shape=jax.ShapeDtypeStruct((N_EXPERTS * cap, D_MODEL), BF16),
        grid_spec=grid_spec,
        compiler_params=_cparams(("arbitrary", "arbitrary")),
    )(a0.reshape(-1), s1.reshape(-1), slot, aff, h2,
      w_gate, w_up, w_down)


CMB_ROWS = COMB_ROWS
CMB_WIN = LANES


def _combine_kernel(a0_ref, s1_ref, slot_ref, y_hbm, x_ref, g_ref, gate_ref, *rest, cap, nblk):
    *o_refs, win_scr, ovf_scr, f_scr, sem, osem = rest
    j = pl.program_id(0)
    E, R, Tb = N_EXPERTS, CMB_WIN, CMB_ROWS

    def window(e, start, dst, s):
        return pltpu.make_async_copy(y_hbm.at[pl.ds(e * cap + start, R)], dst, s)

    def issue(blk, slot):
        for e in range(E):
            a0 = pl.multiple_of(a0_ref[e * nblk + blk], BF16_SUB)
            window(e, a0, win_scr.at[slot, pl.ds(e * R, R)], sem.at[slot]).start()

    @pl.when(j == 0)
    def _():
        issue(0, 0)

    slot = j % 2
    pltpu.make_async_copy(y_hbm.at[pl.ds(0, E * R)], win_scr.at[slot], sem.at[slot]).wait()

    @pl.when(j + 1 < nblk)
    def _():
        issue(j + 1, 1 - slot)

    sl = slot_ref[...].astype(jnp.int32)
    q = lax.broadcasted_iota(jnp.int32, (Tb, R), 1)
    onehot = jnp.concatenate(
        [(sl[:, e:e + 1] - a0_ref[e * nblk + j] == q) for e in range(E)], axis=1).astype(F32).astype(BF16)
    f_scr[...] = _dot(onehot, win_scr[slot])

    for e in range(E):
        end = s1_ref[e * nblk + j]

        def more(c):
            return c < end

        def extra(c, e=e):
            c0 = pl.multiple_of(jnp.minimum(c, cap - R), BF16_SUB)
            cp = window(e, c0, ovf_scr, osem)
            cp.start()
            cp.wait()
            hit = (sl[:, e:e + 1] - c0 == q) & (q + c0 >= c)
            f_scr[...] += _dot(hit.astype(F32).astype(BF16), ovf_scr[...])
            return c + R

        lax.while_loop(more, extra, a0_ref[e * nblk + j] + R)

    res = x_ref[...] + gate_ref[0] * _rms(f_scr[...], g_ref[...])
    if len(o_refs) == 1:
        o_refs[0][...] = res
    else:
        @pl.when(j < NS // Tb)
        def _():
            o_refs[0][...] = res

        @pl.when(j >= NS // Tb)
        def _():
            o_refs[1][...] = res


def _combine(a0, s1, slot_t, y, x, g, mod, layer, split_out=False):
    cap = y.shape[0] // N_EXPERTS
    nblk = NT // CMB_ROWS
    if split_out:
        out_shape = [jax.ShapeDtypeStruct((NS, D_MODEL), F32), jax.ShapeDtypeStruct((NP, D_MODEL), F32)]
        out_specs = _stream_specs(D_MODEL, tile=CMB_ROWS)
    else:
        out_shape = jax.ShapeDtypeStruct((NT, D_MODEL), F32)
        out_specs = pl.BlockSpec((CMB_ROWS, D_MODEL), lambda i, *_: (i, 0))
    grid_spec = pltpu.PrefetchScalarGridSpec(
        num_scalar_prefetch=2,
        grid=(nblk,),
        in_specs=[pl.BlockSpec((CMB_ROWS, LANES), lambda i, *_: (i, 0)),
                  pl.BlockSpec(memory_space=pl.ANY),
                  pl.BlockSpec((CMB_ROWS, D_MODEL), lambda i, *_: (i, 0)),
                  pl.BlockSpec((1, D_MODEL), lambda i, *_: (0, 0)),
                  _mod_spec(layer, 5, CMB_ROWS)],
        out_specs=out_specs,
        scratch_shapes=[pltpu.VMEM((2, N_EXPERTS * CMB_WIN, D_MODEL), BF16),
                        pltpu.VMEM((CMB_WIN, D_MODEL), BF16),
                        pltpu.VMEM((CMB_ROWS, D_MODEL), F32),
                        pltpu.SemaphoreType.DMA((2,)), pltpu.SemaphoreType.DMA(())])
    return pl.pallas_call(
        functools.partial(_combine_kernel, cap=cap, nblk=nblk),
        out_shape=out_shape,
        grid_spec=grid_spec,
        compiler_params=_cparams(("arbitrary",)),
    )(a0.reshape(-1), s1.reshape(-1), slot_t, y, x, g.reshape(1, D_MODEL), mod)


CUM_BLK = 256
SCALE_STEP = 2.0 ** -16
SCALE_ITERS = 10
BISECT_ITERS = 64


def _prefix_count(mask):
    n = mask.shape[1]
    tri = (lax.broadcasted_iota(jnp.int32, (CUM_BLK, CUM_BLK), 0)
           <= lax.broadcasted_iota(jnp.int32, (CUM_BLK, CUM_BLK), 1)).astype(F32).astype(BF16)
    carry = jnp.zeros((mask.shape[0], 1), F32)
    outs = []
    for b in range(n // CUM_BLK):
        c = _dot(mask[:, b * CUM_BLK:(b + 1) * CUM_BLK].astype(F32).astype(BF16), tri) + carry
        outs.append(c)
        carry = c[:, CUM_BLK - 1:CUM_BLK]
    return jnp.concatenate(outs, axis=1)


def _route_kernel(lg_ref, aff_ref, slot_ref, slott_ref, s0_ref):
    x = lg_ref[...]
    ex = jnp.exp(x - jnp.max(x, axis=0, keepdims=True))
    aff = ex / jnp.sum(ex, axis=0, keepdims=True)
    aff_ref[...] = aff
    streams = ((0, NS), (NS, NP))
    caps = [float(EC_CAPACITY * n // N_EXPERTS) for _, n in streams]
    parts = [aff[:, lo:lo + n] for lo, n in streams]

    def enough(k, thr):
        return jnp.sum((parts[k] >= thr).astype(F32), axis=1, keepdims=True) >= caps[k]

    def scale_step(_, carry):
        out = []
        for k, (lo_v, hi_v, found_f) in enumerate(carry):
            found = found_f > 0.5
            mid = hi_v * SCALE_STEP
            ge = enough(k, mid)
            out.append((jnp.where(found | ~ge, lo_v, mid), jnp.where(found | ge, hi_v, mid),
                        jnp.where(found | ge, 1.0, 0.0)))
        return tuple(out)

    def bisect_step(_, carry):
        out = []
        for k, (lo_v, hi_v) in enumerate(carry):
            mid = 0.5 * (lo_v + hi_v)
            ge = enough(k, mid)
            out.append((jnp.where(ge, mid, lo_v), jnp.where(ge, hi_v, mid)))
        return tuple(out)

    col = lambda v: jnp.full((N_EXPERTS, 1), v, F32)
    start = tuple((col(0.0), col(2.0), col(0.0)) for _ in streams)
    scaled = lax.fori_loop(0, SCALE_ITERS, scale_step, start)
    bounds = lax.fori_loop(0, BISECT_ITERS, bisect_step, tuple((lo_v, hi_v) for lo_v, hi_v, _ in scaled))
    slots = []
    off = 0.0
    for k, (lo_v, hi_v) in enumerate(bounds):
        above = parts[k] >= hi_v
        edge = (parts[k] >= lo_v) & ~above
        need = caps[k] - jnp.sum(above.astype(F32), axis=1, keepdims=True)
        edge_rank = _prefix_count(edge) - edge.astype(F32)
        sel = above | (edge & (edge_rank < need))
        slots.append(jnp.where(sel, _prefix_count(sel) - 1.0 + off, -1.0))
        off += caps[k]
    slot = jnp.concatenate(slots, axis=1)
    slot_ref[...] = slot
    pad = jnp.full((LANES - N_EXPERTS, NT), -1.0, F32)
    slott_ref[...] = jnp.concatenate([slot, pad], axis=0).T
    tok = lax.broadcasted_iota(jnp.int32, (NT, LANES), 0)
    edge = lax.broadcasted_iota(jnp.int32, (NT, LANES), 1) * CMB_ROWS
    s0_ref[...] = _dot((slot >= 0.0).astype(F32).astype(BF16), (tok < edge).astype(F32).astype(BF16))


def _route_tables(logits_t):
    return pl.pallas_call(
        _route_kernel,
        out_shape=[jax.ShapeDtypeStruct((N_EXPERTS, NT), F32), jax.ShapeDtypeStruct((N_EXPERTS, NT), F32),
                   jax.ShapeDtypeStruct((NT, LANES), F32), jax.ShapeDtypeStruct((N_EXPERTS, LANES), F32)],
        compiler_params=pltpu.CompilerParams(vmem_limit_bytes=VMEM_LIMIT),
    )(logits_t)


def _route(logits_t, h2, x, g, mod, layer, w_gate, w_up, w_down, split_out=False):
    aff, slot, slot_t, s0 = _route_tables(logits_t)
    cap = EC_CAPACITY * NT // N_EXPERTS
    s0 = s0[:, :NT // CMB_ROWS + 1].astype(jnp.int32)

    def windows(starts, win):
        return jnp.minimum(starts[:, :-1] // BF16_SUB * BF16_SUB, cap - win), starts[:, 1:]

    a0g, s1g = windows(s0[:, ::COMB_ROWS // CMB_ROWS], COMB_WIN)
    y = _expert_ffn(a0g, s1g, slot, aff, h2, layer, w_gate, w_up, w_down)
    a0c, s1c = windows(s0, CMB_WIN)
    return _combine(a0c, s1c, slot_t, y, x, g, mod, layer, split_out)


def kernel(x_prompt, x_sample, cache_b_k, cache_b_v, cache_c_k, cache_c_v, c, c_ctx, ada_w, ada_b, norm_g,
           ab_w_in, hy_short_w, hy_short_b, hy_w1, hy_b1, hy_w2, hy_b2, hy_freq, hy_w3, hy_log_decay, hy_bias,
           b_sink, ab_w_out, c_w_qkv, c_q_norm, c_k_norm, c_w_out, ec_router, ec_w_gate, ec_w_up, ec_w_down):
    x = (x_sample.reshape(NS, D_MODEL), x_prompt.reshape(NP, D_MODEL))
    cond = jnp.concatenate([c, c_ctx[None], jnp.zeros((MOD_ROWS - DEC_BATCH - 1, D_MODEL), F32)], axis=0)
    mod = _adaln(cond, ada_w, ada_b).reshape(DEPTH * MOD_ROWS * 6, 1, D_MODEL)
    rope = _rope_tables(DEC_SEQ)
    hd = HEAD_DIM

    e = 0
    p_hy, p_at = _norm_mod_matmul(x, norm_g[0, 0], mod, 0, ab_w_in[e].astype(BF16), (3 * HY_CH, D_MODEL - HY_CH + 2 * B_KV * hd))
    ya = []
    for L, nb, row0, seqs in ((DEC_SEQ, DEC_BATCH, 0, 2), (SEQ, BATCH, NS, 4)):
        table_lo, table_bf, table_t_bf = _dft_tables(L)
        h = _hyena_filters(L, hy_w1[e], hy_b1[e], hy_w2[e], hy_b2[e], hy_freq[e], hy_w3[e], hy_log_decay[e])
        h = h.at[0, HY_CH:].set(0.0)
        spec = _filter_spectrum(L, table_bf, table_lo, h)
        ya.append(_hyena(p_hy, hy_short_w[e], hy_short_b[e].reshape(1, -1), hy_bias[e].reshape(1, -1),
                         table_bf, table_t_bf, spec, L, nb, row0, seqs))
    ya = tuple(ya)
    ctx_b = (cache_b_k[:, e].reshape(DEC_BATCH, PAST_LEN, B_KV * hd), cache_b_v[:, e].reshape(DEC_BATCH, PAST_LEN, B_KV * hd))
    call, args = _attention(p_at, B_HEADS * hd, 2, L=DEC_SEQ, Tq=256, nb=DEC_BATCH, row0=0, H=B_HEADS, KVH=B_KV,
                            ctx=ctx_b, sink=b_sink[e], rope=rope, window=True)
    yb_s = call(*args)[0]
    call, args = _attention(p_at, B_HEADS * hd, 2, L=SEQ, Tq=256, nb=BATCH, row0=NS, H=B_HEADS, KVH=B_KV,
                            sink=b_sink[e])
    yb_p = call(*args)[0]
    yb = (yb_s, yb_p)
    state_b_k = p_at[NS:, B_HEADS * hd:(B_HEADS + B_KV) * hd].reshape(BATCH, 1, SEQ, B_KV, hd)
    state_b_v = p_at[NS:, (B_HEADS + B_KV) * hd:].reshape(BATCH, 1, SEQ, B_KV, hd)
    x, h2, lg = _out_proj(ya, 0, yb, 0, ab_w_out[e].astype(BF16), x, norm_g[0, 1], norm_g[0, 2], mod, 0, ec_router[0].T)
    x = _route(lg, h2, x, norm_g[0, 3], mod, 0, ec_w_gate, ec_w_up, ec_w_down)

    o = 0
    (p,) = _norm_mod_matmul(x, norm_g[1, 0], mod, 1, c_w_qkv[o].astype(BF16), ((C_HEADS + 2 * C_KV) * hd,))
    ctx_c = (cache_c_k[:, o].reshape(DEC_BATCH, PAST_LEN, C_KV * hd), cache_c_v[:, o].reshape(DEC_BATCH, PAST_LEN, C_KV * hd))
    nrm = (c_q_norm[o], c_k_norm[o])
    call, args = _attention(p, C_HEADS * hd, 2, L=DEC_SEQ, Tq=256, nb=DEC_BATCH, row0=0, H=C_HEADS, KVH=C_KV,
                            ctx=ctx_c, norm=nrm, rope=rope)
    yc_s = call(*args)[0]
    call, args = _attention(p, C_HEADS * hd, 2, L=SEQ, Tq=256, nb=BATCH, row0=NS, H=C_HEADS, KVH=C_KV,
                            norm=nrm, emit_k=True)
    yc_p, kn = call(*args)
    yc = (yc_s, yc_p)
    state_c_k = kn.reshape(BATCH, 1, SEQ, C_KV, hd)
    state_c_v = p[NS:, (C_HEADS + C_KV) * hd:].reshape(BATCH, 1, SEQ, C_KV, hd)
    x, h2, lg = _out_proj(yc, 0, yc, 1, c_w_out[o].astype(BF16), x, norm_g[1, 1], norm_g[1, 2], mod, 1, ec_router[1].T)
    xs, xp = _route(lg, h2, x, norm_g[1, 3], mod, 1, ec_w_gate, ec_w_up, ec_w_down, split_out=True)
    y_sample = xs.reshape(DEC_BATCH, DEC_SEQ, D_MODEL)
    y_prompt = xp.reshape(BATCH, SEQ, D_MODEL)
    return (y_prompt, y_sample, state_b_k, state_b_v, state_c_k, state_c_v)
```

```python
import functools
import math

import jax
import jax.numpy as jnp
from jax import lax
from jax.experimental import pallas as pl
from jax.experimental.pallas import tpu as pltpu

F32 = jnp.float32
BF16 = jnp.bfloat16

D_MODEL = 1024
BATCH = 16
SEQ = 256
DEPTH = 2
DEC_BATCH = 8
DEC_SEQ = 1024
PAST_LEN = 512
GRID_W = 64
HEAD_DIM = 64
HY_CH = D_MODEL // 2
HY_BANDS = 8
B_HEADS = 8
B_KV = 2
WINDOW = 128
C_HEADS = 16
C_KV = 4
ROPE_BASE = 10000.0
N_EXPERTS = 16
EC_CAPACITY = 2
EXPERT_FF = 2 * D_MODEL
EPS = 1e-6
NEG_INF = -1e30

NS = DEC_BATCH * DEC_SEQ
NP = BATCH * SEQ
NT = NS + NP
MOD_ROWS = 16
LANES = 128
VMEM_LIMIT = 56 * 1024 * 1024


def _cparams(sem):
    return pltpu.CompilerParams(dimension_semantics=sem, vmem_limit_bytes=VMEM_LIMIT)


def _split(x):
    hi = x.astype(BF16)
    lo = (x - hi.astype(F32)).astype(BF16)
    return hi, lo


_NN = (((1,), (0,)), ((), ()))
_NT = (((1,), (1,)), ((), ()))


def _dot(a, b, dims=_NN):
    return lax.dot_general(a, b, dims, preferred_element_type=F32)


def _dot3(a, b, dims=_NN):
    ah, al = _split(a)
    bh, bl = _split(b)
    return _dot(ah, bh, dims) + _dot(al, bh, dims) + _dot(ah, bl, dims)


def _rms(x, g):
    return x * lax.rsqrt(jnp.mean(x * x, axis=-1, keepdims=True) + EPS) * g


def _mod_row(tile_rows):
    def f(i):
        return jnp.minimum(i * tile_rows // DEC_SEQ, DEC_BATCH)
    return f


def _adaln_kernel(c_ref, w_ref, b_ref, o_ref):
    s = jax.nn.silu(c_ref[...])
    o_ref[0] = _dot3(s, w_ref[0]) + b_ref[0]


def _adaln(cond, ada_w, ada_b):
    tn = 1536
    return pl.pallas_call(
        _adaln_kernel,
        out_shape=jax.ShapeDtypeStruct((DEPTH, MOD_ROWS, 6 * D_MODEL), F32),
        grid=(DEPTH, 6 * D_MODEL // tn),
        in_specs=[pl.BlockSpec((MOD_ROWS, D_MODEL), lambda l, j: (0, 0)),
                  pl.BlockSpec((1, D_MODEL, tn), lambda l, j: (l, 0, j)),
                  pl.BlockSpec((1, 1, tn), lambda l, j: (l, 0, j))],
        out_specs=pl.BlockSpec((1, MOD_ROWS, tn), lambda l, j: (l, 0, j)),
        compiler_params=_cparams(("arbitrary", "arbitrary")),
    )(cond, ada_w, ada_b.reshape(DEPTH, 1, 6 * D_MODEL))


def _mod_spec(layer, which, tile_rows):
    row = _mod_row(tile_rows)
    return pl.BlockSpec((1, 1, D_MODEL), lambda i, *_: ((layer * MOD_ROWS + row(i)) * 6 + which, 0, 0))


ROW_TILE = 512
NS_TILES = NS // ROW_TILE


def _stream_specs(width, col=0, tile=ROW_TILE):
    ns = NS // tile
    return [pl.BlockSpec((tile, width), lambda i, *_: (jnp.minimum(i, ns - 1), col)),
            pl.BlockSpec((tile, width), lambda i, *_: (jnp.maximum(i - ns, 0), col))]


def _row_specs(a, width, col=0):
    if isinstance(a, tuple):
        return _stream_specs(width, col), list(a)
    return [pl.BlockSpec((ROW_TILE, width), lambda i, *_: (i, col))], [a]


ROW_CHUNK = 128


def _take_rows(refs, paired, rows):
    if paired:
        return jnp.where(pl.program_id(0) < NS_TILES, refs[0][rows, :], refs[1][rows, :]), refs[2:]
    return refs[0][rows, :], refs[1:]


def _nmm_kernel(*refs, paired):
    for c in range(ROW_TILE // ROW_CHUNK):
        rows = slice(c * ROW_CHUNK, (c + 1) * ROW_CHUNK)
        x, (g_ref, sh_ref, sc_ref, w_ref, *o_refs) = _take_rows(refs, paired, rows)
        h = _rms(x, g_ref[...]) * (1.0 + sc_ref[0]) + sh_ref[0]
        p = _dot(h.astype(BF16), w_ref[...])
        off = 0
        for o in o_refs:
            n = o.shape[1]
            o[rows, :] = p[:, off:off + n]
            off += n


def _norm_mod_matmul(x, g, mod, layer, w_bf, splits):
    tm = ROW_TILE
    n_out = w_bf.shape[1]
    x_specs, x_args = _row_specs(x, D_MODEL)
    return pl.pallas_call(
        functools.partial(_nmm_kernel, paired=isinstance(x, tuple)),
        out_shape=[jax.ShapeDtypeStruct((NT, n), F32) for n in splits],
        grid=(NT // tm,),
        in_specs=x_specs + [pl.BlockSpec((1, D_MODEL), lambda i: (0, 0)),
                            _mod_spec(layer, 0, tm), _mod_spec(layer, 1, tm),
                            pl.BlockSpec((D_MODEL, n_out), lambda i: (0, 0))],
        out_specs=[pl.BlockSpec((tm, n), lambda i: (i, 0)) for n in splits],
        compiler_params=_cparams(("arbitrary",)),
    )(*x_args, g.reshape(1, D_MODEL), mod, mod, w_bf)


def _hyena_filters(L, w1, b1, w2, b2, freq, w3, log_decay):
    hp = lax.Precision.HIGHEST
    t = jnp.linspace(0.0, 1.0, L, dtype=F32)[:, None]
    w = (2.0 * math.pi / L) * jnp.arange(L, dtype=F32)[:, None]
    bands = jnp.linspace(1e-4, HY_BANDS - 1, HY_BANDS, dtype=F32)[None, :]
    z = jnp.concatenate([t, jnp.cos(bands * w), -jnp.sin(bands * w)], axis=-1)
    h = jnp.sin(freq[0] * (jnp.dot(z, w1, precision=hp) + b1))
    h = jnp.sin(freq[1] * (jnp.dot(h, w2, precision=hp) + b2))
    return jnp.dot(h, w3, precision=hp) * jnp.exp(-t * jnp.exp(log_decay))


def _dft_kernel(fl_ref, fb_ref, gb_ref, *, L):
    n = 2 * L
    w = 2.0 * math.pi / n
    r = lax.broadcasted_iota(jnp.int32, (L, LANES), 0)
    b = lax.broadcasted_iota(jnp.int32, (L, LANES), 1)
    r1 = lax.broadcasted_iota(jnp.int32, (L, 1), 0)
    ang_b = ((r * b) & (n - 1)).astype(F32) * w
    cb = jnp.cos(ang_b)
    sb = jnp.sin(ang_b)
    by_col = jnp.where((b & 1) == 0, 1.0, -1.0)
    by_row = jnp.where((r & 1) == 0, 1.0, -1.0)
    for a in range(L // LANES):
        ang_a = ((r1 * (a * LANES)) & (n - 1)).astype(F32) * w
        ca = jnp.cos(ang_a)
        sa = jnp.sin(ang_a)
        cs = ca * cb - sa * sb
        sn = -(sa * cb + ca * sb)
        cols = slice(a * LANES, (a + 1) * LANES)
        fs = jnp.where(r == 0, by_col, sn)
        cs_hi, cs_lo = _split(cs)
        fs_hi, fs_lo = _split(fs)
        fb_ref[0:L, cols] = cs_hi
        fb_ref[L:2 * L, cols] = fs_hi
        fl_ref[0:L, cols] = cs_lo
        fl_ref[L:2 * L, cols] = fs_lo
        gb_ref[:, cols] = cs_hi
        gs = jnp.where(b == 0, by_row, sn) if a == 0 else sn
        gb_ref[:, L + a * LANES:L + (a + 1) * LANES] = gs.astype(BF16)


def _dft_tables(L):
    assert L & (L - 1) == 0
    return pl.pallas_call(
        functools.partial(_dft_kernel, L=L),
        out_shape=[jax.ShapeDtypeStruct((2 * L, L), BF16), jax.ShapeDtypeStruct((2 * L, L), BF16),
                   jax.ShapeDtypeStruct((L, 2 * L), BF16)],
        compiler_params=pltpu.CompilerParams(vmem_limit_bytes=VMEM_LIMIT),
    )()


def _filt_kernel(fb_ref, fl_ref, hf_ref, hb_ref, p_ref, q_ref, s_ref, *, L):
    hf = hf_ref[...]
    hb = hb_ref[...]
    ct = hf.shape[1]
    nrm = lax.rsqrt(jnp.sum(hf * hf, axis=0, keepdims=True) + jnp.sum(hb * hb, axis=0, keepdims=True) + EPS)
    hh, hl = _split(jnp.concatenate([hf, hb], axis=1))
    fb = fb_ref[...]
    fh = _dot(fb, hh) + _dot(fl_ref[...], hh) + _dot(fb, hl)
    row = lax.broadcasted_iota(jnp.int32, (L, ct), 0)
    w = jnp.where(row == 0, 1.0 / (2 * L), 2.0 / (2 * L)) * nrm
    re = fh[:L, :ct] + fh[:L, ct:]
    p_ref[...] = re * w
    q_ref[...] = jnp.where(row == 0, 0.0, fh[L:, :ct] - fh[L:, ct:]) * w
    s_ref[...] = jnp.where(row == 0, fh[L:, :ct] + fh[L:, ct:], re) * w


def _filter_spectrum(L, table_hi, table_lo, h):
    ct = 128
    nc = HY_CH // ct
    shp = jax.ShapeDtypeStruct((L, HY_CH), F32)
    return pl.pallas_call(
        functools.partial(_filt_kernel, L=L),
        out_shape=[shp, shp, shp],
        grid=(nc,),
        in_specs=[pl.BlockSpec((2 * L, L), lambda c: (0, 0)),
                  pl.BlockSpec((2 * L, L), lambda c: (0, 0)),
                  pl.BlockSpec((L, ct), lambda c: (0, c)),
                  pl.BlockSpec((L, ct), lambda c: (0, nc + c))],
        out_specs=[pl.BlockSpec((L, ct), lambda c: (0, c))] * 3,
        compiler_params=_cparams(("arbitrary",)),
    )(table_hi, table_lo, h, h)


def _hyena_kernel(x0_ref, x1_ref, v_ref, w0_ref, w1_ref, wv_ref, b0_ref, b1_ref, bv_ref, bias_ref,
                  f_ref, g_ref, p_ref, q_ref, s_ref, o_ref, *, L, seqs):
    ct = x0_ref.shape[1]
    row = lax.broadcasted_iota(jnp.int32, (L, ct), 0)
    for n in range(seqs):
        rows = slice(n * L, (n + 1) * L)

        def sconv(x_ref, w_ref, b_ref, rows=rows):
            x = x_ref[rows, :]
            xm = jnp.where(row == 0, 0.0, pltpu.roll(x, 1, 0))
            xp = jnp.where(row == L - 1, 0.0, pltpu.roll(x, L - 1, 0))
            return xm * w_ref[0:1, :] + x * w_ref[1:2, :] + xp * w_ref[2:3, :] + b_ref[...]

        z = sconv(v_ref, wv_ref, bv_ref) * sconv(x1_ref, w1_ref, b1_ref)
        zf = _dot(f_ref[...], z.astype(BF16))
        a = zf[:L]
        b = zf[L:]
        p = p_ref[...]
        q = q_ref[...]
        yre = a * p - b * q
        yim = a * q + b * s_ref[...]
        ycat = jnp.concatenate([yre, yim], axis=0).astype(BF16)
        y = _dot(g_ref[...], ycat) + z * bias_ref[...]
        o_ref[rows, :] = sconv(x0_ref, w0_ref, b0_ref) * y


def _hyena(p_hy, short_w, short_b, bias, table_bf, table_t_bf, spec, L, nb, row0, seqs):
    ct = 256
    nc = HY_CH // ct
    P, Q, S = spec
    blk = seqs * L

    def xs(k):
        return pl.BlockSpec((blk, ct), lambda b, c: (row0 // blk + b, k * nc + c))

    def ws(rows, k):
        return pl.BlockSpec((rows, ct), lambda b, c: (0, k * nc + c))

    cs = pl.BlockSpec((L, ct), lambda b, c: (0, c))
    return pl.pallas_call(
        functools.partial(_hyena_kernel, L=L, seqs=seqs),
        out_shape=jax.ShapeDtypeStruct((nb * L, HY_CH), F32),
        grid=(nb // seqs, nc),
        in_specs=[xs(0), xs(1), xs(2), ws(3, 0), ws(3, 1), ws(3, 2), ws(1, 0), ws(1, 1), ws(1, 2), ws(1, 0),
                  pl.BlockSpec((2 * L, L), lambda b, c: (0, 0)),
                  pl.BlockSpec((L, 2 * L), lambda b, c: (0, 0)),
                  cs, cs, cs],
        out_specs=pl.BlockSpec((blk, ct), lambda b, c: (b, c)),
        compiler_params=_cparams(("arbitrary", "arbitrary")),
    )(p_hy, p_hy, p_hy, short_w, short_w, short_w, short_b, short_b, short_b, bias,
      table_bf, table_t_bf, P, Q, S)


def _rope_tables(L):
    half = HEAD_DIM // 2
    pos = jnp.arange(L)
    r = (pos // GRID_W).astype(F32)
    col = (pos % GRID_W).astype(F32)
    inv = ROPE_BASE ** (-jnp.arange(0, half, 2, dtype=F32) / half)
    ar = r[:, None] * inv[None, :]
    ac = col[:, None] * inv[None, :]
    z = jnp.zeros_like(ar)
    cos = jnp.concatenate([jnp.cos(ar), jnp.cos(ar), jnp.cos(ac), jnp.cos(ac)], axis=1)
    s1 = jnp.concatenate([-jnp.sin(ar), z, -jnp.sin(ac), z], axis=1)
    s2 = jnp.concatenate([z, jnp.sin(ar), z, jnp.sin(ac)], axis=1)
    rep = LANES // HEAD_DIM
    return jnp.tile(cos, (1, rep)), jnp.tile(s1, (1, rep)), jnp.tile(s2, (1, rep))


def _attn_kernel(*refs, L, Tq, H, KVH, Lc, window, use_sink, use_norm, use_rope, emit_k):
    G = H // KVH
    hd = HEAD_DIM
    per = LANES // hd
    pad = WINDOW if window else 0
    it = iter(refs)
    q_ref = next(it)
    kv_ref = next(it)
    ck_ref = cv_ref = sink_ref = qn_ref = kn_ref = bd_ref = rc_ref = rs1_ref = rs2_ref = ko_ref = None
    if Lc:
        ck_ref = next(it)
        cv_ref = next(it)
    if use_sink:
        sink_ref = next(it)
    if use_norm:
        qn_ref = next(it)
        kn_ref = next(it)
        bd_ref = next(it)
    if use_rope:
        rc_ref = next(it)
        rs1_ref = next(it)
        rs2_ref = next(it)
    o_ref = next(it)
    if emit_k:
        ko_ref = next(it)
    k_scr = next(it)
    v_scr = next(it)
    i = pl.program_id(1)
    lat0 = Lc + pad

    def head_norm(x, g_ref):
        hi, lo = _split(x * x)
        ms = _dot(hi, bd_ref[...]) + _dot(lo, bd_ref[...])
        return x * lax.rsqrt(ms + EPS) * g_ref[...]

    def rope(x, rows):
        return (x * rc_ref[rows, :] + pltpu.roll(x, LANES - 16, 1) * rs1_ref[rows, :]
                + pltpu.roll(x, 16, 1) * rs2_ref[rows, :])

    def put(scr, g, rows, x):
        scr[g, 0, rows, :] = x.astype(BF16)
        scr[g, 1, rows, :] = pltpu.roll(x, hd, 1).astype(BF16)

    @pl.when(i == 0)
    def _():
        n_kg = KVH // per
        if window:
            zpad = jnp.zeros((n_kg, per, pad, LANES), BF16)
            for scr in (k_scr, v_scr):
                scr[:, :, Lc:Lc + pad, :] = zpad
                scr[:, :, lat0 + L:lat0 + L + pad, :] = zpad
        for g in range(n_kg):
            cols = slice(LANES * g, LANES * (g + 1))
            kg = kv_ref[:, cols]
            if use_norm:
                kg = head_norm(kg, kn_ref)
                if emit_k:
                    ko_ref[:, cols] = kg
            if use_rope:
                kg = rope(kg, slice(None))
            put(k_scr, g, slice(lat0, lat0 + L), kg)
            put(v_scr, g, slice(lat0, lat0 + L), kv_ref[:, KVH * hd + LANES * g:KVH * hd + LANES * (g + 1)])
            if Lc:
                put(k_scr, g, slice(0, Lc), ck_ref[0][:, cols])
                put(v_scr, g, slice(0, Lc), cv_ref[0][:, cols])

    rows = pl.ds(pl.multiple_of(i * Tq, Tq), Tq)
    low = lax.broadcasted_iota(jnp.int32, (Tq, LANES), 1) < hd
    groups_per_kv = G // per
    hrow = lax.broadcasted_iota(jnp.int32, (groups_per_kv * Tq, 1), 0) // Tq
    for kvh in range(KVH):
        qgs = []
        for g in range(kvh * groups_per_kv, (kvh + 1) * groups_per_kv):
            qg = q_ref[:, LANES * g:LANES * (g + 1)]
            if use_norm:
                qg = head_norm(qg, qn_ref)
            if use_rope:
                qg = rope(qg, rows)
            qgs.append((qg * (hd ** -0.5)).astype(BF16))
        outs = []
        for half in range(per):
            keep = low if half == 0 else ~low
            qs = jnp.concatenate([jnp.where(keep, qg, jnp.zeros_like(qg)) for qg in qgs], axis=0)
            var = (kvh % per) ^ half
            kg = kvh // per
            if use_sink:
                sk = jnp.zeros((groups_per_kv * Tq, 1), F32)
                for n in range(groups_per_kv):
                    sk = jnp.where(hrow == n, sink_ref[0, (kvh * groups_per_kv + n) * per + half], sk)
            if window:
                W = Tq + 2 * pad
                wrows = pl.ds(pl.multiple_of(Lc + i * Tq, Tq), W)
                s_c = _dot(qs, k_scr[kg, var, 0:Lc, :], _NT)
                s_w = _dot(qs, k_scr[kg, var, wrows, :], _NT).reshape(groups_per_kv, Tq, W)
                r = lax.broadcasted_iota(jnp.int32, (Tq, W), 0)
                c = lax.broadcasted_iota(jnp.int32, (Tq, W), 1)
                kpos = i * Tq - pad + c
                valid = (kpos >= 0) & (kpos < L) & (jnp.abs(r + pad - c) <= WINDOW)
                s_w = jnp.where(valid[None], s_w, NEG_INF).reshape(groups_per_kv * Tq, W)
                m = jnp.maximum(jnp.max(s_c, axis=-1, keepdims=True), jnp.max(s_w, axis=-1, keepdims=True))
                if use_sink:
                    m = jnp.maximum(m, sk)
                e_c = jnp.exp(s_c - m)
                e_w = jnp.exp(s_w - m)
                den = jnp.sum(e_c, axis=-1, keepdims=True) + jnp.sum(e_w, axis=-1, keepdims=True)
                o = (_dot(e_c.astype(BF16), v_scr[kg, var, 0:Lc, :])
                     + _dot(e_w.astype(BF16), v_scr[kg, var, wrows, :]))
            else:
                s = _dot(qs, k_scr[kg, var], _NT)
                m = jnp.max(s, axis=-1, keepdims=True)
                if use_sink:
                    m = jnp.maximum(m, sk)
                e = jnp.exp(s - m)
                den = jnp.sum(e, axis=-1, keepdims=True)
                o = _dot(e.astype(BF16), v_scr[kg, var])
            if use_sink:
                den = den + jnp.exp(sk - m)
            outs.append(o / den)
        for n in range(groups_per_kv):
            g = kvh * groups_per_kv + n
            o_ref[:, LANES * g:LANES * (g + 1)] = jnp.where(low, outs[0][n * Tq:(n + 1) * Tq],
                                                            outs[1][n * Tq:(n + 1) * Tq])


def _attention(p, q_cols, kv_colblk, *, L, Tq, nb, row0, H, KVH, ctx=None, sink=None, norm=None, rope=None,
               window=False, emit_k=False):
    hd = HEAD_DIM
    nq = L // Tq
    Lc = 0 if ctx is None else ctx[0].shape[1]
    pad = WINDOW if window else 0
    Lk = Lc + L + 2 * pad
    kvw = 2 * KVH * hd
    args = [p, p]
    in_specs = [pl.BlockSpec((Tq, q_cols), lambda b, i: (row0 // Tq + b * nq + i, 0)),
                pl.BlockSpec((L, kvw), lambda b, i: (row0 // L + b, kv_colblk))]
    if ctx is not None:
        for a in ctx:
            args.append(a)
            in_specs.append(pl.BlockSpec((1, Lc, KVH * hd), lambda b, i: (b, 0, 0)))
    if sink is not None:
        args.append(sink.reshape(1, H))
        in_specs.append(pl.BlockSpec(memory_space=pltpu.SMEM))
    if norm is not None:
        bd = (jnp.arange(LANES)[:, None] // hd == jnp.arange(LANES)[None, :] // hd).astype(F32) / hd
        for a in (jnp.tile(norm[0], LANES // hd).reshape(1, LANES), jnp.tile(norm[1], LANES // hd).reshape(1, LANES)):
            args.append(a)
            in_specs.append(pl.BlockSpec((1, LANES), lambda b, i: (0, 0)))
        args.append(bd.astype(BF16))
        in_specs.append(pl.BlockSpec((LANES, LANES), lambda b, i: (0, 0)))
    if rope is not None:
        for a in rope:
            args.append(a)
            in_specs.append(pl.BlockSpec((L, LANES), lambda b, i: (0, 0)))
    out_shape = [jax.ShapeDtypeStruct((nb * L, H * hd), F32)]
    out_specs = [pl.BlockSpec((Tq, H * hd), lambda b, i: (b * nq + i, 0))]
    if emit_k:
        out_shape.append(jax.ShapeDtypeStruct((nb * L, KVH * hd), F32))
        out_specs.append(pl.BlockSpec((L, KVH * hd), lambda b, i: (b, 0)))
    kern = functools.partial(_attn_kernel, L=L, Tq=Tq, H=H, KVH=KVH, Lc=Lc, window=window,
                             use_sink=sink is not None, use_norm=norm is not None,
                             use_rope=rope is not None, emit_k=emit_k)
    call = pl.pallas_call(
        kern, out_shape=out_shape, grid=(nb, nq), in_specs=in_specs, out_specs=out_specs,
        scratch_shapes=[pltpu.VMEM((KVH * hd // LANES, LANES // hd, Lk, LANES), BF16),
                        pltpu.VMEM((KVH * hd // LANES, LANES // hd, Lk, LANES), BF16)],
        compiler_params=_cparams(("arbitrary", "arbitrary")))
    return call, args


def _oproj_kernel(*refs, paired):
    rows = slice(None)
    al, rest = _take_rows(refs, paired[0], rows)
    ar, rest = _take_rows(rest, paired[1], rows)
    x, rest = _take_rows(rest, paired[2], rows)
    w_ref, g1_ref, gate_ref, g2_ref, sh_ref, sc_ref, wr_ref, x1_ref, h2_ref, lg_ref = rest
    k = al.shape[1]
    y = _dot(al.astype(BF16), w_ref[0:k, :]) + _dot(ar.astype(BF16), w_ref[k:2 * k, :])
    x1 = x + gate_ref[0] * _rms(y, g1_ref[...])
    x1_ref[...] = x1
    h2 = _rms(x1, g2_ref[...]) * (1.0 + sc_ref[0]) + sh_ref[0]
    wh, wl = _split(wr_ref[...])
    hb = h2.astype(BF16)
    h2_ref[...] = hb
    lg_ref[...] = _dot(wh, hb, _NT) + _dot(wl, hb, _NT)


def _out_proj(a_l, l_blk, a_r, r_blk, w_bf, x, g1, g2, mod, layer, wr_t):
    tm = ROW_TILE
    half = D_MODEL // 2
    l_specs, l_args = _row_specs(a_l, half, l_blk)
    r_specs, r_args = _row_specs(a_r, half, r_blk)
    x_specs, x_args = _row_specs(x, D_MODEL)
    paired = tuple(isinstance(a, tuple) for a in (a_l, a_r, x))
    return pl.pallas_call(
        functools.partial(_oproj_kernel, paired=paired),
        out_shape=[jax.ShapeDtypeStruct((NT, D_MODEL), F32), jax.ShapeDtypeStruct((NT, D_MODEL), BF16),
                   jax.ShapeDtypeStruct((N_EXPERTS, NT), F32)],
        grid=(NT // tm,),
        in_specs=l_specs + r_specs + x_specs + [
            pl.BlockSpec((D_MODEL, D_MODEL), lambda i: (0, 0)),
            pl.BlockSpec((1, D_MODEL), lambda i: (0, 0)),
            _mod_spec(layer, 2, tm),
            pl.BlockSpec((1, D_MODEL), lambda i: (0, 0)),
            _mod_spec(layer, 3, tm), _mod_spec(layer, 4, tm),
            pl.BlockSpec((N_EXPERTS, D_MODEL), lambda i: (0, 0))],
        out_specs=[pl.BlockSpec((tm, D_MODEL), lambda i: (i, 0)),
                   pl.BlockSpec((tm, D_MODEL), lambda i: (i, 0)),
                   pl.BlockSpec((N_EXPERTS, tm), lambda i: (0, i))],
        compiler_params=_cparams(("arbitrary",)),
    )(*l_args, *r_args, *x_args, w_bf, g1.reshape(1, D_MODEL), mod, g2.reshape(1, D_MODEL), mod, mod, wr_t)


COMB_ROWS = 512
COMB_WIN = 128
BF16_SUB = 16
F32_SUB = 8
FFN_PARTS = 2


def _ffn_kernel(a0_ref, s1_ref, slot_ref, aff_ref, h_ref, wg_ref, wu_ref, wd_ref, y_ref,
                xn_scr, gn_scr, xc_scr, gc_scr, acc_scr, *, cap, nf, nblk):
    grp = pl.program_id(0)
    f = pl.program_id(1)
    R = COMB_WIN
    per_step = nblk // nf
    erow = jnp.minimum(grp, N_EXPERTS - 1) % F32_SUB

    def add_window(jb, c0, hit_of):
        cols = slice(jb * COMB_ROWS, (jb + 1) * COMB_ROWS)
        q = lax.broadcasted_iota(jnp.int32, (R, COMB_ROWS), 0)
        hit = hit_of(slot_ref[pl.ds(erow, 1), cols].astype(jnp.int32), q)
        rows = pl.ds(c0, R)
        piece = _dot(hit.astype(F32).astype(BF16), h_ref[cols, :])
        xn_scr[rows, :] += piece.astype(BF16)
        gn_scr[rows, :] += jnp.sum(jnp.where(hit, aff_ref[pl.ds(erow, 1), cols], 0.0), axis=1, keepdims=True)

    def window_start(jb):
        return pl.multiple_of(a0_ref[grp * nblk + f * per_step + jb], BF16_SUB)

    def gather():
        for jb in range(per_step):
            a0 = window_start(jb)
            add_window(jb, a0, lambda srow, q, a0=a0: srow - a0 == q)

    def gather_more():
        for jb in range(per_step):
            end = s1_ref[grp * nblk + f * per_step + jb]

            def extra(c, jb=jb):
                c0 = pl.multiple_of(jnp.minimum(c, cap - R), BF16_SUB)
                add_window(jb, c0, lambda srow, q: (srow - c0 == q) & (q + c0 >= c))
                return c + R

            lax.while_loop(lambda c, end=end: c < end, extra, window_start(jb) + R)

    def compute(first):
        wg = wg_ref[0, 0].astype(BF16)
        wu = wu_ref[0, 0].astype(BF16)
        wd = wd_ref[0, 0].astype(BF16)
        part_rows = cap // FFN_PARTS
        for r in range(FFN_PARTS):
            rows = slice(r * part_rows, (r + 1) * part_rows)
            xe = xc_scr[rows, :]
            hmid = (jax.nn.silu(_dot(xe, wg)) * _dot(xe, wu)).astype(BF16)
            part = _dot(hmid, wd)
            if first:
                acc_scr[rows, :] = part
            else:
                acc_scr[rows, :] += part

    @pl.when(f == 0)
    def _():
        xn_scr[...] = jnp.zeros_like(xn_scr)
        gn_scr[...] = jnp.zeros_like(gn_scr)

    @pl.when(grp == 0)
    def _():
        gather()

    for first in (True, False):
        first_step = (f == 0) if first else (f != 0)

        @pl.when((grp > 0) & (grp < N_EXPERTS) & first_step)
        def _():
            gather()
            compute(first)

        @pl.when((grp == N_EXPERTS) & first_step)
        def _():
            compute(first)

    @pl.when(grp < N_EXPERTS)
    def _():
        gather_more()

    @pl.when(f == nf - 1)
    def _():
        @pl.when(grp > 0)
        def _():
            y_ref[...] = (acc_scr[...] * gc_scr[...]).astype(BF16)

        xc_scr[...] = xn_scr[...]
        gc_scr[...] = gn_scr[...]


def _expert_ffn(a0, s1, slot, aff, h2, layer, w_gate, w_up, w_down):
    cap = EC_CAPACITY * NT // N_EXPERTS
    fc = 512
    nf = EXPERT_FF // fc
    nblk = NT // COMB_ROWS
    tok = NT // nf

    def prev(g):
        return jnp.maximum(g - 1, 0)

    def this(g):
        return jnp.minimum(g, N_EXPERTS - 1)

    grid_spec = pltpu.PrefetchScalarGridSpec(
        num_scalar_prefetch=2,
        grid=(N_EXPERTS + 1, nf),
        in_specs=[pl.BlockSpec((F32_SUB, tok), lambda g, f, *_: (this(g) // F32_SUB, f)),
                  pl.BlockSpec((F32_SUB, tok), lambda g, f, *_: (this(g) // F32_SUB, f)),
                  pl.BlockSpec((tok, D_MODEL), lambda g, f, *_: (f, 0)),
                  pl.BlockSpec((1, 1, D_MODEL, fc), lambda g, f, *_: (layer, prev(g), 0, f)),
                  pl.BlockSpec((1, 1, D_MODEL, fc), lambda g, f, *_: (layer, prev(g), 0, f)),
                  pl.BlockSpec((1, 1, fc, D_MODEL), lambda g, f, *_: (layer, prev(g), f, 0))],
        out_specs=pl.BlockSpec((cap, D_MODEL), lambda g, f, *_: (prev(g), 0)),
        scratch_shapes=[pltpu.VMEM((cap, D_MODEL), BF16), pltpu.VMEM((cap, 1), F32),
                        pltpu.VMEM((cap, D_MODEL), BF16), pltpu.VMEM((cap, 1), F32),
                        pltpu.VMEM((cap, D_MODEL), F32)])
    return pl.pallas_call(
        functools.partial(_ffn_kernel, cap=cap, nf=nf, nblk=nblk),
        out_shape=jax.ShapeDtypeStruct((N_EXPERTS * cap, D_MODEL), BF16),
        grid_spec=grid_spec,
        compiler_params=_cparams(("arbitrary", "arbitrary")),
    )(a0.reshape(-1), s1.reshape(-1), slot, aff, h2,
      w_gate, w_up, w_down)


CMB_ROWS = COMB_ROWS
CMB_WIN = LANES


def _combine_kernel(a0_ref, s1_ref, slot_ref, y_hbm, x_ref, g_ref, gate_ref, *rest, cap, nblk):
    *o_refs, win_scr, ovf_scr, f_scr, sem, osem = rest
    j = pl.program_id(0)
    E, R, Tb = N_EXPERTS, CMB_WIN, CMB_ROWS

    def window(e, start, dst, s):
        return pltpu.make_async_copy(y_hbm.at[pl.ds(e * cap + start, R)], dst, s)

    def issue(blk, slot):
        for e in range(E):
            a0 = pl.multiple_of(a0_ref[e * nblk + blk], BF16_SUB)
            window(e, a0, win_scr.at[slot, pl.ds(e * R, R)], sem.at[slot]).start()

    @pl.when(j == 0)
    def _():
        issue(0, 0)

    slot = j % 2
    pltpu.make_async_copy(y_hbm.at[pl.ds(0, E * R)], win_scr.at[slot], sem.at[slot]).wait()

    @pl.when(j + 1 < nblk)
    def _():
        issue(j + 1, 1 - slot)

    sl = slot_ref[...].astype(jnp.int32)
    q = lax.broadcasted_iota(jnp.int32, (Tb, R), 1)
    onehot = jnp.concatenate(
        [(sl[:, e:e + 1] - a0_ref[e * nblk + j] == q) for e in range(E)], axis=1).astype(F32).astype(BF16)
    f_scr[...] = _dot(onehot, win_scr[slot])

    for e in range(E):
        end = s1_ref[e * nblk + j]

        def more(c):
            return c < end

        def extra(c, e=e):
            c0 = pl.multiple_of(jnp.minimum(c, cap - R), BF16_SUB)
            cp = window(e, c0, ovf_scr, osem)
            cp.start()
            cp.wait()
            hit = (sl[:, e:e + 1] - c0 == q) & (q + c0 >= c)
            f_scr[...] += _dot(hit.astype(F32).astype(BF16), ovf_scr[...])
            return c + R

        lax.while_loop(more, extra, a0_ref[e * nblk + j] + R)

    res = x_ref[...] + gate_ref[0] * _rms(f_scr[...], g_ref[...])
    if len(o_refs) == 1:
        o_refs[0][...] = res
    else:
        @pl.when(j < NS // Tb)
        def _():
            o_refs[0][...] = res

        @pl.when(j >= NS // Tb)
        def _():
            o_refs[1][...] = res


def _combine(a0, s1, slot_t, y, x, g, mod, layer, split_out=False):
    cap = y.shape[0] // N_EXPERTS
    nblk = NT // CMB_ROWS
    if split_out:
        out_shape = [jax.ShapeDtypeStruct((NS, D_MODEL), F32), jax.ShapeDtypeStruct((NP, D_MODEL), F32)]
        out_specs = _stream_specs(D_MODEL, tile=CMB_ROWS)
    else:
        out_shape = jax.ShapeDtypeStruct((NT, D_MODEL), F32)
        out_specs = pl.BlockSpec((CMB_ROWS, D_MODEL), lambda i, *_: (i, 0))
    grid_spec = pltpu.PrefetchScalarGridSpec(
        num_scalar_prefetch=2,
        grid=(nblk,),
        in_specs=[pl.BlockSpec((CMB_ROWS, LANES), lambda i, *_: (i, 0)),
                  pl.BlockSpec(memory_space=pl.ANY),
                  pl.BlockSpec((CMB_ROWS, D_MODEL), lambda i, *_: (i, 0)),
                  pl.BlockSpec((1, D_MODEL), lambda i, *_: (0, 0)),
                  _mod_spec(layer, 5, CMB_ROWS)],
        out_specs=out_specs,
        scratch_shapes=[pltpu.VMEM((2, N_EXPERTS * CMB_WIN, D_MODEL), BF16),
                        pltpu.VMEM((CMB_WIN, D_MODEL), BF16),
                        pltpu.VMEM((CMB_ROWS, D_MODEL), F32),
                        pltpu.SemaphoreType.DMA((2,)), pltpu.SemaphoreType.DMA(())])
    return pl.pallas_call(
        functools.partial(_combine_kernel, cap=cap, nblk=nblk),
        out_shape=out_shape,
        grid_spec=grid_spec,
        compiler_params=_cparams(("arbitrary",)),
    )(a0.reshape(-1), s1.reshape(-1), slot_t, y, x, g.reshape(1, D_MODEL), mod)


CUM_BLK = 256
SCALE_STEP = 2.0 ** -16
SCALE_ITERS = 10
BISECT_ITERS = 64


def _prefix_count(mask):
    n = mask.shape[1]
    tri = (lax.broadcasted_iota(jnp.int32, (CUM_BLK, CUM_BLK), 0)
           <= lax.broadcasted_iota(jnp.int32, (CUM_BLK, CUM_BLK), 1)).astype(F32).astype(BF16)
    carry = jnp.zeros((mask.shape[0], 1), F32)
    outs = []
    for b in range(n // CUM_BLK):
        c = _dot(mask[:, b * CUM_BLK:(b + 1) * CUM_BLK].astype(F32).astype(BF16), tri) + carry
        outs.append(c)
        carry = c[:, CUM_BLK - 1:CUM_BLK]
    return jnp.concatenate(outs, axis=1)


def _route_kernel(lg_ref, aff_ref, slot_ref, slott_ref, s0_ref):
    x = lg_ref[...]
    ex = jnp.exp(x - jnp.max(x, axis=0, keepdims=True))
    aff = ex / jnp.sum(ex, axis=0, keepdims=True)
    aff_ref[...] = aff
    streams = ((0, NS), (NS, NP))
    caps = [float(EC_CAPACITY * n // N_EXPERTS) for _, n in streams]
    parts = [aff[:, lo:lo + n] for lo, n in streams]

    def enough(k, thr):
        return jnp.sum((parts[k] >= thr).astype(F32), axis=1, keepdims=True) >= caps[k]

    def scale_step(_, carry):
        out = []
        for k, (lo_v, hi_v, found_f) in enumerate(carry):
            found = found_f > 0.5
            mid = hi_v * SCALE_STEP
            ge = enough(k, mid)
            out.append((jnp.where(found | ~ge, lo_v, mid), jnp.where(found | ge, hi_v, mid),
                        jnp.where(found | ge, 1.0, 0.0)))
        return tuple(out)

    def bisect_step(_, carry):
        out = []
        for k, (lo_v, hi_v) in enumerate(carry):
            mid = 0.5 * (lo_v + hi_v)
            ge = enough(k, mid)
            out.append((jnp.where(ge, mid, lo_v), jnp.where(ge, hi_v, mid)))
        return tuple(out)

    col = lambda v: jnp.full((N_EXPERTS, 1), v, F32)
    start = tuple((col(0.0), col(2.0), col(0.0)) for _ in streams)
    scaled = lax.fori_loop(0, SCALE_ITERS, scale_step, start)
    bounds = lax.fori_loop(0, BISECT_ITERS, bisect_step, tuple((lo_v, hi_v) for lo_v, hi_v, _ in scaled))
    slots = []
    off = 0.0
    for k, (lo_v, hi_v) in enumerate(bounds):
        above = parts[k] >= hi_v
        edge = (parts[k] >= lo_v) & ~above
        need = caps[k] - jnp.sum(above.astype(F32), axis=1, keepdims=True)
        edge_rank = _prefix_count(edge) - edge.astype(F32)
        sel = above | (edge & (edge_rank < need))
        slots.append(jnp.where(sel, _prefix_count(sel) - 1.0 + off, -1.0))
        off += caps[k]
    slot = jnp.concatenate(slots, axis=1)
    slot_ref[...] = slot
    pad = jnp.full((LANES - N_EXPERTS, NT), -1.0, F32)
    slott_ref[...] = jnp.concatenate([slot, pad], axis=0).T
    tok = lax.broadcasted_iota(jnp.int32, (NT, LANES), 0)
    edge = lax.broadcasted_iota(jnp.int32, (NT, LANES), 1) * CMB_ROWS
    s0_ref[...] = _dot((slot >= 0.0).astype(F32).astype(BF16), (tok < edge).astype(F32).astype(BF16))


def _route_tables(logits_t):
    return pl.pallas_call(
        _route_kernel,
        out_shape=[jax.ShapeDtypeStruct((N_EXPERTS, NT), F32), jax.ShapeDtypeStruct((N_EXPERTS, NT), F32),
                   jax.ShapeDtypeStruct((NT, LANES), F32), jax.ShapeDtypeStruct((N_EXPERTS, LANES), F32)],
        compiler_params=pltpu.CompilerParams(vmem_limit_bytes=VMEM_LIMIT),
    )(logits_t)


def _route(logits_t, h2, x, g, mod, layer, w_gate, w_up, w_down, split_out=False):
    aff, slot, slot_t, s0 = _route_tables(logits_t)
    cap = EC_CAPACITY * NT // N_EXPERTS
    s0 = s0[:, :NT // CMB_ROWS + 1].astype(jnp.int32)

    def windows(starts, win):
        return jnp.minimum(starts[:, :-1] // BF16_SUB * BF16_SUB, cap - win), starts[:, 1:]

    a0g, s1g = windows(s0[:, ::COMB_ROWS // CMB_ROWS], COMB_WIN)
    y = _expert_ffn(a0g, s1g, slot, aff, h2, layer, w_gate, w_up, w_down)
    a0c, s1c = windows(s0, CMB_WIN)
    return _combine(a0c, s1c, slot_t, y, x, g, mod, layer, split_out)


def kernel(x_prompt, x_sample, cache_b_k, cache_b_v, cache_c_k, cache_c_v, c, c_ctx, ada_w, ada_b, norm_g,
           ab_w_in, hy_short_w, hy_short_b, hy_w1, hy_b1, hy_w2, hy_b2, hy_freq, hy_w3, hy_log_decay, hy_bias,
           b_sink, ab_w_out, c_w_qkv, c_q_norm, c_k_norm, c_w_out, ec_router, ec_w_gate, ec_w_up, ec_w_down):
    x = (x_sample.reshape(NS, D_MODEL), x_prompt.reshape(NP, D_MODEL))
    cond = jnp.concatenate([c, c_ctx[None], jnp.zeros((MOD_ROWS - DEC_BATCH - 1, D_MODEL), F32)], axis=0)
    mod = _adaln(cond, ada_w, ada_b).reshape(DEPTH * MOD_ROWS * 6, 1, D_MODEL)
    rope = _rope_tables(DEC_SEQ)
    hd = HEAD_DIM

    e = 0
    p_hy, p_at = _norm_mod_matmul(x, norm_g[0, 0], mod, 0, ab_w_in[e].astype(BF16), (3 * HY_CH, D_MODEL - HY_CH + 2 * B_KV * hd))
    ya = []
    for L, nb, row0, seqs in ((DEC_SEQ, DEC_BATCH, 0, 2), (SEQ, BATCH, NS, 4)):
        table_lo, table_bf, table_t_bf = _dft_tables(L)
        h = _hyena_filters(L, hy_w1[e], hy_b1[e], hy_w2[e], hy_b2[e], hy_freq[e], hy_w3[e], hy_log_decay[e])
        h = h.at[0, HY_CH:].set(0.0)
        spec = _filter_spectrum(L, table_bf, table_lo, h)
        ya.append(_hyena(p_hy, hy_short_w[e], hy_short_b[e].reshape(1, -1), hy_bias[e].reshape(1, -1),
                         table_bf, table_t_bf, spec, L, nb, row0, seqs))
    ya = tuple(ya)
    ctx_b = (cache_b_k[:, e].reshape(DEC_BATCH, PAST_LEN, B_KV * hd), cache_b_v[:, e].reshape(DEC_BATCH, PAST_LEN, B_KV * hd))
    call, args = _attention(p_at, B_HEADS * hd, 2, L=DEC_SEQ, Tq=256, nb=DEC_BATCH, row0=0, H=B_HEADS, KVH=B_KV,
                            ctx=ctx_b, sink=b_sink[e], rope=rope, window=True)
    yb_s = call(*args)[0]
    call, args = _attention(p_at, B_HEADS * hd, 2, L=SEQ, Tq=256, nb=BATCH, row0=NS, H=B_HEADS, KVH=B_KV,
                            sink=b_sink[e])
    yb_p = call(*args)[0]
    yb = (yb_s, yb_p)
    state_b_k = p_at[NS:, B_HEADS * hd:(B_HEADS + B_KV) * hd].reshape(BATCH, 1, SEQ, B_KV, hd)
    state_b_v = p_at[NS:, (B_HEADS + B_KV) * hd:].reshape(BATCH, 1, SEQ, B_KV, hd)
    x, h2, lg = _out_proj(ya, 0, yb, 0, ab_w_out[e].astype(BF16), x, norm_g[0, 1], norm_g[0, 2], mod, 0, ec_router[0].T)
    x = _route(lg, h2, x, norm_g[0, 3], mod, 0, ec_w_gate, ec_w_up, ec_w_down)

    o = 0
    (p,) = _norm_mod_matmul(x, norm_g[1, 0], mod, 1, c_w_qkv[o].astype(BF16), ((C_HEADS + 2 * C_KV) * hd,))
    ctx_c = (cache_c_k[:, o].reshape(DEC_BATCH, PAST_LEN, C_KV * hd), cache_c_v[:, o].reshape(DEC_BATCH, PAST_LEN, C_KV * hd))
    nrm = (c_q_norm[o], c_k_norm[o])
    call, args = _attention(p, C_HEADS * hd, 2, L=DEC_SEQ, Tq=256, nb=DEC_BATCH, row0=0, H=C_HEADS, KVH=C_KV,
                            ctx=ctx_c, norm=nrm, rope=rope)
    yc_s = call(*args)[0]
    call, args = _attention(p, C_HEADS * hd, 2, L=SEQ, Tq=256, nb=BATCH, row0=NS, H=C_HEADS, KVH=C_KV,
                            norm=nrm, emit_k=True)
    yc_p, kn = call(*args)
    yc = (yc_s, yc_p)
    state_c_k = kn.reshape(BATCH, 1, SEQ, C_KV, hd)
    state_c_v = p[NS:, (C_HEADS + C_KV) * hd:].reshape(BATCH, 1, SEQ, C_KV, hd)
    x, h2, lg = _out_proj(yc, 0, yc, 1, c_w_out[o].astype(BF16), x, norm_g[1, 1], norm_g[1, 2], mod, 1, ec_router[1].T)
    xs, xp = _route(lg, h2, x, norm_g[1, 3], mod, 1, ec_w_gate, ec_w_up, ec_w_down, split_out=True)
    y_sample = xs.reshape(DEC_BATCH, DEC_SEQ, D_MODEL)
    y_prompt = xp.reshape(BATCH, SEQ, D_MODEL)
    return (y_prompt, y_sample, state_b_k, state_b_v, state_c_k, state_c_v)
```

```python
import functools
import math

import jax
import jax.numpy as jnp
from jax import lax
from jax.experimental import pallas as pl
from jax.experimental.pallas import tpu as pltpu

F32 = jnp.float32
BF16 = jnp.bfloat16

D_MODEL = 1024
BATCH = 16
SEQ = 256
DEPTH = 2
DEC_BATCH = 8
DEC_SEQ = 1024
PAST_LEN = 512
GRID_W = 64
HEAD_DIM = 64
HY_CH = D_MODEL // 2
HY_BANDS = 8
B_HEADS = 8
B_KV = 2
WINDOW = 128
C_HEADS = 16
C_KV = 4
ROPE_BASE = 10000.0
N_EXPERTS = 16
EC_CAPACITY = 2
EXPERT_FF = 2 * D_MODEL
EPS = 1e-6
NEG_INF = -1e30

NS = DEC_BATCH * DEC_SEQ
NP = BATCH * SEQ
NT = NS + NP
MOD_ROWS = 16
LANES = 128
VMEM_LIMIT = 56 * 1024 * 1024


def _cparams(sem):
    return pltpu.CompilerParams(dimension_semantics=sem, vmem_limit_bytes=VMEM_LIMIT)


def _split(x):
    hi = x.astype(BF16)
    lo = (x - hi.astype(F32)).astype(BF16)
    return hi, lo


_NN = (((1,), (0,)), ((), ()))
_NT = (((1,), (1,)), ((), ()))


def _dot(a, b, dims=_NN):
    return lax.dot_general(a, b, dims, preferred_element_type=F32)


def _dot3(a, b, dims=_NN):
    ah, al = _split(a)
    bh, bl = _split(b)
    return _dot(ah, bh, dims) + _dot(al, bh, dims) + _dot(ah, bl, dims)


def _rms(x, g):
    return x * lax.rsqrt(jnp.mean(x * x, axis=-1, keepdims=True) + EPS) * g


def _mod_row(tile_rows):
    def f(i):
        return jnp.minimum(i * tile_rows // DEC_SEQ, DEC_BATCH)
    return f


def _adaln_kernel(c_ref, w_ref, b_ref, o_ref):
    s = jax.nn.silu(c_ref[...])
    o_ref[0] = _dot3(s, w_ref[0]) + b_ref[0]


def _adaln(cond, ada_w, ada_b):
    tn = 1536
    return pl.pallas_call(
        _adaln_kernel,
        out_shape=jax.ShapeDtypeStruct((DEPTH, MOD_ROWS, 6 * D_MODEL), F32),
        grid=(DEPTH, 6 * D_MODEL // tn),
        in_specs=[pl.BlockSpec((MOD_ROWS, D_MODEL), lambda l, j: (0, 0)),
                  pl.BlockSpec((1, D_MODEL, tn), lambda l, j: (l, 0, j)),
                  pl.BlockSpec((1, 1, tn), lambda l, j: (l, 0, j))],
        out_specs=pl.BlockSpec((1, MOD_ROWS, tn), lambda l, j: (l, 0, j)),
        compiler_params=_cparams(("arbitrary", "arbitrary")),
    )(cond, ada_w, ada_b.reshape(DEPTH, 1, 6 * D_MODEL))


def _mod_spec(layer, which, tile_rows):
    row = _mod_row(tile_rows)
    return pl.BlockSpec((1, 1, D_MODEL), lambda i, *_: ((layer * MOD_ROWS + row(i)) * 6 + which, 0, 0))


ROW_TILE = 512
NS_TILES = NS // ROW_TILE


def _stream_specs(width, col=0, tile=ROW_TILE):
    ns = NS // tile
    return [pl.BlockSpec((tile, width), lambda i, *_: (jnp.minimum(i, ns - 1), col)),
            pl.BlockSpec((tile, width), lambda i, *_: (jnp.maximum(i - ns, 0), col))]


def _row_specs(a, width, col=0):
    if isinstance(a, tuple):
        return _stream_specs(width, col), list(a)
    return [pl.BlockSpec((ROW_TILE, width), lambda i, *_: (i, col))], [a]


ROW_CHUNK = 128


def _take_rows(refs, paired, rows):
    if paired:
        return jnp.where(pl.program_id(0) < NS_TILES, refs[0][rows, :], refs[1][rows, :]), refs[2:]
    return refs[0][rows, :], refs[1:]


def _nmm_kernel(*refs, paired):
    for c in range(ROW_TILE // ROW_CHUNK):
        rows = slice(c * ROW_CHUNK, (c + 1) * ROW_CHUNK)
        x, (g_ref, sh_ref, sc_ref, w_ref, *o_refs) = _take_rows(refs, paired, rows)
        h = _rms(x, g_ref[...]) * (1.0 + sc_ref[0]) + sh_ref[0]
        p = _dot(h.astype(BF16), w_ref[...])
        off = 0
        for o in o_refs:
            n = o.shape[1]
            o[rows, :] = p[:, off:off + n]
            off += n


def _norm_mod_matmul(x, g, mod, layer, w_bf, splits):
    tm = ROW_TILE
    n_out = w_bf.shape[1]
    x_specs, x_args = _row_specs(x, D_MODEL)
    return pl.pallas_call(
        functools.partial(_nmm_kernel, paired=isinstance(x, tuple)),
        out_shape=[jax.ShapeDtypeStruct((NT, n), F32) for n in splits],
        grid=(NT // tm,),
        in_specs=x_specs + [pl.BlockSpec((1, D_MODEL), lambda i: (0, 0)),
                            _mod_spec(layer, 0, tm), _mod_spec(layer, 1, tm),
                            pl.BlockSpec((D_MODEL, n_out), lambda i: (0, 0))],
        out_specs=[pl.BlockSpec((tm, n), lambda i: (i, 0)) for n in splits],
        compiler_params=_cparams(("arbitrary",)),
    )(*x_args, g.reshape(1, D_MODEL), mod, mod, w_bf)


def _hyena_filters(L, w1, b1, w2, b2, freq, w3, log_decay):
    hp = lax.Precision.HIGHEST
    t = jnp.linspace(0.0, 1.0, L, dtype=F32)[:, None]
    w = (2.0 * math.pi / L) * jnp.arange(L, dtype=F32)[:, None]
    bands = jnp.linspace(1e-4, HY_BANDS - 1, HY_BANDS, dtype=F32)[None, :]
    z = jnp.concatenate([t, jnp.cos(bands * w), -jnp.sin(bands * w)], axis=-1)
    h = jnp.sin(freq[0] * (jnp.dot(z, w1, precision=hp) + b1))
    h = jnp.sin(freq[1] * (jnp.dot(h, w2, precision=hp) + b2))
    return jnp.dot(h, w3, precision=hp) * jnp.exp(-t * jnp.exp(log_decay))


def _dft_kernel(fl_ref, fb_ref, gb_ref, *, L):
    n = 2 * L
    w = 2.0 * math.pi / n
    r = lax.broadcasted_iota(jnp.int32, (L, LANES), 0)
    b = lax.broadcasted_iota(jnp.int32, (L, LANES), 1)
    r1 = lax.broadcasted_iota(jnp.int32, (L, 1), 0)
    ang_b = ((r * b) & (n - 1)).astype(F32) * w
    cb = jnp.cos(ang_b)
    sb = jnp.sin(ang_b)
    by_col = jnp.where((b & 1) == 0, 1.0, -1.0)
    by_row = jnp.where((r & 1) == 0, 1.0, -1.0)
    for a in range(L // LANES):
        ang_a = ((r1 * (a * LANES)) & (n - 1)).astype(F32) * w
        ca = jnp.cos(ang_a)
        sa = jnp.sin(ang_a)
        cs = ca * cb - sa * sb
        sn = -(sa * cb + ca * sb)
        cols = slice(a * LANES, (a + 1) * LANES)
        fs = jnp.where(r == 0, by_col, sn)
        cs_hi, cs_lo = _split(cs)
        fs_hi, fs_lo = _split(fs)
        fb_ref[0:L, cols] = cs_hi
        fb_ref[L:2 * L, cols] = fs_hi
        fl_ref[0:L, cols] = cs_lo
        fl_ref[L:2 * L, cols] = fs_lo
        gb_ref[:, cols] = cs_hi
        gs = jnp.where(b == 0, by_row, sn) if a == 0 else sn
        gb_ref[:, L + a * LANES:L + (a + 1) * LANES] = gs.astype(BF16)


def _dft_tables(L):
    assert L & (L - 1) == 0
    return pl.pallas_call(
        functools.partial(_dft_kernel, L=L),
        out_shape=[jax.ShapeDtypeStruct((2 * L, L), BF16), jax.ShapeDtypeStruct((2 * L, L), BF16),
                   jax.ShapeDtypeStruct((L, 2 * L), BF16)],
        compiler_params=pltpu.CompilerParams(vmem_limit_bytes=VMEM_LIMIT),
    )()


def _filt_kernel(fb_ref, fl_ref, hf_ref, hb_ref, p_ref, q_ref, s_ref, *, L):
    hf = hf_ref[...]
    hb = hb_ref[...]
    ct = hf.shape[1]
    nrm = lax.rsqrt(jnp.sum(hf * hf, axis=0, keepdims=True) + jnp.sum(hb * hb, axis=0, keepdims=True) + EPS)
    hh, hl = _split(jnp.concatenate([hf, hb], axis=1))
    fb = fb_ref[...]
    fh = _dot(fb, hh) + _dot(fl_ref[...], hh) + _dot(fb, hl)
    row = lax.broadcasted_iota(jnp.int32, (L, ct), 0)
    w = jnp.where(row == 0, 1.0 / (2 * L), 2.0 / (2 * L)) * nrm
    re = fh[:L, :ct] + fh[:L, ct:]
    p_ref[...] = re * w
    q_ref[...] = jnp.where(row == 0, 0.0, fh[L:, :ct] - fh[L:, ct:]) * w
    s_ref[...] = jnp.where(row == 0, fh[L:, :ct] + fh[L:, ct:], re) * w


def _filter_spectrum(L, table_hi, table_lo, h):
    ct = 128
    nc = HY_CH // ct
    shp = jax.ShapeDtypeStruct((L, HY_CH), F32)
    return pl.pallas_call(
        functools.partial(_filt_kernel, L=L),
        out_shape=[shp, shp, shp],
        grid=(nc,),
        in_specs=[pl.BlockSpec((2 * L, L), lambda c: (0, 0)),
                  pl.BlockSpec((2 * L, L), lambda c: (0, 0)),
                  pl.BlockSpec((L, ct), lambda c: (0, c)),
                  pl.BlockSpec((L, ct), lambda c: (0, nc + c))],
        out_specs=[pl.BlockSpec((L, ct), lambda c: (0, c))] * 3,
        compiler_params=_cparams(("arbitrary",)),
    )(table_hi, table_lo, h, h)


def _hyena_kernel(x0_ref, x1_ref, v_ref, w0_ref, w1_ref, wv_ref, b0_ref, b1_ref, bv_ref, bias_ref,
                  f_ref, g_ref, p_ref, q_ref, s_ref, o_ref, *, L, seqs):
    ct = x0_ref.shape[1]
    row = lax.broadcasted_iota(jnp.int32, (L, ct), 0)
    for n in range(seqs):
        rows = slice(n * L, (n + 1) * L)

        def sconv(x_ref, w_ref, b_ref, rows=rows):
            x = x_ref[rows, :]
            xm = jnp.where(row == 0, 0.0, pltpu.roll(x, 1, 0))
            xp = jnp.where(row == L - 1, 0.0, pltpu.roll(x, L - 1, 0))
            return xm * w_ref[0:1, :] + x * w_ref[1:2, :] + xp * w_ref[2:3, :] + b_ref[...]

        z = sconv(v_ref, wv_ref, bv_ref) * sconv(x1_ref, w1_ref, b1_ref)
        zf = _dot(f_ref[...], z.astype(BF16))
        a = zf[:L]
        b = zf[L:]
        p = p_ref[...]
        q = q_ref[...]
        yre = a * p - b * q
        yim = a * q + b * s_ref[...]
        ycat = jnp.concatenate([yre, yim], axis=0).astype(BF16)
        y = _dot(g_ref[...], ycat) + z * bias_ref[...]
        o_ref[rows, :] = sconv(x0_ref, w0_ref, b0_ref) * y


def _hyena(p_hy, short_w, short_b, bias, table_bf, table_t_bf, spec, L, nb, row0, seqs):
    ct = 256
    nc = HY_CH // ct
    P, Q, S = spec
    blk = seqs * L

    def xs(k):
        return pl.BlockSpec((blk, ct), lambda b, c: (row0 // blk + b, k * nc + c))

    def ws(rows, k):
        return pl.BlockSpec((rows, ct), lambda b, c: (0, k * nc + c))

    cs = pl.BlockSpec((L, ct), lambda b, c: (0, c))
    return pl.pallas_call(
        functools.partial(_hyena_kernel, L=L, seqs=seqs),
        out_shape=jax.ShapeDtypeStruct((nb * L, HY_CH), F32),
        grid=(nb // seqs, nc),
        in_specs=[xs(0), xs(1), xs(2), ws(3, 0), ws(3, 1), ws(3, 2), ws(1, 0), ws(1, 1), ws(1, 2), ws(1, 0),
                  pl.BlockSpec((2 * L, L), lambda b, c: (0, 0)),
                  pl.BlockSpec((L, 2 * L), lambda b, c: (0, 0)),
                  cs, cs, cs],
        out_specs=pl.BlockSpec((blk, ct), lambda b, c: (b, c)),
        compiler_params=_cparams(("arbitrary", "arbitrary")),
    )(p_hy, p_hy, p_hy, short_w, short_w, short_w, short_b, short_b, short_b, bias,
      table_bf, table_t_bf, P, Q, S)


def _rope_tables(L):
    half = HEAD_DIM // 2
    pos = jnp.arange(L)
    r = (pos // GRID_W).astype(F32)
    col = (pos % GRID_W).astype(F32)
    inv = ROPE_BASE ** (-jnp.arange(0, half, 2, dtype=F32) / half)
    ar = r[:, None] * inv[None, :]
    ac = col[:, None] * inv[None, :]
    z = jnp.zeros_like(ar)
    cos = jnp.concatenate([jnp.cos(ar), jnp.cos(ar), jnp.cos(ac), jnp.cos(ac)], axis=1)
    s1 = jnp.concatenate([-jnp.sin(ar), z, -jnp.sin(ac), z], axis=1)
    s2 = jnp.concatenate([z, jnp.sin(ar), z, jnp.sin(ac)], axis=1)
    rep = LANES // HEAD_DIM
    return jnp.tile(cos, (1, rep)), jnp.tile(s1, (1, rep)), jnp.tile(s2, (1, rep))


def _attn_kernel(*refs, L, Tq, H, KVH, Lc, window, use_sink, use_norm, use_rope, emit_k):
    G = H // KVH
    hd = HEAD_DIM
    per = LANES // hd
    pad = WINDOW if window else 0
    it = iter(refs)
    q_ref = next(it)
    kv_ref = next(it)
    ck_ref = cv_ref = sink_ref = qn_ref = kn_ref = bd_ref = rc_ref = rs1_ref = rs2_ref = ko_ref = vo_ref = None
    if Lc:
        ck_ref = next(it)
        cv_ref = next(it)
    if use_sink:
        sink_ref = next(it)
    if use_norm:
        qn_ref = next(it)
        kn_ref = next(it)
        bd_ref = next(it)
    if use_rope:
        rc_ref = next(it)
        rs1_ref = next(it)
        rs2_ref = next(it)
    o_ref = next(it)
    if emit_k:
        ko_ref = next(it)
        vo_ref = next(it)
    k_scr = next(it)
    v_scr = next(it)
    i = pl.program_id(1)
    lat0 = Lc + pad

    def head_norm(x, g_ref):
        hi, lo = _split(x * x)
        ms = _dot(hi, bd_ref[...]) + _dot(lo, bd_ref[...])
        return x * lax.rsqrt(ms + EPS) * g_ref[...]

    def rope(x, rows):
        return (x * rc_ref[rows, :] + pltpu.roll(x, LANES - 16, 1) * rs1_ref[rows, :]
                + pltpu.roll(x, 16, 1) * rs2_ref[rows, :])

    def put(scr, g, rows, x):
        scr[g, 0, rows, :] = x.astype(BF16)
        scr[g, 1, rows, :] = pltpu.roll(x, hd, 1).astype(BF16)

    @pl.when(i == 0)
    def _():
        n_kg = KVH // per
        if window:
            zpad = jnp.zeros((n_kg, per, pad, LANES), BF16)
            for scr in (k_scr, v_scr):
                scr[:, :, Lc:Lc + pad, :] = zpad
                scr[:, :, lat0 + L:lat0 + L + pad, :] = zpad
        for g in range(n_kg):
            cols = slice(LANES * g, LANES * (g + 1))
            kg = kv_ref[:, cols]
            vg = kv_ref[:, KVH * hd + LANES * g:KVH * hd + LANES * (g + 1)]
            if use_norm:
                kg = head_norm(kg, kn_ref)
            if emit_k:
                for hh in range(per):
                    ko_ref[0, 0, :, g * per + hh, :] = kg[:, hd * hh:hd * (hh + 1)]
                    vo_ref[0, 0, :, g * per + hh, :] = vg[:, hd * hh:hd * (hh + 1)]
            if use_rope:
                kg = rope(kg, slice(None))
            put(k_scr, g, slice(lat0, lat0 + L), kg)
            put(v_scr, g, slice(lat0, lat0 + L), vg)
            if Lc:
                put(k_scr, g, slice(0, Lc), ck_ref[0][:, cols])
                put(v_scr, g, slice(0, Lc), cv_ref[0][:, cols])

    rows = pl.ds(pl.multiple_of(i * Tq, Tq), Tq)
    low = lax.broadcasted_iota(jnp.int32, (Tq, LANES), 1) < hd
    groups_per_kv = G // per
    hrow = lax.broadcasted_iota(jnp.int32, (groups_per_kv * Tq, 1), 0) // Tq
    for kvh in range(KVH):
        qgs = []
        for g in range(kvh * groups_per_kv, (kvh + 1) * groups_per_kv):
            qg = q_ref[:, LANES * g:LANES * (g + 1)]
            if use_norm:
                qg = head_norm(qg, qn_ref)
            if use_rope:
                qg = rope(qg, rows)
            qgs.append((qg * (hd ** -0.5)).astype(BF16))
        outs = []
        for half in range(per):
            keep = low if half == 0 else ~low
            qs = jnp.concatenate([jnp.where(keep, qg, jnp.zeros_like(qg)) for qg in qgs], axis=0)
            var = (kvh % per) ^ half
            kg = kvh // per
            if use_sink:
                sk = jnp.zeros((groups_per_kv * Tq, 1), F32)
                for n in range(groups_per_kv):
                    sk = jnp.where(hrow == n, sink_ref[0, (kvh * groups_per_kv + n) * per + half], sk)
            if window:
                W = Tq + 2 * pad
                wrows = pl.ds(pl.multiple_of(Lc + i * Tq, Tq), W)
                s_c = _dot(qs, k_scr[kg, var, 0:Lc, :], _NT)
                s_w = _dot(qs, k_scr[kg, var, wrows, :], _NT).reshape(groups_per_kv, Tq, W)
                r = lax.broadcasted_iota(jnp.int32, (Tq, W), 0)
                c = lax.broadcasted_iota(jnp.int32, (Tq, W), 1)
                kpos = i * Tq - pad + c
                valid = (kpos >= 0) & (kpos < L) & (jnp.abs(r + pad - c) <= WINDOW)
                s_w = jnp.where(valid[None], s_w, NEG_INF).reshape(groups_per_kv * Tq, W)
                m = jnp.maximum(jnp.max(s_c, axis=-1, keepdims=True), jnp.max(s_w, axis=-1, keepdims=True))
                if use_sink:
                    m = jnp.maximum(m, sk)
                e_c = jnp.exp(s_c - m)
                e_w = jnp.exp(s_w - m)
                den = jnp.sum(e_c, axis=-1, keepdims=True) + jnp.sum(e_w, axis=-1, keepdims=True)
                o = (_dot(e_c.astype(BF16), v_scr[kg, var, 0:Lc, :])
                     + _dot(e_w.astype(BF16), v_scr[kg, var, wrows, :]))
            else:
                s = _dot(qs, k_scr[kg, var], _NT)
                m = jnp.max(s, axis=-1, keepdims=True)
                if use_sink:
                    m = jnp.maximum(m, sk)
                e = jnp.exp(s - m)
                den = jnp.sum(e, axis=-1, keepdims=True)
                o = _dot(e.astype(BF16), v_scr[kg, var])
            if use_sink:
                den = den + jnp.exp(sk - m)
            outs.append(o / den)
        for n in range(groups_per_kv):
            g = kvh * groups_per_kv + n
            o_ref[:, LANES * g:LANES * (g + 1)] = jnp.where(low, outs[0][n * Tq:(n + 1) * Tq],
                                                            outs[1][n * Tq:(n + 1) * Tq])


def _attention(p, q_cols, kv_colblk, *, L, Tq, nb, row0, H, KVH, ctx=None, sink=None, norm=None, rope=None,
               window=False, emit_k=False):
    hd = HEAD_DIM
    nq = L // Tq
    Lc = 0 if ctx is None else ctx[0].shape[1]
    pad = WINDOW if window else 0
    Lk = Lc + L + 2 * pad
    kvw = 2 * KVH * hd
    args = [p, p]
    in_specs = [pl.BlockSpec((Tq, q_cols), lambda b, i: (row0 // Tq + b * nq + i, 0)),
                pl.BlockSpec((L, kvw), lambda b, i: (row0 // L + b, kv_colblk))]
    if ctx is not None:
        for a in ctx:
            args.append(a)
            in_specs.append(pl.BlockSpec((1, Lc, KVH * hd), lambda b, i: (b, 0, 0)))
    if sink is not None:
        args.append(sink.reshape(1, H))
        in_specs.append(pl.BlockSpec(memory_space=pltpu.SMEM))
    if norm is not None:
        bd = (jnp.arange(LANES)[:, None] // hd == jnp.arange(LANES)[None, :] // hd).astype(F32) / hd
        for a in (jnp.tile(norm[0], LANES // hd).reshape(1, LANES), jnp.tile(norm[1], LANES // hd).reshape(1, LANES)):
            args.append(a)
            in_specs.append(pl.BlockSpec((1, LANES), lambda b, i: (0, 0)))
        args.append(bd.astype(BF16))
        in_specs.append(pl.BlockSpec((LANES, LANES), lambda b, i: (0, 0)))
    if rope is not None:
        for a in rope:
            args.append(a)
            in_specs.append(pl.BlockSpec((L, LANES), lambda b, i: (0, 0)))
    out_shape = [jax.ShapeDtypeStruct((nb * L, H * hd), F32)]
    out_specs = [pl.BlockSpec((Tq, H * hd), lambda b, i: (b * nq + i, 0))]
    if emit_k:
        for _ in range(2):
            out_shape.append(jax.ShapeDtypeStruct((nb, 1, L, KVH, hd), F32))
            out_specs.append(pl.BlockSpec((1, 1, L, KVH, hd), lambda b, i: (b, 0, 0, 0, 0)))
    kern = functools.partial(_attn_kernel, L=L, Tq=Tq, H=H, KVH=KVH, Lc=Lc, window=window,
                             use_sink=sink is not None, use_norm=norm is not None,
                             use_rope=rope is not None, emit_k=emit_k)
    call = pl.pallas_call(
        kern, out_shape=out_shape, grid=(nb, nq), in_specs=in_specs, out_specs=out_specs,
        scratch_shapes=[pltpu.VMEM((KVH * hd // LANES, LANES // hd, Lk, LANES), BF16),
                        pltpu.VMEM((KVH * hd // LANES, LANES // hd, Lk, LANES), BF16)],
        compiler_params=_cparams(("arbitrary", "arbitrary")))
    return call, args


def _oproj_kernel(*refs, paired):
    rows = slice(None)
    al, rest = _take_rows(refs, paired[0], rows)
    ar, rest = _take_rows(rest, paired[1], rows)
    x, rest = _take_rows(rest, paired[2], rows)
    w_ref, g1_ref, gate_ref, g2_ref, sh_ref, sc_ref, wr_ref, x1_ref, h2_ref, lg_ref = rest
    k = al.shape[1]
    y = _dot(al.astype(BF16), w_ref[0:k, :]) + _dot(ar.astype(BF16), w_ref[k:2 * k, :])
    x1 = x + gate_ref[0] * _rms(y, g1_ref[...])
    x1_ref[...] = x1
    h2 = _rms(x1, g2_ref[...]) * (1.0 + sc_ref[0]) + sh_ref[0]
    wh, wl = _split(wr_ref[...])
    hb = h2.astype(BF16)
    h2_ref[...] = hb
    lg_ref[...] = _dot(wh, hb, _NT) + _dot(wl, hb, _NT)


def _out_proj(a_l, l_blk, a_r, r_blk, w_bf, x, g1, g2, mod, layer, wr_t):
    tm = ROW_TILE
    half = D_MODEL // 2
    l_specs, l_args = _row_specs(a_l, half, l_blk)
    r_specs, r_args = _row_specs(a_r, half, r_blk)
    x_specs, x_args = _row_specs(x, D_MODEL)
    paired = tuple(isinstance(a, tuple) for a in (a_l, a_r, x))
    return pl.pallas_call(
        functools.partial(_oproj_kernel, paired=paired),
        out_shape=[jax.ShapeDtypeStruct((NT, D_MODEL), F32), jax.ShapeDtypeStruct((NT, D_MODEL), BF16),
                   jax.ShapeDtypeStruct((N_EXPERTS, NT), F32)],
        grid=(NT // tm,),
        in_specs=l_specs + r_specs + x_specs + [
            pl.BlockSpec((D_MODEL, D_MODEL), lambda i: (0, 0)),
            pl.BlockSpec((1, D_MODEL), lambda i: (0, 0)),
            _mod_spec(layer, 2, tm),
            pl.BlockSpec((1, D_MODEL), lambda i: (0, 0)),
            _mod_spec(layer, 3, tm), _mod_spec(layer, 4, tm),
            pl.BlockSpec((N_EXPERTS, D_MODEL), lambda i: (0, 0))],
        out_specs=[pl.BlockSpec((tm, D_MODEL), lambda i: (i, 0)),
                   pl.BlockSpec((tm, D_MODEL), lambda i: (i, 0)),
                   pl.BlockSpec((N_EXPERTS, tm), lambda i: (0, i))],
        compiler_params=_cparams(("arbitrary",)),
    )(*l_args, *r_args, *x_args, w_bf, g1.reshape(1, D_MODEL), mod, g2.reshape(1, D_MODEL), mod, mod, wr_t)


COMB_ROWS = 512
COMB_WIN = 128
BF16_SUB = 16
F32_SUB = 8
FFN_PARTS = 2


def _ffn_kernel(a0_ref, s1_ref, slot_ref, aff_ref, h_ref, wg_ref, wu_ref, wd_ref, y_ref,
                xn_scr, gn_scr, xc_scr, gc_scr, acc_scr, *, cap, nf, nblk):
    grp = pl.program_id(0)
    f = pl.program_id(1)
    R = COMB_WIN
    per_step = nblk // nf
    erow = jnp.minimum(grp, N_EXPERTS - 1) % F32_SUB

    def add_window(jb, c0, hit_of):
        cols = slice(jb * COMB_ROWS, (jb + 1) * COMB_ROWS)
        q = lax.broadcasted_iota(jnp.int32, (R, COMB_ROWS), 0)
        hit = hit_of(slot_ref[pl.ds(erow, 1), cols].astype(jnp.int32), q)
        rows = pl.ds(c0, R)
        piece = _dot(hit.astype(F32).astype(BF16), h_ref[cols, :])
        xn_scr[rows, :] += piece.astype(BF16)
        gn_scr[rows, :] += jnp.sum(jnp.where(hit, aff_ref[pl.ds(erow, 1), cols], 0.0), axis=1, keepdims=True)

    def window_start(jb):
        return pl.multiple_of(a0_ref[grp * nblk + f * per_step + jb], BF16_SUB)

    def gather():
        for jb in range(per_step):
            a0 = window_start(jb)
            add_window(jb, a0, lambda srow, q, a0=a0: srow - a0 == q)

    def gather_more():
        for jb in range(per_step):
            end = s1_ref[grp * nblk + f * per_step + jb]

            def extra(c, jb=jb):
                c0 = pl.multiple_of(jnp.minimum(c, cap - R), BF16_SUB)
                add_window(jb, c0, lambda srow, q: (srow - c0 == q) & (q + c0 >= c))
                return c + R

            lax.while_loop(lambda c, end=end: c < end, extra, window_start(jb) + R)

    def compute(first):
        wg = wg_ref[0, 0].astype(BF16)
        wu = wu_ref[0, 0].astype(BF16)
        wd = wd_ref[0, 0].astype(BF16)
        part_rows = cap // FFN_PARTS
        for r in range(FFN_PARTS):
            rows = slice(r * part_rows, (r + 1) * part_rows)
            xe = xc_scr[rows, :]
            hmid = (jax.nn.silu(_dot(xe, wg)) * _dot(xe, wu)).astype(BF16)
            part = _dot(hmid, wd)
            if first:
                acc_scr[rows, :] = part
            else:
                acc_scr[rows, :] += part

    @pl.when(f == 0)
    def _():
        xn_scr[...] = jnp.zeros_like(xn_scr)
        gn_scr[...] = jnp.zeros_like(gn_scr)

    @pl.when(grp == 0)
    def _():
        gather()

    for first in (True, False):
        first_step = (f == 0) if first else (f != 0)

        @pl.when((grp > 0) & (grp < N_EXPERTS) & first_step)
        def _():
            gather()
            compute(first)

        @pl.when((grp == N_EXPERTS) & first_step)
        def _():
            compute(first)

    @pl.when(grp < N_EXPERTS)
    def _():
        gather_more()

    @pl.when(f == nf - 1)
    def _():
        @pl.when(grp > 0)
        def _():
            y_ref[...] = (acc_scr[...] * gc_scr[...]).astype(BF16)

        xc_scr[...] = xn_scr[...]
        gc_scr[...] = gn_scr[...]


def _expert_ffn(a0, s1, slot, aff, h2, layer, w_gate, w_up, w_down):
    cap = EC_CAPACITY * NT // N_EXPERTS
    fc = 512
    nf = EXPERT_FF // fc
    nblk = NT // COMB_ROWS
    tok = NT // nf

    def prev(g):
        return jnp.maximum(g - 1, 0)

    def this(g):
        return jnp.minimum(g, N_EXPERTS - 1)

    grid_spec = pltpu.PrefetchScalarGridSpec(
        num_scalar_prefetch=2,
        grid=(N_EXPERTS + 1, nf),
        in_specs=[pl.BlockSpec((F32_SUB, tok), lambda g, f, *_: (this(g) // F32_SUB, f)),
                  pl.BlockSpec((F32_SUB, tok), lambda g, f, *_: (this(g) // F32_SUB, f)),
                  pl.BlockSpec((tok, D_MODEL), lambda g, f, *_: (f, 0)),
                  pl.BlockSpec((1, 1, D_MODEL, fc), lambda g, f, *_: (layer, prev(g), 0, f)),
                  pl.BlockSpec((1, 1, D_MODEL, fc), lambda g, f, *_: (layer, prev(g), 0, f)),
                  pl.BlockSpec((1, 1, fc, D_MODEL), lambda g, f, *_: (layer, prev(g), f, 0))],
        out_specs=pl.BlockSpec((cap, D_MODEL), lambda g, f, *_: (prev(g), 0)),
        scratch_shapes=[pltpu.VMEM((cap, D_MODEL), BF16), pltpu.VMEM((cap, 1), F32),
                        pltpu.VMEM((cap, D_MODEL), BF16), pltpu.VMEM((cap, 1), F32),
                        pltpu.VMEM((cap, D_MODEL), F32)])
    return pl.pallas_call(
        functools.partial(_ffn_kernel, cap=cap, nf=nf, nblk=nblk),
        out_shape=jax.ShapeDtypeStruct((N_EXPERTS * cap, D_MODEL), BF16),
        grid_spec=grid_spec,
        compiler_params=_cparams(("arbitrary", "arbitrary")),
    )(a0.reshape(-1), s1.reshape(-1), slot, aff, h2,
      w_gate, w_up, w_down)


CMB_ROWS = COMB_ROWS
CMB_WIN = LANES


def _combine_kernel(a0_ref, s1_ref, slot_ref, y_hbm, x_ref, g_ref, gate_ref, *rest, cap, nblk):
    *o_refs, win_scr, ovf_scr, f_scr, sem, osem = rest
    j = pl.program_id(0)
    E, R, Tb = N_EXPERTS, CMB_WIN, CMB_ROWS

    def window(e, start, dst, s):
        return pltpu.make_async_copy(y_hbm.at[pl.ds(e * cap + start, R)], dst, s)

    def issue(blk, slot):
        for e in range(E):
            a0 = pl.multiple_of(a0_ref[e * nblk + blk], BF16_SUB)
            window(e, a0, win_scr.at[slot, pl.ds(e * R, R)], sem.at[slot]).start()

    @pl.when(j == 0)
    def _():
        issue(0, 0)

    slot = j % 2
    pltpu.make_async_copy(y_hbm.at[pl.ds(0, E * R)], win_scr.at[slot], sem.at[slot]).wait()

    @pl.when(j + 1 < nblk)
    def _():
        issue(j + 1, 1 - slot)

    sl = slot_ref[...].astype(jnp.int32)
    q = lax.broadcasted_iota(jnp.int32, (Tb, R), 1)
    onehot = jnp.concatenate(
        [(sl[:, e:e + 1] - a0_ref[e * nblk + j] == q) for e in range(E)], axis=1).astype(F32).astype(BF16)
    f_scr[...] = _dot(onehot, win_scr[slot])

    for e in range(E):
        end = s1_ref[e * nblk + j]

        def more(c):
            return c < end

        def extra(c, e=e):
            c0 = pl.multiple_of(jnp.minimum(c, cap - R), BF16_SUB)
            cp = window(e, c0, ovf_scr, osem)
            cp.start()
            cp.wait()
            hit = (sl[:, e:e + 1] - c0 == q) & (q + c0 >= c)
            f_scr[...] += _dot(hit.astype(F32).astype(BF16), ovf_scr[...])
            return c + R

        lax.while_loop(more, extra, a0_ref[e * nblk + j] + R)

    res = x_ref[...] + gate_ref[0] * _rms(f_scr[...], g_ref[...])
    if len(o_refs) == 1:
        o_refs[0][...] = res
    else:
        @pl.when(j < NS // Tb)
        def _():
            o_refs[0][...] = res

        @pl.when(j >= NS // Tb)
        def _():
            o_refs[1][...] = res


def _combine(a0, s1, slot_t, y, x, g, mod, layer, split_out=False):
    cap = y.shape[0] // N_EXPERTS
    nblk = NT // CMB_ROWS
    if split_out:
        out_shape = [jax.ShapeDtypeStruct((NS, D_MODEL), F32), jax.ShapeDtypeStruct((NP, D_MODEL), F32)]
        out_specs = _stream_specs(D_MODEL, tile=CMB_ROWS)
    else:
        out_shape = jax.ShapeDtypeStruct((NT, D_MODEL), F32)
        out_specs = pl.BlockSpec((CMB_ROWS, D_MODEL), lambda i, *_: (i, 0))
    grid_spec = pltpu.PrefetchScalarGridSpec(
        num_scalar_prefetch=2,
        grid=(nblk,),
        in_specs=[pl.BlockSpec((CMB_ROWS, LANES), lambda i, *_: (i, 0)),
                  pl.BlockSpec(memory_space=pl.ANY),
                  pl.BlockSpec((CMB_ROWS, D_MODEL), lambda i, *_: (i, 0)),
                  pl.BlockSpec((1, D_MODEL), lambda i, *_: (0, 0)),
                  _mod_spec(layer, 5, CMB_ROWS)],
        out_specs=out_specs,
        scratch_shapes=[pltpu.VMEM((2, N_EXPERTS * CMB_WIN, D_MODEL), BF16),
                        pltpu.VMEM((CMB_WIN, D_MODEL), BF16),
                        pltpu.VMEM((CMB_ROWS, D_MODEL), F32),
                        pltpu.SemaphoreType.DMA((2,)), pltpu.SemaphoreType.DMA(())])
    return pl.pallas_call(
        functools.partial(_combine_kernel, cap=cap, nblk=nblk),
        out_shape=out_shape,
        grid_spec=grid_spec,
        compiler_params=_cparams(("arbitrary",)),
    )(a0.reshape(-1), s1.reshape(-1), slot_t, y, x, g.reshape(1, D_MODEL), mod)


CUM_BLK = 256
SCALE_STEP = 2.0 ** -16
SCALE_ITERS = 10
BISECT_ITERS = 64


def _prefix_count(mask):
    n = mask.shape[1]
    tri = (lax.broadcasted_iota(jnp.int32, (CUM_BLK, CUM_BLK), 0)
           <= lax.broadcasted_iota(jnp.int32, (CUM_BLK, CUM_BLK), 1)).astype(F32).astype(BF16)
    carry = jnp.zeros((mask.shape[0], 1), F32)
    outs = []
    for b in range(n // CUM_BLK):
        c = _dot(mask[:, b * CUM_BLK:(b + 1) * CUM_BLK].astype(F32).astype(BF16), tri) + carry
        outs.append(c)
        carry = c[:, CUM_BLK - 1:CUM_BLK]
    return jnp.concatenate(outs, axis=1)


def _route_kernel(lg_ref, aff_ref, slot_ref, slott_ref, s0_ref):
    x = lg_ref[...]
    ex = jnp.exp(x - jnp.max(x, axis=0, keepdims=True))
    aff = ex / jnp.sum(ex, axis=0, keepdims=True)
    aff_ref[...] = aff
    streams = ((0, NS), (NS, NP))
    caps = [float(EC_CAPACITY * n // N_EXPERTS) for _, n in streams]
    parts = [aff[:, lo:lo + n] for lo, n in streams]

    def enough(k, thr):
        return jnp.sum((parts[k] >= thr).astype(F32), axis=1, keepdims=True) >= caps[k]

    def scale_step(_, carry):
        out = []
        for k, (lo_v, hi_v, found_f) in enumerate(carry):
            found = found_f > 0.5
            mid = hi_v * SCALE_STEP
            ge = enough(k, mid)
            out.append((jnp.where(found | ~ge, lo_v, mid), jnp.where(found | ge, hi_v, mid),
                        jnp.where(found | ge, 1.0, 0.0)))
        return tuple(out)

    def bisect_step(_, carry):
        out = []
        for k, (lo_v, hi_v) in enumerate(carry):
            mid = 0.5 * (lo_v + hi_v)
            ge = enough(k, mid)
            out.append((jnp.where(ge, mid, lo_v), jnp.where(ge, hi_v, mid)))
        return tuple(out)

    col = lambda v: jnp.full((N_EXPERTS, 1), v, F32)
    start = tuple((col(0.0), col(2.0), col(0.0)) for _ in streams)
    scaled = lax.fori_loop(0, SCALE_ITERS, scale_step, start)
    bounds = lax.fori_loop(0, BISECT_ITERS, bisect_step, tuple((lo_v, hi_v) for lo_v, hi_v, _ in scaled))
    slots = []
    off = 0.0
    for k, (lo_v, hi_v) in enumerate(bounds):
        above = parts[k] >= hi_v
        edge = (parts[k] >= lo_v) & ~above
        need = caps[k] - jnp.sum(above.astype(F32), axis=1, keepdims=True)
        edge_rank = _prefix_count(edge) - edge.astype(F32)
        sel = above | (edge & (edge_rank < need))
        slots.append(jnp.where(sel, _prefix_count(sel) - 1.0 + off, -1.0))
        off += caps[k]
    slot = jnp.concatenate(slots, axis=1)
    slot_ref[...] = slot
    pad = jnp.full((LANES - N_EXPERTS, NT), -1.0, F32)
    slott_ref[...] = jnp.concatenate([slot, pad], axis=0).T
    tok = lax.broadcasted_iota(jnp.int32, (NT, LANES), 0)
    edge = lax.broadcasted_iota(jnp.int32, (NT, LANES), 1) * CMB_ROWS
    s0_ref[...] = _dot((slot >= 0.0).astype(F32).astype(BF16), (tok < edge).astype(F32).astype(BF16))


def _route_tables(logits_t):
    return pl.pallas_call(
        _route_kernel,
        out_shape=[jax.ShapeDtypeStruct((N_EXPERTS, NT), F32), jax.ShapeDtypeStruct((N_EXPERTS, NT), F32),
                   jax.ShapeDtypeStruct((NT, LANES), F32), jax.ShapeDtypeStruct((N_EXPERTS, LANES), F32)],
        compiler_params=pltpu.CompilerParams(vmem_limit_bytes=VMEM_LIMIT),
    )(logits_t)


def _route(logits_t, h2, x, g, mod, layer, w_gate, w_up, w_down, split_out=False):
    aff, slot, slot_t, s0 = _route_tables(logits_t)
    cap = EC_CAPACITY * NT // N_EXPERTS
    s0 = s0[:, :NT // CMB_ROWS + 1].astype(jnp.int32)

    def windows(starts, win):
        return jnp.minimum(starts[:, :-1] // BF16_SUB * BF16_SUB, cap - win), starts[:, 1:]

    a0g, s1g = windows(s0[:, ::COMB_ROWS // CMB_ROWS], COMB_WIN)
    y = _expert_ffn(a0g, s1g, slot, aff, h2, layer, w_gate, w_up, w_down)
    a0c, s1c = windows(s0, CMB_WIN)
    return _combine(a0c, s1c, slot_t, y, x, g, mod, layer, split_out)


def kernel(x_prompt, x_sample, cache_b_k, cache_b_v, cache_c_k, cache_c_v, c, c_ctx, ada_w, ada_b, norm_g,
           ab_w_in, hy_short_w, hy_short_b, hy_w1, hy_b1, hy_w2, hy_b2, hy_freq, hy_w3, hy_log_decay, hy_bias,
           b_sink, ab_w_out, c_w_qkv, c_q_norm, c_k_norm, c_w_out, ec_router, ec_w_gate, ec_w_up, ec_w_down):
    x = (x_sample.reshape(NS, D_MODEL), x_prompt.reshape(NP, D_MODEL))
    cond = jnp.concatenate([c, c_ctx[None], jnp.zeros((MOD_ROWS - DEC_BATCH - 1, D_MODEL), F32)], axis=0)
    mod = _adaln(cond, ada_w, ada_b).reshape(DEPTH * MOD_ROWS * 6, 1, D_MODEL)
    rope = _rope_tables(DEC_SEQ)
    hd = HEAD_DIM

    e = 0
    p_hy, p_at = _norm_mod_matmul(x, norm_g[0, 0], mod, 0, ab_w_in[e].astype(BF16), (3 * HY_CH, D_MODEL - HY_CH + 2 * B_KV * hd))
    ya = []
    for L, nb, row0, seqs in ((DEC_SEQ, DEC_BATCH, 0, 2), (SEQ, BATCH, NS, 4)):
        table_lo, table_bf, table_t_bf = _dft_tables(L)
        h = _hyena_filters(L, hy_w1[e], hy_b1[e], hy_w2[e], hy_b2[e], hy_freq[e], hy_w3[e], hy_log_decay[e])
        h = h.at[0, HY_CH:].set(0.0)
        spec = _filter_spectrum(L, table_bf, table_lo, h)
        ya.append(_hyena(p_hy, hy_short_w[e], hy_short_b[e].reshape(1, -1), hy_bias[e].reshape(1, -1),
                         table_bf, table_t_bf, spec, L, nb, row0, seqs))
    ya = tuple(ya)
    ctx_b = (cache_b_k[:, e].reshape(DEC_BATCH, PAST_LEN, B_KV * hd), cache_b_v[:, e].reshape(DEC_BATCH, PAST_LEN, B_KV * hd))
    call, args = _attention(p_at, B_HEADS * hd, 2, L=DEC_SEQ, Tq=256, nb=DEC_BATCH, row0=0, H=B_HEADS, KVH=B_KV,
                            ctx=ctx_b, sink=b_sink[e], rope=rope, window=True)
    yb_s = call(*args)[0]
    call, args = _attention(p_at, B_HEADS * hd, 2, L=SEQ, Tq=256, nb=BATCH, row0=NS, H=B_HEADS, KVH=B_KV,
                            sink=b_sink[e], emit_k=True)
    yb_p, state_b_k, state_b_v = call(*args)
    yb = (yb_s, yb_p)
    x, h2, lg = _out_proj(ya, 0, yb, 0, ab_w_out[e].astype(BF16), x, norm_g[0, 1], norm_g[0, 2], mod, 0, ec_router[0].T)
    x = _route(lg, h2, x, norm_g[0, 3], mod, 0, ec_w_gate, ec_w_up, ec_w_down)

    o = 0
    (p,) = _norm_mod_matmul(x, norm_g[1, 0], mod, 1, c_w_qkv[o].astype(BF16), ((C_HEADS + 2 * C_KV) * hd,))
    ctx_c = (cache_c_k[:, o].reshape(DEC_BATCH, PAST_LEN, C_KV * hd), cache_c_v[:, o].reshape(DEC_BATCH, PAST_LEN, C_KV * hd))
    nrm = (c_q_norm[o], c_k_norm[o])
    call, args = _attention(p, C_HEADS * hd, 2, L=DEC_SEQ, Tq=256, nb=DEC_BATCH, row0=0, H=C_HEADS, KVH=C_KV,
                            ctx=ctx_c, norm=nrm, rope=rope)
    yc_s = call(*args)[0]
    call, args = _attention(p, C_HEADS * hd, 2, L=SEQ, Tq=256, nb=BATCH, row0=NS, H=C_HEADS, KVH=C_KV,
                            norm=nrm, emit_k=True)
    yc_p, state_c_k, state_c_v = call(*args)
    yc = (yc_s, yc_p)
    x, h2, lg = _out_proj(yc, 0, yc, 1, c_w_out[o].astype(BF16), x, norm_g[1, 1], norm_g[1, 2], mod, 1, ec_router[1].T)
    xs, xp = _route(lg, h2, x, norm_g[1, 3], mod, 1, ec_w_gate, ec_w_up, ec_w_down, split_out=True)
    y_sample = xs.reshape(DEC_BATCH, DEC_SEQ, D_MODEL)
    y_prompt = xp.reshape(BATCH, SEQ, D_MODEL)
    return (y_prompt, y_sample, state_b_k, state_b_v, state_c_k, state_c_v)
```

```python
import functools
import math

import jax
import jax.numpy as jnp
from jax import lax
from jax.experimental import pallas as pl
from jax.experimental.pallas import tpu as pltpu

F32 = jnp.float32
BF16 = jnp.bfloat16

D_MODEL = 1024
BATCH = 16
SEQ = 256
DEPTH = 2
DEC_BATCH = 8
DEC_SEQ = 1024
PAST_LEN = 512
GRID_W = 64
HEAD_DIM = 64
HY_CH = D_MODEL // 2
HY_BANDS = 8
B_HEADS = 8
B_KV = 2
WINDOW = 128
C_HEADS = 16
C_KV = 4
ROPE_BASE = 10000.0
N_EXPERTS = 16
EC_CAPACITY = 2
EXPERT_FF = 2 * D_MODEL
EPS = 1e-6
NEG_INF = -1e30

NS = DEC_BATCH * DEC_SEQ
NP = BATCH * SEQ
NT = NS + NP
MOD_ROWS = 16
LANES = 128
VMEM_LIMIT = 56 * 1024 * 1024


def _cparams(sem):
    return pltpu.CompilerParams(dimension_semantics=sem, vmem_limit_bytes=VMEM_LIMIT)


def _split(x):
    hi = x.astype(BF16)
    lo = (x - hi.astype(F32)).astype(BF16)
    return hi, lo


_NN = (((1,), (0,)), ((), ()))
_NT = (((1,), (1,)), ((), ()))


def _dot(a, b, dims=_NN):
    return lax.dot_general(a, b, dims, preferred_element_type=F32)


def _dot3(a, b, dims=_NN):
    ah, al = _split(a)
    bh, bl = _split(b)
    return _dot(ah, bh, dims) + _dot(al, bh, dims) + _dot(ah, bl, dims)


def _rms(x, g):
    return x * lax.rsqrt(jnp.mean(x * x, axis=-1, keepdims=True) + EPS) * g


def _mod_row(tile_rows):
    def f(i):
        return jnp.minimum(i * tile_rows // DEC_SEQ, DEC_BATCH)
    return f


def _adaln_kernel(c_ref, w_ref, b_ref, o_ref):
    s = jax.nn.silu(c_ref[...])
    o_ref[0] = _dot3(s, w_ref[0]) + b_ref[0]


def _adaln(cond, ada_w, ada_b):
    tn = 1536
    return pl.pallas_call(
        _adaln_kernel,
        out_shape=jax.ShapeDtypeStruct((DEPTH, MOD_ROWS, 6 * D_MODEL), F32),
        grid=(DEPTH, 6 * D_MODEL // tn),
        in_specs=[pl.BlockSpec((MOD_ROWS, D_MODEL), lambda l, j: (0, 0)),
                  pl.BlockSpec((1, D_MODEL, tn), lambda l, j: (l, 0, j)),
                  pl.BlockSpec((1, 1, tn), lambda l, j: (l, 0, j))],
        out_specs=pl.BlockSpec((1, MOD_ROWS, tn), lambda l, j: (l, 0, j)),
        compiler_params=_cparams(("arbitrary", "arbitrary")),
    )(cond, ada_w, ada_b.reshape(DEPTH, 1, 6 * D_MODEL))


def _mod_spec(layer, which, tile_rows):
    row = _mod_row(tile_rows)
    return pl.BlockSpec((1, 1, D_MODEL), lambda i, *_: ((layer * MOD_ROWS + row(i)) * 6 + which, 0, 0))


ROW_TILE = 512


def _stream_specs(width, col=0, tile=ROW_TILE):
    ns = NS // tile
    return [pl.BlockSpec((tile, width), lambda i, *_: (jnp.minimum(i, ns - 1), col)),
            pl.BlockSpec((tile, width), lambda i, *_: (jnp.maximum(i - ns, 0), col))]


def _row_specs(a, width, col=0, tile=ROW_TILE):
    if isinstance(a, tuple):
        return _stream_specs(width, col, tile), list(a)
    return [pl.BlockSpec((tile, width), lambda i, *_: (i, col))], [a]


ROW_CHUNK = 128


def _take_rows(refs, paired, rows, tile=ROW_TILE):
    if paired:
        return jnp.where(pl.program_id(0) < NS // tile, refs[0][rows, :], refs[1][rows, :]), refs[2:]
    return refs[0][rows, :], refs[1:]


NMM_TILE = 1024


def _nmm_kernel(*refs, paired):
    for c in range(NMM_TILE // ROW_CHUNK):
        rows = slice(c * ROW_CHUNK, (c + 1) * ROW_CHUNK)
        x, (g_ref, sh_ref, sc_ref, w_ref, *o_refs) = _take_rows(refs, paired, rows, NMM_TILE)
        h = _rms(x, g_ref[...]) * (1.0 + sc_ref[0]) + sh_ref[0]
        p = _dot(h.astype(BF16), w_ref[...])
        off = 0
        for o in o_refs:
            n = o.shape[1]
            o[rows, :] = p[:, off:off + n]
            off += n


def _norm_mod_matmul(x, g, mod, layer, w_bf, splits):
    tm = NMM_TILE
    n_out = w_bf.shape[1]
    x_specs, x_args = _row_specs(x, D_MODEL, tile=tm)
    return pl.pallas_call(
        functools.partial(_nmm_kernel, paired=isinstance(x, tuple)),
        out_shape=[jax.ShapeDtypeStruct((NT, n), F32) for n in splits],
        grid=(NT // tm,),
        in_specs=x_specs + [pl.BlockSpec((1, D_MODEL), lambda i: (0, 0)),
                            _mod_spec(layer, 0, tm), _mod_spec(layer, 1, tm),
                            pl.BlockSpec((D_MODEL, n_out), lambda i: (0, 0))],
        out_specs=[pl.BlockSpec((tm, n), lambda i: (i, 0)) for n in splits],
        compiler_params=_cparams(("arbitrary",)),
    )(*x_args, g.reshape(1, D_MODEL), mod, mod, w_bf)


def _hyena_filters(L, w1, b1, w2, b2, freq, w3, log_decay):
    hp = lax.Precision.HIGHEST
    t = jnp.linspace(0.0, 1.0, L, dtype=F32)[:, None]
    w = (2.0 * math.pi / L) * jnp.arange(L, dtype=F32)[:, None]
    bands = jnp.linspace(1e-4, HY_BANDS - 1, HY_BANDS, dtype=F32)[None, :]
    z = jnp.concatenate([t, jnp.cos(bands * w), -jnp.sin(bands * w)], axis=-1)
    h = jnp.sin(freq[0] * (jnp.dot(z, w1, precision=hp) + b1))
    h = jnp.sin(freq[1] * (jnp.dot(h, w2, precision=hp) + b2))
    return jnp.dot(h, w3, precision=hp) * jnp.exp(-t * jnp.exp(log_decay))


def _dft_kernel(fl_ref, fb_ref, gb_ref, *, L):
    n = 2 * L
    w = 2.0 * math.pi / n
    r = lax.broadcasted_iota(jnp.int32, (L, LANES), 0)
    b = lax.broadcasted_iota(jnp.int32, (L, LANES), 1)
    r1 = lax.broadcasted_iota(jnp.int32, (L, 1), 0)
    ang_b = ((r * b) & (n - 1)).astype(F32) * w
    cb = jnp.cos(ang_b)
    sb = jnp.sin(ang_b)
    by_col = jnp.where((b & 1) == 0, 1.0, -1.0)
    by_row = jnp.where((r & 1) == 0, 1.0, -1.0)
    for a in range(L // LANES):
        ang_a = ((r1 * (a * LANES)) & (n - 1)).astype(F32) * w
        ca = jnp.cos(ang_a)
        sa = jnp.sin(ang_a)
        cs = ca * cb - sa * sb
        sn = -(sa * cb + ca * sb)
        cols = slice(a * LANES, (a + 1) * LANES)
        fs = jnp.where(r == 0, by_col, sn)
        cs_hi, cs_lo = _split(cs)
        fs_hi, fs_lo = _split(fs)
        fb_ref[0:L, cols] = cs_hi
        fb_ref[L:2 * L, cols] = fs_hi
        fl_ref[0:L, cols] = cs_lo
        fl_ref[L:2 * L, cols] = fs_lo
        gb_ref[:, cols] = cs_hi
        gs = jnp.where(b == 0, by_row, sn) if a == 0 else sn
        gb_ref[:, L + a * LANES:L + (a + 1) * LANES] = gs.astype(BF16)


def _dft_tables(L):
    assert L & (L - 1) == 0
    return pl.pallas_call(
        functools.partial(_dft_kernel, L=L),
        out_shape=[jax.ShapeDtypeStruct((2 * L, L), BF16), jax.ShapeDtypeStruct((2 * L, L), BF16),
                   jax.ShapeDtypeStruct((L, 2 * L), BF16)],
        compiler_params=pltpu.CompilerParams(vmem_limit_bytes=VMEM_LIMIT),
    )()


def _filt_kernel(fb_ref, fl_ref, hf_ref, hb_ref, p_ref, q_ref, s_ref, *, L):
    hf = hf_ref[...]
    hb = hb_ref[...]
    ct = hf.shape[1]
    nrm = lax.rsqrt(jnp.sum(hf * hf, axis=0, keepdims=True) + jnp.sum(hb * hb, axis=0, keepdims=True) + EPS)
    hh, hl = _split(jnp.concatenate([hf, hb], axis=1))
    fb = fb_ref[...]
    fh = _dot(fb, hh) + _dot(fl_ref[...], hh) + _dot(fb, hl)
    row = lax.broadcasted_iota(jnp.int32, (L, ct), 0)
    w = jnp.where(row == 0, 1.0 / (2 * L), 2.0 / (2 * L)) * nrm
    re = fh[:L, :ct] + fh[:L, ct:]
    p_ref[...] = re * w
    q_ref[...] = jnp.where(row == 0, 0.0, fh[L:, :ct] - fh[L:, ct:]) * w
    s_ref[...] = jnp.where(row == 0, fh[L:, :ct] + fh[L:, ct:], re) * w


def _filter_spectrum(L, table_hi, table_lo, h):
    ct = 128
    nc = HY_CH // ct
    shp = jax.ShapeDtypeStruct((L, HY_CH), F32)
    return pl.pallas_call(
        functools.partial(_filt_kernel, L=L),
        out_shape=[shp, shp, shp],
        grid=(nc,),
        in_specs=[pl.BlockSpec((2 * L, L), lambda c: (0, 0)),
                  pl.BlockSpec((2 * L, L), lambda c: (0, 0)),
                  pl.BlockSpec((L, ct), lambda c: (0, c)),
                  pl.BlockSpec((L, ct), lambda c: (0, nc + c))],
        out_specs=[pl.BlockSpec((L, ct), lambda c: (0, c))] * 3,
        compiler_params=_cparams(("arbitrary",)),
    )(table_hi, table_lo, h, h)


def _hyena_kernel(x0_ref, x1_ref, v_ref, w0_ref, w1_ref, wv_ref, b0_ref, b1_ref, bv_ref, bias_ref,
                  f_ref, g_ref, p_ref, q_ref, s_ref, o_ref, *, L, seqs):
    ct = x0_ref.shape[1]
    row = lax.broadcasted_iota(jnp.int32, (L, ct), 0)
    for n in range(seqs):
        rows = slice(n * L, (n + 1) * L)

        def sconv(x_ref, w_ref, b_ref, rows=rows):
            x = x_ref[rows, :]
            xm = jnp.where(row == 0, 0.0, pltpu.roll(x, 1, 0))
            xp = jnp.where(row == L - 1, 0.0, pltpu.roll(x, L - 1, 0))
            return xm * w_ref[0:1, :] + x * w_ref[1:2, :] + xp * w_ref[2:3, :] + b_ref[...]

        z = sconv(v_ref, wv_ref, bv_ref) * sconv(x1_ref, w1_ref, b1_ref)
        zf = _dot(f_ref[...], z.astype(BF16))
        a = zf[:L]
        b = zf[L:]
        p = p_ref[...]
        q = q_ref[...]
        yre = a * p - b * q
        yim = a * q + b * s_ref[...]
        ycat = jnp.concatenate([yre, yim], axis=0).astype(BF16)
        y = _dot(g_ref[...], ycat) + z * bias_ref[...]
        o_ref[rows, :] = sconv(x0_ref, w0_ref, b0_ref) * y


def _hyena(p_hy, short_w, short_b, bias, table_bf, table_t_bf, spec, L, nb, row0, seqs):
    ct = 256
    nc = HY_CH // ct
    P, Q, S = spec
    blk = seqs * L

    def xs(k):
        return pl.BlockSpec((blk, ct), lambda b, c: (row0 // blk + b, k * nc + c))

    def ws(rows, k):
        return pl.BlockSpec((rows, ct), lambda b, c: (0, k * nc + c))

    cs = pl.BlockSpec((L, ct), lambda b, c: (0, c))
    return pl.pallas_call(
        functools.partial(_hyena_kernel, L=L, seqs=seqs),
        out_shape=jax.ShapeDtypeStruct((nb * L, HY_CH), F32),
        grid=(nb // seqs, nc),
        in_specs=[xs(0), xs(1), xs(2), ws(3, 0), ws(3, 1), ws(3, 2), ws(1, 0), ws(1, 1), ws(1, 2), ws(1, 0),
                  pl.BlockSpec((2 * L, L), lambda b, c: (0, 0)),
                  pl.BlockSpec((L, 2 * L), lambda b, c: (0, 0)),
                  cs, cs, cs],
        out_specs=pl.BlockSpec((blk, ct), lambda b, c: (b, c)),
        compiler_params=_cparams(("arbitrary", "arbitrary")),
    )(p_hy, p_hy, p_hy, short_w, short_w, short_w, short_b, short_b, short_b, bias,
      table_bf, table_t_bf, P, Q, S)


def _rope_tables(L):
    half = HEAD_DIM // 2
    pos = jnp.arange(L)
    r = (pos // GRID_W).astype(F32)
    col = (pos % GRID_W).astype(F32)
    inv = ROPE_BASE ** (-jnp.arange(0, half, 2, dtype=F32) / half)
    ar = r[:, None] * inv[None, :]
    ac = col[:, None] * inv[None, :]
    z = jnp.zeros_like(ar)
    cos = jnp.concatenate([jnp.cos(ar), jnp.cos(ar), jnp.cos(ac), jnp.cos(ac)], axis=1)
    s1 = jnp.concatenate([-jnp.sin(ar), z, -jnp.sin(ac), z], axis=1)
    s2 = jnp.concatenate([z, jnp.sin(ar), z, jnp.sin(ac)], axis=1)
    rep = LANES // HEAD_DIM
    return jnp.tile(cos, (1, rep)), jnp.tile(s1, (1, rep)), jnp.tile(s2, (1, rep))


def _attn_kernel(*refs, L, Tq, H, KVH, Lc, window, use_sink, use_norm, use_rope, emit_k):
    G = H // KVH
    hd = HEAD_DIM
    per = LANES // hd
    pad = WINDOW if window else 0
    it = iter(refs)
    q_ref = next(it)
    kv_ref = next(it)
    ck_ref = cv_ref = sink_ref = qn_ref = kn_ref = bd_ref = rc_ref = rs1_ref = rs2_ref = ko_ref = None
    if Lc:
        ck_ref = next(it)
        cv_ref = next(it)
    if use_sink:
        sink_ref = next(it)
    if use_norm:
        qn_ref = next(it)
        kn_ref = next(it)
        bd_ref = next(it)
    if use_rope:
        rc_ref = next(it)
        rs1_ref = next(it)
        rs2_ref = next(it)
    o_ref = next(it)
    if emit_k:
        ko_ref = next(it)
    k_scr = next(it)
    v_scr = next(it)
    i = pl.program_id(1)
    lat0 = Lc + pad

    def head_norm(x, g_ref):
        hi, lo = _split(x * x)
        ms = _dot(hi, bd_ref[...]) + _dot(lo, bd_ref[...])
        return x * lax.rsqrt(ms + EPS) * g_ref[...]

    def rope(x, rows):
        return (x * rc_ref[rows, :] + pltpu.roll(x, LANES - 16, 1) * rs1_ref[rows, :]
                + pltpu.roll(x, 16, 1) * rs2_ref[rows, :])

    def put(scr, g, rows, x):
        scr[g, 0, rows, :] = x.astype(BF16)
        scr[g, 1, rows, :] = pltpu.roll(x, hd, 1).astype(BF16)

    @pl.when(i == 0)
    def _():
        n_kg = KVH // per
        if window:
            zpad = jnp.zeros((n_kg, per, pad, LANES), BF16)
            for scr in (k_scr, v_scr):
                scr[:, :, Lc:Lc + pad, :] = zpad
                scr[:, :, lat0 + L:lat0 + L + pad, :] = zpad
        for g in range(n_kg):
            cols = slice(LANES * g, LANES * (g + 1))
            kg = kv_ref[:, cols]
            if use_norm:
                kg = head_norm(kg, kn_ref)
                if emit_k:
                    ko_ref[:, cols] = kg
            if use_rope:
                kg = rope(kg, slice(None))
            put(k_scr, g, slice(lat0, lat0 + L), kg)
            put(v_scr, g, slice(lat0, lat0 + L), kv_ref[:, KVH * hd + LANES * g:KVH * hd + LANES * (g + 1)])
            if Lc:
                put(k_scr, g, slice(0, Lc), ck_ref[0][:, cols])
                put(v_scr, g, slice(0, Lc), cv_ref[0][:, cols])

    rows = pl.ds(pl.multiple_of(i * Tq, Tq), Tq)
    low = lax.broadcasted_iota(jnp.int32, (Tq, LANES), 1) < hd
    groups_per_kv = G // per
    hrow = lax.broadcasted_iota(jnp.int32, (groups_per_kv * Tq, 1), 0) // Tq
    for kvh in range(KVH):
        qgs = []
        for g in range(kvh * groups_per_kv, (kvh + 1) * groups_per_kv):
            qg = q_ref[:, LANES * g:LANES * (g + 1)]
            if use_norm:
                qg = head_norm(qg, qn_ref)
            if use_rope:
                qg = rope(qg, rows)
            qgs.append((qg * (hd ** -0.5)).astype(BF16))
        outs = []
        for half in range(per):
            keep = low if half == 0 else ~low
            qs = jnp.concatenate([jnp.where(keep, qg, jnp.zeros_like(qg)) for qg in qgs], axis=0)
            var = (kvh % per) ^ half
            kg = kvh // per
            if use_sink:
                sk = jnp.zeros((groups_per_kv * Tq, 1), F32)
                for n in range(groups_per_kv):
                    sk = jnp.where(hrow == n, sink_ref[0, (kvh * groups_per_kv + n) * per + half], sk)
            if window:
                W = Tq + 2 * pad
                wrows = pl.ds(pl.multiple_of(Lc + i * Tq, Tq), W)
                s_c = _dot(qs, k_scr[kg, var, 0:Lc, :], _NT)
                s_w = _dot(qs, k_scr[kg, var, wrows, :], _NT).reshape(groups_per_kv, Tq, W)
                r = lax.broadcasted_iota(jnp.int32, (Tq, W), 0)
                c = lax.broadcasted_iota(jnp.int32, (Tq, W), 1)
                kpos = i * Tq - pad + c
                valid = (kpos >= 0) & (kpos < L) & (jnp.abs(r + pad - c) <= WINDOW)
                s_w = jnp.where(valid[None], s_w, NEG_INF).reshape(groups_per_kv * Tq, W)
                m = jnp.maximum(jnp.max(s_c, axis=-1, keepdims=True), jnp.max(s_w, axis=-1, keepdims=True))
                if use_sink:
                    m = jnp.maximum(m, sk)
                e_c = jnp.exp(s_c - m)
                e_w = jnp.exp(s_w - m)
                den = jnp.sum(e_c, axis=-1, keepdims=True) + jnp.sum(e_w, axis=-1, keepdims=True)
                o = (_dot(e_c.astype(BF16), v_scr[kg, var, 0:Lc, :])
                     + _dot(e_w.astype(BF16), v_scr[kg, var, wrows, :]))
            else:
                s = _dot(qs, k_scr[kg, var], _NT)
                m = jnp.max(s, axis=-1, keepdims=True)
                if use_sink:
                    m = jnp.maximum(m, sk)
                e = jnp.exp(s - m)
                den = jnp.sum(e, axis=-1, keepdims=True)
                o = _dot(e.astype(BF16), v_scr[kg, var])
            if use_sink:
                den = den + jnp.exp(sk - m)
            outs.append(o / den)
        for n in range(groups_per_kv):
            g = kvh * groups_per_kv + n
            o_ref[:, LANES * g:LANES * (g + 1)] = jnp.where(low, outs[0][n * Tq:(n + 1) * Tq],
                                                            outs[1][n * Tq:(n + 1) * Tq])


def _attention(p, q_cols, kv_colblk, *, L, Tq, nb, row0, H, KVH, ctx=None, sink=None, norm=None, rope=None,
               window=False, emit_k=False):
    hd = HEAD_DIM
    nq = L // Tq
    Lc = 0 if ctx is None else ctx[0].shape[1]
    pad = WINDOW if window else 0
    Lk = Lc + L + 2 * pad
    kvw = 2 * KVH * hd
    args = [p, p]
    in_specs = [pl.BlockSpec((Tq, q_cols), lambda b, i: (row0 // Tq + b * nq + i, 0)),
                pl.BlockSpec((L, kvw), lambda b, i: (row0 // L + b, kv_colblk))]
    if ctx is not None:
        for a in ctx:
            args.append(a)
            in_specs.append(pl.BlockSpec((1, Lc, KVH * hd), lambda b, i: (b, 0, 0)))
    if sink is not None:
        args.append(sink.reshape(1, H))
        in_specs.append(pl.BlockSpec(memory_space=pltpu.SMEM))
    if norm is not None:
        bd = (jnp.arange(LANES)[:, None] // hd == jnp.arange(LANES)[None, :] // hd).astype(F32) / hd
        for a in (jnp.tile(norm[0], LANES // hd).reshape(1, LANES), jnp.tile(norm[1], LANES // hd).reshape(1, LANES)):
            args.append(a)
            in_specs.append(pl.BlockSpec((1, LANES), lambda b, i: (0, 0)))
        args.append(bd.astype(BF16))
        in_specs.append(pl.BlockSpec((LANES, LANES), lambda b, i: (0, 0)))
    if rope is not None:
        for a in rope:
            args.append(a)
            in_specs.append(pl.BlockSpec((L, LANES), lambda b, i: (0, 0)))
    out_shape = [jax.ShapeDtypeStruct((nb * L, H * hd), F32)]
    out_specs = [pl.BlockSpec((Tq, H * hd), lambda b, i: (b * nq + i, 0))]
    if emit_k:
        out_shape.append(jax.ShapeDtypeStruct((nb * L, KVH * hd), F32))
        out_specs.append(pl.BlockSpec((L, KVH * hd), lambda b, i: (b, 0)))
    kern = functools.partial(_attn_kernel, L=L, Tq=Tq, H=H, KVH=KVH, Lc=Lc, window=window,
                             use_sink=sink is not None, use_norm=norm is not None,
                             use_rope=rope is not None, emit_k=emit_k)
    call = pl.pallas_call(
        kern, out_shape=out_shape, grid=(nb, nq), in_specs=in_specs, out_specs=out_specs,
        scratch_shapes=[pltpu.VMEM((KVH * hd // LANES, LANES // hd, Lk, LANES), BF16),
                        pltpu.VMEM((KVH * hd // LANES, LANES // hd, Lk, LANES), BF16)],
        compiler_params=_cparams(("arbitrary", "arbitrary")))
    return call, args


def _oproj_kernel(*refs, paired):
    rows = slice(None)
    al, rest = _take_rows(refs, paired[0], rows)
    ar, rest = _take_rows(rest, paired[1], rows)
    x, rest = _take_rows(rest, paired[2], rows)
    w_ref, g1_ref, gate_ref, g2_ref, sh_ref, sc_ref, wr_ref, x1_ref, h2_ref, lg_ref = rest
    k = al.shape[1]
    y = _dot(al.astype(BF16), w_ref[0:k, :]) + _dot(ar.astype(BF16), w_ref[k:2 * k, :])
    x1 = x + gate_ref[0] * _rms(y, g1_ref[...])
    x1_ref[...] = x1
    h2 = _rms(x1, g2_ref[...]) * (1.0 + sc_ref[0]) + sh_ref[0]
    wh, wl = _split(wr_ref[...])
    hb = h2.astype(BF16)
    h2_ref[...] = hb
    lg_ref[...] = _dot(wh, hb, _NT) + _dot(wl, hb, _NT)


def _out_proj(a_l, l_blk, a_r, r_blk, w_bf, x, g1, g2, mod, layer, wr_t):
    tm = ROW_TILE
    half = D_MODEL // 2
    l_specs, l_args = _row_specs(a_l, half, l_blk)
    r_specs, r_args = _row_specs(a_r, half, r_blk)
    x_specs, x_args = _row_specs(x, D_MODEL)
    paired = tuple(isinstance(a, tuple) for a in (a_l, a_r, x))
    return pl.pallas_call(
        functools.partial(_oproj_kernel, paired=paired),
        out_shape=[jax.ShapeDtypeStruct((NT, D_MODEL), F32), jax.ShapeDtypeStruct((NT, D_MODEL), BF16),
                   jax.ShapeDtypeStruct((N_EXPERTS, NT), F32)],
        grid=(NT // tm,),
        in_specs=l_specs + r_specs + x_specs + [
            pl.BlockSpec((D_MODEL, D_MODEL), lambda i: (0, 0)),
            pl.BlockSpec((1, D_MODEL), lambda i: (0, 0)),
            _mod_spec(layer, 2, tm),
            pl.BlockSpec((1, D_MODEL), lambda i: (0, 0)),
            _mod_spec(layer, 3, tm), _mod_spec(layer, 4, tm),
            pl.BlockSpec((N_EXPERTS, D_MODEL), lambda i: (0, 0))],
        out_specs=[pl.BlockSpec((tm, D_MODEL), lambda i: (i, 0)),
                   pl.BlockSpec((tm, D_MODEL), lambda i: (i, 0)),
                   pl.BlockSpec((N_EXPERTS, tm), lambda i: (0, i))],
        compiler_params=_cparams(("arbitrary",)),
    )(*l_args, *r_args, *x_args, w_bf, g1.reshape(1, D_MODEL), mod, g2.reshape(1, D_MODEL), mod, mod, wr_t)


COMB_ROWS = 512
COMB_WIN = 128
BF16_SUB = 16
F32_SUB = 8
FFN_PARTS = 2


def _ffn_kernel(a0_ref, s1_ref, slot_ref, aff_ref, h_ref, wg_ref, wu_ref, wd_ref, y_ref,
                xn_scr, gn_scr, xc_scr, gc_scr, acc_scr, *, cap, nf, nblk):
    grp = pl.program_id(0)
    f = pl.program_id(1)
    R = COMB_WIN
    per_step = nblk // nf
    erow = jnp.minimum(grp, N_EXPERTS - 1) % F32_SUB

    def add_window(jb, c0, hit_of):
        cols = slice(jb * COMB_ROWS, (jb + 1) * COMB_ROWS)
        q = lax.broadcasted_iota(jnp.int32, (R, COMB_ROWS), 0)
        hit = hit_of(slot_ref[pl.ds(erow, 1), cols].astype(jnp.int32), q)
        rows = pl.ds(c0, R)
        piece = _dot(hit.astype(F32).astype(BF16), h_ref[cols, :])
        xn_scr[rows, :] += piece.astype(BF16)
        gn_scr[rows, :] += jnp.sum(jnp.where(hit, aff_ref[pl.ds(erow, 1), cols], 0.0), axis=1, keepdims=True)

    def window_start(jb):
        return pl.multiple_of(a0_ref[grp * nblk + f * per_step + jb], BF16_SUB)

    def gather():
        for jb in range(per_step):
            a0 = window_start(jb)
            add_window(jb, a0, lambda srow, q, a0=a0: srow - a0 == q)

    def gather_more():
        for jb in range(per_step):
            end = s1_ref[grp * nblk + f * per_step + jb]

            def extra(c, jb=jb):
                c0 = pl.multiple_of(jnp.minimum(c, cap - R), BF16_SUB)
                add_window(jb, c0, lambda srow, q: (srow - c0 == q) & (q + c0 >= c))
                return c + R

            lax.while_loop(lambda c, end=end: c < end, extra, window_start(jb) + R)

    def compute(first):
        wg = wg_ref[0, 0].astype(BF16)
        wu = wu_ref[0, 0].astype(BF16)
        wd = wd_ref[0, 0].astype(BF16)
        part_rows = cap // FFN_PARTS
        for r in range(FFN_PARTS):
            rows = slice(r * part_rows, (r + 1) * part_rows)
            xe = xc_scr[rows, :]
            hmid = (jax.nn.silu(_dot(xe, wg)) * _dot(xe, wu)).astype(BF16)
            part = _dot(hmid, wd)
            if first:
                acc_scr[rows, :] = part
            else:
                acc_scr[rows, :] += part

    @pl.when(f == 0)
    def _():
        xn_scr[...] = jnp.zeros_like(xn_scr)
        gn_scr[...] = jnp.zeros_like(gn_scr)

    @pl.when(grp == 0)
    def _():
        gather()

    for first in (True, False):
        first_step = (f == 0) if first else (f != 0)

        @pl.when((grp > 0) & (grp < N_EXPERTS) & first_step)
        def _():
            gather()
            compute(first)

        @pl.when((grp == N_EXPERTS) & first_step)
        def _():
            compute(first)

    @pl.when(grp < N_EXPERTS)
    def _():
        gather_more()

    @pl.when(f == nf - 1)
    def _():
        @pl.when(grp > 0)
        def _():
            y_ref[...] = (acc_scr[...] * gc_scr[...]).astype(BF16)

        xc_scr[...] = xn_scr[...]
        gc_scr[...] = gn_scr[...]


def _expert_ffn(a0, s1, slot, aff, h2, layer, w_gate, w_up, w_down):
    cap = EC_CAPACITY * NT // N_EXPERTS
    fc = 512
    nf = EXPERT_FF // fc
    nblk = NT // COMB_ROWS
    tok = NT // nf

    def prev(g):
        return jnp.maximum(g - 1, 0)

    def this(g):
        return jnp.minimum(g, N_EXPERTS - 1)

    grid_spec = pltpu.PrefetchScalarGridSpec(
        num_scalar_prefetch=2,
        grid=(N_EXPERTS + 1, nf),
        in_specs=[pl.BlockSpec((F32_SUB, tok), lambda g, f, *_: (this(g) // F32_SUB, f)),
                  pl.BlockSpec((F32_SUB, tok), lambda g, f, *_: (this(g) // F32_SUB, f)),
                  pl.BlockSpec((tok, D_MODEL), lambda g, f, *_: (f, 0)),
                  pl.BlockSpec((1, 1, D_MODEL, fc), lambda g, f, *_: (layer, prev(g), 0, f)),
                  pl.BlockSpec((1, 1, D_MODEL, fc), lambda g, f, *_: (layer, prev(g), 0, f)),
                  pl.BlockSpec((1, 1, fc, D_MODEL), lambda g, f, *_: (layer, prev(g), f, 0))],
        out_specs=pl.BlockSpec((cap, D_MODEL), lambda g, f, *_: (prev(g), 0)),
        scratch_shapes=[pltpu.VMEM((cap, D_MODEL), BF16), pltpu.VMEM((cap, 1), F32),
                        pltpu.VMEM((cap, D_MODEL), BF16), pltpu.VMEM((cap, 1), F32),
                        pltpu.VMEM((cap, D_MODEL), F32)])
    return pl.pallas_call(
        functools.partial(_ffn_kernel, cap=cap, nf=nf, nblk=nblk),
        out_shape=jax.ShapeDtypeStruct((N_EXPERTS * cap, D_MODEL), BF16),
        grid_spec=grid_spec,
        compiler_params=_cparams(("arbitrary", "arbitrary")),
    )(a0.reshape(-1), s1.reshape(-1), slot, aff, h2,
      w_gate, w_up, w_down)


CMB_ROWS = COMB_ROWS
CMB_WIN = LANES


def _combine_kernel(a0_ref, s1_ref, slot_ref, y_hbm, x_ref, g_ref, gate_ref, *rest, cap, nblk):
    *o_refs, win_scr, ovf_scr, f_scr, sem, osem = rest
    j = pl.program_id(0)
    E, R, Tb = N_EXPERTS, CMB_WIN, CMB_ROWS

    def window(e, start, dst, s):
        return pltpu.make_async_copy(y_hbm.at[pl.ds(e * cap + start, R)], dst, s)

    def issue(blk, slot):
        for e in range(E):
            a0 = pl.multiple_of(a0_ref[e * nblk + blk], BF16_SUB)
            window(e, a0, win_scr.at[slot, pl.ds(e * R, R)], sem.at[slot]).start()

    @pl.when(j == 0)
    def _():
        issue(0, 0)

    slot = j % 2
    pltpu.make_async_copy(y_hbm.at[pl.ds(0, E * R)], win_scr.at[slot], sem.at[slot]).wait()

    @pl.when(j + 1 < nblk)
    def _():
        issue(j + 1, 1 - slot)

    sl = slot_ref[...].astype(jnp.int32)
    q = lax.broadcasted_iota(jnp.int32, (Tb, R), 1)
    onehot = jnp.concatenate(
        [(sl[:, e:e + 1] - a0_ref[e * nblk + j] == q) for e in range(E)], axis=1).astype(F32).astype(BF16)
    f_scr[...] = _dot(onehot, win_scr[slot])

    for e in range(E):
        end = s1_ref[e * nblk + j]

        def more(c):
            return c < end

        def extra(c, e=e):
            c0 = pl.multiple_of(jnp.minimum(c, cap - R), BF16_SUB)
            cp = window(e, c0, ovf_scr, osem)
            cp.start()
            cp.wait()
            hit = (sl[:, e:e + 1] - c0 == q) & (q + c0 >= c)
            f_scr[...] += _dot(hit.astype(F32).astype(BF16), ovf_scr[...])
            return c + R

        lax.while_loop(more, extra, a0_ref[e * nblk + j] + R)

    res = x_ref[...] + gate_ref[0] * _rms(f_scr[...], g_ref[...])
    if len(o_refs) == 1:
        o_refs[0][...] = res
    else:
        @pl.when(j < NS // Tb)
        def _():
            o_refs[0][...] = res

        @pl.when(j >= NS // Tb)
        def _():
            o_refs[1][...] = res


def _combine(a0, s1, slot_t, y, x, g, mod, layer, split_out=False):
    cap = y.shape[0] // N_EXPERTS
    nblk = NT // CMB_ROWS
    if split_out:
        out_shape = [jax.ShapeDtypeStruct((NS, D_MODEL), F32), jax.ShapeDtypeStruct((NP, D_MODEL), F32)]
        out_specs = _stream_specs(D_MODEL, tile=CMB_ROWS)
    else:
        out_shape = jax.ShapeDtypeStruct((NT, D_MODEL), F32)
        out_specs = pl.BlockSpec((CMB_ROWS, D_MODEL), lambda i, *_: (i, 0))
    grid_spec = pltpu.PrefetchScalarGridSpec(
        num_scalar_prefetch=2,
        grid=(nblk,),
        in_specs=[pl.BlockSpec((CMB_ROWS, LANES), lambda i, *_: (i, 0)),
                  pl.BlockSpec(memory_space=pl.ANY),
                  pl.BlockSpec((CMB_ROWS, D_MODEL), lambda i, *_: (i, 0)),
                  pl.BlockSpec((1, D_MODEL), lambda i, *_: (0, 0)),
                  _mod_spec(layer, 5, CMB_ROWS)],
        out_specs=out_specs,
        scratch_shapes=[pltpu.VMEM((2, N_EXPERTS * CMB_WIN, D_MODEL), BF16),
                        pltpu.VMEM((CMB_WIN, D_MODEL), BF16),
                        pltpu.VMEM((CMB_ROWS, D_MODEL), F32),
                        pltpu.SemaphoreType.DMA((2,)), pltpu.SemaphoreType.DMA(())])
    return pl.pallas_call(
        functools.partial(_combine_kernel, cap=cap, nblk=nblk),
        out_shape=out_shape,
        grid_spec=grid_spec,
        compiler_params=_cparams(("arbitrary",)),
    )(a0.reshape(-1), s1.reshape(-1), slot_t, y, x, g.reshape(1, D_MODEL), mod)


CUM_BLK = 256
SCALE_STEP = 2.0 ** -16
SCALE_ITERS = 10
BISECT_ITERS = 48


def _prefix_count(mask):
    n = mask.shape[1]
    tri = (lax.broadcasted_iota(jnp.int32, (CUM_BLK, CUM_BLK), 0)
           <= lax.broadcasted_iota(jnp.int32, (CUM_BLK, CUM_BLK), 1)).astype(F32).astype(BF16)
    carry = jnp.zeros((mask.shape[0], 1), F32)
    outs = []
    for b in range(n // CUM_BLK):
        c = _dot(mask[:, b * CUM_BLK:(b + 1) * CUM_BLK].astype(F32).astype(BF16), tri) + carry
        outs.append(c)
        carry = c[:, CUM_BLK - 1:CUM_BLK]
    return jnp.concatenate(outs, axis=1)


def _route_kernel(lg_ref, aff_ref, slot_ref, slott_ref, s0_ref):
    x = lg_ref[...]
    ex = jnp.exp(x - jnp.max(x, axis=0, keepdims=True))
    aff = ex / jnp.sum(ex, axis=0, keepdims=True)
    aff_ref[...] = aff
    streams = ((0, NS), (NS, NP))
    caps = [float(EC_CAPACITY * n // N_EXPERTS) for _, n in streams]
    parts = [aff[:, lo:lo + n] for lo, n in streams]

    def enough(k, thr):
        return jnp.sum((parts[k] >= thr).astype(F32), axis=1, keepdims=True) >= caps[k]

    def scale_step(_, carry):
        out = []
        for k, (lo_v, hi_v, found_f) in enumerate(carry):
            found = found_f > 0.5
            mid = hi_v * SCALE_STEP
            ge = enough(k, mid)
            out.append((jnp.where(found | ~ge, lo_v, mid), jnp.where(found | ge, hi_v, mid),
                        jnp.where(found | ge, 1.0, 0.0)))
        return tuple(out)

    def bisect_step(_, carry):
        out = []
        for k, (lo_v, hi_v) in enumerate(carry):
            mid = 0.5 * (lo_v + hi_v)
            ge = enough(k, mid)
            out.append((jnp.where(ge, mid, lo_v), jnp.where(ge, hi_v, mid)))
        return tuple(out)

    col = lambda v: jnp.full((N_EXPERTS, 1), v, F32)
    start = tuple((col(0.0), col(2.0), col(0.0)) for _ in streams)
    scaled = lax.fori_loop(0, SCALE_ITERS, scale_step, start)
    bounds = lax.fori_loop(0, BISECT_ITERS, bisect_step, tuple((lo_v, hi_v) for lo_v, hi_v, _ in scaled))
    slots = []
    off = 0.0
    for k, (lo_v, hi_v) in enumerate(bounds):
        above = parts[k] >= hi_v
        edge = (parts[k] >= lo_v) & ~above
        need = caps[k] - jnp.sum(above.astype(F32), axis=1, keepdims=True)
        edge_rank = _prefix_count(edge) - edge.astype(F32)
        sel = above | (edge & (edge_rank < need))
        slots.append(jnp.where(sel, _prefix_count(sel) - 1.0 + off, -1.0))
        off += caps[k]
    slot = jnp.concatenate(slots, axis=1)
    slot_ref[...] = slot
    pad = jnp.full((LANES - N_EXPERTS, NT), -1.0, F32)
    slott_ref[...] = jnp.concatenate([slot, pad], axis=0).T
    tok = lax.broadcasted_iota(jnp.int32, (NT, LANES), 0)
    edge = lax.broadcasted_iota(jnp.int32, (NT, LANES), 1) * CMB_ROWS
    s0_ref[...] = _dot((slot >= 0.0).astype(F32).astype(BF16), (tok < edge).astype(F32).astype(BF16))


def _route_tables(logits_t):
    return pl.pallas_call(
        _route_kernel,
        out_shape=[jax.ShapeDtypeStruct((N_EXPERTS, NT), F32), jax.ShapeDtypeStruct((N_EXPERTS, NT), F32),
                   jax.ShapeDtypeStruct((NT, LANES), F32), jax.ShapeDtypeStruct((N_EXPERTS, LANES), F32)],
        compiler_params=pltpu.CompilerParams(vmem_limit_bytes=VMEM_LIMIT),
    )(logits_t)


def _route(logits_t, h2, x, g, mod, layer, w_gate, w_up, w_down, split_out=False):
    aff, slot, slot_t, s0 = _route_tables(logits_t)
    cap = EC_CAPACITY * NT // N_EXPERTS
    s0 = s0[:, :NT // CMB_ROWS + 1].astype(jnp.int32)

    def windows(starts, win):
        return jnp.minimum(starts[:, :-1] // BF16_SUB * BF16_SUB, cap - win), starts[:, 1:]

    a0g, s1g = windows(s0[:, ::COMB_ROWS // CMB_ROWS], COMB_WIN)
    y = _expert_ffn(a0g, s1g, slot, aff, h2, layer, w_gate, w_up, w_down)
    a0c, s1c = windows(s0, CMB_WIN)
    return _combine(a0c, s1c, slot_t, y, x, g, mod, layer, split_out)


def kernel(x_prompt, x_sample, cache_b_k, cache_b_v, cache_c_k, cache_c_v, c, c_ctx, ada_w, ada_b, norm_g,
           ab_w_in, hy_short_w, hy_short_b, hy_w1, hy_b1, hy_w2, hy_b2, hy_freq, hy_w3, hy_log_decay, hy_bias,
           b_sink, ab_w_out, c_w_qkv, c_q_norm, c_k_norm, c_w_out, ec_router, ec_w_gate, ec_w_up, ec_w_down):
    x = (x_sample.reshape(NS, D_MODEL), x_prompt.reshape(NP, D_MODEL))
    cond = jnp.concatenate([c, c_ctx[None], jnp.zeros((MOD_ROWS - DEC_BATCH - 1, D_MODEL), F32)], axis=0)
    mod = _adaln(cond, ada_w, ada_b).reshape(DEPTH * MOD_ROWS * 6, 1, D_MODEL)
    rope = _rope_tables(DEC_SEQ)
    hd = HEAD_DIM

    e = 0
    p_hy, p_at = _norm_mod_matmul(x, norm_g[0, 0], mod, 0, ab_w_in[e].astype(BF16), (3 * HY_CH, D_MODEL - HY_CH + 2 * B_KV * hd))
    ya = []
    for L, nb, row0, seqs in ((DEC_SEQ, DEC_BATCH, 0, 2), (SEQ, BATCH, NS, 4)):
        table_lo, table_bf, table_t_bf = _dft_tables(L)
        h = _hyena_filters(L, hy_w1[e], hy_b1[e], hy_w2[e], hy_b2[e], hy_freq[e], hy_w3[e], hy_log_decay[e])
        h = h.at[0, HY_CH:].set(0.0)
        spec = _filter_spectrum(L, table_bf, table_lo, h)
        ya.append(_hyena(p_hy, hy_short_w[e], hy_short_b[e].reshape(1, -1), hy_bias[e].reshape(1, -1),
                         table_bf, table_t_bf, spec, L, nb, row0, seqs))
    ya = tuple(ya)
    ctx_b = (cache_b_k[:, e].reshape(DEC_BATCH, PAST_LEN, B_KV * hd), cache_b_v[:, e].reshape(DEC_BATCH, PAST_LEN, B_KV * hd))
    call, args = _attention(p_at, B_HEADS * hd, 2, L=DEC_SEQ, Tq=256, nb=DEC_BATCH, row0=0, H=B_HEADS, KVH=B_KV,
                            ctx=ctx_b, sink=b_sink[e], rope=rope, window=True)
    yb_s = call(*args)[0]
    call, args = _attention(p_at, B_HEADS * hd, 2, L=SEQ, Tq=256, nb=BATCH, row0=NS, H=B_HEADS, KVH=B_KV,
                            sink=b_sink[e])
    yb_p = call(*args)[0]
    yb = (yb_s, yb_p)
    state_b_k = p_at[NS:, B_HEADS * hd:(B_HEADS + B_KV) * hd].reshape(BATCH, 1, SEQ, B_KV, hd)
    state_b_v = p_at[NS:, (B_HEADS + B_KV) * hd:].reshape(BATCH, 1, SEQ, B_KV, hd)
    x, h2, lg = _out_proj(ya, 0, yb, 0, ab_w_out[e].astype(BF16), x, norm_g[0, 1], norm_g[0, 2], mod, 0, ec_router[0].T)
    x = _route(lg, h2, x, norm_g[0, 3], mod, 0, ec_w_gate, ec_w_up, ec_w_down)

    o = 0
    (p,) = _norm_mod_matmul(x, norm_g[1, 0], mod, 1, c_w_qkv[o].astype(BF16), ((C_HEADS + 2 * C_KV) * hd,))
    ctx_c = (cache_c_k[:, o].reshape(DEC_BATCH, PAST_LEN, C_KV * hd), cache_c_v[:, o].reshape(DEC_BATCH, PAST_LEN, C_KV * hd))
    nrm = (c_q_norm[o], c_k_norm[o])
    call, args = _attention(p, C_HEADS * hd, 2, L=DEC_SEQ, Tq=256, nb=DEC_BATCH, row0=0, H=C_HEADS, KVH=C_KV,
                            ctx=ctx_c, norm=nrm, rope=rope)
    yc_s = call(*args)[0]
    call, args = _attention(p, C_HEADS * hd, 2, L=SEQ, Tq=256, nb=BATCH, row0=NS, H=C_HEADS, KVH=C_KV,
                            norm=nrm, emit_k=True)
    yc_p, kn = call(*args)
    yc = (yc_s, yc_p)
    state_c_k = kn.reshape(BATCH, 1, SEQ, C_KV, hd)
    state_c_v = p[NS:, (C_HEADS + C_KV) * hd:].reshape(BATCH, 1, SEQ, C_KV, hd)
    x, h2, lg = _out_proj(yc, 0, yc, 1, c_w_out[o].astype(BF16), x, norm_g[1, 1], norm_g[1, 2], mod, 1, ec_router[1].T)
    xs, xp = _route(lg, h2, x, norm_g[1, 3], mod, 1, ec_w_gate, ec_w_up, ec_w_down, split_out=True)
    y_sample = xs.reshape(DEC_BATCH, DEC_SEQ, D_MODEL)
    y_prompt = xp.reshape(BATCH, SEQ, D_MODEL)
    return (y_prompt, y_sample, state_b_k, state_b_v, state_c_k, state_c_v)
```

```python
import functools
import math

import jax
import jax.numpy as jnp
from jax import lax
from jax.experimental import pallas as pl
from jax.experimental.pallas import tpu as pltpu

F32 = jnp.float32
BF16 = jnp.bfloat16

D_MODEL = 1024
BATCH = 16
SEQ = 256
DEPTH = 2
DEC_BATCH = 8
DEC_SEQ = 1024
PAST_LEN = 512
GRID_W = 64
HEAD_DIM = 64
HY_CH = D_MODEL // 2
HY_BANDS = 8
B_HEADS = 8
B_KV = 2
WINDOW = 128
C_HEADS = 16
C_KV = 4
ROPE_BASE = 10000.0
N_EXPERTS = 16
EC_CAPACITY = 2
EXPERT_FF = 2 * D_MODEL
EPS = 1e-6
NEG_INF = -1e30

NS = DEC_BATCH * DEC_SEQ
NP = BATCH * SEQ
NT = NS + NP
MOD_ROWS = 16
LANES = 128
VMEM_LIMIT = 56 * 1024 * 1024


def _cparams(sem):
    return pltpu.CompilerParams(dimension_semantics=sem, vmem_limit_bytes=VMEM_LIMIT)


def _split(x):
    hi = x.astype(BF16)
    lo = (x - hi.astype(F32)).astype(BF16)
    return hi, lo


_NN = (((1,), (0,)), ((), ()))
_NT = (((1,), (1,)), ((), ()))


def _dot(a, b, dims=_NN):
    return lax.dot_general(a, b, dims, preferred_element_type=F32)


def _dot3(a, b, dims=_NN):
    ah, al = _split(a)
    bh, bl = _split(b)
    return _dot(ah, bh, dims) + _dot(al, bh, dims) + _dot(ah, bl, dims)


def _rms(x, g):
    return x * lax.rsqrt(jnp.mean(x * x, axis=-1, keepdims=True) + EPS) * g


def _mod_row(tile_rows):
    def f(i):
        return jnp.minimum(i * tile_rows // DEC_SEQ, DEC_BATCH)
    return f


def _adaln_kernel(c_ref, w_ref, b_ref, o_ref):
    s = jax.nn.silu(c_ref[...])
    o_ref[0] = _dot3(s, w_ref[0]) + b_ref[0]


def _adaln(cond, ada_w, ada_b):
    tn = 1536
    return pl.pallas_call(
        _adaln_kernel,
        out_shape=jax.ShapeDtypeStruct((DEPTH, MOD_ROWS, 6 * D_MODEL), F32),
        grid=(DEPTH, 6 * D_MODEL // tn),
        in_specs=[pl.BlockSpec((MOD_ROWS, D_MODEL), lambda l, j: (0, 0)),
                  pl.BlockSpec((1, D_MODEL, tn), lambda l, j: (l, 0, j)),
                  pl.BlockSpec((1, 1, tn), lambda l, j: (l, 0, j))],
        out_specs=pl.BlockSpec((1, MOD_ROWS, tn), lambda l, j: (l, 0, j)),
        compiler_params=_cparams(("arbitrary", "arbitrary")),
    )(cond, ada_w, ada_b.reshape(DEPTH, 1, 6 * D_MODEL))


def _mod_spec(layer, which, tile_rows):
    row = _mod_row(tile_rows)
    return pl.BlockSpec((1, 1, D_MODEL), lambda i, *_: ((layer * MOD_ROWS + row(i)) * 6 + which, 0, 0))


ROW_TILE = 1024


def _stream_specs(width, col=0, tile=ROW_TILE):
    ns = NS // tile
    return [pl.BlockSpec((tile, width), lambda i, *_: (jnp.minimum(i, ns - 1), col)),
            pl.BlockSpec((tile, width), lambda i, *_: (jnp.maximum(i - ns, 0), col))]


def _row_specs(a, width, col=0, tile=ROW_TILE):
    if isinstance(a, tuple):
        return _stream_specs(width, col, tile), list(a)
    return [pl.BlockSpec((tile, width), lambda i, *_: (i, col))], [a]


ROW_CHUNK = 128


def _take_rows(refs, paired, rows, tile=ROW_TILE):
    if paired:
        return jnp.where(pl.program_id(0) < NS // tile, refs[0][rows, :], refs[1][rows, :]), refs[2:]
    return refs[0][rows, :], refs[1:]


NMM_TILE = 1024


def _nmm_kernel(*refs, paired):
    for c in range(NMM_TILE // ROW_CHUNK):
        rows = slice(c * ROW_CHUNK, (c + 1) * ROW_CHUNK)
        x, (g_ref, sh_ref, sc_ref, w_ref, *o_refs) = _take_rows(refs, paired, rows, NMM_TILE)
        h = _rms(x, g_ref[...]) * (1.0 + sc_ref[0]) + sh_ref[0]
        p = _dot(h.astype(BF16), w_ref[...])
        off = 0
        for o in o_refs:
            n = o.shape[1]
            o[rows, :] = p[:, off:off + n]
            off += n


def _norm_mod_matmul(x, g, mod, layer, w_bf, splits):
    tm = NMM_TILE
    n_out = w_bf.shape[1]
    x_specs, x_args = _row_specs(x, D_MODEL, tile=tm)
    return pl.pallas_call(
        functools.partial(_nmm_kernel, paired=isinstance(x, tuple)),
        out_shape=[jax.ShapeDtypeStruct((NT, n), F32) for n in splits],
        grid=(NT // tm,),
        in_specs=x_specs + [pl.BlockSpec((1, D_MODEL), lambda i: (0, 0)),
                            _mod_spec(layer, 0, tm), _mod_spec(layer, 1, tm),
                            pl.BlockSpec((D_MODEL, n_out), lambda i: (0, 0))],
        out_specs=[pl.BlockSpec((tm, n), lambda i: (i, 0)) for n in splits],
        compiler_params=_cparams(("arbitrary",)),
    )(*x_args, g.reshape(1, D_MODEL), mod, mod, w_bf)


def _hyena_filters(L, w1, b1, w2, b2, freq, w3, log_decay):
    hp = lax.Precision.HIGHEST
    t = jnp.linspace(0.0, 1.0, L, dtype=F32)[:, None]
    w = (2.0 * math.pi / L) * jnp.arange(L, dtype=F32)[:, None]
    bands = jnp.linspace(1e-4, HY_BANDS - 1, HY_BANDS, dtype=F32)[None, :]
    z = jnp.concatenate([t, jnp.cos(bands * w), -jnp.sin(bands * w)], axis=-1)
    h = jnp.sin(freq[0] * (jnp.dot(z, w1, precision=hp) + b1))
    h = jnp.sin(freq[1] * (jnp.dot(h, w2, precision=hp) + b2))
    return jnp.dot(h, w3, precision=hp) * jnp.exp(-t * jnp.exp(log_decay))


def _dft_kernel(fl_ref, fb_ref, gb_ref, *, L):
    n = 2 * L
    w = 2.0 * math.pi / n
    r = lax.broadcasted_iota(jnp.int32, (L, LANES), 0)
    b = lax.broadcasted_iota(jnp.int32, (L, LANES), 1)
    r1 = lax.broadcasted_iota(jnp.int32, (L, 1), 0)
    ang_b = ((r * b) & (n - 1)).astype(F32) * w
    cb = jnp.cos(ang_b)
    sb = jnp.sin(ang_b)
    by_col = jnp.where((b & 1) == 0, 1.0, -1.0)
    by_row = jnp.where((r & 1) == 0, 1.0, -1.0)
    for a in range(L // LANES):
        ang_a = ((r1 * (a * LANES)) & (n - 1)).astype(F32) * w
        ca = jnp.cos(ang_a)
        sa = jnp.sin(ang_a)
        cs = ca * cb - sa * sb
        sn = -(sa * cb + ca * sb)
        cols = slice(a * LANES, (a + 1) * LANES)
        fs = jnp.where(r == 0, by_col, sn)
        cs_hi, cs_lo = _split(cs)
        fs_hi, fs_lo = _split(fs)
        fb_ref[0:L, cols] = cs_hi
        fb_ref[L:2 * L, cols] = fs_hi
        fl_ref[0:L, cols] = cs_lo
        fl_ref[L:2 * L, cols] = fs_lo
        gb_ref[:, cols] = cs_hi
        gs = jnp.where(b == 0, by_row, sn) if a == 0 else sn
        gb_ref[:, L + a * LANES:L + (a + 1) * LANES] = gs.astype(BF16)


def _dft_tables(L):
    assert L & (L - 1) == 0
    return pl.pallas_call(
        functools.partial(_dft_kernel, L=L),
        out_shape=[jax.ShapeDtypeStruct((2 * L, L), BF16), jax.ShapeDtypeStruct((2 * L, L), BF16),
                   jax.ShapeDtypeStruct((L, 2 * L), BF16)],
        compiler_params=pltpu.CompilerParams(vmem_limit_bytes=VMEM_LIMIT),
    )()


def _filt_kernel(fb_ref, fl_ref, hf_ref, hb_ref, p_ref, q_ref, s_ref, *, L):
    hf = hf_ref[...]
    hb = hb_ref[...]
    ct = hf.shape[1]
    nrm = lax.rsqrt(jnp.sum(hf * hf, axis=0, keepdims=True) + jnp.sum(hb * hb, axis=0, keepdims=True) + EPS)
    hh, hl = _split(jnp.concatenate([hf, hb], axis=1))
    fb = fb_ref[...]
    fh = _dot(fb, hh) + _dot(fl_ref[...], hh) + _dot(fb, hl)
    row = lax.broadcasted_iota(jnp.int32, (L, ct), 0)
    w = jnp.where(row == 0, 1.0 / (2 * L), 2.0 / (2 * L)) * nrm
    re = fh[:L, :ct] + fh[:L, ct:]
    p_ref[...] = re * w
    q_ref[...] = jnp.where(row == 0, 0.0, fh[L:, :ct] - fh[L:, ct:]) * w
    s_ref[...] = jnp.where(row == 0, fh[L:, :ct] + fh[L:, ct:], re) * w


def _filter_spectrum(L, table_hi, table_lo, h):
    ct = 128
    nc = HY_CH // ct
    shp = jax.ShapeDtypeStruct((L, HY_CH), F32)
    return pl.pallas_call(
        functools.partial(_filt_kernel, L=L),
        out_shape=[shp, shp, shp],
        grid=(nc,),
        in_specs=[pl.BlockSpec((2 * L, L), lambda c: (0, 0)),
                  pl.BlockSpec((2 * L, L), lambda c: (0, 0)),
                  pl.BlockSpec((L, ct), lambda c: (0, c)),
                  pl.BlockSpec((L, ct), lambda c: (0, nc + c))],
        out_specs=[pl.BlockSpec((L, ct), lambda c: (0, c))] * 3,
        compiler_params=_cparams(("arbitrary",)),
    )(table_hi, table_lo, h, h)


def _hyena_kernel(x0_ref, x1_ref, v_ref, w0_ref, w1_ref, wv_ref, b0_ref, b1_ref, bv_ref, bias_ref,
                  f_ref, g_ref, p_ref, q_ref, s_ref, o_ref, *, L, seqs):
    ct = x0_ref.shape[1]
    row = lax.broadcasted_iota(jnp.int32, (L, ct), 0)
    for n in range(seqs):
        rows = slice(n * L, (n + 1) * L)

        def sconv(x_ref, w_ref, b_ref, rows=rows):
            x = x_ref[rows, :]
            xm = jnp.where(row == 0, 0.0, pltpu.roll(x, 1, 0))
            xp = jnp.where(row == L - 1, 0.0, pltpu.roll(x, L - 1, 0))
            return xm * w_ref[0:1, :] + x * w_ref[1:2, :] + xp * w_ref[2:3, :] + b_ref[...]

        z = sconv(v_ref, wv_ref, bv_ref) * sconv(x1_ref, w1_ref, b1_ref)
        zf = _dot(f_ref[...], z.astype(BF16))
        a = zf[:L]
        b = zf[L:]
        p = p_ref[...]
        q = q_ref[...]
        yre = a * p - b * q
        yim = a * q + b * s_ref[...]
        ycat = jnp.concatenate([yre, yim], axis=0).astype(BF16)
        y = _dot(g_ref[...], ycat) + z * bias_ref[...]
        o_ref[rows, :] = sconv(x0_ref, w0_ref, b0_ref) * y


def _hyena(p_hy, short_w, short_b, bias, table_bf, table_t_bf, spec, L, nb, row0, seqs):
    ct = 256
    nc = HY_CH // ct
    P, Q, S = spec
    blk = seqs * L

    def xs(k):
        return pl.BlockSpec((blk, ct), lambda b, c: (row0 // blk + b, k * nc + c))

    def ws(rows, k):
        return pl.BlockSpec((rows, ct), lambda b, c: (0, k * nc + c))

    cs = pl.BlockSpec((L, ct), lambda b, c: (0, c))
    return pl.pallas_call(
        functools.partial(_hyena_kernel, L=L, seqs=seqs),
        out_shape=jax.ShapeDtypeStruct((nb * L, HY_CH), F32),
        grid=(nb // seqs, nc),
        in_specs=[xs(0), xs(1), xs(2), ws(3, 0), ws(3, 1), ws(3, 2), ws(1, 0), ws(1, 1), ws(1, 2), ws(1, 0),
                  pl.BlockSpec((2 * L, L), lambda b, c: (0, 0)),
                  pl.BlockSpec((L, 2 * L), lambda b, c: (0, 0)),
                  cs, cs, cs],
        out_specs=pl.BlockSpec((blk, ct), lambda b, c: (b, c)),
        compiler_params=_cparams(("arbitrary", "arbitrary")),
    )(p_hy, p_hy, p_hy, short_w, short_w, short_w, short_b, short_b, short_b, bias,
      table_bf, table_t_bf, P, Q, S)


def _rope_tables(L):
    half = HEAD_DIM // 2
    pos = jnp.arange(L)
    r = (pos // GRID_W).astype(F32)
    col = (pos % GRID_W).astype(F32)
    inv = ROPE_BASE ** (-jnp.arange(0, half, 2, dtype=F32) / half)
    ar = r[:, None] * inv[None, :]
    ac = col[:, None] * inv[None, :]
    z = jnp.zeros_like(ar)
    cos = jnp.concatenate([jnp.cos(ar), jnp.cos(ar), jnp.cos(ac), jnp.cos(ac)], axis=1)
    s1 = jnp.concatenate([-jnp.sin(ar), z, -jnp.sin(ac), z], axis=1)
    s2 = jnp.concatenate([z, jnp.sin(ar), z, jnp.sin(ac)], axis=1)
    rep = LANES // HEAD_DIM
    return jnp.tile(cos, (1, rep)), jnp.tile(s1, (1, rep)), jnp.tile(s2, (1, rep))


def _attn_kernel(*refs, L, Tq, H, KVH, Lc, window, use_sink, use_norm, use_rope, emit_k):
    G = H // KVH
    hd = HEAD_DIM
    per = LANES // hd
    pad = WINDOW if window else 0
    it = iter(refs)
    q_ref = next(it)
    kv_ref = next(it)
    ck_ref = cv_ref = sink_ref = qn_ref = kn_ref = bd_ref = rc_ref = rs1_ref = rs2_ref = ko_ref = None
    if Lc:
        ck_ref = next(it)
        cv_ref = next(it)
    if use_sink:
        sink_ref = next(it)
    if use_norm:
        qn_ref = next(it)
        kn_ref = next(it)
        bd_ref = next(it)
    if use_rope:
        rc_ref = next(it)
        rs1_ref = next(it)
        rs2_ref = next(it)
    o_ref = next(it)
    if emit_k:
        ko_ref = next(it)
    k_scr = next(it)
    v_scr = next(it)
    i = pl.program_id(1)
    lat0 = Lc + pad

    def head_norm(x, g_ref):
        hi, lo = _split(x * x)
        ms = _dot(hi, bd_ref[...]) + _dot(lo, bd_ref[...])
        return x * lax.rsqrt(ms + EPS) * g_ref[...]

    def rope(x, rows):
        return (x * rc_ref[rows, :] + pltpu.roll(x, LANES - 16, 1) * rs1_ref[rows, :]
                + pltpu.roll(x, 16, 1) * rs2_ref[rows, :])

    def put(scr, g, rows, x):
        scr[g, 0, rows, :] = x.astype(BF16)
        scr[g, 1, rows, :] = pltpu.roll(x, hd, 1).astype(BF16)

    @pl.when(i == 0)
    def _():
        n_kg = KVH // per
        if window:
            zpad = jnp.zeros((n_kg, per, pad, LANES), BF16)
            for scr in (k_scr, v_scr):
                scr[:, :, Lc:Lc + pad, :] = zpad
                scr[:, :, lat0 + L:lat0 + L + pad, :] = zpad
        for g in range(n_kg):
            cols = slice(LANES * g, LANES * (g + 1))
            kg = kv_ref[:, cols]
            if use_norm:
                kg = head_norm(kg, kn_ref)
                if emit_k:
                    ko_ref[:, cols] = kg
            if use_rope:
                kg = rope(kg, slice(None))
            put(k_scr, g, slice(lat0, lat0 + L), kg)
            put(v_scr, g, slice(lat0, lat0 + L), kv_ref[:, KVH * hd + LANES * g:KVH * hd + LANES * (g + 1)])
            if Lc:
                put(k_scr, g, slice(0, Lc), ck_ref[0][:, cols])
                put(v_scr, g, slice(0, Lc), cv_ref[0][:, cols])

    rows = pl.ds(pl.multiple_of(i * Tq, Tq), Tq)
    low = lax.broadcasted_iota(jnp.int32, (Tq, LANES), 1) < hd
    groups_per_kv = G // per
    hrow = lax.broadcasted_iota(jnp.int32, (groups_per_kv * Tq, 1), 0) // Tq
    for kvh in range(KVH):
        qgs = []
        for g in range(kvh * groups_per_kv, (kvh + 1) * groups_per_kv):
            qg = q_ref[:, LANES * g:LANES * (g + 1)]
            if use_norm:
                qg = head_norm(qg, qn_ref)
            if use_rope:
                qg = rope(qg, rows)
            qgs.append((qg * (hd ** -0.5)).astype(BF16))
        outs = []
        for half in range(per):
            keep = low if half == 0 else ~low
            qs = jnp.concatenate([jnp.where(keep, qg, jnp.zeros_like(qg)) for qg in qgs], axis=0)
            var = (kvh % per) ^ half
            kg = kvh // per
            if use_sink:
                sk = jnp.zeros((groups_per_kv * Tq, 1), F32)
                for n in range(groups_per_kv):
                    sk = jnp.where(hrow == n, sink_ref[0, (kvh * groups_per_kv + n) * per + half], sk)
            if window:
                W = Tq + 2 * pad
                wrows = pl.ds(pl.multiple_of(Lc + i * Tq, Tq), W)
                s_c = _dot(qs, k_scr[kg, var, 0:Lc, :], _NT)
                s_w = _dot(qs, k_scr[kg, var, wrows, :], _NT).reshape(groups_per_kv, Tq, W)
                r = lax.broadcasted_iota(jnp.int32, (Tq, W), 0)
                c = lax.broadcasted_iota(jnp.int32, (Tq, W), 1)
                kpos = i * Tq - pad + c
                valid = (kpos >= 0) & (kpos < L) & (jnp.abs(r + pad - c) <= WINDOW)
                s_w = jnp.where(valid[None], s_w, NEG_INF).reshape(groups_per_kv * Tq, W)
                m = jnp.maximum(jnp.max(s_c, axis=-1, keepdims=True), jnp.max(s_w, axis=-1, keepdims=True))
                if use_sink:
                    m = jnp.maximum(m, sk)
                e_c = jnp.exp(s_c - m)
                e_w = jnp.exp(s_w - m)
                den = jnp.sum(e_c, axis=-1, keepdims=True) + jnp.sum(e_w, axis=-1, keepdims=True)
                o = (_dot(e_c.astype(BF16), v_scr[kg, var, 0:Lc, :])
                     + _dot(e_w.astype(BF16), v_scr[kg, var, wrows, :]))
            else:
                s = _dot(qs, k_scr[kg, var], _NT)
                m = jnp.max(s, axis=-1, keepdims=True)
                if use_sink:
                    m = jnp.maximum(m, sk)
                e = jnp.exp(s - m)
                den = jnp.sum(e, axis=-1, keepdims=True)
                o = _dot(e.astype(BF16), v_scr[kg, var])
            if use_sink:
                den = den + jnp.exp(sk - m)
            outs.append(o / den)
        for n in range(groups_per_kv):
            g = kvh * groups_per_kv + n
            o_ref[:, LANES * g:LANES * (g + 1)] = jnp.where(low, outs[0][n * Tq:(n + 1) * Tq],
                                                            outs[1][n * Tq:(n + 1) * Tq])


def _attention(p, q_cols, kv_colblk, *, L, Tq, nb, row0, H, KVH, ctx=None, sink=None, norm=None, rope=None,
               window=False, emit_k=False):
    hd = HEAD_DIM
    nq = L // Tq
    Lc = 0 if ctx is None else ctx[0].shape[1]
    pad = WINDOW if window else 0
    Lk = Lc + L + 2 * pad
    kvw = 2 * KVH * hd
    args = [p, p]
    in_specs = [pl.BlockSpec((Tq, q_cols), lambda b, i: (row0 // Tq + b * nq + i, 0)),
                pl.BlockSpec((L, kvw), lambda b, i: (row0 // L + b, kv_colblk))]
    if ctx is not None:
        for a in ctx:
            args.append(a)
            in_specs.append(pl.BlockSpec((1, Lc, KVH * hd), lambda b, i: (b, 0, 0)))
    if sink is not None:
        args.append(sink.reshape(1, H))
        in_specs.append(pl.BlockSpec(memory_space=pltpu.SMEM))
    if norm is not None:
        bd = (jnp.arange(LANES)[:, None] // hd == jnp.arange(LANES)[None, :] // hd).astype(F32) / hd
        for a in (jnp.tile(norm[0], LANES // hd).reshape(1, LANES), jnp.tile(norm[1], LANES // hd).reshape(1, LANES)):
            args.append(a)
            in_specs.append(pl.BlockSpec((1, LANES), lambda b, i: (0, 0)))
        args.append(bd.astype(BF16))
        in_specs.append(pl.BlockSpec((LANES, LANES), lambda b, i: (0, 0)))
    if rope is not None:
        for a in rope:
            args.append(a)
            in_specs.append(pl.BlockSpec((L, LANES), lambda b, i: (0, 0)))
    out_shape = [jax.ShapeDtypeStruct((nb * L, H * hd), F32)]
    out_specs = [pl.BlockSpec((Tq, H * hd), lambda b, i: (b * nq + i, 0))]
    if emit_k:
        out_shape.append(jax.ShapeDtypeStruct((nb * L, KVH * hd), F32))
        out_specs.append(pl.BlockSpec((L, KVH * hd), lambda b, i: (b, 0)))
    kern = functools.partial(_attn_kernel, L=L, Tq=Tq, H=H, KVH=KVH, Lc=Lc, window=window,
                             use_sink=sink is not None, use_norm=norm is not None,
                             use_rope=rope is not None, emit_k=emit_k)
    call = pl.pallas_call(
        kern, out_shape=out_shape, grid=(nb, nq), in_specs=in_specs, out_specs=out_specs,
        scratch_shapes=[pltpu.VMEM((KVH * hd // LANES, LANES // hd, Lk, LANES), BF16),
                        pltpu.VMEM((KVH * hd // LANES, LANES // hd, Lk, LANES), BF16)],
        compiler_params=_cparams(("arbitrary", "arbitrary")))
    return call, args


def _oproj_kernel(*refs, paired):
    rows = slice(None)
    al, rest = _take_rows(refs, paired[0], rows)
    ar, rest = _take_rows(rest, paired[1], rows)
    x, rest = _take_rows(rest, paired[2], rows)
    w_ref, g1_ref, gate_ref, g2_ref, sh_ref, sc_ref, wr_ref, x1_ref, h2_ref, lg_ref = rest
    k = al.shape[1]
    y = _dot(al.astype(BF16), w_ref[0:k, :]) + _dot(ar.astype(BF16), w_ref[k:2 * k, :])
    x1 = x + gate_ref[0] * _rms(y, g1_ref[...])
    x1_ref[...] = x1
    h2 = _rms(x1, g2_ref[...]) * (1.0 + sc_ref[0]) + sh_ref[0]
    wh, wl = _split(wr_ref[...])
    hb = h2.astype(BF16)
    h2_ref[...] = hb
    lg_ref[...] = _dot(wh, hb, _NT) + _dot(wl, hb, _NT)


def _out_proj(a_l, l_blk, a_r, r_blk, w_bf, x, g1, g2, mod, layer, wr_t):
    tm = ROW_TILE
    half = D_MODEL // 2
    l_specs, l_args = _row_specs(a_l, half, l_blk)
    r_specs, r_args = _row_specs(a_r, half, r_blk)
    x_specs, x_args = _row_specs(x, D_MODEL)
    paired = tuple(isinstance(a, tuple) for a in (a_l, a_r, x))
    return pl.pallas_call(
        functools.partial(_oproj_kernel, paired=paired),
        out_shape=[jax.ShapeDtypeStruct((NT, D_MODEL), F32), jax.ShapeDtypeStruct((NT, D_MODEL), BF16),
                   jax.ShapeDtypeStruct((N_EXPERTS, NT), F32)],
        grid=(NT // tm,),
        in_specs=l_specs + r_specs + x_specs + [
            pl.BlockSpec((D_MODEL, D_MODEL), lambda i: (0, 0)),
            pl.BlockSpec((1, D_MODEL), lambda i: (0, 0)),
            _mod_spec(layer, 2, tm),
            pl.BlockSpec((1, D_MODEL), lambda i: (0, 0)),
            _mod_spec(layer, 3, tm), _mod_spec(layer, 4, tm),
            pl.BlockSpec((N_EXPERTS, D_MODEL), lambda i: (0, 0))],
        out_specs=[pl.BlockSpec((tm, D_MODEL), lambda i: (i, 0)),
                   pl.BlockSpec((tm, D_MODEL), lambda i: (i, 0)),
                   pl.BlockSpec((N_EXPERTS, tm), lambda i: (0, i))],
        compiler_params=_cparams(("arbitrary",)),
    )(*l_args, *r_args, *x_args, w_bf, g1.reshape(1, D_MODEL), mod, g2.reshape(1, D_MODEL), mod, mod, wr_t)


COMB_ROWS = 512
COMB_WIN = 128
BF16_SUB = 16
F32_SUB = 8
FFN_PARTS = 2


def _ffn_kernel(a0_ref, s1_ref, slot_ref, aff_ref, h_ref, wg_ref, wu_ref, wd_ref, y_ref,
                xn_scr, gn_scr, xc_scr, gc_scr, acc_scr, *, cap, nf, nblk):
    grp = pl.program_id(0)
    f = pl.program_id(1)
    R = COMB_WIN
    per_step = nblk // nf
    erow = jnp.minimum(grp, N_EXPERTS - 1) % F32_SUB

    def add_window(jb, c0, hit_of):
        cols = slice(jb * COMB_ROWS, (jb + 1) * COMB_ROWS)
        q = lax.broadcasted_iota(jnp.int32, (R, COMB_ROWS), 0)
        hit = hit_of(slot_ref[pl.ds(erow, 1), cols].astype(jnp.int32), q)
        rows = pl.ds(c0, R)
        piece = _dot(hit.astype(F32).astype(BF16), h_ref[cols, :])
        xn_scr[rows, :] += piece.astype(BF16)
        gn_scr[rows, :] += jnp.sum(jnp.where(hit, aff_ref[pl.ds(erow, 1), cols], 0.0), axis=1, keepdims=True)

    def window_start(jb):
        return pl.multiple_of(a0_ref[grp * nblk + f * per_step + jb], BF16_SUB)

    def gather():
        for jb in range(per_step):
            a0 = window_start(jb)
            add_window(jb, a0, lambda srow, q, a0=a0: srow - a0 == q)

    def gather_more():
        for jb in range(per_step):
            end = s1_ref[grp * nblk + f * per_step + jb]

            def extra(c, jb=jb):
                c0 = pl.multiple_of(jnp.minimum(c, cap - R), BF16_SUB)
                add_window(jb, c0, lambda srow, q: (srow - c0 == q) & (q + c0 >= c))
                return c + R

            lax.while_loop(lambda c, end=end: c < end, extra, window_start(jb) + R)

    def compute(first):
        wg = wg_ref[0, 0].astype(BF16)
        wu = wu_ref[0, 0].astype(BF16)
        wd = wd_ref[0, 0].astype(BF16)
        part_rows = cap // FFN_PARTS
        for r in range(FFN_PARTS):
            rows = slice(r * part_rows, (r + 1) * part_rows)
            xe = xc_scr[rows, :]
            hmid = (jax.nn.silu(_dot(xe, wg)) * _dot(xe, wu)).astype(BF16)
            part = _dot(hmid, wd)
            if first:
                acc_scr[rows, :] = part
            else:
                acc_scr[rows, :] += part

    @pl.when(f == 0)
    def _():
        xn_scr[...] = jnp.zeros_like(xn_scr)
        gn_scr[...] = jnp.zeros_like(gn_scr)

    @pl.when(grp == 0)
    def _():
        gather()

    for first in (True, False):
        first_step = (f == 0) if first else (f != 0)

        @pl.when((grp > 0) & (grp < N_EXPERTS) & first_step)
        def _():
            gather()
            compute(first)

        @pl.when((grp == N_EXPERTS) & first_step)
        def _():
            compute(first)

    @pl.when(grp < N_EXPERTS)
    def _():
        gather_more()

    @pl.when(f == nf - 1)
    def _():
        @pl.when(grp > 0)
        def _():
            y_ref[...] = (acc_scr[...] * gc_scr[...]).astype(BF16)

        xc_scr[...] = xn_scr[...]
        gc_scr[...] = gn_scr[...]


def _expert_ffn(a0, s1, slot, aff, h2, layer, w_gate, w_up, w_down):
    cap = EC_CAPACITY * NT // N_EXPERTS
    fc = 512
    nf = EXPERT_FF // fc
    nblk = NT // COMB_ROWS
    tok = NT // nf

    def prev(g):
        return jnp.maximum(g - 1, 0)

    def this(g):
        return jnp.minimum(g, N_EXPERTS - 1)

    grid_spec = pltpu.PrefetchScalarGridSpec(
        num_scalar_prefetch=2,
        grid=(N_EXPERTS + 1, nf),
        in_specs=[pl.BlockSpec((F32_SUB, tok), lambda g, f, *_: (this(g) // F32_SUB, f)),
                  pl.BlockSpec((F32_SUB, tok), lambda g, f, *_: (this(g) // F32_SUB, f)),
                  pl.BlockSpec((tok, D_MODEL), lambda g, f, *_: (f, 0)),
                  pl.BlockSpec((1, 1, D_MODEL, fc), lambda g, f, *_: (layer, prev(g), 0, f)),
                  pl.BlockSpec((1, 1, D_MODEL, fc), lambda g, f, *_: (layer, prev(g), 0, f)),
                  pl.BlockSpec((1, 1, fc, D_MODEL), lambda g, f, *_: (layer, prev(g), f, 0))],
        out_specs=pl.BlockSpec((cap, D_MODEL), lambda g, f, *_: (prev(g), 0)),
        scratch_shapes=[pltpu.VMEM((cap, D_MODEL), BF16), pltpu.VMEM((cap, 1), F32),
                        pltpu.VMEM((cap, D_MODEL), BF16), pltpu.VMEM((cap, 1), F32),
                        pltpu.VMEM((cap, D_MODEL), F32)])
    return pl.pallas_call(
        functools.partial(_ffn_kernel, cap=cap, nf=nf, nblk=nblk),
        out_shape=jax.ShapeDtypeStruct((N_EXPERTS * cap, D_MODEL), BF16),
        grid_spec=grid_spec,
        compiler_params=_cparams(("arbitrary", "arbitrary")),
    )(a0.reshape(-1), s1.reshape(-1), slot, aff, h2,
      w_gate, w_up, w_down)


CMB_ROWS = COMB_ROWS
CMB_WIN = LANES


def _combine_kernel(a0_ref, s1_ref, slot_ref, y_hbm, x_ref, g_ref, gate_ref, *rest, cap, nblk):
    *o_refs, win_scr, ovf_scr, f_scr, sem, osem = rest
    j = pl.program_id(0)
    E, R, Tb = N_EXPERTS, CMB_WIN, CMB_ROWS

    def window(e, start, dst, s):
        return pltpu.make_async_copy(y_hbm.at[pl.ds(e * cap + start, R)], dst, s)

    def issue(blk, slot):
        for e in range(E):
            a0 = pl.multiple_of(a0_ref[e * nblk + blk], BF16_SUB)
            window(e, a0, win_scr.at[slot, pl.ds(e * R, R)], sem.at[slot]).start()

    @pl.when(j == 0)
    def _():
        issue(0, 0)

    slot = j % 2
    pltpu.make_async_copy(y_hbm.at[pl.ds(0, E * R)], win_scr.at[slot], sem.at[slot]).wait()

    @pl.when(j + 1 < nblk)
    def _():
        issue(j + 1, 1 - slot)

    sl = slot_ref[...].astype(jnp.int32)
    q = lax.broadcasted_iota(jnp.int32, (Tb, R), 1)
    onehot = jnp.concatenate(
        [(sl[:, e:e + 1] - a0_ref[e * nblk + j] == q) for e in range(E)], axis=1).astype(F32).astype(BF16)
    f_scr[...] = _dot(onehot, win_scr[slot])

    for e in range(E):
        end = s1_ref[e * nblk + j]

        def more(c):
            return c < end

        def extra(c, e=e):
            c0 = pl.multiple_of(jnp.minimum(c, cap - R), BF16_SUB)
            cp = window(e, c0, ovf_scr, osem)
            cp.start()
            cp.wait()
            hit = (sl[:, e:e + 1] - c0 == q) & (q + c0 >= c)
            f_scr[...] += _dot(hit.astype(F32).astype(BF16), ovf_scr[...])
            return c + R

        lax.while_loop(more, extra, a0_ref[e * nblk + j] + R)

    res = x_ref[...] + gate_ref[0] * _rms(f_scr[...], g_ref[...])
    if len(o_refs) == 1:
        o_refs[0][...] = res
    else:
        @pl.when(j < NS // Tb)
        def _():
            o_refs[0][...] = res

        @pl.when(j >= NS // Tb)
        def _():
            o_refs[1][...] = res


def _combine(a0, s1, slot_t, y, x, g, mod, layer, split_out=False):
    cap = y.shape[0] // N_EXPERTS
    nblk = NT // CMB_ROWS
    if split_out:
        out_shape = [jax.ShapeDtypeStruct((NS, D_MODEL), F32), jax.ShapeDtypeStruct((NP, D_MODEL), F32)]
        out_specs = _stream_specs(D_MODEL, tile=CMB_ROWS)
    else:
        out_shape = jax.ShapeDtypeStruct((NT, D_MODEL), F32)
        out_specs = pl.BlockSpec((CMB_ROWS, D_MODEL), lambda i, *_: (i, 0))
    grid_spec = pltpu.PrefetchScalarGridSpec(
        num_scalar_prefetch=2,
        grid=(nblk,),
        in_specs=[pl.BlockSpec((CMB_ROWS, LANES), lambda i, *_: (i, 0)),
                  pl.BlockSpec(memory_space=pl.ANY),
                  pl.BlockSpec((CMB_ROWS, D_MODEL), lambda i, *_: (i, 0)),
                  pl.BlockSpec((1, D_MODEL), lambda i, *_: (0, 0)),
                  _mod_spec(layer, 5, CMB_ROWS)],
        out_specs=out_specs,
        scratch_shapes=[pltpu.VMEM((2, N_EXPERTS * CMB_WIN, D_MODEL), BF16),
                        pltpu.VMEM((CMB_WIN, D_MODEL), BF16),
                        pltpu.VMEM((CMB_ROWS, D_MODEL), F32),
                        pltpu.SemaphoreType.DMA((2,)), pltpu.SemaphoreType.DMA(())])
    return pl.pallas_call(
        functools.partial(_combine_kernel, cap=cap, nblk=nblk),
        out_shape=out_shape,
        grid_spec=grid_spec,
        compiler_params=_cparams(("arbitrary",)),
    )(a0.reshape(-1), s1.reshape(-1), slot_t, y, x, g.reshape(1, D_MODEL), mod)


CUM_BLK = 256
SCALE_STEP = 2.0 ** -16
SCALE_ITERS = 10
BISECT_ITERS = 48


def _prefix_count(mask):
    n = mask.shape[1]
    tri = (lax.broadcasted_iota(jnp.int32, (CUM_BLK, CUM_BLK), 0)
           <= lax.broadcasted_iota(jnp.int32, (CUM_BLK, CUM_BLK), 1)).astype(F32).astype(BF16)
    carry = jnp.zeros((mask.shape[0], 1), F32)
    outs = []
    for b in range(n // CUM_BLK):
        c = _dot(mask[:, b * CUM_BLK:(b + 1) * CUM_BLK].astype(F32).astype(BF16), tri) + carry
        outs.append(c)
        carry = c[:, CUM_BLK - 1:CUM_BLK]
    return jnp.concatenate(outs, axis=1)


def _route_kernel(lg_ref, aff_ref, slot_ref, slott_ref, s0_ref):
    x = lg_ref[...]
    ex = jnp.exp(x - jnp.max(x, axis=0, keepdims=True))
    aff = ex / jnp.sum(ex, axis=0, keepdims=True)
    aff_ref[...] = aff
    streams = ((0, NS), (NS, NP))
    caps = [float(EC_CAPACITY * n // N_EXPERTS) for _, n in streams]
    parts = [aff[:, lo:lo + n] for lo, n in streams]

    def enough(k, thr):
        return jnp.sum((parts[k] >= thr).astype(F32), axis=1, keepdims=True) >= caps[k]

    def scale_step(_, carry):
        out = []
        for k, (lo_v, hi_v, found_f) in enumerate(carry):
            found = found_f > 0.5
            mid = hi_v * SCALE_STEP
            ge = enough(k, mid)
            out.append((jnp.where(found | ~ge, lo_v, mid), jnp.where(found | ge, hi_v, mid),
                        jnp.where(found | ge, 1.0, 0.0)))
        return tuple(out)

    def bisect_step(_, carry):
        out = []
        for k, (lo_v, hi_v) in enumerate(carry):
            mid = 0.5 * (lo_v + hi_v)
            ge = enough(k, mid)
            out.append((jnp.where(ge, mid, lo_v), jnp.where(ge, hi_v, mid)))
        return tuple(out)

    col = lambda v: jnp.full((N_EXPERTS, 1), v, F32)
    start = tuple((col(0.0), col(2.0), col(0.0)) for _ in streams)
    scaled = lax.fori_loop(0, SCALE_ITERS, scale_step, start)
    bounds = lax.fori_loop(0, BISECT_ITERS, bisect_step, tuple((lo_v, hi_v) for lo_v, hi_v, _ in scaled))
    slots = []
    off = 0.0
    for k, (lo_v, hi_v) in enumerate(bounds):
        above = parts[k] >= hi_v
        edge = (parts[k] >= lo_v) & ~above
        need = caps[k] - jnp.sum(above.astype(F32), axis=1, keepdims=True)
        edge_rank = _prefix_count(edge) - edge.astype(F32)
        sel = above | (edge & (edge_rank < need))
        slots.append(jnp.where(sel, _prefix_count(sel) - 1.0 + off, -1.0))
        off += caps[k]
    slot = jnp.concatenate(slots, axis=1)
    slot_ref[...] = slot
    pad = jnp.full((LANES - N_EXPERTS, NT), -1.0, F32)
    slott_ref[...] = jnp.concatenate([slot, pad], axis=0).T
    tok = lax.broadcasted_iota(jnp.int32, (NT, LANES), 0)
    edge = lax.broadcasted_iota(jnp.int32, (NT, LANES), 1) * CMB_ROWS
    s0_ref[...] = _dot((slot >= 0.0).astype(F32).astype(BF16), (tok < edge).astype(F32).astype(BF16))


def _route_tables(logits_t):
    return pl.pallas_call(
        _route_kernel,
        out_shape=[jax.ShapeDtypeStruct((N_EXPERTS, NT), F32), jax.ShapeDtypeStruct((N_EXPERTS, NT), F32),
                   jax.ShapeDtypeStruct((NT, LANES), F32), jax.ShapeDtypeStruct((N_EXPERTS, LANES), F32)],
        compiler_params=pltpu.CompilerParams(vmem_limit_bytes=VMEM_LIMIT),
    )(logits_t)


def _route(logits_t, h2, x, g, mod, layer, w_gate, w_up, w_down, split_out=False):
    aff, slot, slot_t, s0 = _route_tables(logits_t)
    cap = EC_CAPACITY * NT // N_EXPERTS
    s0 = s0[:, :NT // CMB_ROWS + 1].astype(jnp.int32)

    def windows(starts, win):
        return jnp.minimum(starts[:, :-1] // BF16_SUB * BF16_SUB, cap - win), starts[:, 1:]

    a0g, s1g = windows(s0[:, ::COMB_ROWS // CMB_ROWS], COMB_WIN)
    y = _expert_ffn(a0g, s1g, slot, aff, h2, layer, w_gate, w_up, w_down)
    a0c, s1c = windows(s0, CMB_WIN)
    return _combine(a0c, s1c, slot_t, y, x, g, mod, layer, split_out)


def kernel(x_prompt, x_sample, cache_b_k, cache_b_v, cache_c_k, cache_c_v, c, c_ctx, ada_w, ada_b, norm_g,
           ab_w_in, hy_short_w, hy_short_b, hy_w1, hy_b1, hy_w2, hy_b2, hy_freq, hy_w3, hy_log_decay, hy_bias,
           b_sink, ab_w_out, c_w_qkv, c_q_norm, c_k_norm, c_w_out, ec_router, ec_w_gate, ec_w_up, ec_w_down):
    x = (x_sample.reshape(NS, D_MODEL), x_prompt.reshape(NP, D_MODEL))
    cond = jnp.concatenate([c, c_ctx[None], jnp.zeros((MOD_ROWS - DEC_BATCH - 1, D_MODEL), F32)], axis=0)
    mod = _adaln(cond, ada_w, ada_b).reshape(DEPTH * MOD_ROWS * 6, 1, D_MODEL)
    rope = _rope_tables(DEC_SEQ)
    hd = HEAD_DIM

    e = 0
    p_hy, p_at = _norm_mod_matmul(x, norm_g[0, 0], mod, 0, ab_w_in[e].astype(BF16), (3 * HY_CH, D_MODEL - HY_CH + 2 * B_KV * hd))
    ya = []
    for L, nb, row0, seqs in ((DEC_SEQ, DEC_BATCH, 0, 2), (SEQ, BATCH, NS, 4)):
        table_lo, table_bf, table_t_bf = _dft_tables(L)
        h = _hyena_filters(L, hy_w1[e], hy_b1[e], hy_w2[e], hy_b2[e], hy_freq[e], hy_w3[e], hy_log_decay[e])
        h = h.at[0, HY_CH:].set(0.0)
        spec = _filter_spectrum(L, table_bf, table_lo, h)
        ya.append(_hyena(p_hy, hy_short_w[e], hy_short_b[e].reshape(1, -1), hy_bias[e].reshape(1, -1),
                         table_bf, table_t_bf, spec, L, nb, row0, seqs))
    ya = tuple(ya)
    ctx_b = (cache_b_k[:, e].reshape(DEC_BATCH, PAST_LEN, B_KV * hd), cache_b_v[:, e].reshape(DEC_BATCH, PAST_LEN, B_KV * hd))
    call, args = _attention(p_at, B_HEADS * hd, 2, L=DEC_SEQ, Tq=256, nb=DEC_BATCH, row0=0, H=B_HEADS, KVH=B_KV,
                            ctx=ctx_b, sink=b_sink[e], rope=rope, window=True)
    yb_s = call(*args)[0]
    call, args = _attention(p_at, B_HEADS * hd, 2, L=SEQ, Tq=256, nb=BATCH, row0=NS, H=B_HEADS, KVH=B_KV,
                            sink=b_sink[e])
    yb_p = call(*args)[0]
    yb = (yb_s, yb_p)
    state_b_k = p_at[NS:, B_HEADS * hd:(B_HEADS + B_KV) * hd].reshape(BATCH, 1, SEQ, B_KV, hd)
    state_b_v = p_at[NS:, (B_HEADS + B_KV) * hd:].reshape(BATCH, 1, SEQ, B_KV, hd)
    x, h2, lg = _out_proj(ya, 0, yb, 0, ab_w_out[e].astype(BF16), x, norm_g[0, 1], norm_g[0, 2], mod, 0, ec_router[0].T)
    x = _route(lg, h2, x, norm_g[0, 3], mod, 0, ec_w_gate, ec_w_up, ec_w_down)

    o = 0
    (p,) = _norm_mod_matmul(x, norm_g[1, 0], mod, 1, c_w_qkv[o].astype(BF16), ((C_HEADS + 2 * C_KV) * hd,))
    ctx_c = (cache_c_k[:, o].reshape(DEC_BATCH, PAST_LEN, C_KV * hd), cache_c_v[:, o].reshape(DEC_BATCH, PAST_LEN, C_KV * hd))
    nrm = (c_q_norm[o], c_k_norm[o])
    call, args = _attention(p, C_HEADS * hd, 2, L=DEC_SEQ, Tq=256, nb=DEC_BATCH, row0=0, H=C_HEADS, KVH=C_KV,
                            ctx=ctx_c, norm=nrm, rope=rope)
    yc_s = call(*args)[0]
    call, args = _attention(p, C_HEADS * hd, 2, L=SEQ, Tq=256, nb=BATCH, row0=NS, H=C_HEADS, KVH=C_KV,
                            norm=nrm, emit_k=True)
    yc_p, kn = call(*args)
    yc = (yc_s, yc_p)
    state_c_k = kn.reshape(BATCH, 1, SEQ, C_KV, hd)
    state_c_v = p[NS:, (C_HEADS + C_KV) * hd:].reshape(BATCH, 1, SEQ, C_KV, hd)
    x, h2, lg = _out_proj(yc, 0, yc, 1, c_w_out[o].astype(BF16), x, norm_g[1, 1], norm_g[1, 2], mod, 1, ec_router[1].T)
    xs, xp = _route(lg, h2, x, norm_g[1, 3], mod, 1, ec_w_gate, ec_w_up, ec_w_down, split_out=True)
    y_sample = xs.reshape(DEC_BATCH, DEC_SEQ, D_MODEL)
    y_prompt = xp.reshape(BATCH, SEQ, D_MODEL)
    return (y_prompt, y_sample, state_b_k, state_b_v, state_c_k, state_c_v)
```

```python
import functools
import math

import jax
import jax.numpy as jnp
from jax import lax
from jax.experimental import pallas as pl
from jax.experimental.pallas import tpu as pltpu

F32 = jnp.float32
BF16 = jnp.bfloat16

D_MODEL = 1024
BATCH = 16
SEQ = 256
DEPTH = 2
DEC_BATCH = 8
DEC_SEQ = 1024
PAST_LEN = 512
GRID_W = 64
HEAD_DIM = 64
HY_CH = D_MODEL // 2
HY_BANDS = 8
B_HEADS = 8
B_KV = 2
WINDOW = 128
C_HEADS = 16
C_KV = 4
ROPE_BASE = 10000.0
N_EXPERTS = 16
EC_CAPACITY = 2
EXPERT_FF = 2 * D_MODEL
EPS = 1e-6
NEG_INF = -1e30

NS = DEC_BATCH * DEC_SEQ
NP = BATCH * SEQ
NT = NS + NP
MOD_ROWS = 16
LANES = 128
VMEM_LIMIT = 56 * 1024 * 1024


def _cparams(sem):
    return pltpu.CompilerParams(dimension_semantics=sem, vmem_limit_bytes=VMEM_LIMIT)


def _split(x):
    hi = x.astype(BF16)
    lo = (x - hi.astype(F32)).astype(BF16)
    return hi, lo


_NN = (((1,), (0,)), ((), ()))
_NT = (((1,), (1,)), ((), ()))


def _dot(a, b, dims=_NN):
    return lax.dot_general(a, b, dims, preferred_element_type=F32)


def _dot3(a, b, dims=_NN):
    ah, al = _split(a)
    bh, bl = _split(b)
    return _dot(ah, bh, dims) + _dot(al, bh, dims) + _dot(ah, bl, dims)


def _rms(x, g):
    return x * lax.rsqrt(jnp.mean(x * x, axis=-1, keepdims=True) + EPS) * g


def _mod_row(tile_rows):
    def f(i):
        return jnp.minimum(i * tile_rows // DEC_SEQ, DEC_BATCH)
    return f


def _adaln_kernel(c_ref, w_ref, b_ref, o_ref):
    s = jax.nn.silu(c_ref[...])
    o_ref[0] = _dot3(s, w_ref[0]) + b_ref[0]


def _adaln(cond, ada_w, ada_b):
    tn = 1536
    return pl.pallas_call(
        _adaln_kernel,
        out_shape=jax.ShapeDtypeStruct((DEPTH, MOD_ROWS, 6 * D_MODEL), F32),
        grid=(DEPTH, 6 * D_MODEL // tn),
        in_specs=[pl.BlockSpec((MOD_ROWS, D_MODEL), lambda l, j: (0, 0)),
                  pl.BlockSpec((1, D_MODEL, tn), lambda l, j: (l, 0, j)),
                  pl.BlockSpec((1, 1, tn), lambda l, j: (l, 0, j))],
        out_specs=pl.BlockSpec((1, MOD_ROWS, tn), lambda l, j: (l, 0, j)),
        compiler_params=_cparams(("arbitrary", "arbitrary")),
    )(cond, ada_w, ada_b.reshape(DEPTH, 1, 6 * D_MODEL))


def _mod_spec(layer, which, tile_rows):
    row = _mod_row(tile_rows)
    return pl.BlockSpec((1, 1, D_MODEL), lambda i, *_: ((layer * MOD_ROWS + row(i)) * 6 + which, 0, 0))


ROW_TILE = 1024


def _stream_specs(width, col=0, tile=ROW_TILE):
    ns = NS // tile
    return [pl.BlockSpec((tile, width), lambda i, *_: (jnp.minimum(i, ns - 1), col)),
            pl.BlockSpec((tile, width), lambda i, *_: (jnp.maximum(i - ns, 0), col))]


def _row_specs(a, width, col=0, tile=ROW_TILE):
    if isinstance(a, tuple):
        return _stream_specs(width, col, tile), list(a)
    return [pl.BlockSpec((tile, width), lambda i, *_: (i, col))], [a]


ROW_CHUNK = 128


def _take_rows(refs, paired, rows, tile=ROW_TILE):
    if paired:
        return jnp.where(pl.program_id(0) < NS // tile, refs[0][rows, :], refs[1][rows, :]), refs[2:]
    return refs[0][rows, :], refs[1:]


NMM_TILE = 1024


def _nmm_kernel(*refs, paired):
    for c in range(NMM_TILE // ROW_CHUNK):
        rows = slice(c * ROW_CHUNK, (c + 1) * ROW_CHUNK)
        x, (g_ref, sh_ref, sc_ref, w_ref, *o_refs) = _take_rows(refs, paired, rows, NMM_TILE)
        h = _rms(x, g_ref[...]) * (1.0 + sc_ref[0]) + sh_ref[0]
        p = _dot(h.astype(BF16), w_ref[...])
        off = 0
        for o in o_refs:
            n = o.shape[1]
            o[rows, :] = p[:, off:off + n]
            off += n


def _norm_mod_matmul(x, g, mod, layer, w_bf, splits):
    tm = NMM_TILE
    n_out = w_bf.shape[1]
    x_specs, x_args = _row_specs(x, D_MODEL, tile=tm)
    return pl.pallas_call(
        functools.partial(_nmm_kernel, paired=isinstance(x, tuple)),
        out_shape=[jax.ShapeDtypeStruct((NT, n), F32) for n in splits],
        grid=(NT // tm,),
        in_specs=x_specs + [pl.BlockSpec((1, D_MODEL), lambda i: (0, 0)),
                            _mod_spec(layer, 0, tm), _mod_spec(layer, 1, tm),
                            pl.BlockSpec((D_MODEL, n_out), lambda i: (0, 0))],
        out_specs=[pl.BlockSpec((tm, n), lambda i: (i, 0)) for n in splits],
        compiler_params=_cparams(("arbitrary",)),
    )(*x_args, g.reshape(1, D_MODEL), mod, mod, w_bf)


def _hyena_filters(L, w1, b1, w2, b2, freq, w3, log_decay):
    hp = lax.Precision.HIGHEST
    t = jnp.linspace(0.0, 1.0, L, dtype=F32)[:, None]
    w = (2.0 * math.pi / L) * jnp.arange(L, dtype=F32)[:, None]
    bands = jnp.linspace(1e-4, HY_BANDS - 1, HY_BANDS, dtype=F32)[None, :]
    z = jnp.concatenate([t, jnp.cos(bands * w), -jnp.sin(bands * w)], axis=-1)
    h = jnp.sin(freq[0] * (jnp.dot(z, w1, precision=hp) + b1))
    h = jnp.sin(freq[1] * (jnp.dot(h, w2, precision=hp) + b2))
    return jnp.dot(h, w3, precision=hp) * jnp.exp(-t * jnp.exp(log_decay))


def _dft_kernel(fl_ref, fb_ref, gb_ref, *, L):
    n = 2 * L
    w = 2.0 * math.pi / n
    r = lax.broadcasted_iota(jnp.int32, (L, LANES), 0)
    b = lax.broadcasted_iota(jnp.int32, (L, LANES), 1)
    r1 = lax.broadcasted_iota(jnp.int32, (L, 1), 0)
    ang_b = ((r * b) & (n - 1)).astype(F32) * w
    cb = jnp.cos(ang_b)
    sb = jnp.sin(ang_b)
    by_col = jnp.where((b & 1) == 0, 1.0, -1.0)
    by_row = jnp.where((r & 1) == 0, 1.0, -1.0)
    for a in range(L // LANES):
        ang_a = ((r1 * (a * LANES)) & (n - 1)).astype(F32) * w
        ca = jnp.cos(ang_a)
        sa = jnp.sin(ang_a)
        cs = ca * cb - sa * sb
        sn = -(sa * cb + ca * sb)
        cols = slice(a * LANES, (a + 1) * LANES)
        fs = jnp.where(r == 0, by_col, sn)
        cs_hi, cs_lo = _split(cs)
        fs_hi, fs_lo = _split(fs)
        fb_ref[0:L, cols] = cs_hi
        fb_ref[L:2 * L, cols] = fs_hi
        fl_ref[0:L, cols] = cs_lo
        fl_ref[L:2 * L, cols] = fs_lo
        gb_ref[:, cols] = cs_hi
        gs = jnp.where(b == 0, by_row, sn) if a == 0 else sn
        gb_ref[:, L + a * LANES:L + (a + 1) * LANES] = gs.astype(BF16)


def _dft_tables(L):
    assert L & (L - 1) == 0
    return pl.pallas_call(
        functools.partial(_dft_kernel, L=L),
        out_shape=[jax.ShapeDtypeStruct((2 * L, L), BF16), jax.ShapeDtypeStruct((2 * L, L), BF16),
                   jax.ShapeDtypeStruct((L, 2 * L), BF16)],
        compiler_params=pltpu.CompilerParams(vmem_limit_bytes=VMEM_LIMIT),
    )()


def _filt_kernel(fb_ref, fl_ref, hf_ref, hb_ref, p_ref, q_ref, s_ref, *, L):
    hf = hf_ref[...]
    hb = hb_ref[...]
    ct = hf.shape[1]
    nrm = lax.rsqrt(jnp.sum(hf * hf, axis=0, keepdims=True) + jnp.sum(hb * hb, axis=0, keepdims=True) + EPS)
    hh, hl = _split(jnp.concatenate([hf, hb], axis=1))
    fb = fb_ref[...]
    fh = _dot(fb, hh) + _dot(fl_ref[...], hh) + _dot(fb, hl)
    row = lax.broadcasted_iota(jnp.int32, (L, ct), 0)
    w = jnp.where(row == 0, 1.0 / (2 * L), 2.0 / (2 * L)) * nrm
    re = fh[:L, :ct] + fh[:L, ct:]
    p_ref[...] = re * w
    q_ref[...] = jnp.where(row == 0, 0.0, fh[L:, :ct] - fh[L:, ct:]) * w
    s_ref[...] = jnp.where(row == 0, fh[L:, :ct] + fh[L:, ct:], re) * w


def _filter_spectrum(L, table_hi, table_lo, h):
    ct = 128
    nc = HY_CH // ct
    shp = jax.ShapeDtypeStruct((L, HY_CH), F32)
    return pl.pallas_call(
        functools.partial(_filt_kernel, L=L),
        out_shape=[shp, shp, shp],
        grid=(nc,),
        in_specs=[pl.BlockSpec((2 * L, L), lambda c: (0, 0)),
                  pl.BlockSpec((2 * L, L), lambda c: (0, 0)),
                  pl.BlockSpec((L, ct), lambda c: (0, c)),
                  pl.BlockSpec((L, ct), lambda c: (0, nc + c))],
        out_specs=[pl.BlockSpec((L, ct), lambda c: (0, c))] * 3,
        compiler_params=_cparams(("arbitrary",)),
    )(table_hi, table_lo, h, h)


def _hyena_kernel(x0_ref, x1_ref, v_ref, w0_ref, w1_ref, wv_ref, b0_ref, b1_ref, bv_ref, bias_ref,
                  f_ref, g_ref, p_ref, q_ref, s_ref, o_ref, *, L, seqs):
    ct = x0_ref.shape[1]
    row = lax.broadcasted_iota(jnp.int32, (L, ct), 0)
    for n in range(seqs):
        rows = slice(n * L, (n + 1) * L)

        def sconv(x_ref, w_ref, b_ref, rows=rows):
            x = x_ref[rows, :]
            xm = jnp.where(row == 0, 0.0, pltpu.roll(x, 1, 0))
            xp = jnp.where(row == L - 1, 0.0, pltpu.roll(x, L - 1, 0))
            return xm * w_ref[0:1, :] + x * w_ref[1:2, :] + xp * w_ref[2:3, :] + b_ref[...]

        z = sconv(v_ref, wv_ref, bv_ref) * sconv(x1_ref, w1_ref, b1_ref)
        zf = _dot(f_ref[...], z.astype(BF16))
        a = zf[:L]
        b = zf[L:]
        p = p_ref[...]
        q = q_ref[...]
        yre = a * p - b * q
        yim = a * q + b * s_ref[...]
        ycat = jnp.concatenate([yre, yim], axis=0).astype(BF16)
        y = _dot(g_ref[...], ycat) + z * bias_ref[...]
        o_ref[rows, :] = sconv(x0_ref, w0_ref, b0_ref) * y


def _hyena(p_hy, short_w, short_b, bias, table_bf, table_t_bf, spec, L, nb, row0, seqs):
    ct = 256
    nc = HY_CH // ct
    P, Q, S = spec
    blk = seqs * L

    def xs(k):
        return pl.BlockSpec((blk, ct), lambda b, c: (row0 // blk + b, k * nc + c))

    def ws(rows, k):
        return pl.BlockSpec((rows, ct), lambda b, c: (0, k * nc + c))

    cs = pl.BlockSpec((L, ct), lambda b, c: (0, c))
    return pl.pallas_call(
        functools.partial(_hyena_kernel, L=L, seqs=seqs),
        out_shape=jax.ShapeDtypeStruct((nb * L, HY_CH), F32),
        grid=(nb // seqs, nc),
        in_specs=[xs(0), xs(1), xs(2), ws(3, 0), ws(3, 1), ws(3, 2), ws(1, 0), ws(1, 1), ws(1, 2), ws(1, 0),
                  pl.BlockSpec((2 * L, L), lambda b, c: (0, 0)),
                  pl.BlockSpec((L, 2 * L), lambda b, c: (0, 0)),
                  cs, cs, cs],
        out_specs=pl.BlockSpec((blk, ct), lambda b, c: (b, c)),
        compiler_params=_cparams(("arbitrary", "arbitrary")),
    )(p_hy, p_hy, p_hy, short_w, short_w, short_w, short_b, short_b, short_b, bias,
      table_bf, table_t_bf, P, Q, S)


def _rope_tables(L):
    half = HEAD_DIM // 2
    pos = jnp.arange(L)
    r = (pos // GRID_W).astype(F32)
    col = (pos % GRID_W).astype(F32)
    inv = ROPE_BASE ** (-jnp.arange(0, half, 2, dtype=F32) / half)
    ar = r[:, None] * inv[None, :]
    ac = col[:, None] * inv[None, :]
    z = jnp.zeros_like(ar)
    cos = jnp.concatenate([jnp.cos(ar), jnp.cos(ar), jnp.cos(ac), jnp.cos(ac)], axis=1)
    s1 = jnp.concatenate([-jnp.sin(ar), z, -jnp.sin(ac), z], axis=1)
    s2 = jnp.concatenate([z, jnp.sin(ar), z, jnp.sin(ac)], axis=1)
    rep = LANES // HEAD_DIM
    return jnp.tile(cos, (1, rep)), jnp.tile(s1, (1, rep)), jnp.tile(s2, (1, rep))


def _attn_kernel(*refs, L, Tq, H, KVH, Lc, window, use_sink, use_norm, use_rope, emit_k):
    G = H // KVH
    hd = HEAD_DIM
    per = LANES // hd
    pad = WINDOW if window else 0
    it = iter(refs)
    q_ref = next(it)
    kv_ref = next(it)
    ck_ref = cv_ref = sink_ref = qn_ref = kn_ref = bd_ref = rc_ref = rs1_ref = rs2_ref = ko_ref = None
    if Lc:
        ck_ref = next(it)
        cv_ref = next(it)
    if use_sink:
        sink_ref = next(it)
    if use_norm:
        qn_ref = next(it)
        kn_ref = next(it)
        bd_ref = next(it)
    if use_rope:
        rc_ref = next(it)
        rs1_ref = next(it)
        rs2_ref = next(it)
    o_ref = next(it)
    if emit_k:
        ko_ref = next(it)
    k_scr = next(it)
    v_scr = next(it)
    i = pl.program_id(1)
    lat0 = Lc + pad

    def head_norm(x, g_ref):
        hi, lo = _split(x * x)
        ms = _dot(hi, bd_ref[...]) + _dot(lo, bd_ref[...])
        return x * lax.rsqrt(ms + EPS) * g_ref[...]

    def rope(x, rows):
        return (x * rc_ref[rows, :] + pltpu.roll(x, LANES - 16, 1) * rs1_ref[rows, :]
                + pltpu.roll(x, 16, 1) * rs2_ref[rows, :])

    def put(scr, g, rows, x):
        scr[g, 0, rows, :] = x.astype(BF16)
        scr[g, 1, rows, :] = pltpu.roll(x, hd, 1).astype(BF16)

    @pl.when(i == 0)
    def _():
        n_kg = KVH // per
        if window:
            zpad = jnp.zeros((n_kg, per, pad, LANES), BF16)
            for scr in (k_scr, v_scr):
                scr[:, :, Lc:Lc + pad, :] = zpad
                scr[:, :, lat0 + L:lat0 + L + pad, :] = zpad
        for g in range(n_kg):
            cols = slice(LANES * g, LANES * (g + 1))
            kg = kv_ref[:, cols]
            if use_norm:
                kg = head_norm(kg, kn_ref)
                if emit_k:
                    ko_ref[:, cols] = kg
            if use_rope:
                kg = rope(kg, slice(None))
            put(k_scr, g, slice(lat0, lat0 + L), kg)
            put(v_scr, g, slice(lat0, lat0 + L), kv_ref[:, KVH * hd + LANES * g:KVH * hd + LANES * (g + 1)])
            if Lc:
                put(k_scr, g, slice(0, Lc), ck_ref[0][:, cols])
                put(v_scr, g, slice(0, Lc), cv_ref[0][:, cols])

    rows = pl.ds(pl.multiple_of(i * Tq, Tq), Tq)
    low = lax.broadcasted_iota(jnp.int32, (Tq, LANES), 1) < hd
    groups_per_kv = G // per
    hrow = lax.broadcasted_iota(jnp.int32, (groups_per_kv * Tq, 1), 0) // Tq
    for kvh in range(KVH):
        qgs = []
        for g in range(kvh * groups_per_kv, (kvh + 1) * groups_per_kv):
            qg = q_ref[:, LANES * g:LANES * (g + 1)]
            if use_norm:
                qg = head_norm(qg, qn_ref)
            if use_rope:
                qg = rope(qg, rows)
            qgs.append((qg * (hd ** -0.5)).astype(BF16))
        outs = []
        for half in range(per):
            keep = low if half == 0 else ~low
            qs = jnp.concatenate([jnp.where(keep, qg, jnp.zeros_like(qg)) for qg in qgs], axis=0)
            var = (kvh % per) ^ half
            kg = kvh // per
            if use_sink:
                sk = jnp.zeros((groups_per_kv * Tq, 1), F32)
                for n in range(groups_per_kv):
                    sk = jnp.where(hrow == n, sink_ref[0, (kvh * groups_per_kv + n) * per + half], sk)
            if window:
                W = Tq + 2 * pad
                wrows = pl.ds(pl.multiple_of(Lc + i * Tq, Tq), W)
                s_c = _dot(qs, k_scr[kg, var, 0:Lc, :], _NT)
                s_w = _dot(qs, k_scr[kg, var, wrows, :], _NT).reshape(groups_per_kv, Tq, W)
                r = lax.broadcasted_iota(jnp.int32, (Tq, W), 0)
                c = lax.broadcasted_iota(jnp.int32, (Tq, W), 1)
                kpos = i * Tq - pad + c
                valid = (kpos >= 0) & (kpos < L) & (jnp.abs(r + pad - c) <= WINDOW)
                s_w = jnp.where(valid[None], s_w, NEG_INF).reshape(groups_per_kv * Tq, W)
                m = jnp.maximum(jnp.max(s_c, axis=-1, keepdims=True), jnp.max(s_w, axis=-1, keepdims=True))
                if use_sink:
                    m = jnp.maximum(m, sk)
                e_c = jnp.exp(s_c - m)
                e_w = jnp.exp(s_w - m)
                den = jnp.sum(e_c, axis=-1, keepdims=True) + jnp.sum(e_w, axis=-1, keepdims=True)
                o = (_dot(e_c.astype(BF16), v_scr[kg, var, 0:Lc, :])
                     + _dot(e_w.astype(BF16), v_scr[kg, var, wrows, :]))
            else:
                s = _dot(qs, k_scr[kg, var], _NT)
                m = jnp.max(s, axis=-1, keepdims=True)
                if use_sink:
                    m = jnp.maximum(m, sk)
                e = jnp.exp(s - m)
                den = jnp.sum(e, axis=-1, keepdims=True)
                o = _dot(e.astype(BF16), v_scr[kg, var])
            if use_sink:
                den = den + jnp.exp(sk - m)
            outs.append(o / den)
        for n in range(groups_per_kv):
            g = kvh * groups_per_kv + n
            o_ref[:, LANES * g:LANES * (g + 1)] = jnp.where(low, outs[0][n * Tq:(n + 1) * Tq],
                                                            outs[1][n * Tq:(n + 1) * Tq])


def _attention(p, q_cols, kv_colblk, *, L, Tq, nb, row0, H, KVH, ctx=None, sink=None, norm=None, rope=None,
               window=False, emit_k=False):
    hd = HEAD_DIM
    nq = L // Tq
    Lc = 0 if ctx is None else ctx[0].shape[1]
    pad = WINDOW if window else 0
    Lk = Lc + L + 2 * pad
    kvw = 2 * KVH * hd
    args = [p, p]
    in_specs = [pl.BlockSpec((Tq, q_cols), lambda b, i: (row0 // Tq + b * nq + i, 0)),
                pl.BlockSpec((L, kvw), lambda b, i: (row0 // L + b, kv_colblk))]
    if ctx is not None:
        for a in ctx:
            args.append(a)
            in_specs.append(pl.BlockSpec((1, Lc, KVH * hd), lambda b, i: (b, 0, 0)))
    if sink is not None:
        args.append(sink.reshape(1, H))
        in_specs.append(pl.BlockSpec(memory_space=pltpu.SMEM))
    if norm is not None:
        bd = (jnp.arange(LANES)[:, None] // hd == jnp.arange(LANES)[None, :] // hd).astype(F32) / hd
        for a in (jnp.tile(norm[0], LANES // hd).reshape(1, LANES), jnp.tile(norm[1], LANES // hd).reshape(1, LANES)):
            args.append(a)
            in_specs.append(pl.BlockSpec((1, LANES), lambda b, i: (0, 0)))
        args.append(bd.astype(BF16))
        in_specs.append(pl.BlockSpec((LANES, LANES), lambda b, i: (0, 0)))
    if rope is not None:
        for a in rope:
            args.append(a)
            in_specs.append(pl.BlockSpec((L, LANES), lambda b, i: (0, 0)))
    out_shape = [jax.ShapeDtypeStruct((nb * L, H * hd), F32)]
    out_specs = [pl.BlockSpec((Tq, H * hd), lambda b, i: (b * nq + i, 0))]
    if emit_k:
        out_shape.append(jax.ShapeDtypeStruct((nb * L, KVH * hd), F32))
        out_specs.append(pl.BlockSpec((L, KVH * hd), lambda b, i: (b, 0)))
    kern = functools.partial(_attn_kernel, L=L, Tq=Tq, H=H, KVH=KVH, Lc=Lc, window=window,
                             use_sink=sink is not None, use_norm=norm is not None,
                             use_rope=rope is not None, emit_k=emit_k)
    call = pl.pallas_call(
        kern, out_shape=out_shape, grid=(nb, nq), in_specs=in_specs, out_specs=out_specs,
        scratch_shapes=[pltpu.VMEM((KVH * hd // LANES, LANES // hd, Lk, LANES), BF16),
                        pltpu.VMEM((KVH * hd // LANES, LANES // hd, Lk, LANES), BF16)],
        compiler_params=_cparams(("arbitrary", "arbitrary")))
    return call, args


def _oproj_kernel(*refs, paired):
    rows = slice(None)
    al, rest = _take_rows(refs, paired[0], rows)
    ar, rest = _take_rows(rest, paired[1], rows)
    x, rest = _take_rows(rest, paired[2], rows)
    w_ref, g1_ref, gate_ref, g2_ref, sh_ref, sc_ref, wr_ref, x1_ref, h2_ref, lg_ref = rest
    k = al.shape[1]
    y = _dot(al.astype(BF16), w_ref[0:k, :]) + _dot(ar.astype(BF16), w_ref[k:2 * k, :])
    x1 = x + gate_ref[0] * _rms(y, g1_ref[...])
    x1_ref[...] = x1
    h2 = _rms(x1, g2_ref[...]) * (1.0 + sc_ref[0]) + sh_ref[0]
    wh, wl = _split(wr_ref[...])
    hb = h2.astype(BF16)
    h2_ref[...] = hb
    lg_ref[...] = _dot(wh, hb, _NT) + _dot(wl, hb, _NT)


def _out_proj(a_l, l_blk, a_r, r_blk, w_bf, x, g1, g2, mod, layer, wr_t):
    tm = ROW_TILE
    half = D_MODEL // 2
    l_specs, l_args = _row_specs(a_l, half, l_blk)
    r_specs, r_args = _row_specs(a_r, half, r_blk)
    x_specs, x_args = _row_specs(x, D_MODEL)
    paired = tuple(isinstance(a, tuple) for a in (a_l, a_r, x))
    return pl.pallas_call(
        functools.partial(_oproj_kernel, paired=paired),
        out_shape=[jax.ShapeDtypeStruct((NT, D_MODEL), F32), jax.ShapeDtypeStruct((NT, D_MODEL), BF16),
                   jax.ShapeDtypeStruct((N_EXPERTS, NT), F32)],
        grid=(NT // tm,),
        in_specs=l_specs + r_specs + x_specs + [
            pl.BlockSpec((D_MODEL, D_MODEL), lambda i: (0, 0)),
            pl.BlockSpec((1, D_MODEL), lambda i: (0, 0)),
            _mod_spec(layer, 2, tm),
            pl.BlockSpec((1, D_MODEL), lambda i: (0, 0)),
            _mod_spec(layer, 3, tm), _mod_spec(layer, 4, tm),
            pl.BlockSpec((N_EXPERTS, D_MODEL), lambda i: (0, 0))],
        out_specs=[pl.BlockSpec((tm, D_MODEL), lambda i: (i, 0)),
                   pl.BlockSpec((tm, D_MODEL), lambda i: (i, 0)),
                   pl.BlockSpec((N_EXPERTS, tm), lambda i: (0, i))],
        compiler_params=_cparams(("arbitrary",)),
    )(*l_args, *r_args, *x_args, w_bf, g1.reshape(1, D_MODEL), mod, g2.reshape(1, D_MODEL), mod, mod, wr_t)


COMB_ROWS = 512
COMB_WIN = 128
BF16_SUB = 16
F32_SUB = 8
FFN_PARTS = 2


def _ffn_kernel(a0_ref, s1_ref, slot_ref, aff_ref, h_ref, wg_ref, wu_ref, wd_ref, y_ref,
                xn_scr, gn_scr, xc_scr, gc_scr, acc_scr, *, cap, nf, nblk):
    grp = pl.program_id(0)
    f = pl.program_id(1)
    R = COMB_WIN
    per_step = nblk // nf
    erow = jnp.minimum(grp, N_EXPERTS - 1) % F32_SUB

    def add_window(jb, c0, hit_of):
        cols = slice(jb * COMB_ROWS, (jb + 1) * COMB_ROWS)
        q = lax.broadcasted_iota(jnp.int32, (R, COMB_ROWS), 0)
        hit = hit_of(slot_ref[pl.ds(erow, 1), cols].astype(jnp.int32), q)
        rows = pl.ds(c0, R)
        piece = _dot(hit.astype(F32).astype(BF16), h_ref[cols, :])
        xn_scr[rows, :] += piece.astype(BF16)
        gn_scr[rows, :] += jnp.sum(jnp.where(hit, aff_ref[pl.ds(erow, 1), cols], 0.0), axis=1, keepdims=True)

    def window_start(jb):
        return pl.multiple_of(a0_ref[grp * nblk + f * per_step + jb], BF16_SUB)

    def gather():
        for jb in range(per_step):
            a0 = window_start(jb)
            add_window(jb, a0, lambda srow, q, a0=a0: srow - a0 == q)

    def gather_more():
        for jb in range(per_step):
            end = s1_ref[grp * nblk + f * per_step + jb]

            def extra(c, jb=jb):
                c0 = pl.multiple_of(jnp.minimum(c, cap - R), BF16_SUB)
                add_window(jb, c0, lambda srow, q: (srow - c0 == q) & (q + c0 >= c))
                return c + R

            lax.while_loop(lambda c, end=end: c < end, extra, window_start(jb) + R)

    def compute(first):
        wg = wg_ref[0, 0].astype(BF16)
        wu = wu_ref[0, 0].astype(BF16)
        wd = wd_ref[0, 0].astype(BF16)
        part_rows = cap // FFN_PARTS
        for r in range(FFN_PARTS):
            rows = slice(r * part_rows, (r + 1) * part_rows)
            xe = xc_scr[rows, :]
            hmid = (jax.nn.silu(_dot(xe, wg)) * _dot(xe, wu)).astype(BF16)
            part = _dot(hmid, wd)
            if first:
                acc_scr[rows, :] = part
            else:
                acc_scr[rows, :] += part

    @pl.when(f == 0)
    def _():
        xn_scr[...] = jnp.zeros_like(xn_scr)
        gn_scr[...] = jnp.zeros_like(gn_scr)

    @pl.when(grp == 0)
    def _():
        gather()

    for first in (True, False):
        first_step = (f == 0) if first else (f != 0)

        @pl.when((grp > 0) & (grp < N_EXPERTS) & first_step)
        def _():
            gather()
            compute(first)

        @pl.when((grp == N_EXPERTS) & first_step)
        def _():
            compute(first)

    @pl.when(grp < N_EXPERTS)
    def _():
        gather_more()

    @pl.when(f == nf - 1)
    def _():
        @pl.when(grp > 0)
        def _():
            y_ref[...] = (acc_scr[...] * gc_scr[...]).astype(BF16)

        xc_scr[...] = xn_scr[...]
        gc_scr[...] = gn_scr[...]


def _expert_ffn(a0, s1, slot, aff, h2, layer, w_gate, w_up, w_down):
    cap = EC_CAPACITY * NT // N_EXPERTS
    fc = 512
    nf = EXPERT_FF // fc
    nblk = NT // COMB_ROWS
    tok = NT // nf

    def prev(g):
        return jnp.maximum(g - 1, 0)

    def this(g):
        return jnp.minimum(g, N_EXPERTS - 1)

    grid_spec = pltpu.PrefetchScalarGridSpec(
        num_scalar_prefetch=2,
        grid=(N_EXPERTS + 1, nf),
        in_specs=[pl.BlockSpec((F32_SUB, tok), lambda g, f, *_: (this(g) // F32_SUB, f)),
                  pl.BlockSpec((F32_SUB, tok), lambda g, f, *_: (this(g) // F32_SUB, f)),
                  pl.BlockSpec((tok, D_MODEL), lambda g, f, *_: (f, 0)),
                  pl.BlockSpec((1, 1, D_MODEL, fc), lambda g, f, *_: (layer, prev(g), 0, f)),
                  pl.BlockSpec((1, 1, D_MODEL, fc), lambda g, f, *_: (layer, prev(g), 0, f)),
                  pl.BlockSpec((1, 1, fc, D_MODEL), lambda g, f, *_: (layer, prev(g), f, 0))],
        out_specs=pl.BlockSpec((cap, D_MODEL), lambda g, f, *_: (prev(g), 0)),
        scratch_shapes=[pltpu.VMEM((cap, D_MODEL), BF16), pltpu.VMEM((cap, 1), F32),
                        pltpu.VMEM((cap, D_MODEL), BF16), pltpu.VMEM((cap, 1), F32),
                        pltpu.VMEM((cap, D_MODEL), F32)])
    return pl.pallas_call(
        functools.partial(_ffn_kernel, cap=cap, nf=nf, nblk=nblk),
        out_shape=jax.ShapeDtypeStruct((N_EXPERTS * cap, D_MODEL), BF16),
        grid_spec=grid_spec,
        compiler_params=_cparams(("arbitrary", "arbitrary")),
    )(a0.reshape(-1), s1.reshape(-1), slot, aff, h2,
      w_gate, w_up, w_down)


CMB_ROWS = COMB_ROWS
CMB_WIN = LANES


def _combine_kernel(a0_ref, s1_ref, slot_ref, y_hbm, x_ref, g_ref, gate_ref, *rest, cap, nblk):
    *o_refs, win_scr, ovf_scr, f_scr, sem, osem = rest
    j = pl.program_id(0)
    E, R, Tb = N_EXPERTS, CMB_WIN, CMB_ROWS

    def window(e, start, dst, s):
        return pltpu.make_async_copy(y_hbm.at[pl.ds(e * cap + start, R)], dst, s)

    def issue(blk, slot):
        for e in range(E):
            a0 = pl.multiple_of(a0_ref[e * nblk + blk], BF16_SUB)
            window(e, a0, win_scr.at[slot, pl.ds(e * R, R)], sem.at[slot]).start(priority=e % 2)

    @pl.when(j == 0)
    def _():
        issue(0, 0)

    slot = j % 2
    pltpu.make_async_copy(y_hbm.at[pl.ds(0, E * R)], win_scr.at[slot], sem.at[slot]).wait()

    @pl.when(j + 1 < nblk)
    def _():
        issue(j + 1, 1 - slot)

    sl = slot_ref[...].astype(jnp.int32)
    q = lax.broadcasted_iota(jnp.int32, (Tb, R), 1)
    onehot = jnp.concatenate(
        [(sl[:, e:e + 1] - a0_ref[e * nblk + j] == q) for e in range(E)], axis=1).astype(F32).astype(BF16)
    f_scr[...] = _dot(onehot, win_scr[slot])

    for e in range(E):
        end = s1_ref[e * nblk + j]

        def more(c):
            return c < end

        def extra(c, e=e):
            c0 = pl.multiple_of(jnp.minimum(c, cap - R), BF16_SUB)
            cp = window(e, c0, ovf_scr, osem)
            cp.start()
            cp.wait()
            hit = (sl[:, e:e + 1] - c0 == q) & (q + c0 >= c)
            f_scr[...] += _dot(hit.astype(F32).astype(BF16), ovf_scr[...])
            return c + R

        lax.while_loop(more, extra, a0_ref[e * nblk + j] + R)

    res = x_ref[...] + gate_ref[0] * _rms(f_scr[...], g_ref[...])
    if len(o_refs) == 1:
        o_refs[0][...] = res
    else:
        @pl.when(j < NS // Tb)
        def _():
            o_refs[0][...] = res

        @pl.when(j >= NS // Tb)
        def _():
            o_refs[1][...] = res


def _combine(a0, s1, slot_t, y, x, g, mod, layer, split_out=False):
    cap = y.shape[0] // N_EXPERTS
    nblk = NT // CMB_ROWS
    if split_out:
        out_shape = [jax.ShapeDtypeStruct((NS, D_MODEL), F32), jax.ShapeDtypeStruct((NP, D_MODEL), F32)]
        out_specs = _stream_specs(D_MODEL, tile=CMB_ROWS)
    else:
        out_shape = jax.ShapeDtypeStruct((NT, D_MODEL), F32)
        out_specs = pl.BlockSpec((CMB_ROWS, D_MODEL), lambda i, *_: (i, 0))
    grid_spec = pltpu.PrefetchScalarGridSpec(
        num_scalar_prefetch=2,
        grid=(nblk,),
        in_specs=[pl.BlockSpec((CMB_ROWS, LANES), lambda i, *_: (i, 0)),
                  pl.BlockSpec(memory_space=pl.ANY),
                  pl.BlockSpec((CMB_ROWS, D_MODEL), lambda i, *_: (i, 0)),
                  pl.BlockSpec((1, D_MODEL), lambda i, *_: (0, 0)),
                  _mod_spec(layer, 5, CMB_ROWS)],
        out_specs=out_specs,
        scratch_shapes=[pltpu.VMEM((2, N_EXPERTS * CMB_WIN, D_MODEL), BF16),
                        pltpu.VMEM((CMB_WIN, D_MODEL), BF16),
                        pltpu.VMEM((CMB_ROWS, D_MODEL), F32),
                        pltpu.SemaphoreType.DMA((2,)), pltpu.SemaphoreType.DMA(())])
    return pl.pallas_call(
        functools.partial(_combine_kernel, cap=cap, nblk=nblk),
        out_shape=out_shape,
        grid_spec=grid_spec,
        compiler_params=_cparams(("arbitrary",)),
    )(a0.reshape(-1), s1.reshape(-1), slot_t, y, x, g.reshape(1, D_MODEL), mod)


CUM_BLK = 256
SCALE_STEP = 2.0 ** -16
SCALE_ITERS = 10
BISECT_ITERS = 48


def _prefix_count(mask):
    n = mask.shape[1]
    tri = (lax.broadcasted_iota(jnp.int32, (CUM_BLK, CUM_BLK), 0)
           <= lax.broadcasted_iota(jnp.int32, (CUM_BLK, CUM_BLK), 1)).astype(F32).astype(BF16)
    carry = jnp.zeros((mask.shape[0], 1), F32)
    outs = []
    for b in range(n // CUM_BLK):
        c = _dot(mask[:, b * CUM_BLK:(b + 1) * CUM_BLK].astype(F32).astype(BF16), tri) + carry
        outs.append(c)
        carry = c[:, CUM_BLK - 1:CUM_BLK]
    return jnp.concatenate(outs, axis=1)


def _route_kernel(lg_ref, aff_ref, slot_ref, slott_ref, s0_ref):
    x = lg_ref[...]
    ex = jnp.exp(x - jnp.max(x, axis=0, keepdims=True))
    aff = ex / jnp.sum(ex, axis=0, keepdims=True)
    aff_ref[...] = aff
    streams = ((0, NS), (NS, NP))
    caps = [float(EC_CAPACITY * n // N_EXPERTS) for _, n in streams]
    parts = [aff[:, lo:lo + n] for lo, n in streams]

    def enough(k, thr):
        return jnp.sum((parts[k] >= thr).astype(F32), axis=1, keepdims=True) >= caps[k]

    def scale_step(_, carry):
        out = []
        for k, (lo_v, hi_v, found_f) in enumerate(carry):
            found = found_f > 0.5
            mid = hi_v * SCALE_STEP
            ge = enough(k, mid)
            out.append((jnp.where(found | ~ge, lo_v, mid), jnp.where(found | ge, hi_v, mid),
                        jnp.where(found | ge, 1.0, 0.0)))
        return tuple(out)

    def bisect_step(_, carry):
        out = []
        for k, (lo_v, hi_v) in enumerate(carry):
            mid = 0.5 * (lo_v + hi_v)
            ge = enough(k, mid)
            out.append((jnp.where(ge, mid, lo_v), jnp.where(ge, hi_v, mid)))
        return tuple(out)

    col = lambda v: jnp.full((N_EXPERTS, 1), v, F32)
    start = tuple((col(0.0), col(2.0), col(0.0)) for _ in streams)
    scaled = lax.fori_loop(0, SCALE_ITERS, scale_step, start)
    bounds = lax.fori_loop(0, BISECT_ITERS, bisect_step, tuple((lo_v, hi_v) for lo_v, hi_v, _ in scaled))
    slots = []
    off = 0.0
    for k, (lo_v, hi_v) in enumerate(bounds):
        above = parts[k] >= hi_v
        edge = (parts[k] >= lo_v) & ~above
        need = caps[k] - jnp.sum(above.astype(F32), axis=1, keepdims=True)
        edge_rank = _prefix_count(edge) - edge.astype(F32)
        sel = above | (edge & (edge_rank < need))
        slots.append(jnp.where(sel, _prefix_count(sel) - 1.0 + off, -1.0))
        off += caps[k]
    slot = jnp.concatenate(slots, axis=1)
    slot_ref[...] = slot
    pad = jnp.full((LANES - N_EXPERTS, NT), -1.0, F32)
    slott_ref[...] = jnp.concatenate([slot, pad], axis=0).T
    tok = lax.broadcasted_iota(jnp.int32, (NT, LANES), 0)
    edge = lax.broadcasted_iota(jnp.int32, (NT, LANES), 1) * CMB_ROWS
    s0_ref[...] = _dot((slot >= 0.0).astype(F32).astype(BF16), (tok < edge).astype(F32).astype(BF16))


def _route_tables(logits_t):
    return pl.pallas_call(
        _route_kernel,
        out_shape=[jax.ShapeDtypeStruct((N_EXPERTS, NT), F32), jax.ShapeDtypeStruct((N_EXPERTS, NT), F32),
                   jax.ShapeDtypeStruct((NT, LANES), F32), jax.ShapeDtypeStruct((N_EXPERTS, LANES), F32)],
        compiler_params=pltpu.CompilerParams(vmem_limit_bytes=VMEM_LIMIT),
    )(logits_t)


def _route(logits_t, h2, x, g, mod, layer, w_gate, w_up, w_down, split_out=False):
    aff, slot, slot_t, s0 = _route_tables(logits_t)
    cap = EC_CAPACITY * NT // N_EXPERTS
    s0 = s0[:, :NT // CMB_ROWS + 1].astype(jnp.int32)

    def windows(starts, win):
        return jnp.minimum(starts[:, :-1] // BF16_SUB * BF16_SUB, cap - win), starts[:, 1:]

    a0g, s1g = windows(s0[:, ::COMB_ROWS // CMB_ROWS], COMB_WIN)
    y = _expert_ffn(a0g, s1g, slot, aff, h2, layer, w_gate, w_up, w_down)
    a0c, s1c = windows(s0, CMB_WIN)
    return _combine(a0c, s1c, slot_t, y, x, g, mod, layer, split_out)


def kernel(x_prompt, x_sample, cache_b_k, cache_b_v, cache_c_k, cache_c_v, c, c_ctx, ada_w, ada_b, norm_g,
           ab_w_in, hy_short_w, hy_short_b, hy_w1, hy_b1, hy_w2, hy_b2, hy_freq, hy_w3, hy_log_decay, hy_bias,
           b_sink, ab_w_out, c_w_qkv, c_q_norm, c_k_norm, c_w_out, ec_router, ec_w_gate, ec_w_up, ec_w_down):
    x = (x_sample.reshape(NS, D_MODEL), x_prompt.reshape(NP, D_MODEL))
    cond = jnp.concatenate([c, c_ctx[None], jnp.zeros((MOD_ROWS - DEC_BATCH - 1, D_MODEL), F32)], axis=0)
    mod = _adaln(cond, ada_w, ada_b).reshape(DEPTH * MOD_ROWS * 6, 1, D_MODEL)
    rope = _rope_tables(DEC_SEQ)
    hd = HEAD_DIM

    e = 0
    p_hy, p_at = _norm_mod_matmul(x, norm_g[0, 0], mod, 0, ab_w_in[e].astype(BF16), (3 * HY_CH, D_MODEL - HY_CH + 2 * B_KV * hd))
    ya = []
    for L, nb, row0, seqs in ((DEC_SEQ, DEC_BATCH, 0, 2), (SEQ, BATCH, NS, 4)):
        table_lo, table_bf, table_t_bf = _dft_tables(L)
        h = _hyena_filters(L, hy_w1[e], hy_b1[e], hy_w2[e], hy_b2[e], hy_freq[e], hy_w3[e], hy_log_decay[e])
        h = h.at[0, HY_CH:].set(0.0)
        spec = _filter_spectrum(L, table_bf, table_lo, h)
        ya.append(_hyena(p_hy, hy_short_w[e], hy_short_b[e].reshape(1, -1), hy_bias[e].reshape(1, -1),
                         table_bf, table_t_bf, spec, L, nb, row0, seqs))
    ya = tuple(ya)
    ctx_b = (cache_b_k[:, e].reshape(DEC_BATCH, PAST_LEN, B_KV * hd), cache_b_v[:, e].reshape(DEC_BATCH, PAST_LEN, B_KV * hd))
    call, args = _attention(p_at, B_HEADS * hd, 2, L=DEC_SEQ, Tq=256, nb=DEC_BATCH, row0=0, H=B_HEADS, KVH=B_KV,
                            ctx=ctx_b, sink=b_sink[e], rope=rope, window=True)
    yb_s = call(*args)[0]
    call, args = _attention(p_at, B_HEADS * hd, 2, L=SEQ, Tq=256, nb=BATCH, row0=NS, H=B_HEADS, KVH=B_KV,
                            sink=b_sink[e])
    yb_p = call(*args)[0]
    yb = (yb_s, yb_p)
    state_b_k = p_at[NS:, B_HEADS * hd:(B_HEADS + B_KV) * hd].reshape(BATCH, 1, SEQ, B_KV, hd)
    state_b_v = p_at[NS:, (B_HEADS + B_KV) * hd:].reshape(BATCH, 1, SEQ, B_KV, hd)
    x, h2, lg = _out_proj(ya, 0, yb, 0, ab_w_out[e].astype(BF16), x, norm_g[0, 1], norm_g[0, 2], mod, 0, ec_router[0].T)
    x = _route(lg, h2, x, norm_g[0, 3], mod, 0, ec_w_gate, ec_w_up, ec_w_down)

    o = 0
    (p,) = _norm_mod_matmul(x, norm_g[1, 0], mod, 1, c_w_qkv[o].astype(BF16), ((C_HEADS + 2 * C_KV) * hd,))
    ctx_c = (cache_c_k[:, o].reshape(DEC_BATCH, PAST_LEN, C_KV * hd), cache_c_v[:, o].reshape(DEC_BATCH, PAST_LEN, C_KV * hd))
    nrm = (c_q_norm[o], c_k_norm[o])
    call, args = _attention(p, C_HEADS * hd, 2, L=DEC_SEQ, Tq=256, nb=DEC_BATCH, row0=0, H=C_HEADS, KVH=C_KV,
                            ctx=ctx_c, norm=nrm, rope=rope)
    yc_s = call(*args)[0]
    call, args = _attention(p, C_HEADS * hd, 2, L=SEQ, Tq=256, nb=BATCH, row0=NS, H=C_HEADS, KVH=C_KV,
                            norm=nrm, emit_k=True)
    yc_p, kn = call(*args)
    yc = (yc_s, yc_p)
    state_c_k = kn.reshape(BATCH, 1, SEQ, C_KV, hd)
    state_c_v = p[NS:, (C_HEADS + C_KV) * hd:].reshape(BATCH, 1, SEQ, C_KV, hd)
    x, h2, lg = _out_proj(yc, 0, yc, 1, c_w_out[o].astype(BF16), x, norm_g[1, 1], norm_g[1, 2], mod, 1, ec_router[1].T)
    xs, xp = _route(lg, h2, x, norm_g[1, 3], mod, 1, ec_w_gate, ec_w_up, ec_w_down, split_out=True)
    y_sample = xs.reshape(DEC_BATCH, DEC_SEQ, D_MODEL)
    y_prompt = xp.reshape(BATCH, SEQ, D_MODEL)
    return (y_prompt, y_sample, state_b_k, state_b_v, state_c_k, state_c_v)
```
